```python
import jax, jax.numpy as jnp
from jax import lax
import numpy as np

D_MODEL = 1024
BATCH = 32
SEQ = 2048
DEPTH = 1

HEAD_DIM = 64
ROPE_THETA = 10000.0
BAND_BLOCK = 128
A_Q_HEADS = 8
A_KV_HEADS = 2
A_REP = A_Q_HEADS // A_KV_HEADS
A_WINDOW = 128
B_Q_HEADS = 8
B_KV_HEADS = 2
B_REP = B_Q_HEADS // B_KV_HEADS
CMP_BLOCK = 32
CMP_STRIDE = 16
CMP_HIDDEN = 256
SLC_BLOCK = 64
SLC_TOP_N = 16
B_WINDOW = 512
FFN_HIDDEN = ((-(-8 * D_MODEL // 3) + 255) // 256) * 256
ALPHA = (2 * DEPTH) ** 0.25
BETA = (8 * DEPTH) ** -0.25
LN_EPS = 1e-5
NEG = -1e30
BIG = 1e9

A_Q = A_Q_HEADS * HEAD_DIM
A_KV = A_KV_HEADS * HEAD_DIM
B_Q = B_Q_HEADS * HEAD_DIM
B_KV = B_KV_HEADS * HEAD_DIM
N_NSA_GATES = 3 * B_Q_HEADS
SPLIT_SIZES = (A_Q, A_KV, A_KV, B_Q, B_KV, B_KV, B_KV, B_KV, B_KV, B_KV, N_NSA_GATES, 2 * D_MODEL)
D_IN = A_Q + 2 * A_KV + B_Q + 6 * B_KV + N_NSA_GATES + 2 * D_MODEL

kernel_name = "hybrid_swa_sink_nsa_gated_deepnorm"


def layer_norm(x, g, b):
    xf = x.astype(jnp.float32)
    mu = jnp.mean(xf, axis=-1, keepdims=True)
    var = jnp.mean(jnp.square(xf - mu), axis=-1, keepdims=True)
    y = (xf - mu) * lax.rsqrt(var + LN_EPS)
    return (y * g + b).astype(x.dtype)


def rope(x, pos):
    half = x.shape[-1] // 2
    inv = ROPE_THETA ** (-jnp.arange(half, dtype=jnp.float32) / half)
    ang = pos.astype(jnp.float32)[:, None] * inv[None, :]
    cos, sin = jnp.cos(ang), jnp.sin(ang)
    xf = x.astype(jnp.float32)
    x1, x2 = xf[..., :half], xf[..., half:]
    return jnp.concatenate([x1 * cos - x2 * sin, x2 * cos + x1 * sin], axis=-1).astype(x.dtype)


def heads_q(t, n_kv, rep):
    b, s, _ = t.shape
    return t.reshape(b, s, n_kv, rep, HEAD_DIM).transpose(0, 2, 3, 1, 4)


def heads_kv(t, n_kv):
    b, s, _ = t.shape
    return t.reshape(b, s, n_kv, HEAD_DIM).transpose(0, 2, 1, 3)


def merge_heads(o):
    b, g, r, s, d = o.shape
    return o.transpose(0, 3, 1, 2, 4).reshape(b, s, g * r * d)


def banded_attention(q, k, v, window, sinks):
    b, g, r, s, d = q.shape
    nq = s // BAND_BLOCK
    nb = window // BAND_BLOCK
    pad = ((0, 0), (0, 0), (window, 0), (0, 0))
    kp = jnp.pad(k, pad).reshape(b, g, nq + nb, BAND_BLOCK, d)
    vp = jnp.pad(v, pad).reshape(b, g, nq + nb, BAND_BLOCK, d)
    kband = jnp.concatenate([kp[:, :, j:j + nq] for j in range(nb + 1)], axis=3)
    vband = jnp.concatenate([vp[:, :, j:j + nq] for j in range(nb + 1)], axis=3)
    qb = q.reshape(b, g, r, nq, BAND_BLOCK, d)
    sc = jnp.einsum("bgrnqd,bgnkd->bgrnqk", qb, kband).astype(jnp.float32) * (d ** -0.5)
    blk = jnp.arange(nq)[:, None, None] * BAND_BLOCK
    qpos = blk + jnp.arange(BAND_BLOCK)[None, :, None]
    kpos = blk - window + jnp.arange((nb + 1) * BAND_BLOCK)[None, None, :]
    mask = (kpos <= qpos) & (qpos - kpos < window) & (kpos >= 0)
    sc = jnp.where(mask, sc, NEG)
    if sinks is None:
        p = jax.nn.softmax(sc, axis=-1)
    else:
        sk = sinks.astype(jnp.float32)[None, :, :, None, None, None]
        m = jnp.maximum(jnp.max(sc, axis=-1, keepdims=True), sk)
        e = jnp.exp(sc - m)
        p = e / (jnp.sum(e, axis=-1, keepdims=True) + jnp.exp(sk - m))
    o = jnp.einsum("bgrnqk,bgnkd->bgrnqd", p.astype(v.dtype), vband)
    return o.reshape(b, g, r, s, d)


def compress(k, pe, w1, b1, w2):
    b, g, s, d = k.shape
    ratio = CMP_BLOCK // CMP_STRIDE
    nc = s // CMP_STRIDE - ratio + 1
    chunks = k.reshape(b, g, s // CMP_STRIDE, CMP_STRIDE, d)
    blocks = jnp.concatenate([chunks[:, :, j:j + nc] for j in range(ratio)], axis=3)
    flat = (blocks + pe).reshape(b, g, nc, CMP_BLOCK * d)
    return jax.nn.gelu(flat @ w1 + b1) @ w2


def compressed_attention(q, kc, vc):
    s = q.shape[3]
    nc = kc.shape[2]
    sc = jnp.einsum("bgrsd,bgcd->bgrsc", q, kc).astype(jnp.float32) * (HEAD_DIM ** -0.5)
    c_end = jnp.arange(nc) * CMP_STRIDE + CMP_BLOCK - 1
    valid = c_end[None, :] <= jnp.arange(s)[:, None]
    sc = jnp.where(valid, sc, NEG)
    e = jnp.exp(sc - jnp.max(sc, axis=-1, keepdims=True)) * valid
    p = e / jnp.maximum(jnp.sum(e, axis=-1, keepdims=True), 1e-30)
    o = jnp.einsum("bgrsc,bgcd->bgrsd", p.astype(vc.dtype), vc)
    return o, p


def select_blocks(p_cmp):
    s, nc = p_cmp.shape[3], p_cmp.shape[4]
    nb = s // SLC_BLOCK
    c_start = jnp.arange(nc) * CMP_STRIDE
    s_start = jnp.arange(nb) * SLC_BLOCK
    overlap = ((c_start[:, None] < s_start[None, :] + SLC_BLOCK)
               & (s_start[None, :] < c_start[:, None] + CMP_BLOCK)).astype(jnp.float32)
    imp = jnp.einsum("bgrsc,cj->bgsj", p_cmp, overlap)
    cur = (jnp.arange(s) // SLC_BLOCK)[:, None]
    j = jnp.arange(nb)[None, :]
    forced = (j == 0) | (j == cur) | (j == cur - 1)
    score = jnp.where(forced, BIG, jnp.where(j <= cur, imp, -BIG))
    _, idx = lax.top_k(score, min(SLC_TOP_N, nb))
    return idx


def selected_attention(q, k, v, idx):
    b, g, r, s, d = q.shape
    n = idx.shape[-1]
    nb = s // SLC_BLOCK
    nq = s // SLC_BLOCK
    kb = k.reshape(b, g, nb, SLC_BLOCK, d)
    vb = v.reshape(b, g, nb, SLC_BLOCK, d)
    q_blk = q.reshape(b, g, r, nq, SLC_BLOCK, d).transpose(3, 0, 1, 2, 4, 5)
    i_blk = idx.reshape(b, g, nq, SLC_BLOCK, n).transpose(2, 0, 1, 3, 4)
    starts = jnp.arange(nq) * SLC_BLOCK
    gather = jax.vmap(jax.vmap(lambda blocks, ids: blocks[ids]))

    def one_block(args):
        qb, ib, st = args
        kg = gather(kb, ib).reshape(b, g, SLC_BLOCK, n * SLC_BLOCK, d)
        vg = gather(vb, ib).reshape(b, g, SLC_BLOCK, n * SLC_BLOCK, d)
        kpos = (ib[..., None] * SLC_BLOCK + jnp.arange(SLC_BLOCK)).reshape(b, g, SLC_BLOCK, n * SLC_BLOCK)
        qpos = st + jnp.arange(SLC_BLOCK)
        mask = (kpos <= qpos[None, None, :, None])[:, :, None]
        sc = jnp.einsum("bgrqd,bgqkd->bgrqk", qb, kg).astype(jnp.float32) * (d ** -0.5)
        p = jax.nn.softmax(jnp.where(mask, sc, NEG), axis=-1)
        return jnp.einsum("bgrqk,bgqkd->bgrqd", p.astype(vg.dtype), vg)

    out = lax.map(one_block, (q_blk, i_blk, starts))
    return out.transpose(1, 2, 3, 0, 4, 5).reshape(b, g, r, s, d)


def token_mixer(u, w_in, sinks, cmp_pe_k, cmp_w1_k, cmp_b1_k, cmp_w2_k,
                cmp_pe_v, cmp_w1_v, cmp_b1_v, cmp_w2_v, w_proj_a, w_proj_b, w_out):
    b, s, _ = u.shape
    pos = jnp.arange(s)
    z = u @ w_in
    cuts = [int(c) for c in np.cumsum(SPLIT_SIZES)[:-1]]
    qa, ka, va, qn, kc, vc, ksl, vsl, kw, vw, g_nsa, g_merge = jnp.split(z, cuts, axis=-1)

    qa = rope(heads_q(qa, A_KV_HEADS, A_REP), pos)
    ka = rope(heads_kv(ka, A_KV_HEADS), pos)
    oa = banded_attention(qa, ka, heads_kv(va, A_KV_HEADS), A_WINDOW, sinks.reshape(A_KV_HEADS, A_REP))
    oa = merge_heads(oa)

    qn = heads_q(qn, B_KV_HEADS, B_REP)
    qn_rot = rope(qn, pos)
    kcmp = compress(heads_kv(kc, B_KV_HEADS), cmp_pe_k, cmp_w1_k, cmp_b1_k, cmp_w2_k)
    vcmp = compress(heads_kv(vc, B_KV_HEADS), cmp_pe_v, cmp_w1_v, cmp_b1_v, cmp_w2_v)
    o_cmp, p_cmp = compressed_attention(qn, kcmp, vcmp)
    idx = select_blocks(p_cmp)
    o_slc = selected_attention(qn_rot, rope(heads_kv(ksl, B_KV_HEADS), pos), heads_kv(vsl, B_KV_HEADS), idx)
    o_win = banded_attention(qn_rot, rope(heads_kv(kw, B_KV_HEADS), pos), heads_kv(vw, B_KV_HEADS), B_WINDOW, None)
    gn = jax.nn.sigmoid(g_nsa).reshape(b, s, 3, B_KV_HEADS, B_REP).transpose(2, 0, 3, 4, 1)[..., None]
    ob = merge_heads(gn[0] * o_cmp + gn[1] * o_slc + gn[2] * o_win)

    gm = jax.nn.sigmoid(g_merge).reshape(b, s, 2, D_MODEL)
    y = gm[:, :, 0] * (oa @ w_proj_a) + gm[:, :, 1] * (ob @ w_proj_b)
    return y @ w_out


def setup_inputs(seed: int = 0) -> dict:
    key = jax.random.key(seed)
    ks = jax.random.split(key, 24)
    L = DEPTH

    def nrm(k, shape, scale):
        return jax.random.normal(k, shape, jnp.float32) * scale

    fan_c = CMP_BLOCK * HEAD_DIM
    return {
        "x": nrm(ks[0], (BATCH, SEQ, D_MODEL), 1.0),
        "w_in": nrm(ks[1], (L, D_MODEL, D_IN), D_MODEL ** -0.5),
        "sinks": nrm(ks[2], (L, A_Q_HEADS), 0.5),
        "cmp_pe_k": nrm(ks[3], (L, CMP_BLOCK, HEAD_DIM), 0.1),
        "cmp_w1_k": nrm(ks[4], (L, fan_c, CMP_HIDDEN), fan_c ** -0.5),
        "cmp_b1_k": nrm(ks[5], (L, CMP_HIDDEN), 0.01),
        "cmp_w2_k": nrm(ks[6], (L, CMP_HIDDEN, HEAD_DIM), CMP_HIDDEN ** -0.5),
        "cmp_pe_v": nrm(ks[7], (L, CMP_BLOCK, HEAD_DIM), 0.1),
        "cmp_w1_v": nrm(ks[8], (L, fan_c, CMP_HIDDEN), fan_c ** -0.5),
        "cmp_b1_v": nrm(ks[9], (L, CMP_HIDDEN), 0.01),
        "cmp_w2_v": nrm(ks[10], (L, CMP_HIDDEN, HEAD_DIM), CMP_HIDDEN ** -0.5),
        "w_proj_a": nrm(ks[11], (L, A_Q, D_MODEL), A_Q ** -0.5),
        "w_proj_b": nrm(ks[12], (L, B_Q, D_MODEL), B_Q ** -0.5),
        "w_out": nrm(ks[13], (L, D_MODEL, D_MODEL), BETA * D_MODEL ** -0.5),
        "ln1_g": 1.0 + nrm(ks[14], (L, D_MODEL), 0.01),
        "ln1_b": nrm(ks[15], (L, D_MODEL), 0.01),
        "w_gate": nrm(ks[16], (L, D_MODEL, FFN_HIDDEN), D_MODEL ** -0.5),
        "w_up": nrm(ks[17], (L, D_MODEL, FFN_HIDDEN), D_MODEL ** -0.5),
        "w_down": nrm(ks[18], (L, FFN_HIDDEN, D_MODEL), BETA * FFN_HIDDEN ** -0.5),
        "ln2_g": 1.0 + nrm(ks[19], (L, D_MODEL), 0.01),
        "ln2_b": nrm(ks[20], (L, D_MODEL), 0.01),
    }


def reference(x, w_in, sinks, cmp_pe_k, cmp_w1_k, cmp_b1_k, cmp_w2_k, cmp_pe_v, cmp_w1_v, cmp_b1_v,
              cmp_w2_v, w_proj_a, w_proj_b, w_out, ln1_g, ln1_b, w_gate, w_up, w_down, ln2_g, ln2_b):
    for l in range(DEPTH):
        m = token_mixer(x, w_in[l], sinks[l], cmp_pe_k[l], cmp_w1_k[l], cmp_b1_k[l], cmp_w2_k[l],
                        cmp_pe_v[l], cmp_w1_v[l], cmp_b1_v[l], cmp_w2_v[l], w_proj_a[l], w_proj_b[l], w_out[l])
        h = layer_norm(ALPHA * x + m, ln1_g[l], ln1_b[l])
        f = (jax.nn.silu(h @ w_gate[l]) * (h @ w_up[l])) @ w_down[l]
        x = layer_norm(ALPHA * h + f, ln2_g[l], ln2_b[l])
    return x
```

```python
import functools

import jax
import jax.numpy as jnp
import numpy as np
from jax import lax
from jax.experimental import pallas as pl
from jax.experimental.pallas import tpu as pltpu

HEAD_DIM = 64
ROPE_THETA = 10000.0
Q_HEADS = 8
KV_HEADS = 2
REP = Q_HEADS // KV_HEADS
A_WINDOW = 128
B_WINDOW = 512
CMP_BLOCK = 32
CMP_STRIDE = 16
SLC_BLOCK = 64
SLC_TOP_N = 16
LN_EPS = 1e-5
NEG = -1e30
BIG = 1e9
Q_DIM = Q_HEADS * HEAD_DIM
KV_DIM = KV_HEADS * HEAD_DIM
LANES = 128
QBLK = 128
SCALE = HEAD_DIM ** -0.5

F32 = jnp.float32
BF16 = jnp.bfloat16


def _dot(a, b):
    return jnp.dot(a, b, preferred_element_type=F32)


def _dot_nt(a, b):
    return lax.dot_general(a, b, (((1,), (1,)), ((), ())), preferred_element_type=F32)


def _inproj_kernel(x_ref, w_ref, wg_ref, cos_ref, sin_ref,
                   qa_ref, ka_ref, va_ref, qn_ref, qnr_ref, kc_ref, vc_ref,
                   ksl_ref, vsl_ref, kw_ref, vw_ref, gn_ref):
    tm = x_ref.shape[0]
    xb = x_ref[...].astype(BF16)
    cos = cos_ref[...]
    sin = sin_ref[...]
    lane = lax.broadcasted_iota(jnp.int32, (tm, LANES), 1)
    first_half = (lane & (HEAD_DIM - 1)) < (HEAD_DIM // 2)

    def rope(z):
        sw = jnp.where(first_half, pltpu.roll(z, LANES - HEAD_DIM // 2, 1), pltpu.roll(z, HEAD_DIM // 2, 1))
        return z * cos + sw * sin

    def proj(c0, n):
        return _dot(xb, w_ref[:, c0:c0 + n])

    def chunks(z):
        return [z[:, c * LANES:(c + 1) * LANES] for c in range(z.shape[1] // LANES)]

    for c, zc in enumerate(chunks(proj(0, Q_DIM))):
        qa_ref[:, c * LANES:(c + 1) * LANES] = (rope(zc) * SCALE).astype(BF16)
    z = proj(Q_DIM, 2 * KV_DIM)
    ka_ref[...] = rope(z[:, :KV_DIM]).astype(BF16)
    va_ref[...] = z[:, KV_DIM:].astype(BF16)
    c0 = Q_DIM + 2 * KV_DIM
    for c, zc in enumerate(chunks(proj(c0, Q_DIM))):
        qn_ref[:, c * LANES:(c + 1) * LANES] = (zc * SCALE).astype(BF16)
        qnr_ref[:, c * LANES:(c + 1) * LANES] = (rope(zc) * SCALE).astype(BF16)
    c0 += Q_DIM
    z = proj(c0, 2 * KV_DIM)
    kc_ref[...] = z[:, :KV_DIM]
    vc_ref[...] = z[:, KV_DIM:]
    c0 += 2 * KV_DIM
    z = proj(c0, 2 * KV_DIM)
    ksl_ref[...] = rope(z[:, :KV_DIM]).astype(BF16)
    vsl_ref[...] = z[:, KV_DIM:].astype(BF16)
    c0 += 2 * KV_DIM
    z = proj(c0, 2 * KV_DIM)
    kw_ref[...] = rope(z[:, :KV_DIM]).astype(BF16)
    vw_ref[...] = z[:, KV_DIM:].astype(BF16)
    gn_ref[...] = jax.nn.sigmoid(_dot(xb, wg_ref[...]))


def _in_proj(x2, w_main, w_gn, cos, sin, seq, tm):
    n = x2.shape[0]
    d = x2.shape[1]
    spt = seq // tm
    tok = lambda width: pl.BlockSpec((tm, width), lambda i: (i, 0))
    full = lambda a: pl.BlockSpec(a.shape, lambda i: (0, 0))
    tab = pl.BlockSpec((tm, LANES), lambda i: (i % spt, 0))
    widths = [Q_DIM, KV_DIM, KV_DIM, Q_DIM, Q_DIM] + [KV_DIM] * 6 + [LANES]
    dtypes = [BF16, BF16, BF16, BF16, BF16, F32, F32, BF16, BF16, BF16, BF16, F32]
    return pl.pallas_call(
        _inproj_kernel,
        grid=(n // tm,),
        in_specs=[tok(d), full(w_main), full(w_gn), tab, tab],
        out_specs=[tok(w) for w in widths],
        out_shape=[jax.ShapeDtypeStruct((n, w), dt) for w, dt in zip(widths, dtypes)],
        compiler_params=pltpu.CompilerParams(dimension_semantics=("parallel",)),
        name="in_proj",
    )(x2, w_main, w_gn, cos, sin)


def _compress_one(src_ref, pe_ref, w1_ref, b1_ref, w2_ref, out_ref):
    ratio = CMP_BLOCK // CMP_STRIDE
    nchunk = src_ref.shape[0] // CMP_STRIDE
    hid = w1_ref.shape[1]
    parts = [jnp.zeros((KV_HEADS * nchunk, hid), F32) for _ in range(ratio)]
    for l in range(CMP_STRIDE):
        rows = src_ref[pl.ds(l, nchunk, stride=CMP_STRIDE), :]
        rows = jnp.concatenate([rows[:, g * HEAD_DIM:(g + 1) * HEAD_DIM] for g in range(KV_HEADS)], axis=0)
        for j in range(ratio):
            p = j * CMP_STRIDE + l
            a = (rows + pe_ref[p:p + 1, :]).astype(BF16)
            parts[j] = parts[j] + _dot(a, w1_ref[p * HEAD_DIM:(p + 1) * HEAD_DIM, :])
    for g in range(KV_HEADS):
        h = parts[0][g * nchunk:(g + 1) * nchunk]
        for j in range(1, ratio):
            h = h + pltpu.roll(parts[j][g * nchunk:(g + 1) * nchunk], nchunk - j, 0)
        h = jax.nn.gelu(h + b1_ref[...])
        out_ref[g] = _dot(h.astype(BF16), w2_ref[...])


def _compress_kernel(kc_ref, vc_ref, pek_ref, w1k_ref, b1k_ref, w2k_ref,
                     pev_ref, w1v_ref, b1v_ref, w2v_ref, kcmp_ref, vcmp_ref):
    _compress_one(kc_ref, pek_ref, w1k_ref, b1k_ref, w2k_ref, kcmp_ref)
    _compress_one(vc_ref, pev_ref, w1v_ref, b1v_ref, w2v_ref, vcmp_ref)


def _compress(kc, vc, wk, wv, batch, seq):
    nchunk = seq // CMP_STRIDE
    src = pl.BlockSpec((seq, KV_DIM), lambda b: (b, 0))
    full = lambda a: pl.BlockSpec(a.shape, lambda b: (0,) * a.ndim)
    out = pl.BlockSpec((None, KV_HEADS, nchunk, HEAD_DIM), lambda b: (b, 0, 0, 0))
    shape = jax.ShapeDtypeStruct((batch, KV_HEADS, nchunk, HEAD_DIM), F32)
    return pl.pallas_call(
        _compress_kernel,
        grid=(batch,),
        in_specs=[src, src] + [full(a) for a in wk] + [full(a) for a in wv],
        out_specs=[out, out],
        out_shape=[shape, shape],
        compiler_params=pltpu.CompilerParams(dimension_semantics=("parallel",)),
        name="compress",
    )(kc, vc, *wk, *wv)


def _swa_kernel(sink_ref, q_ref, kp_ref, kc_ref, vp_ref, vc_ref, o_ref):
    n = pl.program_id(1)
    band = 2 * QBLK
    qpos = n * QBLK + lax.broadcasted_iota(jnp.int32, (QBLK, band), 0)
    kpos = (n - 1) * QBLK + lax.broadcasted_iota(jnp.int32, (QBLK, band), 1)
    mask = (kpos <= qpos) & (qpos - kpos < A_WINDOW) & (kpos >= 0)
    k = jnp.concatenate([kp_ref[...], kc_ref[...]], axis=0)
    v = jnp.concatenate([vp_ref[...], vc_ref[...]], axis=0)
    for g in range(KV_HEADS):
        kg = k[:, g * HEAD_DIM:(g + 1) * HEAD_DIM]
        vg = v[:, g * HEAD_DIM:(g + 1) * HEAD_DIM]
        for r in range(REP):
            h = g * REP + r
            s = jnp.where(mask, _dot_nt(q_ref[:, h * HEAD_DIM:(h + 1) * HEAD_DIM], kg), NEG)
            sk = sink_ref[h]
            m = jnp.maximum(jnp.max(s, axis=-1, keepdims=True), sk)
            e = jnp.exp(s - m)
            den = jnp.sum(e, axis=-1, keepdims=True) + jnp.exp(sk - m)
            o = _dot(e.astype(BF16), vg) / den
            o_ref[:, h * HEAD_DIM:(h + 1) * HEAD_DIM] = o.astype(o_ref.dtype)


def _swa(sinks, qa, ka, va, batch, seq):
    nq = seq // QBLK
    q_spec = pl.BlockSpec((QBLK, Q_DIM), lambda b, n: (b * nq + n, 0))
    prev = pl.BlockSpec((QBLK, KV_DIM), lambda b, n: (b * nq + jnp.maximum(n - 1, 0), 0))
    cur = pl.BlockSpec((QBLK, KV_DIM), lambda b, n: (b * nq + n, 0))
    return pl.pallas_call(
        _swa_kernel,
        grid=(batch, nq),
        in_specs=[pl.BlockSpec(memory_space=pltpu.SMEM), q_spec, prev, cur, prev, cur],
        out_specs=q_spec,
        out_shape=jax.ShapeDtypeStruct(qa.shape, BF16),
        compiler_params=pltpu.CompilerParams(dimension_semantics=("parallel", "arbitrary")),
        name="swa",
    )(sinks, qa, ka, ka, va, va)


SLC_CHUNK = 256


def _select_blocks(imp, t0):
    rows, nb = imp.shape
    j = lax.broadcasted_iota(jnp.int32, (rows, nb), 1)
    cur = (t0 + lax.broadcasted_iota(jnp.int32, (rows, nb), 0)) >> int(np.log2(SLC_BLOCK))
    forced = (j == 0) | (j == cur) | (j == cur - 1)
    score = jnp.where(forced, BIG, jnp.where(j <= cur, imp, -BIG))
    rank = jnp.zeros((rows, nb), F32)
    for jp in range(nb):
        col = score[:, jp:jp + 1]
        beats = (col > score) | ((col == score) & (j > jp))
        rank = rank + jnp.where(beats, 1.0, 0.0)
    return jnp.where(rank < float(min(SLC_TOP_N, nb)), 1.0, 0.0)


def _nsa_group(g, n, q, qr, gn, kcmp, vcmp, ov, ksl_ref, vsl_ref, kw_ref, vw_ref, o_ref):
    seq = ksl_ref.shape[0]
    nc = kcmp.shape[0]
    nb = seq // SLC_BLOCK
    t0 = n * QBLK
    lo, hi = g * HEAD_DIM, (g + 1) * HEAD_DIM
    heads = [g * REP + r for r in range(REP)]
    hs = lambda a, h: a[:, h * HEAD_DIM:(h + 1) * HEAD_DIM]

    t_c = t0 + lax.broadcasted_iota(jnp.int32, (QBLK, nc), 0)
    c_end = lax.broadcasted_iota(jnp.int32, (QBLK, nc), 1) * CMP_STRIDE + (CMP_BLOCK - 1)
    valid = c_end <= t_c
    validf = jnp.where(valid, 1.0, 0.0)
    kcb = kcmp.astype(BF16)
    vcb = vcmp.astype(BF16)
    o_cmp = []
    psum = jnp.zeros((QBLK, nc), F32)
    for h in heads:
        s = jnp.where(valid, _dot_nt(hs(q, h), kcb), NEG)
        e = jnp.exp(s - jnp.max(s, axis=-1, keepdims=True)) * validf
        p = e / jnp.maximum(jnp.sum(e, axis=-1, keepdims=True), 1e-30)
        o_cmp.append(_dot(p.astype(BF16), vcb))
        psum = psum + p
    p_hi = psum.astype(BF16)
    p_lo = (psum - p_hi.astype(F32)).astype(BF16)
    imp = _dot(p_hi, ov) + _dot(p_lo, ov)
    sel = _select_blocks(imp, t0).astype(BF16)

    qpos = t0 + lax.broadcasted_iota(jnp.int32, (QBLK, SLC_CHUNK), 0)
    kidx = lax.broadcasted_iota(jnp.int32, (QBLK, SLC_CHUNK), 1)
    eblk = lax.broadcasted_iota(jnp.int32, (nb, SLC_CHUNK), 0)
    ekey = lax.broadcasted_iota(jnp.int32, (nb, SLC_CHUNK), 1)

    def slc_step(c, carry):
        start = pl.multiple_of(c * SLC_CHUNK, SLC_CHUNK)
        kch = ksl_ref[pl.ds(start, SLC_CHUNK), lo:hi]
        vch = vsl_ref[pl.ds(start, SLC_CHUNK), lo:hi]
        expand = jnp.where(((ekey + start) >> int(np.log2(SLC_BLOCK))) == eblk, 1.0, 0.0).astype(BF16)
        allowed = (_dot(sel, expand) > 0.5) & (kidx + start <= qpos)
        out = []
        for i, h in enumerate(heads):
            m, l, acc = carry[3 * i:3 * i + 3]
            s = jnp.where(allowed, _dot_nt(hs(qr, h), kch), NEG)
            m_new = jnp.maximum(m, jnp.max(s, axis=-1, keepdims=True))
            a = jnp.exp(m - m_new)
            e = jnp.exp(s - m_new)
            out += [m_new, a * l + jnp.sum(e, axis=-1, keepdims=True), a * acc + _dot(e.astype(BF16), vch)]
        return tuple(out)

    init = (jnp.full((QBLK, 1), NEG, F32), jnp.zeros((QBLK, 1), F32), jnp.zeros((QBLK, HEAD_DIM), F32)) * REP
    nchunks = (t0 + QBLK + SLC_CHUNK - 1) // SLC_CHUNK
    res = lax.fori_loop(0, nchunks, slc_step, init)
    o_slc = [res[3 * i + 2] / res[3 * i + 1] for i in range(REP)]

    wband = B_WINDOW + QBLK
    wstart = pl.multiple_of(jnp.maximum(n - B_WINDOW // QBLK, 0) * QBLK, QBLK)
    kwin = kw_ref[pl.ds(wstart, wband), lo:hi]
    vwin = vw_ref[pl.ds(wstart, wband), lo:hi]
    wq = t0 + lax.broadcasted_iota(jnp.int32, (QBLK, wband), 0)
    wk = wstart + lax.broadcasted_iota(jnp.int32, (QBLK, wband), 1)
    wmask = (wk <= wq) & (wq - wk < B_WINDOW)

    for i, h in enumerate(heads):
        s = jnp.where(wmask, _dot_nt(hs(qr, h), kwin), NEG)
        e = jnp.exp(s - jnp.max(s, axis=-1, keepdims=True))
        o_win = _dot(e.astype(BF16), vwin) / jnp.sum(e, axis=-1, keepdims=True)
        o = (gn[:, h:h + 1] * o_cmp[i] + gn[:, Q_HEADS + h:Q_HEADS + h + 1] * o_slc[i]
             + gn[:, 2 * Q_HEADS + h:2 * Q_HEADS + h + 1] * o_win)
        o_ref[:, h * HEAD_DIM:(h + 1) * HEAD_DIM] = o.astype(o_ref.dtype)


def _nsa_kernel(q_ref, qr_ref, gn_ref, kcmp_ref, vcmp_ref, ov_ref, ksl_ref, vsl_ref, kw_ref, vw_ref, o_ref):
    n = pl.program_id(1)
    q = q_ref[...]
    qr = qr_ref[...]
    gn = gn_ref[...]
    ov = ov_ref[...]
    for g in range(KV_HEADS):
        _nsa_group(g, n, q, qr, gn, kcmp_ref[g], vcmp_ref[g], ov, ksl_ref, vsl_ref, kw_ref, vw_ref, o_ref)


def _nsa(qn, qnr, gn, kcmp, vcmp, ksl, vsl, kw, vw, batch, seq):
    nq = seq // QBLK
    nc = kcmp.shape[2]
    nb = seq // SLC_BLOCK
    c_start = np.arange(nc)[:, None] * CMP_STRIDE
    s_start = np.arange(nb)[None, :] * SLC_BLOCK
    ov = jnp.asarray((c_start < s_start + SLC_BLOCK) & (s_start < c_start + CMP_BLOCK), BF16)
    q_spec = pl.BlockSpec((QBLK, Q_DIM), lambda b, n: (b * nq + n, 0))
    g_spec = pl.BlockSpec((QBLK, LANES), lambda b, n: (b * nq + n, 0))
    c_spec = pl.BlockSpec((None, KV_HEADS, nc, HEAD_DIM), lambda b, n: (b, 0, 0, 0))
    ov_spec = pl.BlockSpec((nc, nb), lambda b, n: (0, 0))
    kv_spec = pl.BlockSpec((seq, KV_DIM), lambda b, n: (b, 0))
    return pl.pallas_call(
        _nsa_kernel,
        grid=(batch, nq),
        in_specs=[q_spec, q_spec, g_spec, c_spec, c_spec, ov_spec, kv_spec, kv_spec, kv_spec, kv_spec],
        out_specs=q_spec,
        out_shape=jax.ShapeDtypeStruct(qn.shape, BF16),
        compiler_params=pltpu.CompilerParams(dimension_semantics=("parallel", "arbitrary")),
        name="nsa",
    )(qn, qnr, gn, kcmp, vcmp, ov, ksl, vsl, kw, vw)


def _layer_norm(r, g, b):
    mu = jnp.mean(r, axis=-1, keepdims=True)
    d = r - mu
    var = jnp.mean(d * d, axis=-1, keepdims=True)
    return d * lax.rsqrt(var + LN_EPS) * g + b


def _post_kernel(alpha, x_ref, oa_ref, ob_ref, wgm_ref, wpa_ref, wpb_ref, wout_ref, g_ref, b_ref, h_ref):
    d = x_ref.shape[1]
    x = x_ref[...]
    xb = x.astype(BF16)
    pa = _dot(oa_ref[...], wpa_ref[...])
    pb = _dot(ob_ref[...], wpb_ref[...])
    y = (jax.nn.sigmoid(_dot(xb, wgm_ref[:, :d])) * pa + jax.nn.sigmoid(_dot(xb, wgm_ref[:, d:])) * pb)
    m = _dot(y.astype(BF16), wout_ref[...])
    h_ref[...] = _layer_norm(alpha * x + m, g_ref[...], b_ref[...])


def _post(x2, oa, ob, w_gm, wpa, wpb, wout, g, b, alpha, tm):
    n, d = x2.shape
    tok = lambda width: pl.BlockSpec((tm, width), lambda i: (i, 0))
    full = lambda a: pl.BlockSpec(a.shape, lambda i: (0, 0))
    return pl.pallas_call(
        functools.partial(_post_kernel, alpha),
        grid=(n // tm,),
        in_specs=[tok(d), tok(Q_DIM), tok(Q_DIM)] + [full(a) for a in (w_gm, wpa, wpb, wout, g, b)],
        out_specs=tok(d),
        out_shape=jax.ShapeDtypeStruct((n, d), F32),
        compiler_params=pltpu.CompilerParams(dimension_semantics=("parallel",)),
        name="post",
    )(x2, oa, ob, w_gm, wpa, wpb, wout, g, b)


FFN_CHUNK = 256


def _ffn_kernel(alpha, h_ref, wg_ref, wu_ref, wd_ref, g_ref, b_ref, o_ref):
    h = h_ref[...]
    hb = h.astype(BF16)
    hidden = wg_ref.shape[1]
    acc = jnp.zeros(h.shape, F32)
    for c in range(hidden // FFN_CHUNK):
        sl = slice(c * FFN_CHUNK, (c + 1) * FFN_CHUNK)
        a = jax.nn.silu(_dot(hb, wg_ref[:, sl])) * _dot(hb, wu_ref[:, sl])
        acc = acc + _dot(a.astype(BF16), wd_ref[sl, :])
    o_ref[...] = _layer_norm(alpha * h + acc, g_ref[...], b_ref[...])


def _ffn(h, wg, wu, wd, g, b, alpha, tm):
    n, d = h.shape
    tok = pl.BlockSpec((tm, d), lambda i: (i, 0))
    once = lambda a: pl.BlockSpec(a.shape, lambda i: (0, 0), pipeline_mode=pl.Buffered(1))
    return pl.pallas_call(
        functools.partial(_ffn_kernel, alpha),
        grid=(n // tm,),
        in_specs=[tok] + [once(a) for a in (wg, wu, wd, g, b)],
        out_specs=tok,
        out_shape=jax.ShapeDtypeStruct((n, d), F32),
        compiler_params=pltpu.CompilerParams(dimension_semantics=("parallel",),
                                             vmem_limit_bytes=56 * 1024 * 1024),
        name="ffn",
    )(h, wg, wu, wd, g, b)


def _rope_tables(seq):
    half = HEAD_DIM // 2
    inv = ROPE_THETA ** (-jnp.arange(half, dtype=F32) / half)
    ang = jnp.arange(seq).astype(F32)[:, None] * inv[None, :]
    cos, sin = jnp.cos(ang), jnp.sin(ang)
    reps = LANES // HEAD_DIM
    return jnp.tile(cos, (1, 2 * reps)), jnp.tile(jnp.concatenate([-sin, sin], axis=1), (1, reps))


def kernel(x, w_in, sinks, cmp_pe_k, cmp_w1_k, cmp_b1_k, cmp_w2_k, cmp_pe_v, cmp_w1_v, cmp_b1_v, cmp_w2_v,
           w_proj_a, w_proj_b, w_out, ln1_g, ln1_b, w_gate, w_up, w_down, ln2_g, ln2_b):
    batch, seq, d = x.shape
    depth = w_in.shape[0]
    alpha = (2 * depth) ** 0.25
    n_main = 2 * Q_DIM + 8 * KV_DIM
    n_gate = 3 * Q_HEADS
    tm = 512
    cos, sin = _rope_tables(seq)
    xt = x.reshape(batch * seq, d)
    for l in range(depth):
        w_main = w_in[l, :, :n_main].astype(BF16)
        w_gn = jnp.pad(w_in[l, :, n_main:n_main + n_gate], ((0, 0), (0, LANES - n_gate))).astype(BF16)
        w_gm = w_in[l, :, n_main + n_gate:].astype(BF16)
        (qa, ka, va, qn, qnr, kc, vc, ksl, vsl, kw, vw, gn) = _in_proj(xt, w_main, w_gn, cos, sin, seq, tm)
        wk = (cmp_pe_k[l], cmp_w1_k[l].astype(BF16), cmp_b1_k[l][None, :], cmp_w2_k[l].astype(BF16))
        wv = (cmp_pe_v[l], cmp_w1_v[l].astype(BF16), cmp_b1_v[l][None, :], cmp_w2_v[l].astype(BF16))
        kcmp, vcmp = _compress(kc, vc, wk, wv, batch, seq)
        oa = _swa(sinks[l], qa, ka, va, batch, seq)
        ob = _nsa(qn, qnr, gn, kcmp, vcmp, ksl, vsl, kw, vw, batch, seq)
        h = _post(xt, oa, ob, w_gm, w_proj_a[l].astype(BF16), w_proj_b[l].astype(BF16), w_out[l].astype(BF16),
                  ln1_g[l][None, :], ln1_b[l][None, :], alpha, tm)
        xt = _ffn(h, w_gate[l].astype(BF16), w_up[l].astype(BF16), w_down[l].astype(BF16),
                  ln2_g[l][None, :], ln2_b[l][None, :], alpha, tm)
    return xt.reshape(batch, seq, d)
```

```python
import functools

import jax
import jax.numpy as jnp
import numpy as np
from jax import lax
from jax.experimental import pallas as pl
from jax.experimental.pallas import tpu as pltpu

HEAD_DIM = 64
ROPE_THETA = 10000.0
Q_HEADS = 8
KV_HEADS = 2
REP = Q_HEADS // KV_HEADS
A_WINDOW = 128
B_WINDOW = 512
CMP_BLOCK = 32
CMP_STRIDE = 16
SLC_BLOCK = 64
SLC_TOP_N = 16
LN_EPS = 1e-5
NEG = -1e30
BIG = 1e9
Q_DIM = Q_HEADS * HEAD_DIM
KV_DIM = KV_HEADS * HEAD_DIM
LANES = 128
QBLK = 128
SCALE = HEAD_DIM ** -0.5

F32 = jnp.float32
BF16 = jnp.bfloat16


def _dot(a, b):
    return jnp.dot(a, b, preferred_element_type=F32)


def _dot_nt(a, b):
    return lax.dot_general(a, b, (((1,), (1,)), ((), ())), preferred_element_type=F32)


def _inproj_kernel(x_ref, w_ref, wg_ref, cos_ref, sin_ref, ke_ref,
                   qa_ref, ka_ref, va_ref, qn_ref, qnr_ref, kc_ref, vc_ref,
                   ksl_ref, vslt_ref, kw_ref, vwt_ref, gnt_ref):
    tm = x_ref.shape[0]
    xb = x_ref[...].astype(BF16)
    cos = cos_ref[...]
    sin = sin_ref[...]
    lane = lax.broadcasted_iota(jnp.int32, (tm, LANES), 1)
    first_half = (lane & (HEAD_DIM - 1)) < (HEAD_DIM // 2)
    low = lane < HEAD_DIM

    def rope(z):
        sw = jnp.where(first_half, pltpu.roll(z, LANES - HEAD_DIM // 2, 1), pltpu.roll(z, HEAD_DIM // 2, 1))
        return z * cos + sw * sin

    def proj(c0, n):
        return _dot(xb, w_ref[:, c0:c0 + n])

    def chunks(z):
        return [z[:, c * LANES:(c + 1) * LANES] for c in range(z.shape[1] // LANES)]

    def put_padded(ref, c, z, fill):
        ref[:, (2 * c) * LANES:(2 * c + 1) * LANES] = jnp.where(low, z, fill).astype(ref.dtype)
        ref[:, (2 * c + 1) * LANES:(2 * c + 2) * LANES] = jnp.where(low, pltpu.roll(z, HEAD_DIM, 1), fill).astype(ref.dtype)

    def put_transposed(ref, z):
        for j in range(tm // LANES):
            ref[j] = z[j * LANES:(j + 1) * LANES, :].T.astype(ref.dtype)

    for c, zc in enumerate(chunks(proj(0, Q_DIM))):
        qa_ref[:, c * LANES:(c + 1) * LANES] = (rope(zc) * SCALE).astype(BF16)
    z = proj(Q_DIM, 2 * KV_DIM)
    ka_ref[...] = rope(z[:, :KV_DIM]).astype(BF16)
    va_ref[...] = z[:, KV_DIM:].astype(BF16)
    c0 = Q_DIM + 2 * KV_DIM
    for c, zc in enumerate(chunks(proj(c0, Q_DIM))):
        put_padded(qn_ref, c, zc * SCALE, 0.0)
        put_padded(qnr_ref, c, rope(zc) * SCALE, 0.0)
    c0 += Q_DIM
    z = proj(c0, 2 * KV_DIM)
    kc_ref[...] = z[:, :KV_DIM]
    vc_ref[...] = z[:, KV_DIM:]
    c0 += 2 * KV_DIM
    z = proj(c0, 2 * KV_DIM)
    put_padded(ksl_ref, 0, rope(z[:, :KV_DIM]), ke_ref[...])
    put_transposed(vslt_ref, z[:, KV_DIM:])
    c0 += 2 * KV_DIM
    z = proj(c0, 2 * KV_DIM)
    put_padded(kw_ref, 0, rope(z[:, :KV_DIM]), 0.0)
    put_transposed(vwt_ref, z[:, KV_DIM:])
    put_transposed(gnt_ref, jax.nn.sigmoid(_dot(xb, wg_ref[...])))


def _in_proj(x2, w_main, w_gn, cos, sin, ke, batch, seq, tm):
    n = x2.shape[0]
    d = x2.shape[1]
    spt = seq // tm
    sub = tm // LANES
    tok = lambda width: pl.BlockSpec((tm, width), lambda i: (i, 0))
    full = lambda a: pl.BlockSpec(a.shape, lambda i: (0, 0))
    tab = pl.BlockSpec((tm, LANES), lambda i: (i % spt, 0))
    tr = pl.BlockSpec((None, sub, LANES, LANES), lambda i: (i // spt, i % spt, 0, 0))
    flat = lambda w, dt: (tok(w), jax.ShapeDtypeStruct((n, w), dt))
    trans = lambda dt: (tr, jax.ShapeDtypeStruct((batch, seq // LANES, LANES, LANES), dt))
    outs = [flat(Q_DIM, BF16), flat(KV_DIM, BF16), flat(KV_DIM, BF16),
            flat(Q_HEADS * LANES, BF16), flat(Q_HEADS * LANES, BF16),
            flat(KV_DIM, F32), flat(KV_DIM, F32),
            flat(KV_HEADS * LANES, BF16), trans(BF16),
            flat(KV_HEADS * LANES, BF16), trans(BF16),
            trans(F32)]
    return pl.pallas_call(
        _inproj_kernel,
        grid=(n // tm,),
        in_specs=[tok(d), full(w_main), full(w_gn), tab, tab, tab],
        out_specs=[o[0] for o in outs],
        out_shape=[o[1] for o in outs],
        compiler_params=pltpu.CompilerParams(dimension_semantics=("parallel",),
                                             vmem_limit_bytes=48 * 1024 * 1024),
        name="in_proj",
    )(x2, w_main, w_gn, cos, sin, ke)


def _compress_one(src_ref, pe_ref, w1_ref, b1_ref, w2_ref):
    ratio = CMP_BLOCK // CMP_STRIDE
    nchunk = src_ref.shape[0] // CMP_STRIDE
    hid = w1_ref.shape[1]
    parts = [jnp.zeros((KV_HEADS * nchunk, hid), F32) for _ in range(ratio)]
    for l in range(CMP_STRIDE):
        rows = src_ref[pl.ds(l, nchunk, stride=CMP_STRIDE), :]
        rows = jnp.concatenate([rows[:, g * HEAD_DIM:(g + 1) * HEAD_DIM] for g in range(KV_HEADS)], axis=0)
        for j in range(ratio):
            p = j * CMP_STRIDE + l
            a = (rows + pe_ref[p:p + 1, :]).astype(BF16)
            parts[j] = parts[j] + _dot(a, w1_ref[p * HEAD_DIM:(p + 1) * HEAD_DIM, :])
    out = []
    for g in range(KV_HEADS):
        h = parts[0][g * nchunk:(g + 1) * nchunk]
        for j in range(1, ratio):
            h = h + pltpu.roll(parts[j][g * nchunk:(g + 1) * nchunk], nchunk - j, 0)
        h = jax.nn.gelu(h + b1_ref[...])
        out.append(_dot(h.astype(BF16), w2_ref[...]))
    return out


def _compress_kernel(kc_ref, vc_ref, pek_ref, w1k_ref, b1k_ref, w2k_ref,
                     pev_ref, w1v_ref, b1v_ref, w2v_ref, kcmp_ref, vcmpt_ref):
    for g, kc in enumerate(_compress_one(kc_ref, pek_ref, w1k_ref, b1k_ref, w2k_ref)):
        kcmp_ref[g] = kc.astype(BF16)
    for g, vc in enumerate(_compress_one(vc_ref, pev_ref, w1v_ref, b1v_ref, w2v_ref)):
        vcmpt_ref[g] = vc.T[:HEAD_DIM, :].astype(BF16)


def _compress(kc, vc, wk, wv, batch, seq):
    nchunk = seq // CMP_STRIDE
    src = pl.BlockSpec((seq, KV_DIM), lambda b: (b, 0))
    full = lambda a: pl.BlockSpec(a.shape, lambda b: (0,) * a.ndim)
    return pl.pallas_call(
        _compress_kernel,
        grid=(batch,),
        in_specs=[src, src] + [full(a) for a in wk] + [full(a) for a in wv],
        out_specs=[pl.BlockSpec((None, KV_HEADS, nchunk, LANES), lambda b: (b, 0, 0, 0)),
                   pl.BlockSpec((None, KV_HEADS, HEAD_DIM, nchunk), lambda b: (b, 0, 0, 0))],
        out_shape=[jax.ShapeDtypeStruct((batch, KV_HEADS, nchunk, LANES), BF16),
                   jax.ShapeDtypeStruct((batch, KV_HEADS, HEAD_DIM, nchunk), BF16)],
        compiler_params=pltpu.CompilerParams(dimension_semantics=("parallel",)),
        name="compress",
    )(kc, vc, *wk, *wv)


def _swa_kernel(sink_ref, q_ref, kp_ref, kc_ref, vp_ref, vc_ref, o_ref):
    n = pl.program_id(1)
    band = 2 * QBLK
    qpos = n * QBLK + lax.broadcasted_iota(jnp.int32, (QBLK, band), 0)
    kpos = (n - 1) * QBLK + lax.broadcasted_iota(jnp.int32, (QBLK, band), 1)
    mask = (kpos <= qpos) & (qpos - kpos < A_WINDOW) & (kpos >= 0)
    k = jnp.concatenate([kp_ref[...], kc_ref[...]], axis=0)
    v = jnp.concatenate([vp_ref[...], vc_ref[...]], axis=0)
    for g in range(KV_HEADS):
        kg = k[:, g * HEAD_DIM:(g + 1) * HEAD_DIM]
        vg = v[:, g * HEAD_DIM:(g + 1) * HEAD_DIM]
        for r in range(REP):
            h = g * REP + r
            s = jnp.where(mask, _dot_nt(q_ref[:, h * HEAD_DIM:(h + 1) * HEAD_DIM], kg), NEG)
            sk = sink_ref[h]
            m = jnp.maximum(jnp.max(s, axis=-1, keepdims=True), sk)
            e = jnp.exp(s - m)
            den = jnp.sum(e, axis=-1, keepdims=True) + jnp.exp(sk - m)
            o = _dot(e.astype(BF16), vg) / den
            o_ref[:, h * HEAD_DIM:(h + 1) * HEAD_DIM] = o.astype(o_ref.dtype)


def _swa(sinks, qa, ka, va, batch, seq):
    nq = seq // QBLK
    q_spec = pl.BlockSpec((QBLK, Q_DIM), lambda b, n: (b * nq + n, 0))
    prev = pl.BlockSpec((QBLK, KV_DIM), lambda b, n: (b * nq + jnp.maximum(n - 1, 0), 0))
    cur = pl.BlockSpec((QBLK, KV_DIM), lambda b, n: (b * nq + n, 0))
    return pl.pallas_call(
        _swa_kernel,
        grid=(batch, nq),
        in_specs=[pl.BlockSpec(memory_space=pltpu.SMEM), q_spec, prev, cur, prev, cur],
        out_specs=q_spec,
        out_shape=jax.ShapeDtypeStruct(qa.shape, BF16),
        compiler_params=pltpu.CompilerParams(dimension_semantics=("parallel", "arbitrary")),
        name="swa",
    )(sinks, qa, ka, ka, va, va)


MASK_COL = HEAD_DIM


def _select_blocks_t(imp, t0):
    nb, cols = imp.shape
    j = lax.broadcasted_iota(jnp.int32, (nb, cols), 0)
    cur = (t0 + lax.broadcasted_iota(jnp.int32, (nb, cols), 1)) >> int(np.log2(SLC_BLOCK))
    forced = (j == 0) | (j == cur) | (j == cur - 1)
    score = jnp.where(forced, BIG, jnp.where(j <= cur, imp, -BIG))
    rank = jnp.zeros((nb, cols), F32)
    for jp in range(nb):
        row = score[jp:jp + 1, :]
        beats = (row > score) | ((row == score) & (j > jp))
        rank = rank + jnp.where(beats, 1.0, 0.0)
    return jnp.where((rank < float(min(SLC_TOP_N, nb))) & (j <= cur), 1.0, 0.0)


def _nsa_kernel(qn_ref, qnr_ref, gnt_ref, kcmp_ref, vcmpt_ref, ovt_ref, cbias_ref, ebias_ref,
                ksl_ref, vslt_ref, kw_ref, vwt_ref, o_ref, qs_ref, s_ref, p_ref, acc_ref):
    n = pl.program_id(1)
    groups = range(KV_HEADS)
    width = REP * QBLK
    nb = ovt_ref.shape[0]
    t0 = pl.multiple_of(n * QBLK, QBLK)
    heads = lambda g: [g * REP + r for r in range(REP)]
    kl = lambda g: slice(g * LANES, (g + 1) * LANES)
    vr = lambda g: slice(g * HEAD_DIM, (g + 1) * HEAD_DIM)
    lane_chunks = lambda a: [a[:, r * QBLK:(r + 1) * QBLK] for r in range(REP)]

    nc = kcmp_ref.shape[1]
    c_end = lax.broadcasted_iota(jnp.int32, (nc, width), 0) * CMP_STRIDE + (CMP_BLOCK - 1)
    tq = t0 + (lax.broadcasted_iota(jnp.int32, (nc, width), 1) & (QBLK - 1))
    valid = c_end <= tq
    validf = jnp.where(valid, 1.0, 0.0)
    p_cmp, o_cmp = [], []
    for g in groups:
        qc = jnp.concatenate([qn_ref[:, h * LANES:(h + 1) * LANES] for h in heads(g)], axis=0)
        s = jnp.where(valid, _dot_nt(kcmp_ref[g], qc), NEG)
        e = jnp.exp(s - jnp.max(s, axis=0, keepdims=True)) * validf
        p = e * (1.0 / jnp.maximum(jnp.sum(e, axis=0, keepdims=True), 1e-30))
        p_cmp.append(p)
        o_cmp.append(_dot(vcmpt_ref[g], p.astype(BF16)))

    jb = lax.broadcasted_iota(jnp.int32, (nb, QBLK), 0)
    curb = (t0 + lax.broadcasted_iota(jnp.int32, (nb, QBLK), 1)) >> int(np.log2(SLC_BLOCK))

    def ranked():
        out = []
        for g in groups:
            pc = lane_chunks(p_cmp[g])
            psum = (pc[0] + pc[1]) + (pc[2] + pc[3])
            hi = psum.astype(BF16)
            lo = (psum - hi.astype(F32)).astype(BF16)
            out.append(_select_blocks_t(_dot(ovt_ref[...], hi) + _dot(ovt_ref[...], lo), t0))
        return tuple(out)

    def all_causal():
        return tuple(jnp.where(jb <= curb, 1.0, 0.0) for _ in groups)

    sel = lax.cond((t0 + QBLK - 1) // SLC_BLOCK < SLC_TOP_N, all_causal, ranked)

    for g in groups:
        mcols = jnp.concatenate([jnp.zeros((MASK_COL, QBLK), F32), sel[g] - 1.0,
                                 jnp.zeros((LANES - MASK_COL - nb, QBLK), F32)], axis=0)
        mpad = mcols.T.astype(BF16)
        for r, h in enumerate(heads(g)):
            qs_ref[g, r * QBLK:(r + 1) * QBLK, :] = qnr_ref[:, h * LANES:(h + 1) * LANES] + mpad

    def attention(k_ref, vt_ref, chunk_of, count, bias0, bias1):
        def scores(g, j, bias):
            start = pl.multiple_of(chunk_of(j) * QBLK, QBLK)
            st = _dot_nt(k_ref[pl.ds(start, QBLK), kl(g)], qs_ref[g])
            if bias is not None:
                st = st + bias
            s_ref[g] = st
            return jnp.max(st, axis=0, keepdims=True)

        def softmax(g, m, l, cmax):
            m_new = jnp.maximum(m, cmax)
            a = jnp.exp(m - m_new)
            sums = []
            for r in range(REP):
                sl = slice(r * QBLK, (r + 1) * QBLK)
                pe = jnp.exp(s_ref[g, :, sl] - m_new[:, sl])
                sums.append(jnp.sum(pe, axis=0, keepdims=True))
                p_ref[g, :, sl] = pe.astype(BF16)
            return m_new, a * l + jnp.concatenate(sums, axis=1), a

        def values(g, j, a):
            acc_ref[g] = a * acc_ref[g] + _dot(vt_ref[chunk_of(j), vr(g), :], p_ref[g])

        def step(j, state, last):
            m, l, a, cmax = state
            for g in groups:
                values(g, j, a[g])
            sm = [softmax(g, m[g], l[g], cmax[g]) for g in groups]
            if not last:
                cmax = tuple(scores(g, j + 2, None) for g in groups)
            return (tuple(s[0] for s in sm), tuple(s[1] for s in sm), tuple(s[2] for s in sm), cmax)

        m0 = jnp.full((1, width), NEG, F32)
        l0 = jnp.zeros((1, width), F32)
        cmax = [scores(g, 0, bias0) for g in groups]
        sm = [softmax(g, m0, l0, cmax[g]) for g in groups]
        cmax = tuple(scores(g, 1, bias1) for g in groups)
        for g in groups:
            acc_ref[g] = jnp.zeros((HEAD_DIM, width), F32)
        state = (tuple(s[0] for s in sm), tuple(s[1] for s in sm), tuple(s[2] for s in sm), cmax)
        state = lax.fori_loop(0, count - 2, lambda j, s: step(j, s, False), state)
        _, l, a, _ = step(count - 2, state, True)
        for g in groups:
            values(g, count - 1, a[g])
        return [acc_ref[g] * (1.0 / l[g]) for g in groups]

    cbias = cbias_ref[...]
    nfull = B_WINDOW // QBLK - 1
    o_slc = attention(ksl_ref, vslt_ref, lambda j: jnp.where(j == 0, n, jnp.maximum(j - 1, 0)),
                      jnp.maximum(n, 1) + 1, cbias, jnp.where(n > 0, 0.0, NEG))
    ce = jnp.maximum(n - nfull - 1, 0)
    o_win = attention(kw_ref, vwt_ref, lambda j: jnp.where(j == 0, n, jnp.where(j == 1, ce, n + 1 - j)),
                      jnp.minimum(n, nfull) + 2, cbias, ebias_ref[...] + jnp.where(n > nfull, 0.0, NEG))

    for g in groups:
        gate = lambda br: jnp.concatenate(
            [gnt_ref[br * Q_HEADS + h:br * Q_HEADS + h + 1, :] for h in heads(g)], axis=1)
        o = gate(0) * o_cmp[g] + gate(1) * o_slc[g] + gate(2) * o_win[g]
        oc = lane_chunks(o)
        for i in range(REP // 2):
            pair = jnp.concatenate([oc[2 * i], oc[2 * i + 1]], axis=0)
            col = (g * REP + 2 * i) * HEAD_DIM
            o_ref[:, col:col + LANES] = pair.T.astype(o_ref.dtype)


def _nsa(qn, qnr, gnt, kcmp, vcmpt, ksl, vslt, kw, vwt, batch, seq):
    nq = seq // QBLK
    nc = kcmp.shape[2]
    nb = seq // SLC_BLOCK
    assert nb <= LANES - MASK_COL and QBLK % SLC_BLOCK == 0 and B_WINDOW % QBLK == 0
    c_start = np.arange(nc)[None, :] * CMP_STRIDE
    s_start = np.arange(nb)[:, None] * SLC_BLOCK
    ovt = jnp.asarray((c_start < s_start + SLC_BLOCK) & (s_start < c_start + CMP_BLOCK), BF16)
    kq = np.arange(QBLK)[:, None] - (np.arange(REP * QBLK)[None, :] % QBLK)
    cbias = jnp.asarray(np.where(kq > 0, NEG, 0.0), F32)
    ebias = jnp.asarray(np.where(kq <= 0, NEG, 0.0), F32)
    q_spec = pl.BlockSpec((QBLK, Q_HEADS * LANES), lambda b, n: (b * nq + n, 0))
    g_spec = pl.BlockSpec((None, None, LANES, LANES), lambda b, n: (b, n, 0, 0))
    kc_spec = pl.BlockSpec((None, KV_HEADS, nc, LANES), lambda b, n: (b, 0, 0, 0))
    vc_spec = pl.BlockSpec((None, KV_HEADS, HEAD_DIM, nc), lambda b, n: (b, 0, 0, 0))
    const = lambda a: pl.BlockSpec(a.shape, lambda b, n: (0, 0))
    k_spec = pl.BlockSpec((seq, KV_HEADS * LANES), lambda b, n: (b, 0))
    vt_spec = pl.BlockSpec((None, nq, LANES, LANES), lambda b, n: (b, 0, 0, 0))
    return pl.pallas_call(
        _nsa_kernel,
        grid=(batch, nq),
        in_specs=[q_spec, q_spec, g_spec, kc_spec, vc_spec, const(ovt), const(cbias), const(ebias),
                  k_spec, vt_spec, k_spec, vt_spec],
        out_specs=pl.BlockSpec((QBLK, Q_DIM), lambda b, n: (b * nq + n, 0)),
        out_shape=jax.ShapeDtypeStruct((batch * seq, Q_DIM), BF16),
        scratch_shapes=[pltpu.VMEM((KV_HEADS, REP * QBLK, LANES), BF16),
                        pltpu.VMEM((KV_HEADS, QBLK, REP * QBLK), F32),
                        pltpu.VMEM((KV_HEADS, QBLK, REP * QBLK), BF16),
                        pltpu.VMEM((KV_HEADS, HEAD_DIM, REP * QBLK), F32)],
        compiler_params=pltpu.CompilerParams(dimension_semantics=("parallel", "arbitrary")),
        name="nsa",
    )(qn, qnr, gnt, kcmp, vcmpt, ovt, cbias, ebias, ksl, vslt, kw, vwt)


def _layer_norm(r, g, b):
    mu = jnp.mean(r, axis=-1, keepdims=True)
    d = r - mu
    var = jnp.mean(d * d, axis=-1, keepdims=True)
    return d * lax.rsqrt(var + LN_EPS) * g + b


def _post_kernel(alpha, x_ref, oa_ref, ob_ref, wgm_ref, wpa_ref, wpb_ref, wout_ref, g_ref, b_ref, h_ref):
    d = x_ref.shape[1]
    x = x_ref[...]
    xb = x.astype(BF16)
    pa = _dot(oa_ref[...], wpa_ref[...])
    pb = _dot(ob_ref[...], wpb_ref[...])
    y = (jax.nn.sigmoid(_dot(xb, wgm_ref[:, :d])) * pa + jax.nn.sigmoid(_dot(xb, wgm_ref[:, d:])) * pb)
    m = _dot(y.astype(BF16), wout_ref[...])
    h_ref[...] = _layer_norm(alpha * x + m, g_ref[...], b_ref[...])


def _post(x2, oa, ob, w_gm, wpa, wpb, wout, g, b, alpha, tm):
    n, d = x2.shape
    tok = lambda width: pl.BlockSpec((tm, width), lambda i: (i, 0))
    full = lambda a: pl.BlockSpec(a.shape, lambda i: (0, 0))
    return pl.pallas_call(
        functools.partial(_post_kernel, alpha),
        grid=(n // tm,),
        in_specs=[tok(d), tok(Q_DIM), tok(Q_DIM)] + [full(a) for a in (w_gm, wpa, wpb, wout, g, b)],
        out_specs=tok(d),
        out_shape=jax.ShapeDtypeStruct((n, d), F32),
        compiler_params=pltpu.CompilerParams(dimension_semantics=("parallel",),
                                             vmem_limit_bytes=48 * 1024 * 1024),
        name="post",
    )(x2, oa, ob, w_gm, wpa, wpb, wout, g, b)


FFN_CHUNK = 256


def _ffn_kernel(alpha, h_ref, wg_ref, wu_ref, wd_ref, g_ref, b_ref, o_ref):
    h = h_ref[...]
    hb = h.astype(BF16)
    hidden = wg_ref.shape[1]
    acc = jnp.zeros(h.shape, F32)
    for c in range(hidden // FFN_CHUNK):
        sl = slice(c * FFN_CHUNK, (c + 1) * FFN_CHUNK)
        a = jax.nn.silu(_dot(hb, wg_ref[:, sl])) * _dot(hb, wu_ref[:, sl])
        acc = acc + _dot(a.astype(BF16), wd_ref[sl, :])
    o_ref[...] = _layer_norm(alpha * h + acc, g_ref[...], b_ref[...])


def _ffn(h, wg, wu, wd, g, b, alpha, tm):
    n, d = h.shape
    tok = pl.BlockSpec((tm, d), lambda i: (i, 0))
    once = lambda a: pl.BlockSpec(a.shape, lambda i: (0, 0), pipeline_mode=pl.Buffered(1))
    return pl.pallas_call(
        functools.partial(_ffn_kernel, alpha),
        grid=(n // tm,),
        in_specs=[tok] + [once(a) for a in (wg, wu, wd, g, b)],
        out_specs=tok,
        out_shape=jax.ShapeDtypeStruct((n, d), F32),
        compiler_params=pltpu.CompilerParams(dimension_semantics=("parallel",),
                                             vmem_limit_bytes=56 * 1024 * 1024),
        name="ffn",
    )(h, wg, wu, wd, g, b)


def _position_tables(seq):
    half = HEAD_DIM // 2
    inv = ROPE_THETA ** (-jnp.arange(half, dtype=F32) / half)
    ang = jnp.arange(seq).astype(F32)[:, None] * inv[None, :]
    cos, sin = jnp.cos(ang), jnp.sin(ang)
    reps = LANES // HEAD_DIM
    lane = np.arange(LANES)[None, :]
    blk = (np.arange(seq) // SLC_BLOCK)[:, None]
    ke = jnp.asarray(np.where(lane - MASK_COL == blk, -NEG, 0.0), F32)
    return jnp.tile(cos, (1, 2 * reps)), jnp.tile(jnp.concatenate([-sin, sin], axis=1), (1, reps)), ke


def kernel(x, w_in, sinks, cmp_pe_k, cmp_w1_k, cmp_b1_k, cmp_w2_k, cmp_pe_v, cmp_w1_v, cmp_b1_v, cmp_w2_v,
           w_proj_a, w_proj_b, w_out, ln1_g, ln1_b, w_gate, w_up, w_down, ln2_g, ln2_b):
    batch, seq, d = x.shape
    depth = w_in.shape[0]
    alpha = (2 * depth) ** 0.25
    n_main = 2 * Q_DIM + 8 * KV_DIM
    n_gate = 3 * Q_HEADS
    tm = 512
    cos, sin, ke = _position_tables(seq)
    pad_cols = lambda w: jnp.pad(w, ((0, 0), (0, LANES - w.shape[1]))).astype(BF16)
    xt = x.reshape(batch * seq, d)
    for l in range(depth):
        w_main = w_in[l, :, :n_main].astype(BF16)
        w_gn = pad_cols(w_in[l, :, n_main:n_main + n_gate])
        w_gm = w_in[l, :, n_main + n_gate:].astype(BF16)
        (qa, ka, va, qn, qnr, kc, vc, ksl, vslt, kw, vwt, gnt) = _in_proj(
            xt, w_main, w_gn, cos, sin, ke, batch, seq, tm)
        wk = (cmp_pe_k[l], cmp_w1_k[l].astype(BF16), cmp_b1_k[l][None, :], pad_cols(cmp_w2_k[l]))
        wv = (cmp_pe_v[l], cmp_w1_v[l].astype(BF16), cmp_b1_v[l][None, :], pad_cols(cmp_w2_v[l]))
        kcmp, vcmpt = _compress(kc, vc, wk, wv, batch, seq)
        oa = _swa(sinks[l], qa, ka, va, batch, seq)
        ob = _nsa(qn, qnr, gnt, kcmp, vcmpt, ksl, vslt, kw, vwt, batch, seq)
        h = _post(xt, oa, ob, w_gm, w_proj_a[l].astype(BF16), w_proj_b[l].astype(BF16), w_out[l].astype(BF16),
                  ln1_g[l][None, :], ln1_b[l][None, :], alpha, tm)
        xt = _ffn(h, w_gate[l].astype(BF16), w_up[l].astype(BF16), w_down[l].astype(BF16),
                  ln2_g[l][None, :], ln2_b[l][None, :], alpha, tm)
    return xt.reshape(batch, seq, d)
```

```python
import functools

import jax
import jax.numpy as jnp
import numpy as np
from jax import lax
from jax.experimental import pallas as pl
from jax.experimental.pallas import tpu as pltpu

HEAD_DIM = 64
ROPE_THETA = 10000.0
Q_HEADS = 8
KV_HEADS = 2
REP = Q_HEADS // KV_HEADS
A_WINDOW = 128
B_WINDOW = 512
CMP_BLOCK = 32
CMP_STRIDE = 16
SLC_BLOCK = 64
SLC_TOP_N = 16
LN_EPS = 1e-5
NEG = -1e30
BIG = 1e9
Q_DIM = Q_HEADS * HEAD_DIM
KV_DIM = KV_HEADS * HEAD_DIM
LANES = 128
QBLK = 128
SCALE = HEAD_DIM ** -0.5

F32 = jnp.float32
BF16 = jnp.bfloat16


def _dot(a, b):
    return jnp.dot(a, b, preferred_element_type=F32)


def _dot_nt(a, b):
    return lax.dot_general(a, b, (((1,), (1,)), ((), ())), preferred_element_type=F32)


def _inproj_kernel(x_ref, w_ref, wg_ref, cos_ref, sin_ref, ke_ref,
                   qa_ref, ka_ref, vat_ref, qn_ref, qnr_ref, kc_ref, vc_ref,
                   ksl_ref, vslt_ref, kw_ref, vwt_ref, gnt_ref):
    tm = x_ref.shape[0]
    xb = x_ref[...].astype(BF16)
    cos = cos_ref[...]
    sin = sin_ref[...]
    lane = lax.broadcasted_iota(jnp.int32, (tm, LANES), 1)
    first_half = (lane & (HEAD_DIM - 1)) < (HEAD_DIM // 2)
    low = lane < HEAD_DIM

    def rope(z):
        sw = jnp.where(first_half, pltpu.roll(z, LANES - HEAD_DIM // 2, 1), pltpu.roll(z, HEAD_DIM // 2, 1))
        return z * cos + sw * sin

    def proj(c0, n):
        return _dot(xb, w_ref[:, c0:c0 + n])

    def chunks(z):
        return [z[:, c * LANES:(c + 1) * LANES] for c in range(z.shape[1] // LANES)]

    def put_padded(ref, c, z, fill):
        ref[:, (2 * c) * LANES:(2 * c + 1) * LANES] = jnp.where(low, z, fill).astype(ref.dtype)
        ref[:, (2 * c + 1) * LANES:(2 * c + 2) * LANES] = jnp.where(low, pltpu.roll(z, HEAD_DIM, 1), fill).astype(ref.dtype)

    def put_transposed(ref, z):
        for j in range(tm // LANES):
            ref[j] = z[j * LANES:(j + 1) * LANES, :].T.astype(ref.dtype)

    for c, zc in enumerate(chunks(proj(0, Q_DIM))):
        put_padded(qa_ref, c, rope(zc) * SCALE, 0.0)
    z = proj(Q_DIM, 2 * KV_DIM)
    put_padded(ka_ref, 0, rope(z[:, :KV_DIM]), 0.0)
    put_transposed(vat_ref, z[:, KV_DIM:])
    c0 = Q_DIM + 2 * KV_DIM
    for c, zc in enumerate(chunks(proj(c0, Q_DIM))):
        put_padded(qn_ref, c, zc * SCALE, 0.0)
        put_padded(qnr_ref, c, rope(zc) * SCALE, 0.0)
    c0 += Q_DIM
    z = proj(c0, 2 * KV_DIM)
    kc_ref[...] = z[:, :KV_DIM]
    vc_ref[...] = z[:, KV_DIM:]
    c0 += 2 * KV_DIM
    z = proj(c0, 2 * KV_DIM)
    put_padded(ksl_ref, 0, rope(z[:, :KV_DIM]), ke_ref[...])
    put_transposed(vslt_ref, z[:, KV_DIM:])
    c0 += 2 * KV_DIM
    z = proj(c0, 2 * KV_DIM)
    put_padded(kw_ref, 0, rope(z[:, :KV_DIM]), 0.0)
    put_transposed(vwt_ref, z[:, KV_DIM:])
    put_transposed(gnt_ref, jax.nn.sigmoid(_dot(xb, wg_ref[...])))


def _in_proj(x2, w_main, w_gn, cos, sin, ke, batch, seq, tm):
    n = x2.shape[0]
    d = x2.shape[1]
    spt = seq // tm
    sub = tm // LANES
    tok = lambda width: pl.BlockSpec((tm, width), lambda i: (i, 0))
    full = lambda a: pl.BlockSpec(a.shape, lambda i: (0, 0))
    tab = pl.BlockSpec((tm, LANES), lambda i: (i % spt, 0))
    tr = pl.BlockSpec((None, sub, LANES, LANES), lambda i: (i // spt, i % spt, 0, 0))
    flat = lambda w, dt: (tok(w), jax.ShapeDtypeStruct((n, w), dt))
    trans = lambda dt: (tr, jax.ShapeDtypeStruct((batch, seq // LANES, LANES, LANES), dt))
    outs = [flat(Q_HEADS * LANES, BF16), flat(KV_HEADS * LANES, BF16), trans(BF16),
            flat(Q_HEADS * LANES, BF16), flat(Q_HEADS * LANES, BF16),
            flat(KV_DIM, F32), flat(KV_DIM, F32),
            flat(KV_HEADS * LANES, BF16), trans(BF16),
            flat(KV_HEADS * LANES, BF16), trans(BF16),
            trans(F32)]
    return pl.pallas_call(
        _inproj_kernel,
        grid=(n // tm,),
        in_specs=[tok(d), full(w_main), full(w_gn), tab, tab, tab],
        out_specs=[o[0] for o in outs],
        out_shape=[o[1] for o in outs],
        compiler_params=pltpu.CompilerParams(dimension_semantics=("parallel",),
                                             vmem_limit_bytes=48 * 1024 * 1024),
        name="in_proj",
    )(x2, w_main, w_gn, cos, sin, ke)


def _compress_one(src_ref, pe_ref, w1_ref, b1_ref, w2_ref):
    ratio = CMP_BLOCK // CMP_STRIDE
    nchunk = src_ref.shape[0] // CMP_STRIDE
    hid = w1_ref.shape[1]
    parts = [jnp.zeros((KV_HEADS * nchunk, hid), F32) for _ in range(ratio)]
    for l in range(CMP_STRIDE):
        rows = src_ref[pl.ds(l, nchunk, stride=CMP_STRIDE), :]
        rows = jnp.concatenate([rows[:, g * HEAD_DIM:(g + 1) * HEAD_DIM] for g in range(KV_HEADS)], axis=0)
        for j in range(ratio):
            p = j * CMP_STRIDE + l
            a = (rows + pe_ref[p:p + 1, :]).astype(BF16)
            parts[j] = parts[j] + _dot(a, w1_ref[p * HEAD_DIM:(p + 1) * HEAD_DIM, :])
    out = []
    for g in range(KV_HEADS):
        h = parts[0][g * nchunk:(g + 1) * nchunk]
        for j in range(1, ratio):
            h = h + pltpu.roll(parts[j][g * nchunk:(g + 1) * nchunk], nchunk - j, 0)
        h = jax.nn.gelu(h + b1_ref[...])
        out.append(_dot(h.astype(BF16), w2_ref[...]))
    return out


def _compress_kernel(kc_ref, vc_ref, pek_ref, w1k_ref, b1k_ref, w2k_ref,
                     pev_ref, w1v_ref, b1v_ref, w2v_ref, kcmp_ref, vcmpt_ref):
    for g, kc in enumerate(_compress_one(kc_ref, pek_ref, w1k_ref, b1k_ref, w2k_ref)):
        kcmp_ref[g] = kc.astype(BF16)
    for g, vc in enumerate(_compress_one(vc_ref, pev_ref, w1v_ref, b1v_ref, w2v_ref)):
        vcmpt_ref[g] = vc.T[:HEAD_DIM, :].astype(BF16)


def _compress(kc, vc, wk, wv, batch, seq):
    nchunk = seq // CMP_STRIDE
    src = pl.BlockSpec((seq, KV_DIM), lambda b: (b, 0))
    full = lambda a: pl.BlockSpec(a.shape, lambda b: (0,) * a.ndim)
    return pl.pallas_call(
        _compress_kernel,
        grid=(batch,),
        in_specs=[src, src] + [full(a) for a in wk] + [full(a) for a in wv],
        out_specs=[pl.BlockSpec((None, KV_HEADS, nchunk, LANES), lambda b: (b, 0, 0, 0)),
                   pl.BlockSpec((None, KV_HEADS, HEAD_DIM, nchunk), lambda b: (b, 0, 0, 0))],
        out_shape=[jax.ShapeDtypeStruct((batch, KV_HEADS, nchunk, LANES), BF16),
                   jax.ShapeDtypeStruct((batch, KV_HEADS, HEAD_DIM, nchunk), BF16)],
        compiler_params=pltpu.CompilerParams(dimension_semantics=("parallel",)),
        name="compress",
    )(kc, vc, *wk, *wv)


def _put_heads(o_ref, g, o):
    for i in range(REP // 2):
        pair = jnp.concatenate([o[:, (2 * i) * QBLK:(2 * i + 1) * QBLK],
                                o[:, (2 * i + 1) * QBLK:(2 * i + 2) * QBLK]], axis=0)
        col = (g * REP + 2 * i) * HEAD_DIM
        o_ref[:, col:col + LANES] = pair.T.astype(o_ref.dtype)


def _swa_kernel(sink_ref, bias_ref, q_ref, k_ref, vt_ref, o_ref):
    n = pl.program_id(1)
    c0 = jnp.maximum(n - 1, 0)
    start = pl.multiple_of(c0 * QBLK, QBLK)
    bias = bias_ref[jnp.minimum(n, 1)]
    for g in range(KV_HEADS):
        heads = [g * REP + r for r in range(REP)]
        vr = slice(g * HEAD_DIM, (g + 1) * HEAD_DIM)
        qrows = jnp.concatenate([q_ref[:, h * LANES:(h + 1) * LANES] for h in heads], axis=0)
        st = _dot_nt(k_ref[pl.ds(start, 2 * QBLK), g * LANES:(g + 1) * LANES], qrows) + bias
        sk = jnp.concatenate([jnp.full((1, QBLK), sink_ref[h], F32) for h in heads], axis=1)
        m = jnp.maximum(jnp.max(st, axis=0, keepdims=True), sk)
        e = jnp.exp(st - m)
        den = jnp.sum(e, axis=0, keepdims=True) + jnp.exp(sk - m)
        vt = jnp.concatenate([vt_ref[c0, vr, :], vt_ref[c0 + 1, vr, :]], axis=1)
        _put_heads(o_ref, g, _dot(vt, e.astype(BF16)) * (1.0 / den))


def _swa(sinks, qa, ka, vat, batch, seq):
    assert A_WINDOW == QBLK
    nq = seq // QBLK
    kq = np.arange(QBLK)[:, None] - (np.arange(REP * QBLK)[None, :] % QBLK)
    after = np.where(kq > 0, NEG, 0.0)
    behind = np.where(kq <= 0, NEG, 0.0)
    bias = jnp.asarray(np.stack([np.concatenate([after, np.full_like(after, NEG)]),
                                 np.concatenate([behind, after])]), F32)
    return pl.pallas_call(
        _swa_kernel,
        grid=(batch, nq),
        in_specs=[pl.BlockSpec(memory_space=pltpu.SMEM),
                  pl.BlockSpec(bias.shape, lambda b, n: (0, 0, 0)),
                  pl.BlockSpec((QBLK, Q_HEADS * LANES), lambda b, n: (b * nq + n, 0)),
                  pl.BlockSpec((seq, KV_HEADS * LANES), lambda b, n: (b, 0)),
                  pl.BlockSpec((None, nq, LANES, LANES), lambda b, n: (b, 0, 0, 0))],
        out_specs=pl.BlockSpec((QBLK, Q_DIM), lambda b, n: (b * nq + n, 0)),
        out_shape=jax.ShapeDtypeStruct((batch * seq, Q_DIM), BF16),
        compiler_params=pltpu.CompilerParams(dimension_semantics=("parallel", "arbitrary")),
        name="swa",
    )(sinks, bias, qa, ka, vat)


MASK_COL = HEAD_DIM


def _select_blocks_t(imp, t0):
    nb, cols = imp.shape
    j = lax.broadcasted_iota(jnp.int32, (nb, cols), 0)
    cur = (t0 + lax.broadcasted_iota(jnp.int32, (nb, cols), 1)) >> int(np.log2(SLC_BLOCK))
    forced = (j == 0) | (j == cur) | (j == cur - 1)
    score = jnp.where(forced, BIG, jnp.where(j <= cur, imp, -BIG))
    rank = jnp.zeros((nb, cols), F32)
    for jp in range(nb):
        row = score[jp:jp + 1, :]
        beats = (row > score) | ((row == score) & (j > jp))
        rank = rank + jnp.where(beats, 1.0, 0.0)
    return jnp.where((rank < float(min(SLC_TOP_N, nb))) & (j <= cur), 1.0, 0.0)


def _nsa_kernel(qn_ref, qnr_ref, gnt_ref, kcmp_ref, vcmpt_ref, ovt_ref, cbias_ref, ebias_ref,
                ksl_ref, vslt_ref, kw_ref, vwt_ref, o_ref, qs_ref, s_ref, p_ref, acc_ref):
    n = pl.program_id(1)
    groups = range(KV_HEADS)
    width = REP * QBLK
    nb = ovt_ref.shape[0]
    t0 = pl.multiple_of(n * QBLK, QBLK)
    heads = lambda g: [g * REP + r for r in range(REP)]
    kl = lambda g: slice(g * LANES, (g + 1) * LANES)
    vr = lambda g: slice(g * HEAD_DIM, (g + 1) * HEAD_DIM)
    lane_chunks = lambda a: [a[:, r * QBLK:(r + 1) * QBLK] for r in range(REP)]

    nc = kcmp_ref.shape[1]
    c_end = lax.broadcasted_iota(jnp.int32, (nc, width), 0) * CMP_STRIDE + (CMP_BLOCK - 1)
    tq = t0 + (lax.broadcasted_iota(jnp.int32, (nc, width), 1) & (QBLK - 1))
    valid = c_end <= tq
    validf = jnp.where(valid, 1.0, 0.0)
    p_cmp, o_cmp = [], []
    for g in groups:
        qc = jnp.concatenate([qn_ref[:, h * LANES:(h + 1) * LANES] for h in heads(g)], axis=0)
        s = jnp.where(valid, _dot_nt(kcmp_ref[g], qc), NEG)
        e = jnp.exp(s - jnp.max(s, axis=0, keepdims=True)) * validf
        p = e * (1.0 / jnp.maximum(jnp.sum(e, axis=0, keepdims=True), 1e-30))
        p_cmp.append(p)
        o_cmp.append(_dot(vcmpt_ref[g], p.astype(BF16)))

    jb = lax.broadcasted_iota(jnp.int32, (nb, QBLK), 0)
    curb = (t0 + lax.broadcasted_iota(jnp.int32, (nb, QBLK), 1)) >> int(np.log2(SLC_BLOCK))

    def ranked():
        out = []
        for g in groups:
            pc = lane_chunks(p_cmp[g])
            psum = (pc[0] + pc[1]) + (pc[2] + pc[3])
            hi = psum.astype(BF16)
            lo = (psum - hi.astype(F32)).astype(BF16)
            out.append(_select_blocks_t(_dot(ovt_ref[...], hi) + _dot(ovt_ref[...], lo), t0))
        return tuple(out)

    def all_causal():
        return tuple(jnp.where(jb <= curb, 1.0, 0.0) for _ in groups)

    sel = lax.cond((t0 + QBLK - 1) // SLC_BLOCK < SLC_TOP_N, all_causal, ranked)

    for g in groups:
        mcols = jnp.concatenate([jnp.zeros((MASK_COL, QBLK), F32), sel[g] - 1.0,
                                 jnp.zeros((LANES - MASK_COL - nb, QBLK), F32)], axis=0)
        mpad = mcols.T.astype(BF16)
        for r, h in enumerate(heads(g)):
            qs_ref[g, r * QBLK:(r + 1) * QBLK, :] = qnr_ref[:, h * LANES:(h + 1) * LANES] + mpad

    def attention(k_ref, vt_ref, chunk_of, count, bias0, bias1):
        def qk(g, j):
            start = pl.multiple_of(chunk_of(j) * QBLK, QBLK)
            return _dot_nt(k_ref[pl.ds(start, QBLK), kl(g)], qs_ref[g])

        def keep(g, st, bias):
            if bias is not None:
                st = st + bias
            s_ref[g] = st
            return jnp.max(st, axis=0, keepdims=True)

        def scores(g, j, bias):
            return keep(g, qk(g, j), bias)

        def softmax(g, m, l, cmax):
            m_new = jnp.maximum(m, cmax)
            a = jnp.exp(m - m_new)
            sums = []
            for r in range(REP):
                sl = slice(r * QBLK, (r + 1) * QBLK)
                pe = jnp.exp(s_ref[g, :, sl] - m_new[:, sl])
                sums.append(jnp.sum(pe, axis=0, keepdims=True))
                p_ref[g, :, sl] = pe.astype(BF16)
            return m_new, a * l + jnp.concatenate(sums, axis=1), a

        def values(g, j, a):
            acc_ref[g] = a * acc_ref[g] + _dot(vt_ref[chunk_of(j), vr(g), :], p_ref[g])

        def step(j, state, last):
            m, l, a, cmax = state
            st = None if last else [qk(g, j + 2) for g in groups]
            for g in groups:
                values(g, j, a[g])
            sm = [softmax(g, m[g], l[g], cmax[g]) for g in groups]
            if not last:
                cmax = tuple(keep(g, st[g], None) for g in groups)
            return (tuple(s[0] for s in sm), tuple(s[1] for s in sm), tuple(s[2] for s in sm), cmax)

        m0 = jnp.full((1, width), NEG, F32)
        l0 = jnp.zeros((1, width), F32)
        cmax = [scores(g, 0, bias0) for g in groups]
        sm = [softmax(g, m0, l0, cmax[g]) for g in groups]
        cmax = tuple(scores(g, 1, bias1) for g in groups)
        for g in groups:
            acc_ref[g] = jnp.zeros((HEAD_DIM, width), F32)
        state = (tuple(s[0] for s in sm), tuple(s[1] for s in sm), tuple(s[2] for s in sm), cmax)
        state = lax.fori_loop(0, count - 2, lambda j, s: step(j, s, False), state)
        _, l, a, _ = step(count - 2, state, True)
        for g in groups:
            values(g, count - 1, a[g])
        return [acc_ref[g] * (1.0 / l[g]) for g in groups]

    cbias = cbias_ref[...]
    nfull = B_WINDOW // QBLK - 1
    o_slc = attention(ksl_ref, vslt_ref, lambda j: jnp.where(j == 0, n, jnp.maximum(j - 1, 0)),
                      jnp.maximum(n, 1) + 1, cbias, jnp.where(n > 0, 0.0, NEG))
    ce = jnp.maximum(n - nfull - 1, 0)
    o_win = attention(kw_ref, vwt_ref, lambda j: jnp.where(j == 0, n, jnp.where(j == 1, ce, n + 1 - j)),
                      jnp.minimum(n, nfull) + 2, cbias, ebias_ref[...] + jnp.where(n > nfull, 0.0, NEG))

    for g in groups:
        gate = lambda br: jnp.concatenate(
            [gnt_ref[br * Q_HEADS + h:br * Q_HEADS + h + 1, :] for h in heads(g)], axis=1)
        _put_heads(o_ref, g, gate(0) * o_cmp[g] + gate(1) * o_slc[g] + gate(2) * o_win[g])


def _nsa(qn, qnr, gnt, kcmp, vcmpt, ksl, vslt, kw, vwt, batch, seq):
    nq = seq // QBLK
    nc = kcmp.shape[2]
    nb = seq // SLC_BLOCK
    assert nb <= LANES - MASK_COL and QBLK % SLC_BLOCK == 0 and B_WINDOW % QBLK == 0
    c_start = np.arange(nc)[None, :] * CMP_STRIDE
    s_start = np.arange(nb)[:, None] * SLC_BLOCK
    ovt = jnp.asarray((c_start < s_start + SLC_BLOCK) & (s_start < c_start + CMP_BLOCK), BF16)
    kq = np.arange(QBLK)[:, None] - (np.arange(REP * QBLK)[None, :] % QBLK)
    cbias = jnp.asarray(np.where(kq > 0, NEG, 0.0), F32)
    ebias = jnp.asarray(np.where(kq <= 0, NEG, 0.0), F32)
    q_spec = pl.BlockSpec((QBLK, Q_HEADS * LANES), lambda b, n: (b * nq + n, 0))
    g_spec = pl.BlockSpec((None, None, LANES, LANES), lambda b, n: (b, n, 0, 0))
    kc_spec = pl.BlockSpec((None, KV_HEADS, nc, LANES), lambda b, n: (b, 0, 0, 0))
    vc_spec = pl.BlockSpec((None, KV_HEADS, HEAD_DIM, nc), lambda b, n: (b, 0, 0, 0))
    const = lambda a: pl.BlockSpec(a.shape, lambda b, n: (0, 0))
    k_spec = pl.BlockSpec((seq, KV_HEADS * LANES), lambda b, n: (b, 0))
    vt_spec = pl.BlockSpec((None, nq, LANES, LANES), lambda b, n: (b, 0, 0, 0))
    return pl.pallas_call(
        _nsa_kernel,
        grid=(batch, nq),
        in_specs=[q_spec, q_spec, g_spec, kc_spec, vc_spec, const(ovt), const(cbias), const(ebias),
                  k_spec, vt_spec, k_spec, vt_spec],
        out_specs=pl.BlockSpec((QBLK, Q_DIM), lambda b, n: (b * nq + n, 0)),
        out_shape=jax.ShapeDtypeStruct((batch * seq, Q_DIM), BF16),
        scratch_shapes=[pltpu.VMEM((KV_HEADS, REP * QBLK, LANES), BF16),
                        pltpu.VMEM((KV_HEADS, QBLK, REP * QBLK), F32),
                        pltpu.VMEM((KV_HEADS, QBLK, REP * QBLK), BF16),
                        pltpu.VMEM((KV_HEADS, HEAD_DIM, REP * QBLK), F32)],
        compiler_params=pltpu.CompilerParams(dimension_semantics=("parallel", "arbitrary")),
        name="nsa",
    )(qn, qnr, gnt, kcmp, vcmpt, ovt, cbias, ebias, ksl, vslt, kw, vwt)


def _layer_norm(r, g, b):
    mu = jnp.mean(r, axis=-1, keepdims=True)
    d = r - mu
    var = jnp.mean(d * d, axis=-1, keepdims=True)
    return d * lax.rsqrt(var + LN_EPS) * g + b


def _post_kernel(alpha, x_ref, oa_ref, ob_ref, wgm_ref, wpa_ref, wpb_ref, wout_ref, g_ref, b_ref, h_ref):
    d = x_ref.shape[1]
    x = x_ref[...]
    xb = x.astype(BF16)
    pa = _dot(oa_ref[...], wpa_ref[...])
    pb = _dot(ob_ref[...], wpb_ref[...])
    y = (jax.nn.sigmoid(_dot(xb, wgm_ref[:, :d])) * pa + jax.nn.sigmoid(_dot(xb, wgm_ref[:, d:])) * pb)
    m = _dot(y.astype(BF16), wout_ref[...])
    h_ref[...] = _layer_norm(alpha * x + m, g_ref[...], b_ref[...])


def _post(x2, oa, ob, w_gm, wpa, wpb, wout, g, b, alpha, tm):
    n, d = x2.shape
    tok = lambda width: pl.BlockSpec((tm, width), lambda i: (i, 0))
    full = lambda a: pl.BlockSpec(a.shape, lambda i: (0, 0))
    return pl.pallas_call(
        functools.partial(_post_kernel, alpha),
        grid=(n // tm,),
        in_specs=[tok(d), tok(Q_DIM), tok(Q_DIM)] + [full(a) for a in (w_gm, wpa, wpb, wout, g, b)],
        out_specs=tok(d),
        out_shape=jax.ShapeDtypeStruct((n, d), F32),
        compiler_params=pltpu.CompilerParams(dimension_semantics=("parallel",),
                                             vmem_limit_bytes=48 * 1024 * 1024),
        name="post",
    )(x2, oa, ob, w_gm, wpa, wpb, wout, g, b)


FFN_CHUNK = 256


def _ffn_kernel(alpha, h_ref, wg_ref, wu_ref, wd_ref, g_ref, b_ref, o_ref):
    h = h_ref[...]
    hb = h.astype(BF16)
    hidden = wg_ref.shape[1]
    acc = jnp.zeros(h.shape, F32)
    for c in range(hidden // FFN_CHUNK):
        sl = slice(c * FFN_CHUNK, (c + 1) * FFN_CHUNK)
        a = jax.nn.silu(_dot(hb, wg_ref[:, sl])) * _dot(hb, wu_ref[:, sl])
        acc = acc + _dot(a.astype(BF16), wd_ref[sl, :])
    o_ref[...] = _layer_norm(alpha * h + acc, g_ref[...], b_ref[...])


def _ffn(h, wg, wu, wd, g, b, alpha, tm):
    n, d = h.shape
    tok = pl.BlockSpec((tm, d), lambda i: (i, 0))
    once = lambda a: pl.BlockSpec(a.shape, lambda i: (0, 0), pipeline_mode=pl.Buffered(1))
    return pl.pallas_call(
        functools.partial(_ffn_kernel, alpha),
        grid=(n // tm,),
        in_specs=[tok] + [once(a) for a in (wg, wu, wd, g, b)],
        out_specs=tok,
        out_shape=jax.ShapeDtypeStruct((n, d), F32),
        compiler_params=pltpu.CompilerParams(dimension_semantics=("parallel",),
                                             vmem_limit_bytes=56 * 1024 * 1024),
        name="ffn",
    )(h, wg, wu, wd, g, b)


def _position_tables(seq):
    half = HEAD_DIM // 2
    inv = ROPE_THETA ** (-jnp.arange(half, dtype=F32) / half)
    ang = jnp.arange(seq).astype(F32)[:, None] * inv[None, :]
    cos, sin = jnp.cos(ang), jnp.sin(ang)
    reps = LANES // HEAD_DIM
    lane = np.arange(LANES)[None, :]
    blk = (np.arange(seq) // SLC_BLOCK)[:, None]
    ke = jnp.asarray(np.where(lane - MASK_COL == blk, -NEG, 0.0), F32)
    return jnp.tile(cos, (1, 2 * reps)), jnp.tile(jnp.concatenate([-sin, sin], axis=1), (1, reps)), ke


def kernel(x, w_in, sinks, cmp_pe_k, cmp_w1_k, cmp_b1_k, cmp_w2_k, cmp_pe_v, cmp_w1_v, cmp_b1_v, cmp_w2_v,
           w_proj_a, w_proj_b, w_out, ln1_g, ln1_b, w_gate, w_up, w_down, ln2_g, ln2_b):
    batch, seq, d = x.shape
    depth = w_in.shape[0]
    alpha = (2 * depth) ** 0.25
    n_main = 2 * Q_DIM + 8 * KV_DIM
    n_gate = 3 * Q_HEADS
    tm = 512
    cos, sin, ke = _position_tables(seq)
    pad_cols = lambda w: jnp.pad(w, ((0, 0), (0, LANES - w.shape[1]))).astype(BF16)
    xt = x.reshape(batch * seq, d)
    for l in range(depth):
        w_main = w_in[l, :, :n_main].astype(BF16)
        w_gn = pad_cols(w_in[l, :, n_main:n_main + n_gate])
        w_gm = w_in[l, :, n_main + n_gate:].astype(BF16)
        (qa, ka, va, qn, qnr, kc, vc, ksl, vslt, kw, vwt, gnt) = _in_proj(
            xt, w_main, w_gn, cos, sin, ke, batch, seq, tm)
        wk = (cmp_pe_k[l], cmp_w1_k[l].astype(BF16), cmp_b1_k[l][None, :], pad_cols(cmp_w2_k[l]))
        wv = (cmp_pe_v[l], cmp_w1_v[l].astype(BF16), cmp_b1_v[l][None, :], pad_cols(cmp_w2_v[l]))
        kcmp, vcmpt = _compress(kc, vc, wk, wv, batch, seq)
        oa = _swa(sinks[l], qa, ka, va, batch, seq)
        ob = _nsa(qn, qnr, gnt, kcmp, vcmpt, ksl, vslt, kw, vwt, batch, seq)
        h = _post(xt, oa, ob, w_gm, w_proj_a[l].astype(BF16), w_proj_b[l].astype(BF16), w_out[l].astype(BF16),
                  ln1_g[l][None, :], ln1_b[l][None, :], alpha, tm)
        xt = _ffn(h, w_gate[l].astype(BF16), w_up[l].astype(BF16), w_down[l].astype(BF16),
                  ln2_g[l][None, :], ln2_b[l][None, :], alpha, tm)
    return xt.reshape(batch, seq, d)
```

```python
import functools

import jax
import jax.numpy as jnp
import numpy as np
from jax import lax
from jax.experimental import pallas as pl
from jax.experimental.pallas import tpu as pltpu

HEAD_DIM = 64
ROPE_THETA = 10000.0
Q_HEADS = 8
KV_HEADS = 2
REP = Q_HEADS // KV_HEADS
A_WINDOW = 128
B_WINDOW = 512
CMP_BLOCK = 32
CMP_STRIDE = 16
SLC_BLOCK = 64
SLC_TOP_N = 16
LN_EPS = 1e-5
NEG = -1e30
BIG = 1e9
Q_DIM = Q_HEADS * HEAD_DIM
KV_DIM = KV_HEADS * HEAD_DIM
LANES = 128
QBLK = 128
SCALE = HEAD_DIM ** -0.5

F32 = jnp.float32
BF16 = jnp.bfloat16


def _dot(a, b):
    return jnp.dot(a, b, preferred_element_type=F32)


def _dot_nt(a, b):
    return lax.dot_general(a, b, (((1,), (1,)), ((), ())), preferred_element_type=F32)


def _inproj_kernel(x_ref, w_ref, wg_ref, cos_ref, sin_ref, ke_ref,
                   qa_ref, ka_ref, vat_ref, qn_ref, qnr_ref, kc_ref, vc_ref,
                   ksl_ref, vslt_ref, kw_ref, vwt_ref, gnt_ref):
    tm = x_ref.shape[0]
    xb = x_ref[...].astype(BF16)
    cos = cos_ref[...]
    sin = sin_ref[...]
    lane = lax.broadcasted_iota(jnp.int32, (tm, LANES), 1)
    first_half = (lane & (HEAD_DIM - 1)) < (HEAD_DIM // 2)
    low = lane < HEAD_DIM

    def rope(z):
        sw = jnp.where(first_half, pltpu.roll(z, LANES - HEAD_DIM // 2, 1), pltpu.roll(z, HEAD_DIM // 2, 1))
        return z * cos + sw * sin

    def proj(c0, n):
        return _dot(xb, w_ref[:, c0:c0 + n])

    def chunks(z):
        return [z[:, c * LANES:(c + 1) * LANES] for c in range(z.shape[1] // LANES)]

    def put_padded(ref, c, z, fill):
        ref[:, (2 * c) * LANES:(2 * c + 1) * LANES] = jnp.where(low, z, fill).astype(ref.dtype)
        ref[:, (2 * c + 1) * LANES:(2 * c + 2) * LANES] = jnp.where(low, pltpu.roll(z, HEAD_DIM, 1), fill).astype(ref.dtype)

    def put_transposed(ref, z):
        for j in range(tm // LANES):
            ref[j] = z[j * LANES:(j + 1) * LANES, :].T.astype(ref.dtype)

    for c, zc in enumerate(chunks(proj(0, Q_DIM))):
        put_padded(qa_ref, c, rope(zc) * SCALE, 0.0)
    z = proj(Q_DIM, 2 * KV_DIM)
    put_padded(ka_ref, 0, rope(z[:, :KV_DIM]), 0.0)
    put_transposed(vat_ref, z[:, KV_DIM:])
    c0 = Q_DIM + 2 * KV_DIM
    for c, zc in enumerate(chunks(proj(c0, Q_DIM))):
        put_padded(qn_ref, c, zc * SCALE, 0.0)
        put_padded(qnr_ref, c, rope(zc) * SCALE, 0.0)
    c0 += Q_DIM
    z = proj(c0, 2 * KV_DIM)
    kc_ref[...] = z[:, :KV_DIM]
    vc_ref[...] = z[:, KV_DIM:]
    c0 += 2 * KV_DIM
    z = proj(c0, 2 * KV_DIM)
    put_padded(ksl_ref, 0, rope(z[:, :KV_DIM]), ke_ref[...])
    put_transposed(vslt_ref, z[:, KV_DIM:])
    c0 += 2 * KV_DIM
    z = proj(c0, 2 * KV_DIM)
    put_padded(kw_ref, 0, rope(z[:, :KV_DIM]), 0.0)
    put_transposed(vwt_ref, z[:, KV_DIM:])
    put_transposed(gnt_ref, jax.nn.sigmoid(_dot(xb, wg_ref[...])))


def _in_proj(x2, w_main, w_gn, cos, sin, ke, batch, seq, tm):
    n = x2.shape[0]
    d = x2.shape[1]
    spt = seq // tm
    sub = tm // LANES
    tok = lambda width: pl.BlockSpec((tm, width), lambda i: (i, 0))
    full = lambda a: pl.BlockSpec(a.shape, lambda i: (0, 0))
    tab = pl.BlockSpec((tm, LANES), lambda i: (i % spt, 0))
    tr = pl.BlockSpec((None, sub, LANES, LANES), lambda i: (i // spt, i % spt, 0, 0))
    flat = lambda w, dt: (tok(w), jax.ShapeDtypeStruct((n, w), dt))
    trans = lambda dt: (tr, jax.ShapeDtypeStruct((batch, seq // LANES, LANES, LANES), dt))
    outs = [flat(Q_HEADS * LANES, BF16), flat(KV_HEADS * LANES, BF16), trans(BF16),
            flat(Q_HEADS * LANES, BF16), flat(Q_HEADS * LANES, BF16),
            flat(KV_DIM, F32), flat(KV_DIM, F32),
            flat(KV_HEADS * LANES, BF16), trans(BF16),
            flat(KV_HEADS * LANES, BF16), trans(BF16),
            trans(F32)]
    return pl.pallas_call(
        _inproj_kernel,
        grid=(n // tm,),
        in_specs=[tok(d), full(w_main), full(w_gn), tab, tab, tab],
        out_specs=[o[0] for o in outs],
        out_shape=[o[1] for o in outs],
        compiler_params=pltpu.CompilerParams(dimension_semantics=("parallel",),
                                             vmem_limit_bytes=48 * 1024 * 1024),
        name="in_proj",
    )(x2, w_main, w_gn, cos, sin, ke)


def _compress_one(src_ref, pe_ref, w1_ref, b1_ref, w2_ref):
    ratio = CMP_BLOCK // CMP_STRIDE
    nchunk = src_ref.shape[0] // CMP_STRIDE
    hid = w1_ref.shape[1]
    parts = [jnp.zeros((KV_HEADS * nchunk, hid), F32) for _ in range(ratio)]
    for l in range(CMP_STRIDE):
        rows = src_ref[pl.ds(l, nchunk, stride=CMP_STRIDE), :]
        rows = jnp.concatenate([rows[:, g * HEAD_DIM:(g + 1) * HEAD_DIM] for g in range(KV_HEADS)], axis=0)
        for j in range(ratio):
            p = j * CMP_STRIDE + l
            a = (rows + pe_ref[p:p + 1, :]).astype(BF16)
            parts[j] = parts[j] + _dot(a, w1_ref[p * HEAD_DIM:(p + 1) * HEAD_DIM, :])
    out = []
    for g in range(KV_HEADS):
        h = parts[0][g * nchunk:(g + 1) * nchunk]
        for j in range(1, ratio):
            h = h + pltpu.roll(parts[j][g * nchunk:(g + 1) * nchunk], nchunk - j, 0)
        h = jax.nn.gelu(h + b1_ref[...])
        out.append(_dot(h.astype(BF16), w2_ref[...]))
    return out


def _compress_kernel(kc_ref, vc_ref, pek_ref, w1k_ref, b1k_ref, w2k_ref,
                     pev_ref, w1v_ref, b1v_ref, w2v_ref, kcmp_ref, vcmpt_ref):
    for g, kc in enumerate(_compress_one(kc_ref, pek_ref, w1k_ref, b1k_ref, w2k_ref)):
        kcmp_ref[g] = kc.astype(BF16)
    for g, vc in enumerate(_compress_one(vc_ref, pev_ref, w1v_ref, b1v_ref, w2v_ref)):
        vcmpt_ref[g] = vc.T[:HEAD_DIM, :].astype(BF16)


def _compress(kc, vc, wk, wv, batch, seq):
    nchunk = seq // CMP_STRIDE
    src = pl.BlockSpec((seq, KV_DIM), lambda b: (b, 0))
    full = lambda a: pl.BlockSpec(a.shape, lambda b: (0,) * a.ndim)
    return pl.pallas_call(
        _compress_kernel,
        grid=(batch,),
        in_specs=[src, src] + [full(a) for a in wk] + [full(a) for a in wv],
        out_specs=[pl.BlockSpec((None, KV_HEADS, nchunk, LANES), lambda b: (b, 0, 0, 0)),
                   pl.BlockSpec((None, KV_HEADS, HEAD_DIM, nchunk), lambda b: (b, 0, 0, 0))],
        out_shape=[jax.ShapeDtypeStruct((batch, KV_HEADS, nchunk, LANES), BF16),
                   jax.ShapeDtypeStruct((batch, KV_HEADS, HEAD_DIM, nchunk), BF16)],
        compiler_params=pltpu.CompilerParams(dimension_semantics=("parallel",)),
        name="compress",
    )(kc, vc, *wk, *wv)


MASK_COL = HEAD_DIM


def _select_blocks_t(imp, t0):
    nb, cols = imp.shape
    j = lax.broadcasted_iota(jnp.int32, (nb, cols), 0)
    cur = (t0 + lax.broadcasted_iota(jnp.int32, (nb, cols), 1)) >> int(np.log2(SLC_BLOCK))
    forced = (j == 0) | (j == cur) | (j == cur - 1)
    score = jnp.where(forced, BIG, jnp.where(j <= cur, imp, -BIG))
    rank = jnp.zeros((nb, cols), F32)
    for jp in range(nb):
        row = score[jp:jp + 1, :]
        beats = (row > score) | ((row == score) & (j > jp))
        rank = rank + jnp.where(beats, 1.0, 0.0)
    return jnp.where((rank < float(min(SLC_TOP_N, nb))) & (j <= cur), 1.0, 0.0)


def _put_heads(o_ref, g, o):
    for i in range(REP // 2):
        pair = jnp.concatenate([o[:, (2 * i) * QBLK:(2 * i + 1) * QBLK],
                                o[:, (2 * i + 1) * QBLK:(2 * i + 2) * QBLK]], axis=0)
        col = (g * REP + 2 * i) * HEAD_DIM
        o_ref[:, col:col + LANES] = pair.T.astype(o_ref.dtype)


class _Pipeline:
    def __init__(self, k_ref, vt_ref, q_ref, s_ref, p_ref, acc_ref, chunk_of):
        self.k_ref, self.vt_ref, self.q_ref = k_ref, vt_ref, q_ref
        self.s_ref, self.p_ref, self.acc_ref = s_ref, p_ref, acc_ref
        self.chunk_of = chunk_of
        self.groups = range(KV_HEADS)

    def qk(self, g, j):
        start = pl.multiple_of(self.chunk_of(j) * QBLK, QBLK)
        return _dot_nt(self.k_ref[pl.ds(start, QBLK), g * LANES:(g + 1) * LANES], self.q_ref[g])

    def keep(self, g, st, bias):
        if bias is not None:
            st = st + bias
        self.s_ref[g] = st
        return jnp.max(st, axis=0, keepdims=True)

    def softmax(self, g, m, l, cmax):
        m_new = jnp.maximum(m, cmax)
        a = jnp.exp(m - m_new)
        sums = []
        for r in range(REP):
            sl = slice(r * QBLK, (r + 1) * QBLK)
            pe = jnp.exp(self.s_ref[g, :, sl] - m_new[:, sl])
            sums.append(jnp.sum(pe, axis=0, keepdims=True))
            self.p_ref[g, :, sl] = pe.astype(BF16)
        return m_new, a * l + jnp.concatenate(sums, axis=1), a

    def values(self, g, j, a):
        vt = self.vt_ref[self.chunk_of(j), g * HEAD_DIM:(g + 1) * HEAD_DIM, :]
        self.acc_ref[g] = a * self.acc_ref[g] + _dot(vt, self.p_ref[g])

    def start(self, st0, st1, bias0, bias1):
        width = self.s_ref.shape[2]
        m0 = jnp.full((1, width), NEG, F32)
        l0 = jnp.zeros((1, width), F32)
        cmax = [self.keep(g, st0[g], bias0) for g in self.groups]
        sm = [self.softmax(g, m0, l0, cmax[g]) for g in self.groups]
        cmax = tuple(self.keep(g, st1[g], bias1) for g in self.groups)
        for g in self.groups:
            self.acc_ref[g] = jnp.zeros(self.acc_ref.shape[1:], F32)
        return (tuple(s[0] for s in sm), tuple(s[1] for s in sm), tuple(s[2] for s in sm), cmax)

    def step(self, j, state, last=False):
        m, l, a, cmax = state
        st = None if last else [self.qk(g, j + 2) for g in self.groups]
        for g in self.groups:
            self.values(g, j, a[g])
        sm = [self.softmax(g, m[g], l[g], cmax[g]) for g in self.groups]
        if not last:
            cmax = tuple(self.keep(g, st[g], None) for g in self.groups)
        return (tuple(s[0] for s in sm), tuple(s[1] for s in sm), tuple(s[2] for s in sm), cmax)

    def finish(self, count, state):
        _, l, a, _ = state
        for g in self.groups:
            self.values(g, count - 1, a[g])
        return [self.acc_ref[g] * (1.0 / l[g]) for g in self.groups]


def _attn_kernel(sink_ref, abias_ref, cbias_ref, ebias_ref, ovt_ref,
                 qa_ref, ka_ref, vat_ref, qn_ref, qnr_ref, gnt_ref, kcmp_ref, vcmpt_ref,
                 ksl_ref, vslt_ref, kw_ref, vwt_ref,
                 oa_ref, ob_ref,
                 qw_ref, sw_ref, pw_ref, accw_ref, qs_ref, ss_ref, ps_ref, accs_ref):
    n = pl.program_id(1)
    groups = range(KV_HEADS)
    width = REP * QBLK
    nb = ovt_ref.shape[0]
    nc = kcmp_ref.shape[1]
    t0 = pl.multiple_of(n * QBLK, QBLK)
    heads = lambda g: [g * REP + r for r in range(REP)]
    rows = lambda ref, g: jnp.concatenate([ref[:, h * LANES:(h + 1) * LANES] for h in heads(g)], axis=0)
    lane_chunks = lambda a: [a[:, r * QBLK:(r + 1) * QBLK] for r in range(REP)]
    cbias = cbias_ref[...]
    nfull = B_WINDOW // QBLK - 1

    ce = jnp.maximum(n - nfull - 1, 0)
    win = _Pipeline(kw_ref, vwt_ref, qw_ref, sw_ref, pw_ref, accw_ref,
                    lambda j: jnp.where(j == 0, n, jnp.where(j == 1, ce, n + 1 - j)))
    win_count = jnp.minimum(n, nfull) + 2
    slc = _Pipeline(ksl_ref, vslt_ref, qs_ref, ss_ref, ps_ref, accs_ref,
                    lambda j: jnp.where(j == 0, n, jnp.maximum(j - 1, 0)))
    slc_count = jnp.maximum(n, 1) + 1

    for g in groups:
        qw_ref[g] = rows(qnr_ref, g)
    a0 = jnp.maximum(n - 1, 0)
    astart = pl.multiple_of(a0 * QBLK, QBLK)
    s_swa = [_dot_nt(ka_ref[pl.ds(astart, 2 * QBLK), g * LANES:(g + 1) * LANES], rows(qa_ref, g)) for g in groups]
    s_cmp = [_dot_nt(kcmp_ref[g], rows(qn_ref, g)) for g in groups]
    w0 = [win.qk(g, 0) for g in groups]
    w1 = [win.qk(g, 1) for g in groups]

    abias = abias_ref[jnp.minimum(n, 1)]
    e_swa, r_swa = [], []
    for g in groups:
        st = s_swa[g] + abias
        sk = jnp.concatenate([jnp.full((1, QBLK), sink_ref[h], F32) for h in heads(g)], axis=1)
        m = jnp.maximum(jnp.max(st, axis=0, keepdims=True), sk)
        e = jnp.exp(st - m)
        e_swa.append(e.astype(BF16))
        r_swa.append(1.0 / (jnp.sum(e, axis=0, keepdims=True) + jnp.exp(sk - m)))

    c_end = lax.broadcasted_iota(jnp.int32, (nc, width), 0) * CMP_STRIDE + (CMP_BLOCK - 1)
    tq = t0 + (lax.broadcasted_iota(jnp.int32, (nc, width), 1) & (QBLK - 1))
    valid = c_end <= tq
    validf = jnp.where(valid, 1.0, 0.0)
    p_cmp = []
    for g in groups:
        s = jnp.where(valid, s_cmp[g], NEG)
        e = jnp.exp(s - jnp.max(s, axis=0, keepdims=True)) * validf
        p_cmp.append(e * (1.0 / jnp.maximum(jnp.sum(e, axis=0, keepdims=True), 1e-30)))

    win_state = win.start(w0, w1, cbias, ebias_ref[...] + jnp.where(n > nfull, 0.0, NEG))

    for g in groups:
        vt = jnp.concatenate([vat_ref[a0, g * HEAD_DIM:(g + 1) * HEAD_DIM, :],
                              vat_ref[a0 + 1, g * HEAD_DIM:(g + 1) * HEAD_DIM, :]], axis=1)
        _put_heads(oa_ref, g, _dot(vt, e_swa[g]) * r_swa[g])
    o_cmp = [_dot(vcmpt_ref[g], p_cmp[g].astype(BF16)) for g in groups]
    imp = []
    for g in groups:
        pc = lane_chunks(p_cmp[g])
        psum = (pc[0] + pc[1]) + (pc[2] + pc[3])
        hi = psum.astype(BF16)
        lo = (psum - hi.astype(F32)).astype(BF16)
        imp.append(_dot(ovt_ref[...], hi) + _dot(ovt_ref[...], lo))

    win_state = lax.fori_loop(0, win_count - 2, win.step, win_state)

    jb = lax.broadcasted_iota(jnp.int32, (nb, QBLK), 0)
    curb = (t0 + lax.broadcasted_iota(jnp.int32, (nb, QBLK), 1)) >> int(np.log2(SLC_BLOCK))
    sel = lax.cond((t0 + QBLK - 1) // SLC_BLOCK < SLC_TOP_N,
                   lambda: tuple(jnp.where(jb <= curb, 1.0, 0.0) for _ in groups),
                   lambda: tuple(_select_blocks_t(imp[g], t0) for g in groups))

    for g in groups:
        mcols = jnp.concatenate([jnp.zeros((MASK_COL, QBLK), F32), sel[g] - 1.0,
                                 jnp.zeros((LANES - MASK_COL - nb, QBLK), F32)], axis=0)
        mpad = mcols.T.astype(BF16)
        for r, h in enumerate(heads(g)):
            qs_ref[g, r * QBLK:(r + 1) * QBLK, :] = qnr_ref[:, h * LANES:(h + 1) * LANES] + mpad
    s0 = [slc.qk(g, 0) for g in groups]
    s1 = [slc.qk(g, 1) for g in groups]
    win_state = win.step(win_count - 2, win_state, last=True)
    slc_state = slc.start(s0, s1, cbias, jnp.where(n > 0, 0.0, NEG))
    o_win = win.finish(win_count, win_state)

    trips = slc_count - 2

    def two_steps(i, state):
        return slc.step(2 * i + 1, slc.step(2 * i, state))

    slc_state = lax.fori_loop(0, trips // 2, two_steps, slc_state)
    slc_state = lax.cond(trips % 2 == 1, lambda s: slc.step(trips - 1, s), lambda s: s, slc_state)

    o_slc = slc.finish(slc_count, slc.step(slc_count - 2, slc_state, last=True))
    for g in groups:
        gate = lambda br: jnp.concatenate(
            [gnt_ref[br * Q_HEADS + h:br * Q_HEADS + h + 1, :] for h in heads(g)], axis=1)
        _put_heads(ob_ref, g, gate(0) * o_cmp[g] + gate(1) * o_slc[g] + gate(2) * o_win[g])


def _attn(sinks, qa, ka, vat, qn, qnr, gnt, kcmp, vcmpt, ksl, vslt, kw, vwt, batch, seq):
    nq = seq // QBLK
    nc = kcmp.shape[2]
    nb = seq // SLC_BLOCK
    width = REP * QBLK
    assert nb <= LANES - MASK_COL and QBLK % SLC_BLOCK == 0 and B_WINDOW % QBLK == 0 and A_WINDOW == QBLK
    c_start = np.arange(nc)[None, :] * CMP_STRIDE
    s_start = np.arange(nb)[:, None] * SLC_BLOCK
    ovt = jnp.asarray((c_start < s_start + SLC_BLOCK) & (s_start < c_start + CMP_BLOCK), BF16)
    kq = np.arange(QBLK)[:, None] - (np.arange(width)[None, :] % QBLK)
    after = np.where(kq > 0, NEG, 0.0)
    behind = np.where(kq <= 0, NEG, 0.0)
    cbias = jnp.asarray(after, F32)
    ebias = jnp.asarray(behind, F32)
    abias = jnp.asarray(np.stack([np.concatenate([after, np.full_like(after, NEG)]),
                                  np.concatenate([behind, after])]), F32)
    q_spec = pl.BlockSpec((QBLK, Q_HEADS * LANES), lambda b, n: (b * nq + n, 0))
    g_spec = pl.BlockSpec((None, None, LANES, LANES), lambda b, n: (b, n, 0, 0))
    kc_spec = pl.BlockSpec((None, KV_HEADS, nc, LANES), lambda b, n: (b, 0, 0, 0))
    vc_spec = pl.BlockSpec((None, KV_HEADS, HEAD_DIM, nc), lambda b, n: (b, 0, 0, 0))
    const = lambda a: pl.BlockSpec(a.shape, lambda b, n: (0,) * a.ndim)
    k_spec = pl.BlockSpec((seq, KV_HEADS * LANES), lambda b, n: (b, 0))
    vt_spec = pl.BlockSpec((None, nq, LANES, LANES), lambda b, n: (b, 0, 0, 0))
    o_spec = pl.BlockSpec((QBLK, Q_DIM), lambda b, n: (b * nq + n, 0))
    o_shape = jax.ShapeDtypeStruct((batch * seq, Q_DIM), BF16)
    pipe_scratch = [pltpu.VMEM((KV_HEADS, width, LANES), BF16),
                    pltpu.VMEM((KV_HEADS, QBLK, width), F32),
                    pltpu.VMEM((KV_HEADS, QBLK, width), BF16),
                    pltpu.VMEM((KV_HEADS, HEAD_DIM, width), F32)]
    return pl.pallas_call(
        _attn_kernel,
        grid=(batch, nq),
        in_specs=[pl.BlockSpec(memory_space=pltpu.SMEM), const(abias), const(cbias), const(ebias), const(ovt),
                  q_spec, k_spec, vt_spec, q_spec, q_spec, g_spec, kc_spec, vc_spec,
                  k_spec, vt_spec, k_spec, vt_spec],
        out_specs=[o_spec, o_spec],
        out_shape=[o_shape, o_shape],
        scratch_shapes=pipe_scratch + pipe_scratch,
        compiler_params=pltpu.CompilerParams(dimension_semantics=("parallel", "arbitrary")),
        name="attn",
    )(sinks, abias, cbias, ebias, ovt, qa, ka, vat, qn, qnr, gnt, kcmp, vcmpt, ksl, vslt, kw, vwt)


def _layer_norm(r, g, b):
    mu = jnp.mean(r, axis=-1, keepdims=True)
    d = r - mu
    var = jnp.mean(d * d, axis=-1, keepdims=True)
    return d * lax.rsqrt(var + LN_EPS) * g + b


def _post_kernel(alpha, x_ref, oa_ref, ob_ref, wgm_ref, wpa_ref, wpb_ref, wout_ref, g_ref, b_ref, h_ref):
    d = x_ref.shape[1]
    x = x_ref[...]
    xb = x.astype(BF16)
    pa = _dot(oa_ref[...], wpa_ref[...])
    pb = _dot(ob_ref[...], wpb_ref[...])
    y = (jax.nn.sigmoid(_dot(xb, wgm_ref[:, :d])) * pa + jax.nn.sigmoid(_dot(xb, wgm_ref[:, d:])) * pb)
    m = _dot(y.astype(BF16), wout_ref[...])
    h_ref[...] = _layer_norm(alpha * x + m, g_ref[...], b_ref[...])


def _post(x2, oa, ob, w_gm, wpa, wpb, wout, g, b, alpha, tm):
    n, d = x2.shape
    tok = lambda width: pl.BlockSpec((tm, width), lambda i: (i, 0))
    full = lambda a: pl.BlockSpec(a.shape, lambda i: (0, 0))
    return pl.pallas_call(
        functools.partial(_post_kernel, alpha),
        grid=(n // tm,),
        in_specs=[tok(d), tok(Q_DIM), tok(Q_DIM)] + [full(a) for a in (w_gm, wpa, wpb, wout, g, b)],
        out_specs=tok(d),
        out_shape=jax.ShapeDtypeStruct((n, d), F32),
        compiler_params=pltpu.CompilerParams(dimension_semantics=("parallel",),
                                             vmem_limit_bytes=48 * 1024 * 1024),
        name="post",
    )(x2, oa, ob, w_gm, wpa, wpb, wout, g, b)


FFN_CHUNK = 256


def _ffn_kernel(alpha, h_ref, wg_ref, wu_ref, wd_ref, g_ref, b_ref, o_ref):
    h = h_ref[...]
    hb = h.astype(BF16)
    hidden = wg_ref.shape[1]
    acc = jnp.zeros(h.shape, F32)
    for c in range(hidden // FFN_CHUNK):
        sl = slice(c * FFN_CHUNK, (c + 1) * FFN_CHUNK)
        a = jax.nn.silu(_dot(hb, wg_ref[:, sl])) * _dot(hb, wu_ref[:, sl])
        acc = acc + _dot(a.astype(BF16), wd_ref[sl, :])
    o_ref[...] = _layer_norm(alpha * h + acc, g_ref[...], b_ref[...])


def _ffn(h, wg, wu, wd, g, b, alpha, tm):
    n, d = h.shape
    tok = pl.BlockSpec((tm, d), lambda i: (i, 0))
    once = lambda a: pl.BlockSpec(a.shape, lambda i: (0, 0), pipeline_mode=pl.Buffered(1))
    return pl.pallas_call(
        functools.partial(_ffn_kernel, alpha),
        grid=(n // tm,),
        in_specs=[tok] + [once(a) for a in (wg, wu, wd, g, b)],
        out_specs=tok,
        out_shape=jax.ShapeDtypeStruct((n, d), F32),
        compiler_params=pltpu.CompilerParams(dimension_semantics=("parallel",),
                                             vmem_limit_bytes=56 * 1024 * 1024),
        name="ffn",
    )(h, wg, wu, wd, g, b)


def _position_tables(seq):
    half = HEAD_DIM // 2
    inv = ROPE_THETA ** (-jnp.arange(half, dtype=F32) / half)
    ang = jnp.arange(seq).astype(F32)[:, None] * inv[None, :]
    cos, sin = jnp.cos(ang), jnp.sin(ang)
    reps = LANES // HEAD_DIM
    lane = np.arange(LANES)[None, :]
    blk = (np.arange(seq) // SLC_BLOCK)[:, None]
    ke = jnp.asarray(np.where(lane - MASK_COL == blk, -NEG, 0.0), F32)
    return jnp.tile(cos, (1, 2 * reps)), jnp.tile(jnp.concatenate([-sin, sin], axis=1), (1, reps)), ke


def kernel(x, w_in, sinks, cmp_pe_k, cmp_w1_k, cmp_b1_k, cmp_w2_k, cmp_pe_v, cmp_w1_v, cmp_b1_v, cmp_w2_v,
           w_proj_a, w_proj_b, w_out, ln1_g, ln1_b, w_gate, w_up, w_down, ln2_g, ln2_b):
    batch, seq, d = x.shape
    depth = w_in.shape[0]
    alpha = (2 * depth) ** 0.25
    n_main = 2 * Q_DIM + 8 * KV_DIM
    n_gate = 3 * Q_HEADS
    tm = 512
    cos, sin, ke = _position_tables(seq)
    pad_cols = lambda w: jnp.pad(w, ((0, 0), (0, LANES - w.shape[1]))).astype(BF16)
    xt = x.reshape(batch * seq, d)
    for l in range(depth):
        w_main = w_in[l, :, :n_main].astype(BF16)
        w_gn = pad_cols(w_in[l, :, n_main:n_main + n_gate])
        w_gm = w_in[l, :, n_main + n_gate:].astype(BF16)
        (qa, ka, vat, qn, qnr, kc, vc, ksl, vslt, kw, vwt, gnt) = _in_proj(
            xt, w_main, w_gn, cos, sin, ke, batch, seq, tm)
        wk = (cmp_pe_k[l], cmp_w1_k[l].astype(BF16), cmp_b1_k[l][None, :], pad_cols(cmp_w2_k[l]))
        wv = (cmp_pe_v[l], cmp_w1_v[l].astype(BF16), cmp_b1_v[l][None, :], pad_cols(cmp_w2_v[l]))
        kcmp, vcmpt = _compress(kc, vc, wk, wv, batch, seq)
        oa, ob = _attn(sinks[l], qa, ka, vat, qn, qnr, gnt, kcmp, vcmpt, ksl, vslt, kw, vwt, batch, seq)
        h = _post(xt, oa, ob, w_gm, w_proj_a[l].astype(BF16), w_proj_b[l].astype(BF16), w_out[l].astype(BF16),
                  ln1_g[l][None, :], ln1_b[l][None, :], alpha, tm)
        xt = _ffn(h, w_gate[l].astype(BF16), w_up[l].astype(BF16), w_down[l].astype(BF16),
                  ln2_g[l][None, :], ln2_b[l][None, :], alpha, tm)
    return xt.reshape(batch, seq, d)
```

```python
import functools

import jax
import jax.numpy as jnp
import numpy as np
from jax import lax
from jax.experimental import pallas as pl
from jax.experimental.pallas import tpu as pltpu

HEAD_DIM = 64
ROPE_THETA = 10000.0
Q_HEADS = 8
KV_HEADS = 2
REP = Q_HEADS // KV_HEADS
A_WINDOW = 128
B_WINDOW = 512
CMP_BLOCK = 32
CMP_STRIDE = 16
SLC_BLOCK = 64
SLC_TOP_N = 16
LN_EPS = 1e-5
NEG = -1e30
BIG = 1e9
Q_DIM = Q_HEADS * HEAD_DIM
KV_DIM = KV_HEADS * HEAD_DIM
LANES = 128
QBLK = 128
KSTEP = 2 * QBLK
LOG2E = 1.4426950408889634
SCALE = HEAD_DIM ** -0.5 * LOG2E
VROWS = HEAD_DIM + 16

F32 = jnp.float32
BF16 = jnp.bfloat16


def _dot(a, b):
    return jnp.dot(a, b, preferred_element_type=F32)


def _dot_nt(a, b):
    return lax.dot_general(a, b, (((1,), (1,)), ((), ())), preferred_element_type=F32)


def _inproj_kernel(x_ref, w_ref, wg_ref, cos_ref, sin_ref, ke_ref,
                   qa_ref, ka_ref, vat_ref, qn_ref, qnr_ref, kc_ref, vc_ref,
                   ksl_ref, vslt_ref, kw_ref, vwt_ref, gnt_ref):
    tm = x_ref.shape[0]
    xb = x_ref[...].astype(BF16)
    cos = cos_ref[...]
    sin = sin_ref[...]
    lane = lax.broadcasted_iota(jnp.int32, (tm, LANES), 1)
    first_half = (lane & (HEAD_DIM - 1)) < (HEAD_DIM // 2)
    low = lane < HEAD_DIM

    def rope(z):
        sw = jnp.where(first_half, pltpu.roll(z, LANES - HEAD_DIM // 2, 1), pltpu.roll(z, HEAD_DIM // 2, 1))
        return z * cos + sw * sin

    def proj(c0, n):
        return _dot(xb, w_ref[:, c0:c0 + n])

    def chunks(z):
        return [z[:, c * LANES:(c + 1) * LANES] for c in range(z.shape[1] // LANES)]

    def put_padded(ref, c, z, fill):
        ref[:, (2 * c) * LANES:(2 * c + 1) * LANES] = jnp.where(low, z, fill).astype(ref.dtype)
        ref[:, (2 * c + 1) * LANES:(2 * c + 2) * LANES] = jnp.where(low, pltpu.roll(z, HEAD_DIM, 1), fill).astype(ref.dtype)

    def put_transposed(ref, z):
        for j in range(tm // LANES):
            ref[j] = z[j * LANES:(j + 1) * LANES, :].T.astype(ref.dtype)

    ones_row = jnp.where(lax.broadcasted_iota(jnp.int32, (VROWS - HEAD_DIM, LANES), 0) == 0, 1.0, 0.0)

    def put_values_t(ref, z):
        keys = ref.shape[2]
        for j in range(tm // keys):
            for i in range(keys // LANES):
                zt = z[j * keys + i * LANES:j * keys + (i + 1) * LANES, :].T
                cols = slice(i * LANES, (i + 1) * LANES)
                for g in range(KV_HEADS):
                    ref[j, g * VROWS:g * VROWS + HEAD_DIM, cols] = zt[g * HEAD_DIM:(g + 1) * HEAD_DIM, :].astype(ref.dtype)
                    ref[j, g * VROWS + HEAD_DIM:(g + 1) * VROWS, cols] = ones_row.astype(ref.dtype)

    for c, zc in enumerate(chunks(proj(0, Q_DIM))):
        put_padded(qa_ref, c, rope(zc) * SCALE, 0.0)
    z = proj(Q_DIM, 2 * KV_DIM)
    put_padded(ka_ref, 0, rope(z[:, :KV_DIM]), 0.0)
    put_values_t(vat_ref, z[:, KV_DIM:])
    c0 = Q_DIM + 2 * KV_DIM
    for c, zc in enumerate(chunks(proj(c0, Q_DIM))):
        put_padded(qn_ref, c, zc * SCALE, 0.0)
        put_padded(qnr_ref, c, rope(zc) * SCALE, 0.0)
    c0 += Q_DIM
    z = proj(c0, 2 * KV_DIM)
    kc_ref[...] = z[:, :KV_DIM]
    vc_ref[...] = z[:, KV_DIM:]
    c0 += 2 * KV_DIM
    z = proj(c0, 2 * KV_DIM)
    put_padded(ksl_ref, 0, rope(z[:, :KV_DIM]), ke_ref[...])
    put_values_t(vslt_ref, z[:, KV_DIM:])
    c0 += 2 * KV_DIM
    z = proj(c0, 2 * KV_DIM)
    put_padded(kw_ref, 0, rope(z[:, :KV_DIM]), 0.0)
    put_values_t(vwt_ref, z[:, KV_DIM:])
    put_transposed(gnt_ref, jax.nn.sigmoid(_dot(xb, wg_ref[...])))


def _in_proj(x2, w_main, w_gn, cos, sin, ke, batch, seq, tm):
    n = x2.shape[0]
    d = x2.shape[1]
    spt = seq // tm
    tok = lambda width: pl.BlockSpec((tm, width), lambda i: (i, 0))
    full = lambda a: pl.BlockSpec(a.shape, lambda i: (0, 0))
    tab = pl.BlockSpec((tm, LANES), lambda i: (i % spt, 0))
    flat = lambda w, dt: (tok(w), jax.ShapeDtypeStruct((n, w), dt))
    trans = lambda r, keys, dt: (pl.BlockSpec((None, tm // keys, r, keys), lambda i: (i // spt, i % spt, 0, 0)),
                                 jax.ShapeDtypeStruct((batch, seq // keys, r, keys), dt))
    outs = [flat(Q_HEADS * LANES, BF16), flat(KV_HEADS * LANES, BF16),
            trans(KV_HEADS * VROWS, QBLK, BF16),
            flat(Q_HEADS * LANES, BF16), flat(Q_HEADS * LANES, BF16),
            flat(KV_DIM, F32), flat(KV_DIM, F32),
            flat(KV_HEADS * LANES, BF16), trans(KV_HEADS * VROWS, KSTEP, BF16),
            flat(KV_HEADS * LANES, BF16), trans(KV_HEADS * VROWS, KSTEP, BF16),
            trans(LANES, LANES, F32)]
    return pl.pallas_call(
        _inproj_kernel,
        grid=(n // tm,),
        in_specs=[tok(d), full(w_main), full(w_gn), tab, tab, tab],
        out_specs=[o[0] for o in outs],
        out_shape=[o[1] for o in outs],
        compiler_params=pltpu.CompilerParams(dimension_semantics=("parallel",),
                                             vmem_limit_bytes=48 * 1024 * 1024),
        name="in_proj",
    )(x2, w_main, w_gn, cos, sin, ke)


def _compress_one(src_ref, pe_ref, w1_ref, b1_ref, w2_ref):
    ratio = CMP_BLOCK // CMP_STRIDE
    nchunk = src_ref.shape[0] // CMP_STRIDE
    hid = w1_ref.shape[1]
    parts = [jnp.zeros((KV_HEADS * nchunk, hid), F32) for _ in range(ratio)]
    for l in range(CMP_STRIDE):
        rows = src_ref[pl.ds(l, nchunk, stride=CMP_STRIDE), :]
        rows = jnp.concatenate([rows[:, g * HEAD_DIM:(g + 1) * HEAD_DIM] for g in range(KV_HEADS)], axis=0)
        for j in range(ratio):
            p = j * CMP_STRIDE + l
            a = (rows + pe_ref[p:p + 1, :]).astype(BF16)
            parts[j] = parts[j] + _dot(a, w1_ref[p * HEAD_DIM:(p + 1) * HEAD_DIM, :])
    out = []
    for g in range(KV_HEADS):
        h = parts[0][g * nchunk:(g + 1) * nchunk]
        for j in range(1, ratio):
            h = h + pltpu.roll(parts[j][g * nchunk:(g + 1) * nchunk], nchunk - j, 0)
        h = jax.nn.gelu(h + b1_ref[...])
        out.append(_dot(h.astype(BF16), w2_ref[...]))
    return out


def _compress_kernel(kc_ref, vc_ref, pek_ref, w1k_ref, b1k_ref, w2k_ref,
                     pev_ref, w1v_ref, b1v_ref, w2v_ref, kcmp_ref, vcmpt_ref):
    for g, kc in enumerate(_compress_one(kc_ref, pek_ref, w1k_ref, b1k_ref, w2k_ref)):
        kcmp_ref[g] = kc.astype(BF16)
    for g, vc in enumerate(_compress_one(vc_ref, pev_ref, w1v_ref, b1v_ref, w2v_ref)):
        vcmpt_ref[g] = vc.T[:HEAD_DIM, :].astype(BF16)


def _compress(kc, vc, wk, wv, batch, seq):
    nchunk = seq // CMP_STRIDE
    src = pl.BlockSpec((seq, KV_DIM), lambda b: (b, 0))
    full = lambda a: pl.BlockSpec(a.shape, lambda b: (0,) * a.ndim)
    return pl.pallas_call(
        _compress_kernel,
        grid=(batch,),
        in_specs=[src, src] + [full(a) for a in wk] + [full(a) for a in wv],
        out_specs=[pl.BlockSpec((None, KV_HEADS, nchunk, LANES), lambda b: (b, 0, 0, 0)),
                   pl.BlockSpec((None, KV_HEADS, HEAD_DIM, nchunk), lambda b: (b, 0, 0, 0))],
        out_shape=[jax.ShapeDtypeStruct((batch, KV_HEADS, nchunk, LANES), BF16),
                   jax.ShapeDtypeStruct((batch, KV_HEADS, HEAD_DIM, nchunk), BF16)],
        compiler_params=pltpu.CompilerParams(dimension_semantics=("parallel",)),
        name="compress",
    )(kc, vc, *wk, *wv)


MASK_COL = HEAD_DIM


def _select_blocks_t(imp, t0):
    nb, cols = imp.shape
    j = lax.broadcasted_iota(jnp.int32, (nb, cols), 0)
    cur = (t0 + lax.broadcasted_iota(jnp.int32, (nb, cols), 1)) >> int(np.log2(SLC_BLOCK))
    forced = (j == 0) | (j == cur) | (j == cur - 1)
    score = jnp.where(forced, BIG, jnp.where(j <= cur, imp, -BIG))
    rank = jnp.zeros((nb, cols), F32)
    for jp in range(nb):
        row = score[jp:jp + 1, :]
        beats = (row > score) | ((row == score) & (j > jp))
        rank = rank + jnp.where(beats, 1.0, 0.0)
    return jnp.where((rank < float(min(SLC_TOP_N, nb))) & (j <= cur), 1.0, 0.0)


def _put_heads(o_ref, g, o):
    for i in range(REP // 2):
        pair = jnp.concatenate([o[:, (2 * i) * QBLK:(2 * i + 1) * QBLK],
                                o[:, (2 * i + 1) * QBLK:(2 * i + 2) * QBLK]], axis=0)
        col = (g * REP + 2 * i) * HEAD_DIM
        o_ref[:, col:col + LANES] = pair.T.astype(o_ref.dtype)


class _Pipeline:
    def __init__(self, k_ref, vt_ref, q_ref, s_ref, p_ref, acc_ref, chunk_of):
        self.k_ref, self.vt_ref, self.q_ref = k_ref, vt_ref, q_ref
        self.s_ref, self.p_ref, self.acc_ref = s_ref, p_ref, acc_ref
        self.chunk_of = chunk_of
        self.groups = range(KV_HEADS)

    def qk(self, g, j):
        start = pl.multiple_of(self.chunk_of(j) * KSTEP, KSTEP)
        return _dot_nt(self.k_ref[pl.ds(start, KSTEP), g * LANES:(g + 1) * LANES], self.q_ref[g])

    def keep(self, g, st, bias):
        if bias is not None:
            st = st + bias
        self.s_ref[g] = st
        return jnp.max(st, axis=0, keepdims=True)

    def softmax(self, g, m, cmax):
        m_new = jnp.maximum(m, cmax)
        a = jnp.exp2(m - m_new)
        self.p_ref[g] = jnp.exp2(self.s_ref[g] - m_new).astype(BF16)
        return m_new, a

    def values(self, g, j, a):
        vt = self.vt_ref[self.chunk_of(j), g * VROWS:(g + 1) * VROWS, :]
        self.acc_ref[g] = a * self.acc_ref[g] + _dot(vt, self.p_ref[g])

    def start(self, st0, st1, bias0, bias1):
        m0 = jnp.full((1, self.s_ref.shape[2]), NEG, F32)
        cmax = [self.keep(g, st0[g], bias0) for g in self.groups]
        sm = [self.softmax(g, m0, cmax[g]) for g in self.groups]
        cmax = tuple(self.keep(g, st1[g], bias1) for g in self.groups)
        for g in self.groups:
            self.acc_ref[g] = jnp.zeros(self.acc_ref.shape[1:], F32)
        return (tuple(s[0] for s in sm), tuple(s[1] for s in sm), cmax)

    def step(self, j, state, last=False, bias=None):
        m, a, cmax = state
        st = None if last else [self.qk(g, j + 2) for g in self.groups]
        for g in self.groups:
            self.values(g, j, a[g])
        sm = [self.softmax(g, m[g], cmax[g]) for g in self.groups]
        if not last:
            cmax = tuple(self.keep(g, st[g], bias) for g in self.groups)
        return (tuple(s[0] for s in sm), tuple(s[1] for s in sm), cmax)

    def finish(self, count, state):
        _, a, _ = state
        out = []
        for g in self.groups:
            self.values(g, count - 1, a[g])
            acc = self.acc_ref[g]
            out.append(acc[:HEAD_DIM] * (1.0 / acc[HEAD_DIM:HEAD_DIM + 1]))
        return out


def _attn_kernel(sink_ref, abias_ref, pbias_ref, ovt_ref,
                 qa_ref, ka_ref, vat_ref, qn_ref, qnr_ref, gnt_ref, kcmp_ref, vcmpt_ref,
                 ksl_ref, vslt_ref, kw_ref, vwt_ref,
                 oa_ref, ob_ref,
                 qw_ref, sw_ref, pw_ref, accw_ref, qs_ref, ss_ref, ps_ref, accs_ref):
    n = pl.program_id(1)
    groups = range(KV_HEADS)
    width = REP * QBLK
    nb = ovt_ref.shape[0]
    nc = kcmp_ref.shape[1]
    t0 = pl.multiple_of(n * QBLK, QBLK)
    heads = lambda g: [g * REP + r for r in range(REP)]
    rows = lambda ref, g: jnp.concatenate([ref[:, h * LANES:(h + 1) * LANES] for h in heads(g)], axis=0)
    lane_chunks = lambda a: [a[:, r * QBLK:(r + 1) * QBLK] for r in range(REP)]
    own = n // 2
    par = n % 2
    gone = lambda cond: jnp.where(cond, 0.0, NEG)

    assert B_WINDOW == 2 * KSTEP
    win = _Pipeline(kw_ref, vwt_ref, qw_ref, sw_ref, pw_ref, accw_ref, lambda j: jnp.maximum(own - j, 0))
    slc = _Pipeline(ksl_ref, vslt_ref, qs_ref, ss_ref, ps_ref, accs_ref,
                    lambda j: jnp.where(j == 0, own, jnp.maximum(j - 1, 0)))
    slc_count = jnp.maximum(own, 1) + 1

    for g in groups:
        qw_ref[g] = rows(qnr_ref, g)
    a0 = jnp.maximum(n - 1, 0)
    astart = pl.multiple_of(a0 * QBLK, QBLK)
    s_swa = [_dot_nt(ka_ref[pl.ds(astart, 2 * QBLK), g * LANES:(g + 1) * LANES], rows(qa_ref, g)) for g in groups]
    s_cmp = [_dot_nt(kcmp_ref[g], rows(qn_ref, g)) for g in groups]
    w0 = [win.qk(g, 0) for g in groups]
    w1 = [win.qk(g, 1) for g in groups]

    abias = abias_ref[jnp.minimum(n, 1)]
    e_swa, x_swa = [], []
    for g in groups:
        st = s_swa[g] + abias
        sk = jnp.concatenate([jnp.full((1, QBLK), sink_ref[h] * LOG2E, F32) for h in heads(g)], axis=1)
        m = jnp.maximum(jnp.max(st, axis=0, keepdims=True), sk)
        e_swa.append(jnp.exp2(st - m).astype(BF16))
        x_swa.append(jnp.exp2(sk - m))

    c_end = lax.broadcasted_iota(jnp.int32, (nc, width), 0) * CMP_STRIDE + (CMP_BLOCK - 1)
    tq = t0 + (lax.broadcasted_iota(jnp.int32, (nc, width), 1) & (QBLK - 1))
    valid = c_end <= tq
    validf = jnp.where(valid, 1.0, 0.0)
    p_cmp = []
    for g in groups:
        s = jnp.where(valid, s_cmp[g], NEG)
        e = jnp.exp2(s - jnp.max(s, axis=0, keepdims=True)) * validf
        p_cmp.append(e * (1.0 / jnp.maximum(jnp.sum(e, axis=0, keepdims=True), 1e-30)))

    win_state = win.start(w0, w1, pbias_ref[2 - par], gone(own >= 1))
    win_state = win.step(0, win_state, bias=pbias_ref[3 + par] + gone(own >= 2))

    for g in groups:
        vt = jnp.concatenate([vat_ref[a0, g * VROWS:(g + 1) * VROWS, :],
                              vat_ref[a0 + 1, g * VROWS:(g + 1) * VROWS, :]], axis=1)
        o = _dot(vt, e_swa[g])
        _put_heads(oa_ref, g, o[:HEAD_DIM] * (1.0 / (o[HEAD_DIM:HEAD_DIM + 1] + x_swa[g])))
    o_cmp = [_dot(vcmpt_ref[g], p_cmp[g].astype(BF16)) for g in groups]
    imp = []
    for g in groups:
        pc = lane_chunks(p_cmp[g])
        psum = (pc[0] + pc[1]) + (pc[2] + pc[3])
        hi = psum.astype(BF16)
        lo = (psum - hi.astype(F32)).astype(BF16)
        imp.append(_dot(ovt_ref[...], hi) + _dot(ovt_ref[...], lo))
    o_win = win.finish(3, win.step(1, win_state, last=True))

    jb = lax.broadcasted_iota(jnp.int32, (nb, QBLK), 0)
    curb = (t0 + lax.broadcasted_iota(jnp.int32, (nb, QBLK), 1)) >> int(np.log2(SLC_BLOCK))
    sel = lax.cond((t0 + QBLK - 1) // SLC_BLOCK < SLC_TOP_N,
                   lambda: tuple(jnp.where(jb <= curb, 1.0, 0.0) for _ in groups),
                   lambda: tuple(_select_blocks_t(imp[g], t0) for g in groups))

    for g in groups:
        mcols = jnp.concatenate([jnp.zeros((MASK_COL, QBLK), F32), sel[g] - 1.0,
                                 jnp.zeros((LANES - MASK_COL - nb, QBLK), F32)], axis=0)
        mpad = mcols.T.astype(BF16)
        for r, h in enumerate(heads(g)):
            qs_ref[g, r * QBLK:(r + 1) * QBLK, :] = qnr_ref[:, h * LANES:(h + 1) * LANES] + mpad
    s0 = [slc.qk(g, 0) for g in groups]
    s1 = [slc.qk(g, 1) for g in groups]
    slc_state = slc.start(s0, s1, pbias_ref[par], gone(own >= 1))

    slc_state = lax.fori_loop(0, slc_count - 2, slc.step, slc_state)

    o_slc = slc.finish(slc_count, slc.step(slc_count - 2, slc_state, last=True))
    for g in groups:
        gate = lambda br: jnp.concatenate(
            [gnt_ref[br * Q_HEADS + h:br * Q_HEADS + h + 1, :] for h in heads(g)], axis=1)
        _put_heads(ob_ref, g, gate(0) * o_cmp[g] + gate(1) * o_slc[g] + gate(2) * o_win[g])


def _attn(sinks, qa, ka, vat, qn, qnr, gnt, kcmp, vcmpt, ksl, vslt, kw, vwt, batch, seq):
    nq = seq // QBLK
    nc = kcmp.shape[2]
    nb = seq // SLC_BLOCK
    width = REP * QBLK
    assert nb <= LANES - MASK_COL and QBLK % SLC_BLOCK == 0 and B_WINDOW % QBLK == 0 and A_WINDOW == QBLK
    c_start = np.arange(nc)[None, :] * CMP_STRIDE
    s_start = np.arange(nb)[:, None] * SLC_BLOCK
    ovt = jnp.asarray((c_start < s_start + SLC_BLOCK) & (s_start < c_start + CMP_BLOCK), BF16)
    kq = np.arange(QBLK)[:, None] - (np.arange(width)[None, :] % QBLK)
    after = np.where(kq > 0, NEG, 0.0)
    behind = np.where(kq <= 0, NEG, 0.0)
    free, none = np.zeros_like(after), np.full_like(after, NEG)
    pair = lambda first, second: np.concatenate([first, second])
    abias = jnp.asarray(np.stack([pair(after, none), pair(behind, after)]), F32)
    pbias = jnp.asarray(np.stack([pair(after, free), pair(free, after), pair(after, none),
                                  pair(behind, free), pair(none, behind)]), F32)
    q_spec = pl.BlockSpec((QBLK, Q_HEADS * LANES), lambda b, n: (b * nq + n, 0))
    g_spec = pl.BlockSpec((None, None, LANES, LANES), lambda b, n: (b, n, 0, 0))
    kc_spec = pl.BlockSpec((None, KV_HEADS, nc, LANES), lambda b, n: (b, 0, 0, 0))
    vc_spec = pl.BlockSpec((None, KV_HEADS, HEAD_DIM, nc), lambda b, n: (b, 0, 0, 0))
    const = lambda a: pl.BlockSpec(a.shape, lambda b, n: (0,) * a.ndim)
    k_spec = pl.BlockSpec((seq, KV_HEADS * LANES), lambda b, n: (b, 0))
    vt_spec = lambda keys: pl.BlockSpec((None, seq // keys, KV_HEADS * VROWS, keys), lambda b, n: (b, 0, 0, 0))
    o_spec = pl.BlockSpec((QBLK, Q_DIM), lambda b, n: (b * nq + n, 0))
    o_shape = jax.ShapeDtypeStruct((batch * seq, Q_DIM), BF16)
    pipe_scratch = [pltpu.VMEM((KV_HEADS, width, LANES), BF16),
                    pltpu.VMEM((KV_HEADS, KSTEP, width), F32),
                    pltpu.VMEM((KV_HEADS, KSTEP, width), BF16),
                    pltpu.VMEM((KV_HEADS, VROWS, width), F32)]
    return pl.pallas_call(
        _attn_kernel,
        grid=(batch, nq),
        in_specs=[pl.BlockSpec(memory_space=pltpu.SMEM), const(abias), const(pbias), const(ovt),
                  q_spec, k_spec, vt_spec(QBLK), q_spec, q_spec, g_spec, kc_spec, vc_spec,
                  k_spec, vt_spec(KSTEP), k_spec, vt_spec(KSTEP)],
        out_specs=[o_spec, o_spec],
        out_shape=[o_shape, o_shape],
        scratch_shapes=pipe_scratch + pipe_scratch,
        compiler_params=pltpu.CompilerParams(dimension_semantics=("parallel", "arbitrary"),
                                             vmem_limit_bytes=48 * 1024 * 1024),
        name="attn",
    )(sinks, abias, pbias, ovt, qa, ka, vat, qn, qnr, gnt, kcmp, vcmpt, ksl, vslt, kw, vwt)


def _layer_norm(r, g, b):
    mu = jnp.mean(r, axis=-1, keepdims=True)
    d = r - mu
    var = jnp.mean(d * d, axis=-1, keepdims=True)
    return d * lax.rsqrt(var + LN_EPS) * g + b


def _post_kernel(alpha, x_ref, oa_ref, ob_ref, wgm_ref, wpa_ref, wpb_ref, wout_ref, g_ref, b_ref, h_ref):
    d = x_ref.shape[1]
    x = x_ref[...]
    xb = x.astype(BF16)
    pa = _dot(oa_ref[...], wpa_ref[...])
    pb = _dot(ob_ref[...], wpb_ref[...])
    y = (jax.nn.sigmoid(_dot(xb, wgm_ref[:, :d])) * pa + jax.nn.sigmoid(_dot(xb, wgm_ref[:, d:])) * pb)
    m = _dot(y.astype(BF16), wout_ref[...])
    h_ref[...] = _layer_norm(alpha * x + m, g_ref[...], b_ref[...])


def _post(x2, oa, ob, w_gm, wpa, wpb, wout, g, b, alpha, tm):
    n, d = x2.shape
    tok = lambda width: pl.BlockSpec((tm, width), lambda i: (i, 0))
    full = lambda a: pl.BlockSpec(a.shape, lambda i: (0, 0))
    return pl.pallas_call(
        functools.partial(_post_kernel, alpha),
        grid=(n // tm,),
        in_specs=[tok(d), tok(Q_DIM), tok(Q_DIM)] + [full(a) for a in (w_gm, wpa, wpb, wout, g, b)],
        out_specs=tok(d),
        out_shape=jax.ShapeDtypeStruct((n, d), F32),
        compiler_params=pltpu.CompilerParams(dimension_semantics=("parallel",),
                                             vmem_limit_bytes=48 * 1024 * 1024),
        name="post",
    )(x2, oa, ob, w_gm, wpa, wpb, wout, g, b)


FFN_CHUNK = 256


def _ffn_kernel(alpha, h_ref, wg_ref, wu_ref, wd_ref, g_ref, b_ref, o_ref):
    h = h_ref[...]
    hb = h.astype(BF16)
    hidden = wg_ref.shape[1]
    acc = jnp.zeros(h.shape, F32)
    for c in range(hidden // FFN_CHUNK):
        sl = slice(c * FFN_CHUNK, (c + 1) * FFN_CHUNK)
        a = jax.nn.silu(_dot(hb, wg_ref[:, sl])) * _dot(hb, wu_ref[:, sl])
        acc = acc + _dot(a.astype(BF16), wd_ref[sl, :])
    o_ref[...] = _layer_norm(alpha * h + acc, g_ref[...], b_ref[...])


def _ffn(h, wg, wu, wd, g, b, alpha, tm):
    n, d = h.shape
    tok = pl.BlockSpec((tm, d), lambda i: (i, 0))
    once = lambda a: pl.BlockSpec(a.shape, lambda i: (0, 0), pipeline_mode=pl.Buffered(1))
    return pl.pallas_call(
        functools.partial(_ffn_kernel, alpha),
        grid=(n // tm,),
        in_specs=[tok] + [once(a) for a in (wg, wu, wd, g, b)],
        out_specs=tok,
        out_shape=jax.ShapeDtypeStruct((n, d), F32),
        compiler_params=pltpu.CompilerParams(dimension_semantics=("parallel",),
                                             vmem_limit_bytes=56 * 1024 * 1024),
        name="ffn",
    )(h, wg, wu, wd, g, b)


def _position_tables(seq):
    half = HEAD_DIM // 2
    inv = ROPE_THETA ** (-jnp.arange(half, dtype=F32) / half)
    ang = jnp.arange(seq).astype(F32)[:, None] * inv[None, :]
    cos, sin = jnp.cos(ang), jnp.sin(ang)
    reps = LANES // HEAD_DIM
    lane = np.arange(LANES)[None, :]
    blk = (np.arange(seq) // SLC_BLOCK)[:, None]
    ke = jnp.asarray(np.where(lane - MASK_COL == blk, -NEG, 0.0), F32)
    return jnp.tile(cos, (1, 2 * reps)), jnp.tile(jnp.concatenate([-sin, sin], axis=1), (1, reps)), ke


def kernel(x, w_in, sinks, cmp_pe_k, cmp_w1_k, cmp_b1_k, cmp_w2_k, cmp_pe_v, cmp_w1_v, cmp_b1_v, cmp_w2_v,
           w_proj_a, w_proj_b, w_out, ln1_g, ln1_b, w_gate, w_up, w_down, ln2_g, ln2_b):
    batch, seq, d = x.shape
    depth = w_in.shape[0]
    alpha = (2 * depth) ** 0.25
    n_main = 2 * Q_DIM + 8 * KV_DIM
    n_gate = 3 * Q_HEADS
    tm = 512
    cos, sin, ke = _position_tables(seq)
    pad_cols = lambda w: jnp.pad(w, ((0, 0), (0, LANES - w.shape[1]))).astype(BF16)
    xt = x.reshape(batch * seq, d)
    for l in range(depth):
        w_main = w_in[l, :, :n_main].astype(BF16)
        w_gn = pad_cols(w_in[l, :, n_main:n_main + n_gate])
        w_gm = w_in[l, :, n_main + n_gate:].astype(BF16)
        (qa, ka, vat, qn, qnr, kc, vc, ksl, vslt, kw, vwt, gnt) = _in_proj(
            xt, w_main, w_gn, cos, sin, ke, batch, seq, tm)
        wk = (cmp_pe_k[l], cmp_w1_k[l].astype(BF16), cmp_b1_k[l][None, :], pad_cols(cmp_w2_k[l]))
        wv = (cmp_pe_v[l], cmp_w1_v[l].astype(BF16), cmp_b1_v[l][None, :], pad_cols(cmp_w2_v[l]))
        kcmp, vcmpt = _compress(kc, vc, wk, wv, batch, seq)
        oa, ob = _attn(sinks[l], qa, ka, vat, qn, qnr, gnt, kcmp, vcmpt, ksl, vslt, kw, vwt, batch, seq)
        h = _post(xt, oa, ob, w_gm, w_proj_a[l].astype(BF16), w_proj_b[l].astype(BF16), w_out[l].astype(BF16),
                  ln1_g[l][None, :], ln1_b[l][None, :], alpha, tm)
        xt = _ffn(h, w_gate[l].astype(BF16), w_up[l].astype(BF16), w_down[l].astype(BF16),
                  ln2_g[l][None, :], ln2_b[l][None, :], alpha, tm)
    return xt.reshape(batch, seq, d)
```

```python
import functools

import jax
import jax.numpy as jnp
import numpy as np
from jax import lax
from jax.experimental import pallas as pl
from jax.experimental.pallas import tpu as pltpu

HEAD_DIM = 64
ROPE_THETA = 10000.0
Q_HEADS = 8
KV_HEADS = 2
REP = Q_HEADS // KV_HEADS
A_WINDOW = 128
B_WINDOW = 512
CMP_BLOCK = 32
CMP_STRIDE = 16
SLC_BLOCK = 64
SLC_TOP_N = 16
LN_EPS = 1e-5
NEG = -1e30
BIG = 1e9
Q_DIM = Q_HEADS * HEAD_DIM
KV_DIM = KV_HEADS * HEAD_DIM
LANES = 128
QBLK = 128
KSTEP = 2 * QBLK
LOG2E = 1.4426950408889634
SCALE = HEAD_DIM ** -0.5 * LOG2E
VROWS = HEAD_DIM + 16

F32 = jnp.float32
BF16 = jnp.bfloat16


def _dot(a, b):
    return jnp.dot(a, b, preferred_element_type=F32)


def _dot_nt(a, b):
    return lax.dot_general(a, b, (((1,), (1,)), ((), ())), preferred_element_type=F32)


def _inproj_kernel(x_ref, w_ref, wg_ref, cos_ref, sin_ref, ke_ref,
                   qa_ref, ka_ref, vat_ref, qn_ref, qnr_ref, kc_ref, vc_ref,
                   ksl_ref, vslt_ref, kw_ref, vwt_ref, gnt_ref):
    tm = x_ref.shape[0]
    xb = x_ref[...].astype(BF16)
    cos = cos_ref[...]
    sin = sin_ref[...]
    lane = lax.broadcasted_iota(jnp.int32, (tm, LANES), 1)
    first_half = (lane & (HEAD_DIM - 1)) < (HEAD_DIM // 2)
    low = lane < HEAD_DIM

    def rope(z):
        sw = jnp.where(first_half, pltpu.roll(z, LANES - HEAD_DIM // 2, 1), pltpu.roll(z, HEAD_DIM // 2, 1))
        return z * cos + sw * sin

    def proj(c0, n):
        return _dot(xb, w_ref[:, c0:c0 + n])

    def chunks(z):
        return [z[:, c * LANES:(c + 1) * LANES] for c in range(z.shape[1] // LANES)]

    def put_padded(ref, c, z, fill):
        ref[:, (2 * c) * LANES:(2 * c + 1) * LANES] = jnp.where(low, z, fill).astype(ref.dtype)
        ref[:, (2 * c + 1) * LANES:(2 * c + 2) * LANES] = jnp.where(low, pltpu.roll(z, HEAD_DIM, 1), fill).astype(ref.dtype)

    def put_transposed(ref, z):
        for j in range(tm // LANES):
            ref[j] = z[j * LANES:(j + 1) * LANES, :].T.astype(ref.dtype)

    ones_row = jnp.where(lax.broadcasted_iota(jnp.int32, (VROWS - HEAD_DIM, LANES), 0) == 0, 1.0, 0.0)

    def put_values_t(ref, z):
        keys = ref.shape[2]
        for j in range(tm // keys):
            for i in range(keys // LANES):
                zt = z[j * keys + i * LANES:j * keys + (i + 1) * LANES, :].T
                cols = slice(i * LANES, (i + 1) * LANES)
                for g in range(KV_HEADS):
                    ref[j, g * VROWS:g * VROWS + HEAD_DIM, cols] = zt[g * HEAD_DIM:(g + 1) * HEAD_DIM, :].astype(ref.dtype)
                    ref[j, g * VROWS + HEAD_DIM:(g + 1) * VROWS, cols] = ones_row.astype(ref.dtype)

    for c, zc in enumerate(chunks(proj(0, Q_DIM))):
        put_padded(qa_ref, c, rope(zc) * SCALE, 0.0)
    z = proj(Q_DIM, 2 * KV_DIM)
    put_padded(ka_ref, 0, rope(z[:, :KV_DIM]), 0.0)
    put_values_t(vat_ref, z[:, KV_DIM:])
    c0 = Q_DIM + 2 * KV_DIM
    for c, zc in enumerate(chunks(proj(c0, Q_DIM))):
        put_padded(qn_ref, c, zc * SCALE, 0.0)
        put_padded(qnr_ref, c, rope(zc) * SCALE, 0.0)
    c0 += Q_DIM
    z = proj(c0, 2 * KV_DIM)
    kc_ref[...] = z[:, :KV_DIM]
    vc_ref[...] = z[:, KV_DIM:]
    c0 += 2 * KV_DIM
    z = proj(c0, 2 * KV_DIM)
    put_padded(ksl_ref, 0, rope(z[:, :KV_DIM]), ke_ref[...])
    put_values_t(vslt_ref, z[:, KV_DIM:])
    c0 += 2 * KV_DIM
    z = proj(c0, 2 * KV_DIM)
    put_padded(kw_ref, 0, rope(z[:, :KV_DIM]), 0.0)
    put_values_t(vwt_ref, z[:, KV_DIM:])
    put_transposed(gnt_ref, jax.nn.sigmoid(_dot(xb, wg_ref[...])))


def _in_proj(x2, w_main, w_gn, cos, sin, ke, batch, seq, tm):
    n = x2.shape[0]
    d = x2.shape[1]
    spt = seq // tm
    tok = lambda width: pl.BlockSpec((tm, width), lambda i: (i, 0))
    full = lambda a: pl.BlockSpec(a.shape, lambda i: (0, 0))
    tab = pl.BlockSpec((tm, LANES), lambda i: (i % spt, 0))
    flat = lambda w, dt: (tok(w), jax.ShapeDtypeStruct((n, w), dt))
    trans = lambda r, keys, dt: (pl.BlockSpec((None, tm // keys, r, keys), lambda i: (i // spt, i % spt, 0, 0)),
                                 jax.ShapeDtypeStruct((batch, seq // keys, r, keys), dt))
    outs = [flat(Q_HEADS * LANES, BF16), flat(KV_HEADS * LANES, BF16),
            trans(KV_HEADS * VROWS, QBLK, BF16),
            flat(Q_HEADS * LANES, BF16), flat(Q_HEADS * LANES, BF16),
            flat(KV_DIM, F32), flat(KV_DIM, F32),
            flat(KV_HEADS * LANES, BF16), trans(KV_HEADS * VROWS, KSTEP, BF16),
            flat(KV_HEADS * LANES, BF16), trans(KV_HEADS * VROWS, KSTEP, BF16),
            trans(LANES, LANES, F32)]
    return pl.pallas_call(
        _inproj_kernel,
        grid=(n // tm,),
        in_specs=[tok(d), full(w_main), full(w_gn), tab, tab, tab],
        out_specs=[o[0] for o in outs],
        out_shape=[o[1] for o in outs],
        compiler_params=pltpu.CompilerParams(dimension_semantics=("parallel",),
                                             vmem_limit_bytes=48 * 1024 * 1024),
        name="in_proj",
    )(x2, w_main, w_gn, cos, sin, ke)


def _compress_one(src_ref, pe_ref, w1_ref, b1_ref, w2_ref):
    ratio = CMP_BLOCK // CMP_STRIDE
    nchunk = src_ref.shape[0] // CMP_STRIDE
    hid = w1_ref.shape[1]
    parts = [jnp.zeros((KV_HEADS * nchunk, hid), F32) for _ in range(ratio)]
    for l in range(CMP_STRIDE):
        rows = src_ref[pl.ds(l, nchunk, stride=CMP_STRIDE), :]
        rows = jnp.concatenate([rows[:, g * HEAD_DIM:(g + 1) * HEAD_DIM] for g in range(KV_HEADS)], axis=0)
        for j in range(ratio):
            p = j * CMP_STRIDE + l
            a = (rows + pe_ref[p:p + 1, :]).astype(BF16)
            parts[j] = parts[j] + _dot(a, w1_ref[p * HEAD_DIM:(p + 1) * HEAD_DIM, :])
    out = []
    for g in range(KV_HEADS):
        h = parts[0][g * nchunk:(g + 1) * nchunk]
        for j in range(1, ratio):
            h = h + pltpu.roll(parts[j][g * nchunk:(g + 1) * nchunk], nchunk - j, 0)
        h = jax.nn.gelu(h + b1_ref[...])
        out.append(_dot(h.astype(BF16), w2_ref[...]))
    return out


def _compress_kernel(kc_ref, vc_ref, pek_ref, w1k_ref, b1k_ref, w2k_ref,
                     pev_ref, w1v_ref, b1v_ref, w2v_ref, kcmp_ref, vcmpt_ref):
    for g, kc in enumerate(_compress_one(kc_ref, pek_ref, w1k_ref, b1k_ref, w2k_ref)):
        kcmp_ref[g] = kc.astype(BF16)
    for g, vc in enumerate(_compress_one(vc_ref, pev_ref, w1v_ref, b1v_ref, w2v_ref)):
        vcmpt_ref[g] = vc.T[:HEAD_DIM, :].astype(BF16)


def _compress(kc, vc, wk, wv, batch, seq):
    nchunk = seq // CMP_STRIDE
    src = pl.BlockSpec((seq, KV_DIM), lambda b: (b, 0))
    full = lambda a: pl.BlockSpec(a.shape, lambda b: (0,) * a.ndim)
    return pl.pallas_call(
        _compress_kernel,
        grid=(batch,),
        in_specs=[src, src] + [full(a) for a in wk] + [full(a) for a in wv],
        out_specs=[pl.BlockSpec((None, KV_HEADS, nchunk, LANES), lambda b: (b, 0, 0, 0)),
                   pl.BlockSpec((None, KV_HEADS, HEAD_DIM, nchunk), lambda b: (b, 0, 0, 0))],
        out_shape=[jax.ShapeDtypeStruct((batch, KV_HEADS, nchunk, LANES), BF16),
                   jax.ShapeDtypeStruct((batch, KV_HEADS, HEAD_DIM, nchunk), BF16)],
        compiler_params=pltpu.CompilerParams(dimension_semantics=("parallel",)),
        name="compress",
    )(kc, vc, *wk, *wv)


MASK_COL = HEAD_DIM


def _select_blocks_t(imp, t0):
    nb, cols = imp.shape
    j = lax.broadcasted_iota(jnp.int32, (nb, cols), 0)
    cur = (t0 + lax.broadcasted_iota(jnp.int32, (nb, cols), 1)) >> int(np.log2(SLC_BLOCK))
    forced = (j == 0) | (j == cur) | (j == cur - 1)
    score = jnp.where(forced, BIG, jnp.where(j <= cur, imp, -BIG))
    rank = jnp.zeros((nb, cols), F32)
    for jp in range(nb):
        row = score[jp:jp + 1, :]
        beats = (row > score) | ((row == score) & (j > jp))
        rank = rank + jnp.where(beats, 1.0, 0.0)
    return jnp.where((rank < float(min(SLC_TOP_N, nb))) & (j <= cur), 1.0, 0.0)


def _put_heads(o_ref, g, o):
    for i in range(REP // 2):
        pair = jnp.concatenate([o[:, (2 * i) * QBLK:(2 * i + 1) * QBLK],
                                o[:, (2 * i + 1) * QBLK:(2 * i + 2) * QBLK]], axis=0)
        col = (g * REP + 2 * i) * HEAD_DIM
        o_ref[:, col:col + LANES] = pair.T.astype(o_ref.dtype)


class _Pipeline:
    def __init__(self, k_ref, vt_ref, q_ref, s_ref, p_ref, acc_ref, chunk_of):
        self.k_ref, self.vt_ref, self.q_ref = k_ref, vt_ref, q_ref
        self.s_ref, self.p_ref, self.acc_ref = s_ref, p_ref, acc_ref
        self.chunk_of = chunk_of
        self.groups = range(KV_HEADS)

    def qk(self, g, j):
        start = pl.multiple_of(self.chunk_of(j) * KSTEP, KSTEP)
        return _dot_nt(self.k_ref[pl.ds(start, KSTEP), g * LANES:(g + 1) * LANES], self.q_ref[g])

    def keep(self, g, st, bias):
        if bias is not None:
            st = st + bias
        self.s_ref[g] = st
        return jnp.max(st, axis=0, keepdims=True)

    def softmax(self, g, m, cmax):
        m_new = jnp.maximum(m, cmax)
        a = jnp.exp2(m - m_new)
        self.p_ref[g] = jnp.exp2(self.s_ref[g] - m_new).astype(BF16)
        return m_new, a

    def values(self, g, j, a):
        vt = self.vt_ref[self.chunk_of(j), g * VROWS:(g + 1) * VROWS, :]
        self.acc_ref[g] = a * self.acc_ref[g] + _dot(vt, self.p_ref[g])

    def start(self, st0, st1, bias0, bias1):
        m0 = jnp.full((1, self.s_ref.shape[2]), NEG, F32)
        cmax = [self.keep(g, st0[g], bias0) for g in self.groups]
        sm = [self.softmax(g, m0, cmax[g]) for g in self.groups]
        cmax = tuple(self.keep(g, st1[g], bias1) for g in self.groups)
        for g in self.groups:
            self.acc_ref[g] = jnp.zeros(self.acc_ref.shape[1:], F32)
        return (tuple(s[0] for s in sm), tuple(s[1] for s in sm), cmax)

    def step(self, j, state, last=False, bias=None):
        m, a, cmax = state
        st = None if last else [self.qk(g, j + 2) for g in self.groups]
        for g in self.groups:
            self.values(g, j, a[g])
        sm = [self.softmax(g, m[g], cmax[g]) for g in self.groups]
        if not last:
            cmax = tuple(self.keep(g, st[g], bias) for g in self.groups)
        return (tuple(s[0] for s in sm), tuple(s[1] for s in sm), cmax)

    def finish(self, count, state):
        _, a, _ = state
        out = []
        for g in self.groups:
            self.values(g, count - 1, a[g])
            acc = self.acc_ref[g]
            out.append(acc[:HEAD_DIM] * (1.0 / acc[HEAD_DIM:HEAD_DIM + 1]))
        return out


def _attn_kernel(sink_ref, abias_ref, pbias_ref, ovt_ref,
                 qa_ref, ka_ref, vat_ref, qn_ref, qnr_ref, gnt_ref, kcmp_ref, vcmpt_ref,
                 ksl_ref, vslt_ref, kw_ref, vwt_ref,
                 oa_ref, ob_ref,
                 qw_ref, sw_ref, pw_ref, accw_ref, qs_ref, ss_ref, ps_ref, accs_ref):
    n = pl.program_id(1)
    groups = range(KV_HEADS)
    width = REP * QBLK
    nb = ovt_ref.shape[0]
    nc = kcmp_ref.shape[1]
    t0 = pl.multiple_of(n * QBLK, QBLK)
    heads = lambda g: [g * REP + r for r in range(REP)]
    rows = lambda ref, g: jnp.concatenate([ref[:, h * LANES:(h + 1) * LANES] for h in heads(g)], axis=0)
    lane_chunks = lambda a: [a[:, r * QBLK:(r + 1) * QBLK] for r in range(REP)]
    own = n // 2
    par = n % 2
    gone = lambda cond: jnp.where(cond, 0.0, NEG)

    assert B_WINDOW == 2 * KSTEP
    win = _Pipeline(kw_ref, vwt_ref, qw_ref, sw_ref, pw_ref, accw_ref, lambda j: jnp.maximum(own - j, 0))
    slc = _Pipeline(ksl_ref, vslt_ref, qs_ref, ss_ref, ps_ref, accs_ref,
                    lambda j: jnp.where(j == 0, own, jnp.maximum(j - 1, 0)))
    slc_count = jnp.maximum(own, 1) + 1

    for g in groups:
        qw_ref[g] = rows(qnr_ref, g)
    a0 = jnp.maximum(n - 1, 0)
    astart = pl.multiple_of(a0 * QBLK, QBLK)
    s_swa = [_dot_nt(ka_ref[pl.ds(astart, 2 * QBLK), g * LANES:(g + 1) * LANES], rows(qa_ref, g)) for g in groups]
    s_cmp = [_dot_nt(kcmp_ref[g], rows(qn_ref, g)) for g in groups]
    w0 = [win.qk(g, 0) for g in groups]
    w1 = [win.qk(g, 1) for g in groups]

    abias = abias_ref[jnp.minimum(n, 1)]
    e_swa, x_swa = [], []
    for g in groups:
        st = s_swa[g] + abias
        sk = jnp.concatenate([jnp.full((1, QBLK), sink_ref[h] * LOG2E, F32) for h in heads(g)], axis=1)
        m = jnp.maximum(jnp.max(st, axis=0, keepdims=True), sk)
        e_swa.append(jnp.exp2(st - m).astype(BF16))
        x_swa.append(jnp.exp2(sk - m))

    c_end = lax.broadcasted_iota(jnp.int32, (nc, width), 0) * CMP_STRIDE + (CMP_BLOCK - 1)
    tq = t0 + (lax.broadcasted_iota(jnp.int32, (nc, width), 1) & (QBLK - 1))
    valid = c_end <= tq
    validf = jnp.where(valid, 1.0, 0.0)
    p_cmp = []
    for g in groups:
        s = jnp.where(valid, s_cmp[g], NEG)
        e = jnp.exp2(s - jnp.max(s, axis=0, keepdims=True)) * validf
        p_cmp.append(e * (1.0 / jnp.maximum(jnp.sum(e, axis=0, keepdims=True), 1e-30)))

    win_state = win.start(w0, w1, pbias_ref[2 - par], gone(own >= 1))
    win_state = win.step(0, win_state, bias=pbias_ref[3 + par] + gone(own >= 2))

    for g in groups:
        vt = jnp.concatenate([vat_ref[a0, g * VROWS:(g + 1) * VROWS, :],
                              vat_ref[a0 + 1, g * VROWS:(g + 1) * VROWS, :]], axis=1)
        o = _dot(vt, e_swa[g])
        _put_heads(oa_ref, g, o[:HEAD_DIM] * (1.0 / (o[HEAD_DIM:HEAD_DIM + 1] + x_swa[g])))
    o_cmp = [_dot(vcmpt_ref[g], p_cmp[g].astype(BF16)) for g in groups]
    imp = []
    for g in groups:
        pc = lane_chunks(p_cmp[g])
        psum = (pc[0] + pc[1]) + (pc[2] + pc[3])
        hi = psum.astype(BF16)
        lo = (psum - hi.astype(F32)).astype(BF16)
        imp.append(_dot(ovt_ref[...], hi) + _dot(ovt_ref[...], lo))
    o_win = win.finish(3, win.step(1, win_state, last=True))

    jb = lax.broadcasted_iota(jnp.int32, (nb, QBLK), 0)
    curb = (t0 + lax.broadcasted_iota(jnp.int32, (nb, QBLK), 1)) >> int(np.log2(SLC_BLOCK))
    sel = lax.cond((t0 + QBLK - 1) // SLC_BLOCK < SLC_TOP_N,
                   lambda: tuple(jnp.where(jb <= curb, 1.0, 0.0) for _ in groups),
                   lambda: tuple(_select_blocks_t(imp[g], t0) for g in groups))

    for g in groups:
        mcols = jnp.concatenate([jnp.zeros((MASK_COL, QBLK), F32), sel[g] - 1.0,
                                 jnp.zeros((LANES - MASK_COL - nb, QBLK), F32)], axis=0)
        mpad = mcols.T.astype(BF16)
        for r, h in enumerate(heads(g)):
            qs_ref[g, r * QBLK:(r + 1) * QBLK, :] = qnr_ref[:, h * LANES:(h + 1) * LANES] + mpad
    s0 = [slc.qk(g, 0) for g in groups]
    s1 = [slc.qk(g, 1) for g in groups]
    slc_state = slc.start(s0, s1, pbias_ref[par], gone(own >= 1))

    no_bias = gone(own >= 0)
    slc_state = lax.fori_loop(0, slc_count - 2, lambda j, s: slc.step(j, s, bias=no_bias), slc_state)

    o_slc = slc.finish(slc_count, slc.step(slc_count - 2, slc_state, last=True))
    for g in groups:
        gate = lambda br: jnp.concatenate(
            [gnt_ref[br * Q_HEADS + h:br * Q_HEADS + h + 1, :] for h in heads(g)], axis=1)
        _put_heads(ob_ref, g, gate(0) * o_cmp[g] + gate(1) * o_slc[g] + gate(2) * o_win[g])


def _attn(sinks, qa, ka, vat, qn, qnr, gnt, kcmp, vcmpt, ksl, vslt, kw, vwt, batch, seq):
    nq = seq // QBLK
    nc = kcmp.shape[2]
    nb = seq // SLC_BLOCK
    width = REP * QBLK
    assert nb <= LANES - MASK_COL and QBLK % SLC_BLOCK == 0 and B_WINDOW % QBLK == 0 and A_WINDOW == QBLK
    c_start = np.arange(nc)[None, :] * CMP_STRIDE
    s_start = np.arange(nb)[:, None] * SLC_BLOCK
    ovt = jnp.asarray((c_start < s_start + SLC_BLOCK) & (s_start < c_start + CMP_BLOCK), BF16)
    kq = np.arange(QBLK)[:, None] - (np.arange(width)[None, :] % QBLK)
    after = np.where(kq > 0, NEG, 0.0)
    behind = np.where(kq <= 0, NEG, 0.0)
    free, none = np.zeros_like(after), np.full_like(after, NEG)
    pair = lambda first, second: np.concatenate([first, second])
    abias = jnp.asarray(np.stack([pair(after, none), pair(behind, after)]), F32)
    pbias = jnp.asarray(np.stack([pair(after, free), pair(free, after), pair(after, none),
                                  pair(behind, free), pair(none, behind)]), F32)
    q_spec = pl.BlockSpec((QBLK, Q_HEADS * LANES), lambda b, n: (b * nq + n, 0))
    g_spec = pl.BlockSpec((None, None, LANES, LANES), lambda b, n: (b, n, 0, 0))
    kc_spec = pl.BlockSpec((None, KV_HEADS, nc, LANES), lambda b, n: (b, 0, 0, 0))
    vc_spec = pl.BlockSpec((None, KV_HEADS, HEAD_DIM, nc), lambda b, n: (b, 0, 0, 0))
    const = lambda a: pl.BlockSpec(a.shape, lambda b, n: (0,) * a.ndim)
    k_spec = pl.BlockSpec((seq, KV_HEADS * LANES), lambda b, n: (b, 0))
    vt_spec = lambda keys: pl.BlockSpec((None, seq // keys, KV_HEADS * VROWS, keys), lambda b, n: (b, 0, 0, 0))
    o_spec = pl.BlockSpec((QBLK, Q_DIM), lambda b, n: (b * nq + n, 0))
    o_shape = jax.ShapeDtypeStruct((batch * seq, Q_DIM), BF16)
    pipe_scratch = [pltpu.VMEM((KV_HEADS, width, LANES), BF16),
                    pltpu.VMEM((KV_HEADS, KSTEP, width), F32),
                    pltpu.VMEM((KV_HEADS, KSTEP, width), BF16),
                    pltpu.VMEM((KV_HEADS, VROWS, width), F32)]
    return pl.pallas_call(
        _attn_kernel,
        grid=(batch, nq),
        in_specs=[pl.BlockSpec(memory_space=pltpu.SMEM), const(abias), const(pbias), const(ovt),
                  q_spec, k_spec, vt_spec(QBLK), q_spec, q_spec, g_spec, kc_spec, vc_spec,
                  k_spec, vt_spec(KSTEP), k_spec, vt_spec(KSTEP)],
        out_specs=[o_spec, o_spec],
        out_shape=[o_shape, o_shape],
        scratch_shapes=pipe_scratch + pipe_scratch,
        compiler_params=pltpu.CompilerParams(dimension_semantics=("parallel", "arbitrary"),
                                             vmem_limit_bytes=48 * 1024 * 1024),
        name="attn",
    )(sinks, abias, pbias, ovt, qa, ka, vat, qn, qnr, gnt, kcmp, vcmpt, ksl, vslt, kw, vwt)


def _layer_norm(r, g, b):
    mu = jnp.mean(r, axis=-1, keepdims=True)
    d = r - mu
    var = jnp.mean(d * d, axis=-1, keepdims=True)
    return d * lax.rsqrt(var + LN_EPS) * g + b


def _post_kernel(alpha, x_ref, oa_ref, ob_ref, wgm_ref, wpa_ref, wpb_ref, wout_ref, g_ref, b_ref, h_ref):
    d = x_ref.shape[1]
    x = x_ref[...]
    xb = x.astype(BF16)
    pa = _dot(oa_ref[...], wpa_ref[...])
    pb = _dot(ob_ref[...], wpb_ref[...])
    y = (jax.nn.sigmoid(_dot(xb, wgm_ref[:, :d])) * pa + jax.nn.sigmoid(_dot(xb, wgm_ref[:, d:])) * pb)
    m = _dot(y.astype(BF16), wout_ref[...])
    h_ref[...] = _layer_norm(alpha * x + m, g_ref[...], b_ref[...])


def _post(x2, oa, ob, w_gm, wpa, wpb, wout, g, b, alpha, tm):
    n, d = x2.shape
    tok = lambda width: pl.BlockSpec((tm, width), lambda i: (i, 0))
    full = lambda a: pl.BlockSpec(a.shape, lambda i: (0, 0))
    return pl.pallas_call(
        functools.partial(_post_kernel, alpha),
        grid=(n // tm,),
        in_specs=[tok(d), tok(Q_DIM), tok(Q_DIM)] + [full(a) for a in (w_gm, wpa, wpb, wout, g, b)],
        out_specs=tok(d),
        out_shape=jax.ShapeDtypeStruct((n, d), F32),
        compiler_params=pltpu.CompilerParams(dimension_semantics=("parallel",),
                                             vmem_limit_bytes=48 * 1024 * 1024),
        name="post",
    )(x2, oa, ob, w_gm, wpa, wpb, wout, g, b)


FFN_CHUNK = 256


def _ffn_kernel(alpha, h_ref, wg_ref, wu_ref, wd_ref, g_ref, b_ref, o_ref):
    h = h_ref[...]
    hb = h.astype(BF16)
    hidden = wg_ref.shape[1]
    acc = jnp.zeros(h.shape, F32)
    for c in range(hidden // FFN_CHUNK):
        sl = slice(c * FFN_CHUNK, (c + 1) * FFN_CHUNK)
        a = jax.nn.silu(_dot(hb, wg_ref[:, sl])) * _dot(hb, wu_ref[:, sl])
        acc = acc + _dot(a.astype(BF16), wd_ref[sl, :])
    o_ref[...] = _layer_norm(alpha * h + acc, g_ref[...], b_ref[...])


def _ffn(h, wg, wu, wd, g, b, alpha, tm):
    n, d = h.shape
    tok = pl.BlockSpec((tm, d), lambda i: (i, 0))
    once = lambda a: pl.BlockSpec(a.shape, lambda i: (0, 0), pipeline_mode=pl.Buffered(1))
    return pl.pallas_call(
        functools.partial(_ffn_kernel, alpha),
        grid=(n // tm,),
        in_specs=[tok] + [once(a) for a in (wg, wu, wd, g, b)],
        out_specs=tok,
        out_shape=jax.ShapeDtypeStruct((n, d), F32),
        compiler_params=pltpu.CompilerParams(dimension_semantics=("parallel",),
                                             vmem_limit_bytes=56 * 1024 * 1024),
        name="ffn",
    )(h, wg, wu, wd, g, b)


def _position_tables(seq):
    half = HEAD_DIM // 2
    inv = ROPE_THETA ** (-jnp.arange(half, dtype=F32) / half)
    ang = jnp.arange(seq).astype(F32)[:, None] * inv[None, :]
    cos, sin = jnp.cos(ang), jnp.sin(ang)
    reps = LANES // HEAD_DIM
    lane = np.arange(LANES)[None, :]
    blk = (np.arange(seq) // SLC_BLOCK)[:, None]
    ke = jnp.asarray(np.where(lane - MASK_COL == blk, -NEG, 0.0), F32)
    return jnp.tile(cos, (1, 2 * reps)), jnp.tile(jnp.concatenate([-sin, sin], axis=1), (1, reps)), ke


def kernel(x, w_in, sinks, cmp_pe_k, cmp_w1_k, cmp_b1_k, cmp_w2_k, cmp_pe_v, cmp_w1_v, cmp_b1_v, cmp_w2_v,
           w_proj_a, w_proj_b, w_out, ln1_g, ln1_b, w_gate, w_up, w_down, ln2_g, ln2_b):
    batch, seq, d = x.shape
    depth = w_in.shape[0]
    alpha = (2 * depth) ** 0.25
    n_main = 2 * Q_DIM + 8 * KV_DIM
    n_gate = 3 * Q_HEADS
    tm = 512
    cos, sin, ke = _position_tables(seq)
    pad_cols = lambda w: jnp.pad(w, ((0, 0), (0, LANES - w.shape[1]))).astype(BF16)
    xt = x.reshape(batch * seq, d)
    for l in range(depth):
        w_main = w_in[l, :, :n_main].astype(BF16)
        w_gn = pad_cols(w_in[l, :, n_main:n_main + n_gate])
        w_gm = w_in[l, :, n_main + n_gate:].astype(BF16)
        (qa, ka, vat, qn, qnr, kc, vc, ksl, vslt, kw, vwt, gnt) = _in_proj(
            xt, w_main, w_gn, cos, sin, ke, batch, seq, tm)
        wk = (cmp_pe_k[l], cmp_w1_k[l].astype(BF16), cmp_b1_k[l][None, :], pad_cols(cmp_w2_k[l]))
        wv = (cmp_pe_v[l], cmp_w1_v[l].astype(BF16), cmp_b1_v[l][None, :], pad_cols(cmp_w2_v[l]))
        kcmp, vcmpt = _compress(kc, vc, wk, wv, batch, seq)
        oa, ob = _attn(sinks[l], qa, ka, vat, qn, qnr, gnt, kcmp, vcmpt, ksl, vslt, kw, vwt, batch, seq)
        h = _post(xt, oa, ob, w_gm, w_proj_a[l].astype(BF16), w_proj_b[l].astype(BF16), w_out[l].astype(BF16),
                  ln1_g[l][None, :], ln1_b[l][None, :], alpha, tm)
        xt = _ffn(h, w_gate[l].astype(BF16), w_up[l].astype(BF16), w_down[l].astype(BF16),
                  ln2_g[l][None, :], ln2_b[l][None, :], alpha, tm)
    return xt.reshape(batch, seq, d)
```

```python
import functools

import jax
import jax.numpy as jnp
import numpy as np
from jax import lax
from jax.experimental import pallas as pl
from jax.experimental.pallas import tpu as pltpu

HEAD_DIM = 64
ROPE_THETA = 10000.0
Q_HEADS = 8
KV_HEADS = 2
REP = Q_HEADS // KV_HEADS
A_WINDOW = 128
B_WINDOW = 512
CMP_BLOCK = 32
CMP_STRIDE = 16
SLC_BLOCK = 64
SLC_TOP_N = 16
LN_EPS = 1e-5
NEG = -1e30
BIG = 1e9
Q_DIM = Q_HEADS * HEAD_DIM
KV_DIM = KV_HEADS * HEAD_DIM
LANES = 128
QBLK = 256
KSTEP = QBLK
LOG2E = 1.4426950408889634
SCALE = HEAD_DIM ** -0.5 * LOG2E
VROWS = HEAD_DIM + 16

F32 = jnp.float32
BF16 = jnp.bfloat16


def _dot(a, b):
    return jnp.dot(a, b, preferred_element_type=F32)


def _dot_nt(a, b):
    return lax.dot_general(a, b, (((1,), (1,)), ((), ())), preferred_element_type=F32)


def _inproj_kernel(x_ref, w_ref, wg_ref, cos_ref, sin_ref, ke_ref,
                   qa_ref, ka_ref, vat_ref, qn_ref, qnr_ref, kc_ref, vc_ref,
                   ksl_ref, vslt_ref, kw_ref, vwt_ref, gnt_ref):
    tm = x_ref.shape[0]
    xb = x_ref[...].astype(BF16)
    cos = cos_ref[...]
    sin = sin_ref[...]
    lane = lax.broadcasted_iota(jnp.int32, (tm, LANES), 1)
    first_half = (lane & (HEAD_DIM - 1)) < (HEAD_DIM // 2)
    low = lane < HEAD_DIM

    def rope(z):
        sw = jnp.where(first_half, pltpu.roll(z, LANES - HEAD_DIM // 2, 1), pltpu.roll(z, HEAD_DIM // 2, 1))
        return z * cos + sw * sin

    def proj(c0, n):
        return _dot(xb, w_ref[:, c0:c0 + n])

    def chunks(z):
        return [z[:, c * LANES:(c + 1) * LANES] for c in range(z.shape[1] // LANES)]

    def put_padded(ref, c, z, fill):
        ref[:, (2 * c) * LANES:(2 * c + 1) * LANES] = jnp.where(low, z, fill).astype(ref.dtype)
        ref[:, (2 * c + 1) * LANES:(2 * c + 2) * LANES] = jnp.where(low, pltpu.roll(z, HEAD_DIM, 1), fill).astype(ref.dtype)

    def put_transposed(ref, z):
        for j in range(tm // LANES):
            ref[j] = z[j * LANES:(j + 1) * LANES, :].T.astype(ref.dtype)

    ones_row = jnp.where(lax.broadcasted_iota(jnp.int32, (VROWS - HEAD_DIM, LANES), 0) == 0, 1.0, 0.0)

    def put_values_t(ref, z):
        keys = ref.shape[2]
        for j in range(tm // keys):
            for i in range(keys // LANES):
                zt = z[j * keys + i * LANES:j * keys + (i + 1) * LANES, :].T
                cols = slice(i * LANES, (i + 1) * LANES)
                for g in range(KV_HEADS):
                    ref[j, g * VROWS:g * VROWS + HEAD_DIM, cols] = zt[g * HEAD_DIM:(g + 1) * HEAD_DIM, :].astype(ref.dtype)
                    ref[j, g * VROWS + HEAD_DIM:(g + 1) * VROWS, cols] = ones_row.astype(ref.dtype)

    for c, zc in enumerate(chunks(proj(0, Q_DIM))):
        put_padded(qa_ref, c, rope(zc) * SCALE, 0.0)
    z = proj(Q_DIM, 2 * KV_DIM)
    put_padded(ka_ref, 0, rope(z[:, :KV_DIM]), 0.0)
    put_values_t(vat_ref, z[:, KV_DIM:])
    c0 = Q_DIM + 2 * KV_DIM
    for c, zc in enumerate(chunks(proj(c0, Q_DIM))):
        put_padded(qn_ref, c, zc * SCALE, 0.0)
        put_padded(qnr_ref, c, rope(zc) * SCALE, 0.0)
    c0 += Q_DIM
    z = proj(c0, 2 * KV_DIM)
    kc_ref[...] = z[:, :KV_DIM]
    vc_ref[...] = z[:, KV_DIM:]
    c0 += 2 * KV_DIM
    z = proj(c0, 2 * KV_DIM)
    put_padded(ksl_ref, 0, rope(z[:, :KV_DIM]), ke_ref[...])
    put_values_t(vslt_ref, z[:, KV_DIM:])
    c0 += 2 * KV_DIM
    z = proj(c0, 2 * KV_DIM)
    put_padded(kw_ref, 0, rope(z[:, :KV_DIM]), 0.0)
    put_values_t(vwt_ref, z[:, KV_DIM:])
    put_transposed(gnt_ref, jax.nn.sigmoid(_dot(xb, wg_ref[...])))


def _in_proj(x2, w_main, w_gn, cos, sin, ke, batch, seq, tm):
    n = x2.shape[0]
    d = x2.shape[1]
    spt = seq // tm
    tok = lambda width: pl.BlockSpec((tm, width), lambda i: (i, 0))
    full = lambda a: pl.BlockSpec(a.shape, lambda i: (0, 0))
    tab = pl.BlockSpec((tm, LANES), lambda i: (i % spt, 0))
    flat = lambda w, dt: (tok(w), jax.ShapeDtypeStruct((n, w), dt))
    trans = lambda r, keys, dt: (pl.BlockSpec((None, tm // keys, r, keys), lambda i: (i // spt, i % spt, 0, 0)),
                                 jax.ShapeDtypeStruct((batch, seq // keys, r, keys), dt))
    outs = [flat(Q_HEADS * LANES, BF16), flat(KV_HEADS * LANES, BF16),
            trans(KV_HEADS * VROWS, A_WINDOW, BF16),
            flat(Q_HEADS * LANES, BF16), flat(Q_HEADS * LANES, BF16),
            flat(KV_DIM, F32), flat(KV_DIM, F32),
            flat(KV_HEADS * LANES, BF16), trans(KV_HEADS * VROWS, KSTEP, BF16),
            flat(KV_HEADS * LANES, BF16), trans(KV_HEADS * VROWS, KSTEP, BF16),
            trans(LANES, LANES, F32)]
    return pl.pallas_call(
        _inproj_kernel,
        grid=(n // tm,),
        in_specs=[tok(d), full(w_main), full(w_gn), tab, tab, tab],
        out_specs=[o[0] for o in outs],
        out_shape=[o[1] for o in outs],
        compiler_params=pltpu.CompilerParams(dimension_semantics=("parallel",),
                                             vmem_limit_bytes=48 * 1024 * 1024),
        name="in_proj",
    )(x2, w_main, w_gn, cos, sin, ke)


def _compress_one(src_ref, pe_ref, w1_ref, b1_ref, w2_ref):
    ratio = CMP_BLOCK // CMP_STRIDE
    nchunk = src_ref.shape[0] // CMP_STRIDE
    hid = w1_ref.shape[1]
    parts = [jnp.zeros((KV_HEADS * nchunk, hid), F32) for _ in range(ratio)]
    for l in range(CMP_STRIDE):
        rows = src_ref[pl.ds(l, nchunk, stride=CMP_STRIDE), :]
        rows = jnp.concatenate([rows[:, g * HEAD_DIM:(g + 1) * HEAD_DIM] for g in range(KV_HEADS)], axis=0)
        for j in range(ratio):
            p = j * CMP_STRIDE + l
            a = (rows + pe_ref[p:p + 1, :]).astype(BF16)
            parts[j] = parts[j] + _dot(a, w1_ref[p * HEAD_DIM:(p + 1) * HEAD_DIM, :])
    out = []
    for g in range(KV_HEADS):
        h = parts[0][g * nchunk:(g + 1) * nchunk]
        for j in range(1, ratio):
            h = h + pltpu.roll(parts[j][g * nchunk:(g + 1) * nchunk], nchunk - j, 0)
        h = jax.nn.gelu(h + b1_ref[...])
        out.append(_dot(h.astype(BF16), w2_ref[...]))
    return out


def _compress_kernel(kc_ref, vc_ref, pek_ref, w1k_ref, b1k_ref, w2k_ref,
                     pev_ref, w1v_ref, b1v_ref, w2v_ref, kcmp_ref, vcmpt_ref):
    for g, kc in enumerate(_compress_one(kc_ref, pek_ref, w1k_ref, b1k_ref, w2k_ref)):
        kcmp_ref[g] = kc.astype(BF16)
    for g, vc in enumerate(_compress_one(vc_ref, pev_ref, w1v_ref, b1v_ref, w2v_ref)):
        vcmpt_ref[g] = vc.T[:HEAD_DIM, :].astype(BF16)


def _compress(kc, vc, wk, wv, batch, seq):
    nchunk = seq // CMP_STRIDE
    src = pl.BlockSpec((seq, KV_DIM), lambda b: (b, 0))
    full = lambda a: pl.BlockSpec(a.shape, lambda b: (0,) * a.ndim)
    return pl.pallas_call(
        _compress_kernel,
        grid=(batch,),
        in_specs=[src, src] + [full(a) for a in wk] + [full(a) for a in wv],
        out_specs=[pl.BlockSpec((None, KV_HEADS, nchunk, LANES), lambda b: (b, 0, 0, 0)),
                   pl.BlockSpec((None, KV_HEADS, HEAD_DIM, nchunk), lambda b: (b, 0, 0, 0))],
        out_shape=[jax.ShapeDtypeStruct((batch, KV_HEADS, nchunk, LANES), BF16),
                   jax.ShapeDtypeStruct((batch, KV_HEADS, HEAD_DIM, nchunk), BF16)],
        compiler_params=pltpu.CompilerParams(dimension_semantics=("parallel",)),
        name="compress",
    )(kc, vc, *wk, *wv)


MASK_COL = HEAD_DIM


def _select_blocks_t(imp, t0):
    nb, cols = imp.shape
    j = lax.broadcasted_iota(jnp.int32, (nb, cols), 0)
    cur = (t0 + lax.broadcasted_iota(jnp.int32, (nb, cols), 1)) >> int(np.log2(SLC_BLOCK))
    forced = (j == 0) | (j == cur) | (j == cur - 1)
    score = jnp.where(forced, BIG, jnp.where(j <= cur, imp, -BIG))
    rank = jnp.zeros((nb, cols), F32)
    for jp in range(nb):
        row = score[jp:jp + 1, :]
        beats = (row > score) | ((row == score) & (j > jp))
        rank = rank + jnp.where(beats, 1.0, 0.0)
    return jnp.where((rank < float(min(SLC_TOP_N, nb))) & (j <= cur), 1.0, 0.0)


def _put_heads(o_ref, g, o):
    for i in range(REP // 2):
        pair = jnp.concatenate([o[:, (2 * i) * QBLK:(2 * i + 1) * QBLK],
                                o[:, (2 * i + 1) * QBLK:(2 * i + 2) * QBLK]], axis=0)
        col = (g * REP + 2 * i) * HEAD_DIM
        o_ref[:, col:col + LANES] = pair.T.astype(o_ref.dtype)


class _Pipeline:
    def __init__(self, k_ref, vt_ref, q_ref, s_ref, p_ref, acc_ref, chunk_of):
        self.k_ref, self.vt_ref, self.q_ref = k_ref, vt_ref, q_ref
        self.s_ref, self.p_ref, self.acc_ref = s_ref, p_ref, acc_ref
        self.chunk_of = chunk_of
        self.groups = range(KV_HEADS)

    def qk(self, g, j):
        start = pl.multiple_of(self.chunk_of(j) * KSTEP, KSTEP)
        return _dot_nt(self.k_ref[pl.ds(start, KSTEP), g * LANES:(g + 1) * LANES], self.q_ref[g])

    def keep(self, g, st, bias):
        if bias is not None:
            st = st + bias
        self.s_ref[g] = st
        return jnp.max(st, axis=0, keepdims=True)

    def softmax(self, g, m, cmax):
        m_new = jnp.maximum(m, cmax)
        a = jnp.exp2(m - m_new)
        self.p_ref[g] = jnp.exp2(self.s_ref[g] - m_new).astype(BF16)
        return m_new, a

    def values(self, g, j, a):
        vt = self.vt_ref[self.chunk_of(j), g * VROWS:(g + 1) * VROWS, :]
        self.acc_ref[g] = a * self.acc_ref[g] + _dot(vt, self.p_ref[g])

    def start(self, st0, st1, bias0, bias1):
        m0 = jnp.full((1, self.s_ref.shape[2]), NEG, F32)
        cmax = [self.keep(g, st0[g], bias0) for g in self.groups]
        sm = [self.softmax(g, m0, cmax[g]) for g in self.groups]
        cmax = tuple(self.keep(g, st1[g], bias1) for g in self.groups)
        for g in self.groups:
            self.acc_ref[g] = jnp.zeros(self.acc_ref.shape[1:], F32)
        return (tuple(s[0] for s in sm), tuple(s[1] for s in sm), cmax)

    def step(self, j, state, last=False, bias=None):
        m, a, cmax = state
        st = None if last else [self.qk(g, j + 2) for g in self.groups]
        for g in self.groups:
            self.values(g, j, a[g])
        sm = [self.softmax(g, m[g], cmax[g]) for g in self.groups]
        if not last:
            cmax = tuple(self.keep(g, st[g], bias) for g in self.groups)
        return (tuple(s[0] for s in sm), tuple(s[1] for s in sm), cmax)

    def finish(self, count, state):
        _, a, _ = state
        out = []
        for g in self.groups:
            self.values(g, count - 1, a[g])
            acc = self.acc_ref[g]
            out.append(acc[:HEAD_DIM] * (1.0 / acc[HEAD_DIM:HEAD_DIM + 1]))
        return out


def _attn_kernel(sink_ref, abias_ref, pbias_ref, ovt_ref,
                 qa_ref, ka_ref, vat_ref, qn_ref, qnr_ref, gnt_ref, kcmp_ref, vcmpt_ref,
                 ksl_ref, vslt_ref, kw_ref, vwt_ref,
                 oa_ref, ob_ref,
                 qw_ref, sw_ref, pw_ref, accw_ref, qs_ref, ss_ref, ps_ref, accs_ref):
    n = pl.program_id(1)
    groups = range(KV_HEADS)
    width = REP * QBLK
    nb = ovt_ref.shape[0]
    nc = kcmp_ref.shape[1]
    t0 = pl.multiple_of(n * QBLK, QBLK)
    heads = lambda g: [g * REP + r for r in range(REP)]
    rows = lambda ref, g: jnp.concatenate([ref[:, h * LANES:(h + 1) * LANES] for h in heads(g)], axis=0)
    lane_chunks = lambda a: [a[:, r * QBLK:(r + 1) * QBLK] for r in range(REP)]
    gone = lambda cond: jnp.where(cond, 0.0, NEG)

    assert B_WINDOW == 2 * KSTEP
    win = _Pipeline(kw_ref, vwt_ref, qw_ref, sw_ref, pw_ref, accw_ref, lambda j: jnp.maximum(n - j, 0))
    slc = _Pipeline(ksl_ref, vslt_ref, qs_ref, ss_ref, ps_ref, accs_ref,
                    lambda j: jnp.where(j == 0, n, jnp.maximum(j - 1, 0)))
    slc_count = jnp.maximum(n, 1) + 1

    for g in groups:
        qw_ref[g] = rows(qnr_ref, g)
    band = A_WINDOW + QBLK
    a0 = jnp.maximum(n * (QBLK // A_WINDOW) - 1, 0)
    astart = pl.multiple_of(a0 * A_WINDOW, A_WINDOW)
    s_swa = [_dot_nt(ka_ref[pl.ds(astart, band), g * LANES:(g + 1) * LANES], rows(qa_ref, g)) for g in groups]
    s_cmp = [_dot_nt(kcmp_ref[g], rows(qn_ref, g)) for g in groups]
    w0 = [win.qk(g, 0) for g in groups]
    w1 = [win.qk(g, 1) for g in groups]

    abias = abias_ref[jnp.minimum(n, 1)]
    e_swa, x_swa = [], []
    for g in groups:
        st = s_swa[g] + abias
        sk = jnp.concatenate([jnp.full((1, QBLK), sink_ref[h] * LOG2E, F32) for h in heads(g)], axis=1)
        m = jnp.maximum(jnp.max(st, axis=0, keepdims=True), sk)
        e_swa.append(jnp.exp2(st - m).astype(BF16))
        x_swa.append(jnp.exp2(sk - m))

    c_end = lax.broadcasted_iota(jnp.int32, (nc, width), 0) * CMP_STRIDE + (CMP_BLOCK - 1)
    tq = t0 + (lax.broadcasted_iota(jnp.int32, (nc, width), 1) & (QBLK - 1))
    valid = c_end <= tq
    validf = jnp.where(valid, 1.0, 0.0)
    p_cmp = []
    for g in groups:
        s = jnp.where(valid, s_cmp[g], NEG)
        e = jnp.exp2(s - jnp.max(s, axis=0, keepdims=True)) * validf
        p_cmp.append(e * (1.0 / jnp.maximum(jnp.sum(e, axis=0, keepdims=True), 1e-30)))

    win_state = win.start(w0, w1, pbias_ref[0], gone(n >= 1))
    win_state = win.step(0, win_state, bias=pbias_ref[1] + gone(n >= 2))

    for g in groups:
        vt = jnp.concatenate([vat_ref[a0 + i, g * VROWS:(g + 1) * VROWS, :]
                              for i in range(band // A_WINDOW)], axis=1)
        o = _dot(vt, e_swa[g])
        _put_heads(oa_ref, g, o[:HEAD_DIM] * (1.0 / (o[HEAD_DIM:HEAD_DIM + 1] + x_swa[g])))
    o_cmp = [_dot(vcmpt_ref[g], p_cmp[g].astype(BF16)) for g in groups]
    imp = []
    for g in groups:
        pc = lane_chunks(p_cmp[g])
        psum = (pc[0] + pc[1]) + (pc[2] + pc[3])
        hi = psum.astype(BF16)
        lo = (psum - hi.astype(F32)).astype(BF16)
        imp.append(_dot(ovt_ref[...], hi) + _dot(ovt_ref[...], lo))
    o_win = win.finish(3, win.step(1, win_state, last=True))

    jb = lax.broadcasted_iota(jnp.int32, (nb, QBLK), 0)
    curb = (t0 + lax.broadcasted_iota(jnp.int32, (nb, QBLK), 1)) >> int(np.log2(SLC_BLOCK))
    sel = lax.cond((t0 + QBLK - 1) // SLC_BLOCK < SLC_TOP_N,
                   lambda: tuple(jnp.where(jb <= curb, 1.0, 0.0) for _ in groups),
                   lambda: tuple(_select_blocks_t(imp[g], t0) for g in groups))

    for g in groups:
        mcols = jnp.concatenate([jnp.zeros((MASK_COL, QBLK), F32), sel[g] - 1.0,
                                 jnp.zeros((LANES - MASK_COL - nb, QBLK), F32)], axis=0)
        mpad = mcols.T.astype(BF16)
        for r, h in enumerate(heads(g)):
            qs_ref[g, r * QBLK:(r + 1) * QBLK, :] = qnr_ref[:, h * LANES:(h + 1) * LANES] + mpad
    s0 = [slc.qk(g, 0) for g in groups]
    s1 = [slc.qk(g, 1) for g in groups]
    slc_state = slc.start(s0, s1, pbias_ref[0], gone(n >= 1))

    no_bias = gone(n >= 0)
    slc_state = lax.fori_loop(0, slc_count - 2, lambda j, s: slc.step(j, s, bias=no_bias), slc_state)

    o_slc = slc.finish(slc_count, slc.step(slc_count - 2, slc_state, last=True))
    for g in groups:
        gate = lambda br: jnp.concatenate(
            [gnt_ref[i, br * Q_HEADS + h:br * Q_HEADS + h + 1, :]
             for h in heads(g) for i in range(gnt_ref.shape[0])], axis=1)
        _put_heads(ob_ref, g, gate(0) * o_cmp[g] + gate(1) * o_slc[g] + gate(2) * o_win[g])


def _attn(sinks, qa, ka, vat, qn, qnr, gnt, kcmp, vcmpt, ksl, vslt, kw, vwt, batch, seq):
    nq = seq // QBLK
    nc = kcmp.shape[2]
    nb = seq // SLC_BLOCK
    width = REP * QBLK
    assert nb <= LANES - MASK_COL and QBLK % SLC_BLOCK == 0 and QBLK % A_WINDOW == 0 and A_WINDOW % LANES == 0
    c_start = np.arange(nc)[None, :] * CMP_STRIDE
    s_start = np.arange(nb)[:, None] * SLC_BLOCK
    ovt = jnp.asarray((c_start < s_start + SLC_BLOCK) & (s_start < c_start + CMP_BLOCK), BF16)
    ql = np.arange(width)[None, :] % QBLK
    masked = lambda visible: np.where(visible, 0.0, NEG)
    kq = np.arange(KSTEP)[:, None] - ql
    pbias = jnp.asarray(np.stack([masked(kq <= 0),
                                  masked(kq > 0)]), F32)
    band = np.arange(A_WINDOW + QBLK)[:, None]
    in_window = lambda dist: (dist >= 0) & (dist < A_WINDOW)
    abias = jnp.asarray(np.stack([masked(in_window(ql - band)),
                                  masked(in_window(ql - (band - A_WINDOW)))]), F32)
    q_spec = pl.BlockSpec((QBLK, Q_HEADS * LANES), lambda b, n: (b * nq + n, 0))
    g_spec = pl.BlockSpec((None, QBLK // LANES, LANES, LANES), lambda b, n: (b, n, 0, 0))
    kc_spec = pl.BlockSpec((None, KV_HEADS, nc, LANES), lambda b, n: (b, 0, 0, 0))
    vc_spec = pl.BlockSpec((None, KV_HEADS, HEAD_DIM, nc), lambda b, n: (b, 0, 0, 0))
    const = lambda a: pl.BlockSpec(a.shape, lambda b, n: (0,) * a.ndim, pipeline_mode=pl.Buffered(1))
    k_spec = pl.BlockSpec((seq, KV_HEADS * LANES), lambda b, n: (b, 0))
    vt_spec = lambda keys: pl.BlockSpec((None, seq // keys, KV_HEADS * VROWS, keys), lambda b, n: (b, 0, 0, 0))
    o_spec = pl.BlockSpec((QBLK, Q_DIM), lambda b, n: (b * nq + n, 0))
    o_shape = jax.ShapeDtypeStruct((batch * seq, Q_DIM), BF16)
    pipe_scratch = [pltpu.VMEM((KV_HEADS, width, LANES), BF16),
                    pltpu.VMEM((KV_HEADS, KSTEP, width), F32),
                    pltpu.VMEM((KV_HEADS, KSTEP, width), BF16),
                    pltpu.VMEM((KV_HEADS, VROWS, width), F32)]
    return pl.pallas_call(
        _attn_kernel,
        grid=(batch, nq),
        in_specs=[pl.BlockSpec(memory_space=pltpu.SMEM), const(abias), const(pbias), const(ovt),
                  q_spec, k_spec, vt_spec(A_WINDOW), q_spec, q_spec, g_spec, kc_spec, vc_spec,
                  k_spec, vt_spec(KSTEP), k_spec, vt_spec(KSTEP)],
        out_specs=[o_spec, o_spec],
        out_shape=[o_shape, o_shape],
        scratch_shapes=pipe_scratch + pipe_scratch,
        compiler_params=pltpu.CompilerParams(dimension_semantics=("parallel", "arbitrary"),
                                             vmem_limit_bytes=56 * 1024 * 1024),
        name="attn",
    )(sinks, abias, pbias, ovt, qa, ka, vat, qn, qnr, gnt, kcmp, vcmpt, ksl, vslt, kw, vwt)


def _layer_norm(r, g, b):
    mu = jnp.mean(r, axis=-1, keepdims=True)
    d = r - mu
    var = jnp.mean(d * d, axis=-1, keepdims=True)
    return d * lax.rsqrt(var + LN_EPS) * g + b


def _post_kernel(alpha, x_ref, oa_ref, ob_ref, wgm_ref, wpa_ref, wpb_ref, wout_ref, g_ref, b_ref, h_ref):
    d = x_ref.shape[1]
    x = x_ref[...]
    xb = x.astype(BF16)
    pa = _dot(oa_ref[...], wpa_ref[...])
    pb = _dot(ob_ref[...], wpb_ref[...])
    y = (jax.nn.sigmoid(_dot(xb, wgm_ref[:, :d])) * pa + jax.nn.sigmoid(_dot(xb, wgm_ref[:, d:])) * pb)
    m = _dot(y.astype(BF16), wout_ref[...])
    h_ref[...] = _layer_norm(alpha * x + m, g_ref[...], b_ref[...])


def _post(x2, oa, ob, w_gm, wpa, wpb, wout, g, b, alpha, tm):
    n, d = x2.shape
    tok = lambda width: pl.BlockSpec((tm, width), lambda i: (i, 0))
    full = lambda a: pl.BlockSpec(a.shape, lambda i: (0, 0))
    return pl.pallas_call(
        functools.partial(_post_kernel, alpha),
        grid=(n // tm,),
        in_specs=[tok(d), tok(Q_DIM), tok(Q_DIM)] + [full(a) for a in (w_gm, wpa, wpb, wout, g, b)],
        out_specs=tok(d),
        out_shape=jax.ShapeDtypeStruct((n, d), F32),
        compiler_params=pltpu.CompilerParams(dimension_semantics=("parallel",),
                                             vmem_limit_bytes=48 * 1024 * 1024),
        name="post",
    )(x2, oa, ob, w_gm, wpa, wpb, wout, g, b)


FFN_CHUNK = 256


def _ffn_kernel(alpha, h_ref, wg_ref, wu_ref, wd_ref, g_ref, b_ref, o_ref):
    h = h_ref[...]
    hb = h.astype(BF16)
    hidden = wg_ref.shape[1]
    acc = jnp.zeros(h.shape, F32)
    for c in range(hidden // FFN_CHUNK):
        sl = slice(c * FFN_CHUNK, (c + 1) * FFN_CHUNK)
        a = jax.nn.silu(_dot(hb, wg_ref[:, sl])) * _dot(hb, wu_ref[:, sl])
        acc = acc + _dot(a.astype(BF16), wd_ref[sl, :])
    o_ref[...] = _layer_norm(alpha * h + acc, g_ref[...], b_ref[...])


def _ffn(h, wg, wu, wd, g, b, alpha, tm):
    n, d = h.shape
    tok = pl.BlockSpec((tm, d), lambda i: (i, 0))
    once = lambda a: pl.BlockSpec(a.shape, lambda i: (0, 0), pipeline_mode=pl.Buffered(1))
    return pl.pallas_call(
        functools.partial(_ffn_kernel, alpha),
        grid=(n // tm,),
        in_specs=[tok] + [once(a) for a in (wg, wu, wd, g, b)],
        out_specs=tok,
        out_shape=jax.ShapeDtypeStruct((n, d), F32),
        compiler_params=pltpu.CompilerParams(dimension_semantics=("parallel",),
                                             vmem_limit_bytes=56 * 1024 * 1024),
        name="ffn",
    )(h, wg, wu, wd, g, b)


def _position_tables(seq):
    half = HEAD_DIM // 2
    inv = ROPE_THETA ** (-jnp.arange(half, dtype=F32) / half)
    ang = jnp.arange(seq).astype(F32)[:, None] * inv[None, :]
    cos, sin = jnp.cos(ang), jnp.sin(ang)
    reps = LANES // HEAD_DIM
    lane = np.arange(LANES)[None, :]
    blk = (np.arange(seq) // SLC_BLOCK)[:, None]
    ke = jnp.asarray(np.where(lane - MASK_COL == blk, -NEG, 0.0), F32)
    return jnp.tile(cos, (1, 2 * reps)), jnp.tile(jnp.concatenate([-sin, sin], axis=1), (1, reps)), ke


def kernel(x, w_in, sinks, cmp_pe_k, cmp_w1_k, cmp_b1_k, cmp_w2_k, cmp_pe_v, cmp_w1_v, cmp_b1_v, cmp_w2_v,
           w_proj_a, w_proj_b, w_out, ln1_g, ln1_b, w_gate, w_up, w_down, ln2_g, ln2_b):
    batch, seq, d = x.shape
    depth = w_in.shape[0]
    alpha = (2 * depth) ** 0.25
    n_main = 2 * Q_DIM + 8 * KV_DIM
    n_gate = 3 * Q_HEADS
    tm = 512
    cos, sin, ke = _position_tables(seq)
    pad_cols = lambda w: jnp.pad(w, ((0, 0), (0, LANES - w.shape[1]))).astype(BF16)
    xt = x.reshape(batch * seq, d)
    for l in range(depth):
        w_main = w_in[l, :, :n_main].astype(BF16)
        w_gn = pad_cols(w_in[l, :, n_main:n_main + n_gate])
        w_gm = w_in[l, :, n_main + n_gate:].astype(BF16)
        (qa, ka, vat, qn, qnr, kc, vc, ksl, vslt, kw, vwt, gnt) = _in_proj(
            xt, w_main, w_gn, cos, sin, ke, batch, seq, tm)
        wk = (cmp_pe_k[l], cmp_w1_k[l].astype(BF16), cmp_b1_k[l][None, :], pad_cols(cmp_w2_k[l]))
        wv = (cmp_pe_v[l], cmp_w1_v[l].astype(BF16), cmp_b1_v[l][None, :], pad_cols(cmp_w2_v[l]))
        kcmp, vcmpt = _compress(kc, vc, wk, wv, batch, seq)
        oa, ob = _attn(sinks[l], qa, ka, vat, qn, qnr, gnt, kcmp, vcmpt, ksl, vslt, kw, vwt, batch, seq)
        h = _post(xt, oa, ob, w_gm, w_proj_a[l].astype(BF16), w_proj_b[l].astype(BF16), w_out[l].astype(BF16),
                  ln1_g[l][None, :], ln1_b[l][None, :], alpha, tm)
        xt = _ffn(h, w_gate[l].astype(BF16), w_up[l].astype(BF16), w_down[l].astype(BF16),
                  ln2_g[l][None, :], ln2_b[l][None, :], alpha, tm)
    return xt.reshape(batch, seq, d)
```

```python
import functools

import jax
import jax.numpy as jnp
import numpy as np
from jax import lax
from jax.experimental import pallas as pl
from jax.experimental.pallas import tpu as pltpu

HEAD_DIM = 64
ROPE_THETA = 10000.0
Q_HEADS = 8
KV_HEADS = 2
REP = Q_HEADS // KV_HEADS
A_WINDOW = 128
B_WINDOW = 512
CMP_BLOCK = 32
CMP_STRIDE = 16
SLC_BLOCK = 64
SLC_TOP_N = 16
LN_EPS = 1e-5
NEG = -1e30
BIG = 1e9
Q_DIM = Q_HEADS * HEAD_DIM
KV_DIM = KV_HEADS * HEAD_DIM
LANES = 128
QBLK = 256
KSTEP = QBLK
LOG2E = 1.4426950408889634
SCALE = HEAD_DIM ** -0.5 * LOG2E
VROWS = HEAD_DIM + 16

F32 = jnp.float32
BF16 = jnp.bfloat16


def _dot(a, b):
    return jnp.dot(a, b, preferred_element_type=F32)


def _dot_nt(a, b):
    return lax.dot_general(a, b, (((1,), (1,)), ((), ())), preferred_element_type=F32)


def _inproj_kernel(x_ref, w_ref, wg_ref, cos_ref, sin_ref, ke_ref,
                   qa_ref, ka_ref, vat_ref, qn_ref, qnr_ref, kc_ref, vc_ref,
                   ksl_ref, vslt_ref, kw_ref, vwt_ref, gnt_ref):
    tm = x_ref.shape[0]
    xb = x_ref[...].astype(BF16)
    cos = cos_ref[...]
    sin = sin_ref[...]
    lane = lax.broadcasted_iota(jnp.int32, (tm, LANES), 1)
    first_half = (lane & (HEAD_DIM - 1)) < (HEAD_DIM // 2)
    low = lane < HEAD_DIM

    def rope(z):
        sw = jnp.where(first_half, pltpu.roll(z, LANES - HEAD_DIM // 2, 1), pltpu.roll(z, HEAD_DIM // 2, 1))
        return z * cos + sw * sin

    def proj(c0, n):
        return _dot(xb, w_ref[:, c0:c0 + n])

    def chunks(z):
        return [z[:, c * LANES:(c + 1) * LANES] for c in range(z.shape[1] // LANES)]

    def put_padded(ref, c, z, fill):
        ref[:, (2 * c) * LANES:(2 * c + 1) * LANES] = jnp.where(low, z, fill).astype(ref.dtype)
        ref[:, (2 * c + 1) * LANES:(2 * c + 2) * LANES] = jnp.where(low, pltpu.roll(z, HEAD_DIM, 1), fill).astype(ref.dtype)

    def put_transposed(ref, z):
        for j in range(tm // LANES):
            ref[j] = z[j * LANES:(j + 1) * LANES, :].T.astype(ref.dtype)

    ones_row = jnp.where(lax.broadcasted_iota(jnp.int32, (VROWS - HEAD_DIM, LANES), 0) == 0, 1.0, 0.0)

    def put_values_t(ref, z):
        keys = ref.shape[2]
        for j in range(tm // keys):
            for i in range(keys // LANES):
                zt = z[j * keys + i * LANES:j * keys + (i + 1) * LANES, :].T
                cols = slice(i * LANES, (i + 1) * LANES)
                for g in range(KV_HEADS):
                    ref[j, g * VROWS:g * VROWS + HEAD_DIM, cols] = zt[g * HEAD_DIM:(g + 1) * HEAD_DIM, :].astype(ref.dtype)
                    ref[j, g * VROWS + HEAD_DIM:(g + 1) * VROWS, cols] = ones_row.astype(ref.dtype)

    for c, zc in enumerate(chunks(proj(0, Q_DIM))):
        put_padded(qa_ref, c, rope(zc) * SCALE, 0.0)
    z = proj(Q_DIM, 2 * KV_DIM)
    put_padded(ka_ref, 0, rope(z[:, :KV_DIM]), 0.0)
    put_values_t(vat_ref, z[:, KV_DIM:])
    c0 = Q_DIM + 2 * KV_DIM
    for c, zc in enumerate(chunks(proj(c0, Q_DIM))):
        put_padded(qn_ref, c, zc * SCALE, 0.0)
        put_padded(qnr_ref, c, rope(zc) * SCALE, 0.0)
    c0 += Q_DIM
    z = proj(c0, 2 * KV_DIM)
    kc_ref[...] = z[:, :KV_DIM]
    vc_ref[...] = z[:, KV_DIM:]
    c0 += 2 * KV_DIM
    z = proj(c0, 2 * KV_DIM)
    put_padded(ksl_ref, 0, rope(z[:, :KV_DIM]), ke_ref[...])
    put_values_t(vslt_ref, z[:, KV_DIM:])
    c0 += 2 * KV_DIM
    z = proj(c0, 2 * KV_DIM)
    put_padded(kw_ref, 0, rope(z[:, :KV_DIM]), 0.0)
    put_values_t(vwt_ref, z[:, KV_DIM:])
    put_transposed(gnt_ref, jax.nn.sigmoid(_dot(xb, wg_ref[...])))


def _in_proj(x2, w_main, w_gn, cos, sin, ke, batch, seq, tm):
    n = x2.shape[0]
    d = x2.shape[1]
    spt = seq // tm
    tok = lambda width: pl.BlockSpec((tm, width), lambda i: (i, 0))
    full = lambda a: pl.BlockSpec(a.shape, lambda i: (0, 0))
    tab = pl.BlockSpec((tm, LANES), lambda i: (i % spt, 0))
    flat = lambda w, dt: (tok(w), jax.ShapeDtypeStruct((n, w), dt))
    trans = lambda r, keys, dt: (pl.BlockSpec((None, tm // keys, r, keys), lambda i: (i // spt, i % spt, 0, 0)),
                                 jax.ShapeDtypeStruct((batch, seq // keys, r, keys), dt))
    outs = [flat(Q_HEADS * LANES, BF16), flat(KV_HEADS * LANES, BF16),
            trans(KV_HEADS * VROWS, A_WINDOW, BF16),
            flat(Q_HEADS * LANES, BF16), flat(Q_HEADS * LANES, BF16),
            flat(KV_DIM, F32), flat(KV_DIM, F32),
            flat(KV_HEADS * LANES, BF16), trans(KV_HEADS * VROWS, KSTEP, BF16),
            flat(KV_HEADS * LANES, BF16), trans(KV_HEADS * VROWS, KSTEP, BF16),
            trans(LANES, LANES, F32)]
    return pl.pallas_call(
        _inproj_kernel,
        grid=(n // tm,),
        in_specs=[tok(d), full(w_main), full(w_gn), tab, tab, tab],
        out_specs=[o[0] for o in outs],
        out_shape=[o[1] for o in outs],
        compiler_params=pltpu.CompilerParams(dimension_semantics=("parallel",),
                                             vmem_limit_bytes=48 * 1024 * 1024),
        name="in_proj",
    )(x2, w_main, w_gn, cos, sin, ke)


def _compress_one(src_ref, pe_ref, w1_ref, b1_ref, w2_ref):
    ratio = CMP_BLOCK // CMP_STRIDE
    nchunk = src_ref.shape[0] // CMP_STRIDE
    hid = w1_ref.shape[1]
    parts = [jnp.zeros((KV_HEADS * nchunk, hid), F32) for _ in range(ratio)]
    for l in range(CMP_STRIDE):
        rows = src_ref[pl.ds(l, nchunk, stride=CMP_STRIDE), :]
        rows = jnp.concatenate([rows[:, g * HEAD_DIM:(g + 1) * HEAD_DIM] for g in range(KV_HEADS)], axis=0)
        for j in range(ratio):
            p = j * CMP_STRIDE + l
            a = (rows + pe_ref[p:p + 1, :]).astype(BF16)
            parts[j] = parts[j] + _dot(a, w1_ref[p * HEAD_DIM:(p + 1) * HEAD_DIM, :])
    out = []
    for g in range(KV_HEADS):
        h = parts[0][g * nchunk:(g + 1) * nchunk]
        for j in range(1, ratio):
            h = h + pltpu.roll(parts[j][g * nchunk:(g + 1) * nchunk], nchunk - j, 0)
        h = jax.nn.gelu(h + b1_ref[...])
        out.append(_dot(h.astype(BF16), w2_ref[...]))
    return out


def _compress_kernel(kc_ref, vc_ref, pek_ref, w1k_ref, b1k_ref, w2k_ref,
                     pev_ref, w1v_ref, b1v_ref, w2v_ref, kcmp_ref, vcmpt_ref):
    for g, kc in enumerate(_compress_one(kc_ref, pek_ref, w1k_ref, b1k_ref, w2k_ref)):
        kcmp_ref[g] = kc.astype(BF16)
    for g, vc in enumerate(_compress_one(vc_ref, pev_ref, w1v_ref, b1v_ref, w2v_ref)):
        vcmpt_ref[g] = vc.T[:HEAD_DIM, :].astype(BF16)


def _compress(kc, vc, wk, wv, batch, seq):
    nchunk = seq // CMP_STRIDE
    src = pl.BlockSpec((seq, KV_DIM), lambda b: (b, 0))
    full = lambda a: pl.BlockSpec(a.shape, lambda b: (0,) * a.ndim)
    return pl.pallas_call(
        _compress_kernel,
        grid=(batch,),
        in_specs=[src, src] + [full(a) for a in wk] + [full(a) for a in wv],
        out_specs=[pl.BlockSpec((None, KV_HEADS, nchunk, LANES), lambda b: (b, 0, 0, 0)),
                   pl.BlockSpec((None, KV_HEADS, HEAD_DIM, nchunk), lambda b: (b, 0, 0, 0))],
        out_shape=[jax.ShapeDtypeStruct((batch, KV_HEADS, nchunk, LANES), BF16),
                   jax.ShapeDtypeStruct((batch, KV_HEADS, HEAD_DIM, nchunk), BF16)],
        compiler_params=pltpu.CompilerParams(dimension_semantics=("parallel",)),
        name="compress",
    )(kc, vc, *wk, *wv)


MASK_COL = HEAD_DIM


def _select_blocks_t(imp, t0):
    nb, cols = imp.shape
    j = lax.broadcasted_iota(jnp.int32, (nb, cols), 0)
    cur = (t0 + lax.broadcasted_iota(jnp.int32, (nb, cols), 1)) >> int(np.log2(SLC_BLOCK))
    forced = (j == 0) | (j == cur) | (j == cur - 1)
    score = jnp.where(forced, BIG, jnp.where(j <= cur, imp, -BIG))
    sub = 8
    tiles = [score[v * sub:(v + 1) * sub] for v in range(nb // sub)]
    ranks = [jnp.zeros((sub, cols), F32) for _ in tiles]
    jl = lax.broadcasted_iota(jnp.int32, (sub, cols), 0)
    for jp in range(nb):
        row = score[jp:jp + 1, :]
        for v, tile in enumerate(tiles):
            if v * sub > jp:
                beats = row >= tile
            elif (v + 1) * sub - 1 <= jp:
                beats = row > tile
            else:
                beats = (row > tile) | ((row == tile) & (jl > jp - v * sub))
            ranks[v] = jnp.where(beats, ranks[v] + 1.0, ranks[v])
    rank = jnp.concatenate(ranks, axis=0)
    return jnp.where((rank < float(min(SLC_TOP_N, nb))) & (j <= cur), 1.0, 0.0)


def _put_heads(o_ref, g, o, nq=QBLK):
    for i in range(REP // 2):
        pair = jnp.concatenate([o[:, (2 * i) * nq:(2 * i + 1) * nq],
                                o[:, (2 * i + 1) * nq:(2 * i + 2) * nq]], axis=0)
        col = (g * REP + 2 * i) * HEAD_DIM
        o_ref[:, col:col + LANES] = pair.T.astype(o_ref.dtype)


class _Pipeline:
    def __init__(self, k_ref, vt_ref, q_ref, s_ref, p_ref, acc_ref, chunk_of):
        self.k_ref, self.vt_ref, self.q_ref = k_ref, vt_ref, q_ref
        self.s_ref, self.p_ref, self.acc_ref = s_ref, p_ref, acc_ref
        self.chunk_of = chunk_of
        self.groups = range(KV_HEADS)

    def qk(self, g, j):
        start = pl.multiple_of(self.chunk_of(j) * KSTEP, KSTEP)
        return _dot_nt(self.k_ref[pl.ds(start, KSTEP), g * LANES:(g + 1) * LANES], self.q_ref[g])

    def keep(self, g, st, bias):
        if bias is not None:
            st = st + bias
        self.s_ref[g] = st
        return jnp.max(st, axis=0, keepdims=True)

    def softmax(self, g, m, cmax):
        m_new = jnp.maximum(m, cmax)
        a = jnp.exp2(m - m_new)
        self.p_ref[g] = jnp.exp2(self.s_ref[g] - m_new).astype(BF16)
        return m_new, a

    def values(self, g, j, a):
        vt = self.vt_ref[self.chunk_of(j), g * VROWS:(g + 1) * VROWS, :]
        self.acc_ref[g] = a * self.acc_ref[g] + _dot(vt, self.p_ref[g])

    def start(self, st0, st1, bias0, bias1):
        m0 = jnp.full((1, self.s_ref.shape[2]), NEG, F32)
        cmax = [self.keep(g, st0[g], bias0) for g in self.groups]
        sm = [self.softmax(g, m0, cmax[g]) for g in self.groups]
        cmax = tuple(self.keep(g, st1[g], bias1) for g in self.groups)
        for g in self.groups:
            self.acc_ref[g] = jnp.zeros(self.acc_ref.shape[1:], F32)
        return (tuple(s[0] for s in sm), tuple(s[1] for s in sm), cmax)

    def step(self, j, state, last=False, bias=None):
        m, a, cmax = state
        st = None if last else [self.qk(g, j + 2) for g in self.groups]
        for g in self.groups:
            self.values(g, j, a[g])
        sm = [self.softmax(g, m[g], cmax[g]) for g in self.groups]
        if not last:
            cmax = tuple(self.keep(g, st[g], bias) for g in self.groups)
        return (tuple(s[0] for s in sm), tuple(s[1] for s in sm), cmax)

    def finish(self, count, state):
        _, a, _ = state
        out = []
        for g in self.groups:
            self.values(g, count - 1, a[g])
            acc = self.acc_ref[g]
            out.append(acc[:HEAD_DIM] * (1.0 / acc[HEAD_DIM:HEAD_DIM + 1]))
        return out


def _attn_kernel(sink_ref, abias_ref, pbias_ref, cbias_ref, ovt_ref,
                 qa_ref, ka_ref, vat_ref, qn_ref, qnr_ref, gnt_ref, kcmp_ref, vcmpt_ref,
                 ksl_ref, vslt_ref, kw_ref, vwt_ref,
                 oa_ref, ob_ref,
                 qw_ref, sw_ref, pw_ref, accw_ref, qs_ref, ss_ref, ps_ref, accs_ref):
    n = pl.program_id(1)
    groups = range(KV_HEADS)
    width = REP * QBLK
    nb = ovt_ref.shape[0]
    nc = kcmp_ref.shape[1]
    t0 = pl.multiple_of(n * QBLK, QBLK)
    heads = lambda g: [g * REP + r for r in range(REP)]
    rows = lambda ref, g: jnp.concatenate([ref[:, h * LANES:(h + 1) * LANES] for h in heads(g)], axis=0)
    lane_chunks = lambda a: [a[:, r * QBLK:(r + 1) * QBLK] for r in range(REP)]
    gone = lambda cond: jnp.where(cond, 0.0, NEG)

    assert B_WINDOW == 2 * KSTEP
    win = _Pipeline(kw_ref, vwt_ref, qw_ref, sw_ref, pw_ref, accw_ref, lambda j: jnp.maximum(n - j, 0))
    slc = _Pipeline(ksl_ref, vslt_ref, qs_ref, ss_ref, ps_ref, accs_ref,
                    lambda j: jnp.where(j == 0, n, jnp.maximum(j - 1, 0)))
    slc_count = jnp.maximum(n, 1) + 1

    for g in groups:
        qw_ref[g] = rows(qnr_ref, g)
    parts = range(QBLK // A_WINDOW)
    a_chunk = [n * len(parts) + i for i in parts]
    a_first = [jnp.maximum(c - 1, 0) for c in a_chunk]
    part_rows = lambda i, g: jnp.concatenate(
        [qa_ref[i * A_WINDOW:(i + 1) * A_WINDOW, h * LANES:(h + 1) * LANES] for h in heads(g)], axis=0)
    s_swa = [[_dot_nt(ka_ref[pl.ds(pl.multiple_of(a_first[i] * A_WINDOW, A_WINDOW), 2 * A_WINDOW),
                             g * LANES:(g + 1) * LANES], part_rows(i, g)) for g in groups] for i in parts]
    s_cmp = [_dot_nt(kcmp_ref[g], rows(qn_ref, g)) for g in groups]
    w0 = [win.qk(g, 0) for g in groups]
    w1 = [win.qk(g, 1) for g in groups]

    e_swa, x_swa = [], []
    for i in parts:
        abias = abias_ref[jnp.minimum(a_chunk[i], 1)]
        for g in groups:
            st = s_swa[i][g] + abias
            sk = jnp.concatenate([jnp.full((1, A_WINDOW), sink_ref[h] * LOG2E, F32) for h in heads(g)], axis=1)
            m = jnp.maximum(jnp.max(st, axis=0, keepdims=True), sk)
            e_swa.append(jnp.exp2(st - m).astype(BF16))
            x_swa.append(jnp.exp2(sk - m))

    cstart = pl.multiple_of(cbias_ref.shape[0] - nc - n * (QBLK // CMP_STRIDE), 8)
    cbias = cbias_ref[pl.ds(cstart, nc), :]
    tq = t0 + (lax.broadcasted_iota(jnp.int32, (1, width), 1) & (QBLK - 1))
    sees_any = tq >= CMP_BLOCK - 1
    p_cmp = []
    for g in groups:
        s = s_cmp[g] + cbias
        e = jnp.exp2(s - jnp.max(s, axis=0, keepdims=True))
        inv = jnp.where(sees_any, 1.0 / jnp.maximum(jnp.sum(e, axis=0, keepdims=True), 1e-30), 0.0)
        p_cmp.append(e * inv)

    win_state = win.start(w0, w1, pbias_ref[0], gone(n >= 1))
    win_state = win.step(0, win_state, bias=pbias_ref[1] + gone(n >= 2))

    for i in parts:
        for g in groups:
            vt = jnp.concatenate([vat_ref[a_first[i] + k, g * VROWS:(g + 1) * VROWS, :] for k in range(2)], axis=1)
            o = _dot(vt, e_swa[i * KV_HEADS + g])
            o = o[:HEAD_DIM] * (1.0 / (o[HEAD_DIM:HEAD_DIM + 1] + x_swa[i * KV_HEADS + g]))
            _put_heads(oa_ref.at[i * A_WINDOW:(i + 1) * A_WINDOW, :], g, o, A_WINDOW)
    o_cmp = [_dot(vcmpt_ref[g], p_cmp[g].astype(BF16)) for g in groups]
    imp = []
    for g in groups:
        pc = lane_chunks(p_cmp[g])
        psum = (pc[0] + pc[1]) + (pc[2] + pc[3])
        hi = psum.astype(BF16)
        lo = (psum - hi.astype(F32)).astype(BF16)
        imp.append(_dot(ovt_ref[...], hi) + _dot(ovt_ref[...], lo))
    o_win = win.finish(3, win.step(1, win_state, last=True))

    jb = lax.broadcasted_iota(jnp.int32, (nb, QBLK), 0)
    curb = (t0 + lax.broadcasted_iota(jnp.int32, (nb, QBLK), 1)) >> int(np.log2(SLC_BLOCK))
    sel = lax.cond((t0 + QBLK - 1) // SLC_BLOCK < SLC_TOP_N,
                   lambda: tuple(jnp.where(jb <= curb, 1.0, 0.0) for _ in groups),
                   lambda: tuple(_select_blocks_t(imp[g], t0) for g in groups))

    for g in groups:
        mcols = jnp.concatenate([jnp.zeros((MASK_COL, QBLK), F32), sel[g] - 1.0,
                                 jnp.zeros((LANES - MASK_COL - nb, QBLK), F32)], axis=0)
        mpad = mcols.T.astype(BF16)
        for r, h in enumerate(heads(g)):
            qs_ref[g, r * QBLK:(r + 1) * QBLK, :] = qnr_ref[:, h * LANES:(h + 1) * LANES] + mpad
    s0 = [slc.qk(g, 0) for g in groups]
    s1 = [slc.qk(g, 1) for g in groups]
    slc_state = slc.start(s0, s1, pbias_ref[0], gone(n >= 1))

    no_bias = gone(n >= 0)
    slc_state = lax.fori_loop(0, slc_count - 2, lambda j, s: slc.step(j, s, bias=no_bias), slc_state)

    o_slc = slc.finish(slc_count, slc.step(slc_count - 2, slc_state, last=True))
    for g in groups:
        gate = lambda br: jnp.concatenate(
            [gnt_ref[i, br * Q_HEADS + h:br * Q_HEADS + h + 1, :]
             for h in heads(g) for i in range(gnt_ref.shape[0])], axis=1)
        _put_heads(ob_ref, g, gate(0) * o_cmp[g] + gate(1) * o_slc[g] + gate(2) * o_win[g])


def _attn(sinks, qa, ka, vat, qn, qnr, gnt, kcmp, vcmpt, ksl, vslt, kw, vwt, batch, seq):
    nq = seq // QBLK
    nc = kcmp.shape[2]
    nb = seq // SLC_BLOCK
    width = REP * QBLK
    assert nb <= LANES - MASK_COL and QBLK % SLC_BLOCK == 0 and QBLK % A_WINDOW == 0 and A_WINDOW % LANES == 0
    c_start = np.arange(nc)[None, :] * CMP_STRIDE
    s_start = np.arange(nb)[:, None] * SLC_BLOCK
    ovt = jnp.asarray((c_start < s_start + SLC_BLOCK) & (s_start < c_start + CMP_BLOCK), BF16)
    ql = np.arange(width)[None, :] % QBLK
    masked = lambda visible: np.where(visible, 0.0, NEG)
    kq = np.arange(KSTEP)[:, None] - ql
    pbias = jnp.asarray(np.stack([masked(kq <= 0),
                                  masked(kq > 0)]), F32)
    band = np.arange(2 * A_WINDOW)[:, None]
    qpart = np.arange(REP * A_WINDOW)[None, :] % A_WINDOW
    in_window = lambda dist: (dist >= 0) & (dist < A_WINDOW)
    abias = jnp.asarray(np.stack([masked(in_window(qpart - band)),
                                  masked(in_window(qpart - (band - A_WINDOW)))]), F32)
    per_blk = QBLK // CMP_STRIDE
    d = np.arange(nc + (nq - 1) * per_blk)[:, None] - (nq - 1) * per_blk
    cbias = jnp.asarray(masked(d * CMP_STRIDE + CMP_BLOCK - 1 <= ql), F32)
    q_spec = pl.BlockSpec((QBLK, Q_HEADS * LANES), lambda b, n: (b * nq + n, 0))
    g_spec = pl.BlockSpec((None, QBLK // LANES, LANES, LANES), lambda b, n: (b, n, 0, 0))
    kc_spec = pl.BlockSpec((None, KV_HEADS, nc, LANES), lambda b, n: (b, 0, 0, 0))
    vc_spec = pl.BlockSpec((None, KV_HEADS, HEAD_DIM, nc), lambda b, n: (b, 0, 0, 0))
    const = lambda a: pl.BlockSpec(a.shape, lambda b, n: (0,) * a.ndim, pipeline_mode=pl.Buffered(1))
    k_spec = pl.BlockSpec((seq, KV_HEADS * LANES), lambda b, n: (b, 0))
    vt_spec = lambda keys: pl.BlockSpec((None, seq // keys, KV_HEADS * VROWS, keys), lambda b, n: (b, 0, 0, 0))
    o_spec = pl.BlockSpec((QBLK, Q_DIM), lambda b, n: (b * nq + n, 0))
    o_shape = jax.ShapeDtypeStruct((batch * seq, Q_DIM), BF16)
    pipe_scratch = [pltpu.VMEM((KV_HEADS, width, LANES), BF16),
                    pltpu.VMEM((KV_HEADS, KSTEP, width), F32),
                    pltpu.VMEM((KV_HEADS, KSTEP, width), BF16),
                    pltpu.VMEM((KV_HEADS, VROWS, width), F32)]
    return pl.pallas_call(
        _attn_kernel,
        grid=(batch, nq),
        in_specs=[pl.BlockSpec(memory_space=pltpu.SMEM), const(abias), const(pbias), const(cbias), const(ovt),
                  q_spec, k_spec, vt_spec(A_WINDOW), q_spec, q_spec, g_spec, kc_spec, vc_spec,
                  k_spec, vt_spec(KSTEP), k_spec, vt_spec(KSTEP)],
        out_specs=[o_spec, o_spec],
        out_shape=[o_shape, o_shape],
        scratch_shapes=pipe_scratch + pipe_scratch,
        compiler_params=pltpu.CompilerParams(dimension_semantics=("parallel", "arbitrary"),
                                             vmem_limit_bytes=56 * 1024 * 1024),
        name="attn",
    )(sinks, abias, pbias, cbias, ovt, qa, ka, vat, qn, qnr, gnt, kcmp, vcmpt, ksl, vslt, kw, vwt)


def _layer_norm(r, g, b):
    mu = jnp.mean(r, axis=-1, keepdims=True)
    d = r - mu
    var = jnp.mean(d * d, axis=-1, keepdims=True)
    return d * lax.rsqrt(var + LN_EPS) * g + b


POST_PARTS = 2


def _post_kernel(alpha, x_ref, oa_ref, ob_ref, wgm_ref, wpa_ref, wpb_ref, wout_ref, g_ref, b_ref, h_ref):
    d = x_ref.shape[1]
    rows = [slice(i * (x_ref.shape[0] // POST_PARTS), (i + 1) * (x_ref.shape[0] // POST_PARTS))
            for i in range(POST_PARTS)]
    ys = []
    for r in rows:
        xb = x_ref[r, :].astype(BF16)
        pa = _dot(oa_ref[r, :], wpa_ref[...])
        pb = _dot(ob_ref[r, :], wpb_ref[...])
        ys.append(jax.nn.sigmoid(_dot(xb, wgm_ref[:, :d])) * pa + jax.nn.sigmoid(_dot(xb, wgm_ref[:, d:])) * pb)
    for r, y in zip(rows, ys):
        m = _dot(y.astype(BF16), wout_ref[...])
        h_ref[r, :] = _layer_norm(alpha * x_ref[r, :] + m, g_ref[...], b_ref[...])


def _post(x2, oa, ob, w_gm, wpa, wpb, wout, g, b, alpha, tm):
    n, d = x2.shape
    tok = lambda width: pl.BlockSpec((tm, width), lambda i: (i, 0))
    full = lambda a: pl.BlockSpec(a.shape, lambda i: (0, 0))
    return pl.pallas_call(
        functools.partial(_post_kernel, alpha),
        grid=(n // tm,),
        in_specs=[tok(d), tok(Q_DIM), tok(Q_DIM)] + [full(a) for a in (w_gm, wpa, wpb, wout, g, b)],
        out_specs=tok(d),
        out_shape=jax.ShapeDtypeStruct((n, d), F32),
        compiler_params=pltpu.CompilerParams(dimension_semantics=("parallel",),
                                             vmem_limit_bytes=48 * 1024 * 1024),
        name="post",
    )(x2, oa, ob, w_gm, wpa, wpb, wout, g, b)


FFN_CHUNK = 256


def _ffn_kernel(alpha, h_ref, wg_ref, wu_ref, wd_ref, g_ref, b_ref, o_ref):
    h = h_ref[...]
    hb = h.astype(BF16)
    hidden = wg_ref.shape[1]
    acc = jnp.zeros(h.shape, F32)
    for c in range(hidden // FFN_CHUNK):
        sl = slice(c * FFN_CHUNK, (c + 1) * FFN_CHUNK)
        a = jax.nn.silu(_dot(hb, wg_ref[:, sl])) * _dot(hb, wu_ref[:, sl])
        acc = acc + _dot(a.astype(BF16), wd_ref[sl, :])
    o_ref[...] = _layer_norm(alpha * h + acc, g_ref[...], b_ref[...])


def _ffn(h, wg, wu, wd, g, b, alpha, tm):
    n, d = h.shape
    tok = pl.BlockSpec((tm, d), lambda i: (i, 0))
    once = lambda a: pl.BlockSpec(a.shape, lambda i: (0, 0), pipeline_mode=pl.Buffered(1))
    return pl.pallas_call(
        functools.partial(_ffn_kernel, alpha),
        grid=(n // tm,),
        in_specs=[tok] + [once(a) for a in (wg, wu, wd, g, b)],
        out_specs=tok,
        out_shape=jax.ShapeDtypeStruct((n, d), F32),
        compiler_params=pltpu.CompilerParams(dimension_semantics=("parallel",),
                                             vmem_limit_bytes=56 * 1024 * 1024),
        name="ffn",
    )(h, wg, wu, wd, g, b)


def _position_tables(seq):
    half = HEAD_DIM // 2
    inv = ROPE_THETA ** (-jnp.arange(half, dtype=F32) / half)
    ang = jnp.arange(seq).astype(F32)[:, None] * inv[None, :]
    cos, sin = jnp.cos(ang), jnp.sin(ang)
    reps = LANES // HEAD_DIM
    lane = np.arange(LANES)[None, :]
    blk = (np.arange(seq) // SLC_BLOCK)[:, None]
    ke = jnp.asarray(np.where(lane - MASK_COL == blk, -NEG, 0.0), F32)
    return jnp.tile(cos, (1, 2 * reps)), jnp.tile(jnp.concatenate([-sin, sin], axis=1), (1, reps)), ke


def kernel(x, w_in, sinks, cmp_pe_k, cmp_w1_k, cmp_b1_k, cmp_w2_k, cmp_pe_v, cmp_w1_v, cmp_b1_v, cmp_w2_v,
           w_proj_a, w_proj_b, w_out, ln1_g, ln1_b, w_gate, w_up, w_down, ln2_g, ln2_b):
    batch, seq, d = x.shape
    depth = w_in.shape[0]
    alpha = (2 * depth) ** 0.25
    n_main = 2 * Q_DIM + 8 * KV_DIM
    n_gate = 3 * Q_HEADS
    tm = 512
    cos, sin, ke = _position_tables(seq)
    pad_cols = lambda w: jnp.pad(w, ((0, 0), (0, LANES - w.shape[1]))).astype(BF16)
    xt = x.reshape(batch * seq, d)
    for l in range(depth):
        w_main = w_in[l, :, :n_main].astype(BF16)
        w_gn = pad_cols(w_in[l, :, n_main:n_main + n_gate])
        w_gm = w_in[l, :, n_main + n_gate:].astype(BF16)
        (qa, ka, vat, qn, qnr, kc, vc, ksl, vslt, kw, vwt, gnt) = _in_proj(
            xt, w_main, w_gn, cos, sin, ke, batch, seq, tm)
        wk = (cmp_pe_k[l], cmp_w1_k[l].astype(BF16), cmp_b1_k[l][None, :], pad_cols(cmp_w2_k[l]))
        wv = (cmp_pe_v[l], cmp_w1_v[l].astype(BF16), cmp_b1_v[l][None, :], pad_cols(cmp_w2_v[l]))
        kcmp, vcmpt = _compress(kc, vc, wk, wv, batch, seq)
        oa, ob = _attn(sinks[l], qa, ka, vat, qn, qnr, gnt, kcmp, vcmpt, ksl, vslt, kw, vwt, batch, seq)
        h = _post(xt, oa, ob, w_gm, w_proj_a[l].astype(BF16), w_proj_b[l].astype(BF16), w_out[l].astype(BF16),
                  ln1_g[l][None, :], ln1_b[l][None, :], alpha, tm)
        xt = _ffn(h, w_gate[l].astype(BF16), w_up[l].astype(BF16), w_down[l].astype(BF16),
                  ln2_g[l][None, :], ln2_b[l][None, :], alpha, tm)
    return xt.reshape(batch, seq, d)
```

```python
import functools

import jax
import jax.numpy as jnp
import numpy as np
from jax import lax
from jax.experimental import pallas as pl
from jax.experimental.pallas import tpu as pltpu

HEAD_DIM = 64
ROPE_THETA = 10000.0
Q_HEADS = 8
KV_HEADS = 2
REP = Q_HEADS // KV_HEADS
A_WINDOW = 128
B_WINDOW = 512
CMP_BLOCK = 32
CMP_STRIDE = 16
SLC_BLOCK = 64
SLC_TOP_N = 16
LN_EPS = 1e-5
NEG = -1e30
BIG = 1e9
Q_DIM = Q_HEADS * HEAD_DIM
KV_DIM = KV_HEADS * HEAD_DIM
LANES = 128
QBLK = 256
KSTEP = QBLK
LOG2E = 1.4426950408889634
SCALE = HEAD_DIM ** -0.5 * LOG2E
VROWS = HEAD_DIM + 16

F32 = jnp.float32
BF16 = jnp.bfloat16


def _dot(a, b):
    return jnp.dot(a, b, preferred_element_type=F32)


def _dot_nt(a, b):
    return lax.dot_general(a, b, (((1,), (1,)), ((), ())), preferred_element_type=F32)


def _inproj_kernel(x_ref, w_ref, wg_ref, cos_ref, sin_ref, ke_ref,
                   qa_ref, ka_ref, vat_ref, qn_ref, qnr_ref, kc_ref, vc_ref,
                   ksl_ref, vslt_ref, kw_ref, vwt_ref, gnt_ref):
    tm = x_ref.shape[0]
    xb = x_ref[...].astype(BF16)
    cos = cos_ref[...]
    sin = sin_ref[...]
    lane = lax.broadcasted_iota(jnp.int32, (tm, LANES), 1)
    first_half = (lane & (HEAD_DIM - 1)) < (HEAD_DIM // 2)
    low = lane < HEAD_DIM

    def rope(z):
        sw = jnp.where(first_half, pltpu.roll(z, LANES - HEAD_DIM // 2, 1), pltpu.roll(z, HEAD_DIM // 2, 1))
        return z * cos + sw * sin

    def proj(c0, n):
        return _dot(xb, w_ref[:, c0:c0 + n])

    def chunks(z):
        return [z[:, c * LANES:(c + 1) * LANES] for c in range(z.shape[1] // LANES)]

    def put_padded(ref, c, z, fill):
        ref[:, (2 * c) * LANES:(2 * c + 1) * LANES] = jnp.where(low, z, fill).astype(ref.dtype)
        ref[:, (2 * c + 1) * LANES:(2 * c + 2) * LANES] = jnp.where(low, pltpu.roll(z, HEAD_DIM, 1), fill).astype(ref.dtype)

    def put_transposed(ref, z):
        for j in range(tm // LANES):
            ref[j] = z[j * LANES:(j + 1) * LANES, :].T.astype(ref.dtype)

    ones_row = jnp.where(lax.broadcasted_iota(jnp.int32, (VROWS - HEAD_DIM, LANES), 0) == 0, 1.0, 0.0)

    def put_values_t(ref, z):
        keys = ref.shape[2]
        for j in range(tm // keys):
            for i in range(keys // LANES):
                zt = z[j * keys + i * LANES:j * keys + (i + 1) * LANES, :].T
                cols = slice(i * LANES, (i + 1) * LANES)
                for g in range(KV_HEADS):
                    ref[j, g * VROWS:g * VROWS + HEAD_DIM, cols] = zt[g * HEAD_DIM:(g + 1) * HEAD_DIM, :].astype(ref.dtype)
                    ref[j, g * VROWS + HEAD_DIM:(g + 1) * VROWS, cols] = ones_row.astype(ref.dtype)

    for c, zc in enumerate(chunks(proj(0, Q_DIM))):
        put_padded(qa_ref, c, rope(zc) * SCALE, 0.0)
    z = proj(Q_DIM, 2 * KV_DIM)
    put_padded(ka_ref, 0, rope(z[:, :KV_DIM]), 0.0)
    put_values_t(vat_ref, z[:, KV_DIM:])
    c0 = Q_DIM + 2 * KV_DIM
    for c, zc in enumerate(chunks(proj(c0, Q_DIM))):
        put_padded(qn_ref, c, zc * SCALE, 0.0)
        put_padded(qnr_ref, c, rope(zc) * SCALE, 0.0)
    c0 += Q_DIM
    z = proj(c0, 2 * KV_DIM)
    kc_ref[...] = z[:, :KV_DIM]
    vc_ref[...] = z[:, KV_DIM:]
    c0 += 2 * KV_DIM
    z = proj(c0, 2 * KV_DIM)
    put_padded(ksl_ref, 0, rope(z[:, :KV_DIM]), ke_ref[...])
    put_values_t(vslt_ref, z[:, KV_DIM:])
    c0 += 2 * KV_DIM
    z = proj(c0, 2 * KV_DIM)
    put_padded(kw_ref, 0, rope(z[:, :KV_DIM]), 0.0)
    put_values_t(vwt_ref, z[:, KV_DIM:])
    put_transposed(gnt_ref, jax.nn.sigmoid(_dot(xb, wg_ref[...])))


def _in_proj(x2, w_main, w_gn, cos, sin, ke, batch, seq, tm):
    n = x2.shape[0]
    d = x2.shape[1]
    spt = seq // tm
    tok = lambda width: pl.BlockSpec((tm, width), lambda i: (i, 0))
    full = lambda a: pl.BlockSpec(a.shape, lambda i: (0, 0))
    tab = pl.BlockSpec((tm, LANES), lambda i: (i % spt, 0))
    flat = lambda w, dt: (tok(w), jax.ShapeDtypeStruct((n, w), dt))
    trans = lambda r, keys, dt: (pl.BlockSpec((None, tm // keys, r, keys), lambda i: (i // spt, i % spt, 0, 0)),
                                 jax.ShapeDtypeStruct((batch, seq // keys, r, keys), dt))
    outs = [flat(Q_HEADS * LANES, BF16), flat(KV_HEADS * LANES, BF16),
            trans(KV_HEADS * VROWS, A_WINDOW, BF16),
            flat(Q_HEADS * LANES, BF16), flat(Q_HEADS * LANES, BF16),
            flat(KV_DIM, F32), flat(KV_DIM, F32),
            flat(KV_HEADS * LANES, BF16), trans(KV_HEADS * VROWS, KSTEP, BF16),
            flat(KV_HEADS * LANES, BF16), trans(KV_HEADS * VROWS, KSTEP, BF16),
            trans(LANES, LANES, F32)]
    return pl.pallas_call(
        _inproj_kernel,
        grid=(n // tm,),
        in_specs=[tok(d), full(w_main), full(w_gn), tab, tab, tab],
        out_specs=[o[0] for o in outs],
        out_shape=[o[1] for o in outs],
        compiler_params=pltpu.CompilerParams(dimension_semantics=("parallel",),
                                             vmem_limit_bytes=48 * 1024 * 1024),
        name="in_proj",
    )(x2, w_main, w_gn, cos, sin, ke)


def _compress_one(src_ref, pe_ref, w1_ref, b1_ref, w2_ref):
    ratio = CMP_BLOCK // CMP_STRIDE
    nchunk = src_ref.shape[0] // CMP_STRIDE
    hid = w1_ref.shape[1]
    parts = [jnp.zeros((KV_HEADS * nchunk, hid), F32) for _ in range(ratio)]
    for l in range(CMP_STRIDE):
        rows = src_ref[pl.ds(l, nchunk, stride=CMP_STRIDE), :]
        rows = jnp.concatenate([rows[:, g * HEAD_DIM:(g + 1) * HEAD_DIM] for g in range(KV_HEADS)], axis=0)
        for j in range(ratio):
            p = j * CMP_STRIDE + l
            a = (rows + pe_ref[p:p + 1, :]).astype(BF16)
            parts[j] = parts[j] + _dot(a, w1_ref[p * HEAD_DIM:(p + 1) * HEAD_DIM, :])
    out = []
    for g in range(KV_HEADS):
        h = parts[0][g * nchunk:(g + 1) * nchunk]
        for j in range(1, ratio):
            h = h + pltpu.roll(parts[j][g * nchunk:(g + 1) * nchunk], nchunk - j, 0)
        h = jax.nn.gelu(h + b1_ref[...])
        out.append(_dot(h.astype(BF16), w2_ref[...]))
    return out


def _compress_kernel(kc_ref, vc_ref, pek_ref, w1k_ref, b1k_ref, w2k_ref,
                     pev_ref, w1v_ref, b1v_ref, w2v_ref, kcmp_ref, vcmpt_ref):
    for g, kc in enumerate(_compress_one(kc_ref, pek_ref, w1k_ref, b1k_ref, w2k_ref)):
        kcmp_ref[g] = kc.astype(BF16)
    for g, vc in enumerate(_compress_one(vc_ref, pev_ref, w1v_ref, b1v_ref, w2v_ref)):
        vcmpt_ref[g] = vc.T[:HEAD_DIM, :].astype(BF16)


def _compress(kc, vc, wk, wv, batch, seq):
    nchunk = seq // CMP_STRIDE
    src = pl.BlockSpec((seq, KV_DIM), lambda b: (b, 0))
    full = lambda a: pl.BlockSpec(a.shape, lambda b: (0,) * a.ndim)
    return pl.pallas_call(
        _compress_kernel,
        grid=(batch,),
        in_specs=[src, src] + [full(a) for a in wk] + [full(a) for a in wv],
        out_specs=[pl.BlockSpec((None, KV_HEADS, nchunk, LANES), lambda b: (b, 0, 0, 0)),
                   pl.BlockSpec((None, KV_HEADS, HEAD_DIM, nchunk), lambda b: (b, 0, 0, 0))],
        out_shape=[jax.ShapeDtypeStruct((batch, KV_HEADS, nchunk, LANES), BF16),
                   jax.ShapeDtypeStruct((batch, KV_HEADS, HEAD_DIM, nchunk), BF16)],
        compiler_params=pltpu.CompilerParams(dimension_semantics=("parallel",)),
        name="compress",
    )(kc, vc, *wk, *wv)


MASK_COL = HEAD_DIM


def _select_blocks_t(imp, t0):
    nb, cols = imp.shape
    j = lax.broadcasted_iota(jnp.int32, (nb, cols), 0)
    cur = (t0 + lax.broadcasted_iota(jnp.int32, (nb, cols), 1)) >> int(np.log2(SLC_BLOCK))
    forced = (j == 0) | (j == cur) | (j == cur - 1)
    score = jnp.where(forced, BIG, jnp.where(j <= cur, imp, -BIG))
    sub = 8
    tiles = [score[v * sub:(v + 1) * sub] for v in range(nb // sub)]
    ranks = [jnp.zeros((sub, cols), F32) for _ in tiles]
    jl = lax.broadcasted_iota(jnp.int32, (sub, cols), 0)
    for jp in range(nb):
        row = score[jp:jp + 1, :]
        for v, tile in enumerate(tiles):
            if v * sub > jp:
                beats = row >= tile
            elif (v + 1) * sub - 1 <= jp:
                beats = row > tile
            else:
                beats = (row > tile) | ((row == tile) & (jl > jp - v * sub))
            ranks[v] = jnp.where(beats, ranks[v] + 1.0, ranks[v])
    rank = jnp.concatenate(ranks, axis=0)
    return jnp.where((rank < float(min(SLC_TOP_N, nb))) & (j <= cur), 1.0, 0.0)


def _put_heads(o_ref, g, o, nq=QBLK):
    for i in range(REP // 2):
        pair = jnp.concatenate([o[:, (2 * i) * nq:(2 * i + 1) * nq],
                                o[:, (2 * i + 1) * nq:(2 * i + 2) * nq]], axis=0)
        col = (g * REP + 2 * i) * HEAD_DIM
        o_ref[:, col:col + LANES] = pair.T.astype(o_ref.dtype)


class _Pipeline:
    def __init__(self, k_ref, vt_ref, q_ref, s_ref, p_ref, acc_ref, chunk_of):
        self.k_ref, self.vt_ref, self.q_ref = k_ref, vt_ref, q_ref
        self.s_ref, self.p_ref, self.acc_ref = s_ref, p_ref, acc_ref
        self.chunk_of = chunk_of
        self.groups = range(KV_HEADS)

    def qk(self, g, j):
        start = pl.multiple_of(self.chunk_of(j) * KSTEP, KSTEP)
        return _dot_nt(self.k_ref[pl.ds(start, KSTEP), g * LANES:(g + 1) * LANES], self.q_ref[g])

    def keep(self, g, st, bias):
        if bias is not None:
            st = st + bias
        st = st.astype(BF16)
        self.s_ref[g] = st
        return jnp.max(st, axis=0, keepdims=True).astype(F32)

    def softmax(self, g, m, cmax):
        m_new = jnp.maximum(m, cmax)
        a = jnp.exp2(m - m_new)
        self.p_ref[g] = jnp.exp2(self.s_ref[g] - m_new.astype(BF16))
        return m_new, a

    def values(self, g, j, a):
        vt = self.vt_ref[self.chunk_of(j), g * VROWS:(g + 1) * VROWS, :]
        self.acc_ref[g] = a * self.acc_ref[g] + _dot(vt, self.p_ref[g])

    def start(self, bias0, bias1):
        m0 = jnp.full((1, self.s_ref.shape[2]), NEG, F32)
        cmax = [self.keep(g, self.qk(g, 0), bias0) for g in self.groups]
        st1 = [self.qk(g, 1) for g in self.groups]
        sm = [self.softmax(g, m0, cmax[g]) for g in self.groups]
        cmax = tuple(self.keep(g, st1[g], bias1) for g in self.groups)
        for g in self.groups:
            self.acc_ref[g] = jnp.zeros(self.acc_ref.shape[1:], F32)
        return (tuple(s[0] for s in sm), tuple(s[1] for s in sm), cmax)

    def step(self, j, state, last=False, bias=None):
        m, a, cmax = state
        st = None if last else [self.qk(g, j + 2) for g in self.groups]
        for g in self.groups:
            self.values(g, j, a[g])
        sm = [self.softmax(g, m[g], cmax[g]) for g in self.groups]
        if not last:
            cmax = tuple(self.keep(g, st[g], bias) for g in self.groups)
        return (tuple(s[0] for s in sm), tuple(s[1] for s in sm), cmax)

    def finish(self, count, state):
        _, a, _ = state
        out = []
        for g in self.groups:
            self.values(g, count - 1, a[g])
            acc = self.acc_ref[g]
            out.append(acc[:HEAD_DIM] * (1.0 / acc[HEAD_DIM:HEAD_DIM + 1]))
        return out


def _attn_kernel(sink_ref, abias_ref, pbias_ref, cbias_ref, ovt_ref,
                 qa_ref, ka_ref, vat_ref, qn_ref, qnr_ref, gnt_ref, kcmp_ref, vcmpt_ref,
                 ksl_ref, vslt_ref, kw_ref, vwt_ref,
                 oa_ref, ob_ref,
                 qw_ref, sw_ref, pw_ref, accw_ref, qs_ref, ss_ref, ps_ref, accs_ref, sa_ref, pa_ref, sc_ref):
    n = pl.program_id(1)
    groups = range(KV_HEADS)
    width = REP * QBLK
    nb = ovt_ref.shape[0]
    nc = kcmp_ref.shape[1]
    t0 = pl.multiple_of(n * QBLK, QBLK)
    heads = lambda g: [g * REP + r for r in range(REP)]
    rows = lambda ref, g: jnp.concatenate([ref[:, h * LANES:(h + 1) * LANES] for h in heads(g)], axis=0)
    lane_chunks = lambda a: [a[:, r * QBLK:(r + 1) * QBLK] for r in range(REP)]
    gone = lambda cond: jnp.where(cond, 0.0, NEG)

    assert B_WINDOW == 2 * KSTEP
    win = _Pipeline(kw_ref, vwt_ref, qw_ref, sw_ref, pw_ref, accw_ref, lambda j: jnp.maximum(n - j, 0))
    slc = _Pipeline(ksl_ref, vslt_ref, qs_ref, ss_ref, ps_ref, accs_ref,
                    lambda j: jnp.where(j == 0, n, jnp.maximum(j - 1, 0)))
    slc_count = jnp.maximum(n, 1) + 1

    for g in groups:
        qw_ref[g] = rows(qnr_ref, g)
    parts = range(QBLK // A_WINDOW)
    a_chunk = [n * len(parts) + i for i in parts]
    a_first = [jnp.maximum(c - 1, 0) for c in a_chunk]
    part_rows = lambda i, g: jnp.concatenate(
        [qa_ref[i * A_WINDOW:(i + 1) * A_WINDOW, h * LANES:(h + 1) * LANES] for h in heads(g)], axis=0)
    amax = {}
    for i in parts:
        abias = abias_ref[jnp.minimum(a_chunk[i], 1)]
        astart = pl.multiple_of(a_first[i] * A_WINDOW, A_WINDOW)
        for g in groups:
            st = _dot_nt(ka_ref[pl.ds(astart, 2 * A_WINDOW), g * LANES:(g + 1) * LANES], part_rows(i, g)) + abias
            st = st.astype(BF16)
            sa_ref[i, g] = st
            amax[i, g] = jnp.max(st, axis=0, keepdims=True).astype(F32)
    cstart = pl.multiple_of(cbias_ref.shape[0] - nc - n * (QBLK // CMP_STRIDE), 8)
    cbias = cbias_ref[pl.ds(cstart, nc), :]
    cmax = []
    for g in groups:
        s = _dot_nt(kcmp_ref[g], rows(qn_ref, g)) + cbias
        sc_ref[g] = s
        cmax.append(jnp.max(s, axis=0, keepdims=True))
    win_state = win.start(pbias_ref[0], gone(n >= 1))

    x_swa = {}
    for i in parts:
        for g in groups:
            sk = jnp.concatenate([jnp.full((1, A_WINDOW), sink_ref[h] * LOG2E, F32) for h in heads(g)], axis=1)
            mb = jnp.maximum(amax[i, g], sk).astype(BF16)
            pa_ref[i, g] = jnp.exp2(sa_ref[i, g] - mb)
            x_swa[i, g] = jnp.exp2(sk - mb.astype(F32))
    tq = t0 + (lax.broadcasted_iota(jnp.int32, (1, width), 1) & (QBLK - 1))
    sees_any = tq >= CMP_BLOCK - 1
    p_cmp = []
    for g in groups:
        e = jnp.exp2(sc_ref[g] - cmax[g])
        inv = jnp.where(sees_any, 1.0 / jnp.maximum(jnp.sum(e, axis=0, keepdims=True), 1e-30), 0.0)
        p_cmp.append(e * inv)

    win_state = win.step(0, win_state, bias=pbias_ref[1] + gone(n >= 2))

    for i in parts:
        for g in groups:
            vt = jnp.concatenate([vat_ref[a_first[i] + k, g * VROWS:(g + 1) * VROWS, :] for k in range(2)], axis=1)
            o = _dot(vt, pa_ref[i, g])
            o = o[:HEAD_DIM] * (1.0 / (o[HEAD_DIM:HEAD_DIM + 1] + x_swa[i, g]))
            _put_heads(oa_ref.at[i * A_WINDOW:(i + 1) * A_WINDOW, :], g, o, A_WINDOW)
    o_cmp = [_dot(vcmpt_ref[g], p_cmp[g].astype(BF16)) for g in groups]
    imp = []
    for g in groups:
        pc = lane_chunks(p_cmp[g])
        psum = (pc[0] + pc[1]) + (pc[2] + pc[3])
        hi = psum.astype(BF16)
        lo = (psum - hi.astype(F32)).astype(BF16)
        imp.append(_dot(ovt_ref[...], hi) + _dot(ovt_ref[...], lo))
    o_win = win.finish(3, win.step(1, win_state, last=True))

    jb = lax.broadcasted_iota(jnp.int32, (nb, QBLK), 0)
    curb = (t0 + lax.broadcasted_iota(jnp.int32, (nb, QBLK), 1)) >> int(np.log2(SLC_BLOCK))
    sel = lax.cond((t0 + QBLK - 1) // SLC_BLOCK < SLC_TOP_N,
                   lambda: tuple(jnp.where(jb <= curb, 1.0, 0.0) for _ in groups),
                   lambda: tuple(_select_blocks_t(imp[g], t0) for g in groups))

    for g in groups:
        mcols = jnp.concatenate([jnp.zeros((MASK_COL, QBLK), F32), sel[g] - 1.0,
                                 jnp.zeros((LANES - MASK_COL - nb, QBLK), F32)], axis=0)
        mpad = mcols.T.astype(BF16)
        for r, h in enumerate(heads(g)):
            qs_ref[g, r * QBLK:(r + 1) * QBLK, :] = qnr_ref[:, h * LANES:(h + 1) * LANES] + mpad
    slc_state = slc.start(pbias_ref[0], gone(n >= 1))

    no_bias = gone(n >= 0)
    slc_state = lax.fori_loop(0, slc_count - 2, lambda j, s: slc.step(j, s, bias=no_bias), slc_state)

    o_slc = slc.finish(slc_count, slc.step(slc_count - 2, slc_state, last=True))
    for g in groups:
        gate = lambda br: jnp.concatenate(
            [gnt_ref[i, br * Q_HEADS + h:br * Q_HEADS + h + 1, :]
             for h in heads(g) for i in range(gnt_ref.shape[0])], axis=1)
        _put_heads(ob_ref, g, gate(0) * o_cmp[g] + gate(1) * o_slc[g] + gate(2) * o_win[g])


def _attn(sinks, qa, ka, vat, qn, qnr, gnt, kcmp, vcmpt, ksl, vslt, kw, vwt, batch, seq):
    nq = seq // QBLK
    nc = kcmp.shape[2]
    nb = seq // SLC_BLOCK
    width = REP * QBLK
    assert nb <= LANES - MASK_COL and QBLK % SLC_BLOCK == 0 and QBLK % A_WINDOW == 0 and A_WINDOW % LANES == 0
    c_start = np.arange(nc)[None, :] * CMP_STRIDE
    s_start = np.arange(nb)[:, None] * SLC_BLOCK
    ovt = jnp.asarray((c_start < s_start + SLC_BLOCK) & (s_start < c_start + CMP_BLOCK), BF16)
    ql = np.arange(width)[None, :] % QBLK
    masked = lambda visible: np.where(visible, 0.0, NEG)
    kq = np.arange(KSTEP)[:, None] - ql
    pbias = jnp.asarray(np.stack([masked(kq <= 0),
                                  masked(kq > 0)]), F32)
    band = np.arange(2 * A_WINDOW)[:, None]
    qpart = np.arange(REP * A_WINDOW)[None, :] % A_WINDOW
    in_window = lambda dist: (dist >= 0) & (dist < A_WINDOW)
    abias = jnp.asarray(np.stack([masked(in_window(qpart - band)),
                                  masked(in_window(qpart - (band - A_WINDOW)))]), F32)
    per_blk = QBLK // CMP_STRIDE
    d = np.arange(nc + (nq - 1) * per_blk)[:, None] - (nq - 1) * per_blk
    cbias = jnp.asarray(masked(d * CMP_STRIDE + CMP_BLOCK - 1 <= ql), F32)
    q_spec = pl.BlockSpec((QBLK, Q_HEADS * LANES), lambda b, n: (b * nq + n, 0))
    g_spec = pl.BlockSpec((None, QBLK // LANES, LANES, LANES), lambda b, n: (b, n, 0, 0))
    kc_spec = pl.BlockSpec((None, KV_HEADS, nc, LANES), lambda b, n: (b, 0, 0, 0))
    vc_spec = pl.BlockSpec((None, KV_HEADS, HEAD_DIM, nc), lambda b, n: (b, 0, 0, 0))
    const = lambda a: pl.BlockSpec(a.shape, lambda b, n: (0,) * a.ndim, pipeline_mode=pl.Buffered(1))
    k_spec = pl.BlockSpec((seq, KV_HEADS * LANES), lambda b, n: (b, 0))
    vt_spec = lambda keys: pl.BlockSpec((None, seq // keys, KV_HEADS * VROWS, keys), lambda b, n: (b, 0, 0, 0))
    o_spec = pl.BlockSpec((QBLK, Q_DIM), lambda b, n: (b * nq + n, 0))
    o_shape = jax.ShapeDtypeStruct((batch * seq, Q_DIM), BF16)
    pipe_scratch = [pltpu.VMEM((KV_HEADS, width, LANES), BF16),
                    pltpu.VMEM((KV_HEADS, KSTEP, width), BF16),
                    pltpu.VMEM((KV_HEADS, KSTEP, width), BF16),
                    pltpu.VMEM((KV_HEADS, VROWS, width), F32)]
    return pl.pallas_call(
        _attn_kernel,
        grid=(batch, nq),
        in_specs=[pl.BlockSpec(memory_space=pltpu.SMEM), const(abias), const(pbias), const(cbias), const(ovt),
                  q_spec, k_spec, vt_spec(A_WINDOW), q_spec, q_spec, g_spec, kc_spec, vc_spec,
                  k_spec, vt_spec(KSTEP), k_spec, vt_spec(KSTEP)],
        out_specs=[o_spec, o_spec],
        out_shape=[o_shape, o_shape],
        scratch_shapes=pipe_scratch + pipe_scratch + [
            pltpu.VMEM((QBLK // A_WINDOW, KV_HEADS, 2 * A_WINDOW, REP * A_WINDOW), BF16),
            pltpu.VMEM((QBLK // A_WINDOW, KV_HEADS, 2 * A_WINDOW, REP * A_WINDOW), BF16),
            pltpu.VMEM((KV_HEADS, nc, width), F32)],
        compiler_params=pltpu.CompilerParams(dimension_semantics=("parallel", "arbitrary"),
                                             vmem_limit_bytes=56 * 1024 * 1024),
        name="attn",
    )(sinks, abias, pbias, cbias, ovt, qa, ka, vat, qn, qnr, gnt, kcmp, vcmpt, ksl, vslt, kw, vwt)


def _layer_norm(r, g, b):
    mu = jnp.mean(r, axis=-1, keepdims=True)
    d = r - mu
    var = jnp.mean(d * d, axis=-1, keepdims=True)
    return d * lax.rsqrt(var + LN_EPS) * g + b


POST_PARTS = 2


def _post_kernel(alpha, x_ref, oa_ref, ob_ref, wgm_ref, wpa_ref, wpb_ref, wout_ref, g_ref, b_ref, h_ref):
    d = x_ref.shape[1]
    rows = [slice(i * (x_ref.shape[0] // POST_PARTS), (i + 1) * (x_ref.shape[0] // POST_PARTS))
            for i in range(POST_PARTS)]
    ys = []
    for r in rows:
        xb = x_ref[r, :].astype(BF16)
        pa = _dot(oa_ref[r, :], wpa_ref[...])
        pb = _dot(ob_ref[r, :], wpb_ref[...])
        ys.append(jax.nn.sigmoid(_dot(xb, wgm_ref[:, :d])) * pa + jax.nn.sigmoid(_dot(xb, wgm_ref[:, d:])) * pb)
    for r, y in zip(rows, ys):
        m = _dot(y.astype(BF16), wout_ref[...])
        h_ref[r, :] = _layer_norm(alpha * x_ref[r, :] + m, g_ref[...], b_ref[...])


def _post(x2, oa, ob, w_gm, wpa, wpb, wout, g, b, alpha, tm):
    n, d = x2.shape
    tok = lambda width: pl.BlockSpec((tm, width), lambda i: (i, 0))
    full = lambda a: pl.BlockSpec(a.shape, lambda i: (0, 0))
    return pl.pallas_call(
        functools.partial(_post_kernel, alpha),
        grid=(n // tm,),
        in_specs=[tok(d), tok(Q_DIM), tok(Q_DIM)] + [full(a) for a in (w_gm, wpa, wpb, wout, g, b)],
        out_specs=tok(d),
        out_shape=jax.ShapeDtypeStruct((n, d), F32),
        compiler_params=pltpu.CompilerParams(dimension_semantics=("parallel",),
                                             vmem_limit_bytes=48 * 1024 * 1024),
        name="post",
    )(x2, oa, ob, w_gm, wpa, wpb, wout, g, b)


FFN_CHUNK = 256


def _ffn_kernel(alpha, h_ref, wg_ref, wu_ref, wd_ref, g_ref, b_ref, o_ref):
    h = h_ref[...]
    hb = h.astype(BF16)
    hidden = wg_ref.shape[1]
    acc = jnp.zeros(h.shape, F32)
    for c in range(hidden // FFN_CHUNK):
        sl = slice(c * FFN_CHUNK, (c + 1) * FFN_CHUNK)
        a = jax.nn.silu(_dot(hb, wg_ref[:, sl])) * _dot(hb, wu_ref[:, sl])
        acc = acc + _dot(a.astype(BF16), wd_ref[sl, :])
    o_ref[...] = _layer_norm(alpha * h + acc, g_ref[...], b_ref[...])


def _ffn(h, wg, wu, wd, g, b, alpha, tm):
    n, d = h.shape
    tok = pl.BlockSpec((tm, d), lambda i: (i, 0))
    once = lambda a: pl.BlockSpec(a.shape, lambda i: (0, 0), pipeline_mode=pl.Buffered(1))
    return pl.pallas_call(
        functools.partial(_ffn_kernel, alpha),
        grid=(n // tm,),
        in_specs=[tok] + [once(a) for a in (wg, wu, wd, g, b)],
        out_specs=tok,
        out_shape=jax.ShapeDtypeStruct((n, d), F32),
        compiler_params=pltpu.CompilerParams(dimension_semantics=("parallel",),
                                             vmem_limit_bytes=56 * 1024 * 1024),
        name="ffn",
    )(h, wg, wu, wd, g, b)


def _position_tables(seq):
    half = HEAD_DIM // 2
    inv = ROPE_THETA ** (-jnp.arange(half, dtype=F32) / half)
    ang = jnp.arange(seq).astype(F32)[:, None] * inv[None, :]
    cos, sin = jnp.cos(ang), jnp.sin(ang)
    reps = LANES // HEAD_DIM
    lane = np.arange(LANES)[None, :]
    blk = (np.arange(seq) // SLC_BLOCK)[:, None]
    ke = jnp.asarray(np.where(lane - MASK_COL == blk, -NEG, 0.0), F32)
    return jnp.tile(cos, (1, 2 * reps)), jnp.tile(jnp.concatenate([-sin, sin], axis=1), (1, reps)), ke


def kernel(x, w_in, sinks, cmp_pe_k, cmp_w1_k, cmp_b1_k, cmp_w2_k, cmp_pe_v, cmp_w1_v, cmp_b1_v, cmp_w2_v,
           w_proj_a, w_proj_b, w_out, ln1_g, ln1_b, w_gate, w_up, w_down, ln2_g, ln2_b):
    batch, seq, d = x.shape
    depth = w_in.shape[0]
    alpha = (2 * depth) ** 0.25
    n_main = 2 * Q_DIM + 8 * KV_DIM
    n_gate = 3 * Q_HEADS
    tm = 512
    cos, sin, ke = _position_tables(seq)
    pad_cols = lambda w: jnp.pad(w, ((0, 0), (0, LANES - w.shape[1]))).astype(BF16)
    xt = x.reshape(batch * seq, d)
    for l in range(depth):
        w_main = w_in[l, :, :n_main].astype(BF16)
        w_gn = pad_cols(w_in[l, :, n_main:n_main + n_gate])
        w_gm = w_in[l, :, n_main + n_gate:].astype(BF16)
        (qa, ka, vat, qn, qnr, kc, vc, ksl, vslt, kw, vwt, gnt) = _in_proj(
            xt, w_main, w_gn, cos, sin, ke, batch, seq, tm)
        wk = (cmp_pe_k[l], cmp_w1_k[l].astype(BF16), cmp_b1_k[l][None, :], pad_cols(cmp_w2_k[l]))
        wv = (cmp_pe_v[l], cmp_w1_v[l].astype(BF16), cmp_b1_v[l][None, :], pad_cols(cmp_w2_v[l]))
        kcmp, vcmpt = _compress(kc, vc, wk, wv, batch, seq)
        oa, ob = _attn(sinks[l], qa, ka, vat, qn, qnr, gnt, kcmp, vcmpt, ksl, vslt, kw, vwt, batch, seq)
        h = _post(xt, oa, ob, w_gm, w_proj_a[l].astype(BF16), w_proj_b[l].astype(BF16), w_out[l].astype(BF16),
                  ln1_g[l][None, :], ln1_b[l][None, :], alpha, tm)
        xt = _ffn(h, w_gate[l].astype(BF16), w_up[l].astype(BF16), w_down[l].astype(BF16),
                  ln2_g[l][None, :], ln2_b[l][None, :], alpha, tm)
    return xt.reshape(batch, seq, d)
```

```python
import functools

import jax
import jax.numpy as jnp
import numpy as np
from jax import lax
from jax.experimental import pallas as pl
from jax.experimental.pallas import tpu as pltpu

HEAD_DIM = 64
ROPE_THETA = 10000.0
Q_HEADS = 8
KV_HEADS = 2
REP = Q_HEADS // KV_HEADS
A_WINDOW = 128
B_WINDOW = 512
CMP_BLOCK = 32
CMP_STRIDE = 16
SLC_BLOCK = 64
SLC_TOP_N = 16
LN_EPS = 1e-5
NEG = -1e30
BIG = 1e9
Q_DIM = Q_HEADS * HEAD_DIM
KV_DIM = KV_HEADS * HEAD_DIM
LANES = 128
QBLK = 256
KSTEP = QBLK
LOG2E = 1.4426950408889634
SCALE = HEAD_DIM ** -0.5 * LOG2E
VROWS = HEAD_DIM + 16

F32 = jnp.float32
BF16 = jnp.bfloat16


def _dot(a, b):
    return jnp.dot(a, b, preferred_element_type=F32)


def _inproj_kernel(x_ref, w_ref, wg_ref, cos_ref, sin_ref, ke_ref,
                   qa_ref, ka_ref, vat_ref, qn_ref, qnr_ref, kc_ref, vc_ref,
                   ksl_ref, vslt_ref, kw_ref, vwt_ref, gnt_ref):
    tm = x_ref.shape[0]
    xb = x_ref[...].astype(BF16)
    cos = cos_ref[...]
    sin = sin_ref[...]
    lane = lax.broadcasted_iota(jnp.int32, (tm, LANES), 1)
    first_half = (lane & (HEAD_DIM - 1)) < (HEAD_DIM // 2)
    low = lane < HEAD_DIM

    def rope(z):
        sw = jnp.where(first_half, pltpu.roll(z, LANES - HEAD_DIM // 2, 1), pltpu.roll(z, HEAD_DIM // 2, 1))
        return z * cos + sw * sin

    def proj(c0, n):
        return _dot(xb, w_ref[:, c0:c0 + n])

    def chunks(z):
        return [z[:, c * LANES:(c + 1) * LANES] for c in range(z.shape[1] // LANES)]

    def put_padded(ref, c, z, fill):
        ref[:, (2 * c) * LANES:(2 * c + 1) * LANES] = jnp.where(low, z, fill).astype(ref.dtype)
        ref[:, (2 * c + 1) * LANES:(2 * c + 2) * LANES] = jnp.where(low, pltpu.roll(z, HEAD_DIM, 1), fill).astype(ref.dtype)

    def put_transposed(ref, z):
        for j in range(tm // LANES):
            ref[j] = z[j * LANES:(j + 1) * LANES, :].T.astype(ref.dtype)

    def put_heads_t(ref, c, z):
        for j in range(tm // LANES):
            ref[j, c * LANES:(c + 1) * LANES, :] = z[j * LANES:(j + 1) * LANES, :].T.astype(ref.dtype)

    ones_row = jnp.where(lax.broadcasted_iota(jnp.int32, (VROWS - HEAD_DIM, LANES), 0) == 0, 1.0, 0.0)

    def put_values_t(ref, z):
        keys = ref.shape[2]
        for j in range(tm // keys):
            for i in range(keys // LANES):
                zt = z[j * keys + i * LANES:j * keys + (i + 1) * LANES, :].T
                cols = slice(i * LANES, (i + 1) * LANES)
                for g in range(KV_HEADS):
                    ref[j, g * VROWS:g * VROWS + HEAD_DIM, cols] = zt[g * HEAD_DIM:(g + 1) * HEAD_DIM, :].astype(ref.dtype)
                    ref[j, g * VROWS + HEAD_DIM:(g + 1) * VROWS, cols] = ones_row.astype(ref.dtype)

    for c, zc in enumerate(chunks(proj(0, Q_DIM))):
        put_heads_t(qa_ref, c, rope(zc) * SCALE)
    z = proj(Q_DIM, 2 * KV_DIM)
    put_padded(ka_ref, 0, rope(z[:, :KV_DIM]), 0.0)
    put_values_t(vat_ref, z[:, KV_DIM:])
    c0 = Q_DIM + 2 * KV_DIM
    for c, zc in enumerate(chunks(proj(c0, Q_DIM))):
        put_heads_t(qn_ref, c, zc * SCALE)
        put_heads_t(qnr_ref, c, rope(zc) * SCALE)
    c0 += Q_DIM
    z = proj(c0, 2 * KV_DIM)
    kc_ref[...] = z[:, :KV_DIM]
    vc_ref[...] = z[:, KV_DIM:]
    c0 += 2 * KV_DIM
    z = proj(c0, 2 * KV_DIM)
    put_padded(ksl_ref, 0, rope(z[:, :KV_DIM]), ke_ref[...])
    put_values_t(vslt_ref, z[:, KV_DIM:])
    c0 += 2 * KV_DIM
    z = proj(c0, 2 * KV_DIM)
    put_padded(kw_ref, 0, rope(z[:, :KV_DIM]), 0.0)
    put_values_t(vwt_ref, z[:, KV_DIM:])
    put_transposed(gnt_ref, jax.nn.sigmoid(_dot(xb, wg_ref[...])))


def _in_proj(x2, w_main, w_gn, cos, sin, ke, batch, seq, tm):
    n = x2.shape[0]
    d = x2.shape[1]
    spt = seq // tm
    tok = lambda width: pl.BlockSpec((tm, width), lambda i: (i, 0))
    full = lambda a: pl.BlockSpec(a.shape, lambda i: (0, 0))
    tab = pl.BlockSpec((tm, LANES), lambda i: (i % spt, 0))
    flat = lambda w, dt: (tok(w), jax.ShapeDtypeStruct((n, w), dt))
    trans = lambda r, keys, dt: (pl.BlockSpec((None, tm // keys, r, keys), lambda i: (i // spt, i % spt, 0, 0)),
                                 jax.ShapeDtypeStruct((batch, seq // keys, r, keys), dt))
    outs = [trans(Q_DIM, LANES, BF16), flat(KV_HEADS * LANES, BF16),
            trans(KV_HEADS * VROWS, A_WINDOW, BF16),
            trans(Q_DIM, LANES, BF16), trans(Q_DIM, LANES, BF16),
            flat(KV_DIM, F32), flat(KV_DIM, F32),
            flat(KV_HEADS * LANES, BF16), trans(KV_HEADS * VROWS, KSTEP, BF16),
            flat(KV_HEADS * LANES, BF16), trans(KV_HEADS * VROWS, KSTEP, BF16),
            trans(LANES, LANES, F32)]
    return pl.pallas_call(
        _inproj_kernel,
        grid=(n // tm,),
        in_specs=[tok(d), full(w_main), full(w_gn), tab, tab, tab],
        out_specs=[o[0] for o in outs],
        out_shape=[o[1] for o in outs],
        compiler_params=pltpu.CompilerParams(dimension_semantics=("parallel",),
                                             vmem_limit_bytes=48 * 1024 * 1024),
        name="in_proj",
    )(x2, w_main, w_gn, cos, sin, ke)


def _compress_one(src_ref, pe_ref, w1_ref, b1_ref, w2_ref):
    ratio = CMP_BLOCK // CMP_STRIDE
    nchunk = src_ref.shape[0] // CMP_STRIDE
    hid = w1_ref.shape[1]
    parts = [jnp.zeros((KV_HEADS * nchunk, hid), F32) for _ in range(ratio)]
    for l in range(CMP_STRIDE):
        rows = src_ref[pl.ds(l, nchunk, stride=CMP_STRIDE), :]
        rows = jnp.concatenate([rows[:, g * HEAD_DIM:(g + 1) * HEAD_DIM] for g in range(KV_HEADS)], axis=0)
        for j in range(ratio):
            p = j * CMP_STRIDE + l
            a = (rows + pe_ref[p:p + 1, :]).astype(BF16)
            parts[j] = parts[j] + _dot(a, w1_ref[p * HEAD_DIM:(p + 1) * HEAD_DIM, :])
    out = []
    for g in range(KV_HEADS):
        h = parts[0][g * nchunk:(g + 1) * nchunk]
        for j in range(1, ratio):
            h = h + pltpu.roll(parts[j][g * nchunk:(g + 1) * nchunk], nchunk - j, 0)
        h = jax.nn.gelu(h + b1_ref[...])
        out.append(_dot(h.astype(BF16), w2_ref[...]))
    return out


def _compress_kernel(kc_ref, vc_ref, pek_ref, w1k_ref, b1k_ref, w2k_ref,
                     pev_ref, w1v_ref, b1v_ref, w2v_ref, kcmp_ref, vcmpt_ref):
    for g, kc in enumerate(_compress_one(kc_ref, pek_ref, w1k_ref, b1k_ref, w2k_ref)):
        kcmp_ref[g] = kc.astype(BF16)
    for g, vc in enumerate(_compress_one(vc_ref, pev_ref, w1v_ref, b1v_ref, w2v_ref)):
        vcmpt_ref[g] = vc.T[:HEAD_DIM, :].astype(BF16)


def _compress(kc, vc, wk, wv, batch, seq):
    nchunk = seq // CMP_STRIDE
    src = pl.BlockSpec((seq, KV_DIM), lambda b: (b, 0))
    full = lambda a: pl.BlockSpec(a.shape, lambda b: (0,) * a.ndim)
    return pl.pallas_call(
        _compress_kernel,
        grid=(batch,),
        in_specs=[src, src] + [full(a) for a in wk] + [full(a) for a in wv],
        out_specs=[pl.BlockSpec((None, KV_HEADS, nchunk, LANES), lambda b: (b, 0, 0, 0)),
                   pl.BlockSpec((None, KV_HEADS, HEAD_DIM, nchunk), lambda b: (b, 0, 0, 0))],
        out_shape=[jax.ShapeDtypeStruct((batch, KV_HEADS, nchunk, LANES), BF16),
                   jax.ShapeDtypeStruct((batch, KV_HEADS, HEAD_DIM, nchunk), BF16)],
        compiler_params=pltpu.CompilerParams(dimension_semantics=("parallel",)),
        name="compress",
    )(kc, vc, *wk, *wv)


MASK_COL = HEAD_DIM


def _select_blocks_t(imp, t0):
    nb, cols = imp.shape
    j = lax.broadcasted_iota(jnp.int32, (nb, cols), 0)
    cur = (t0 + lax.broadcasted_iota(jnp.int32, (nb, cols), 1)) >> int(np.log2(SLC_BLOCK))
    forced = (j == 0) | (j == cur) | (j == cur - 1)
    score = jnp.where(forced, BIG, jnp.where(j <= cur, imp, -BIG))
    sub = 8
    tiles = [score[v * sub:(v + 1) * sub] for v in range(nb // sub)]
    ranks = [jnp.zeros((sub, cols), F32) for _ in tiles]
    jl = lax.broadcasted_iota(jnp.int32, (sub, cols), 0)
    for jp in range(nb):
        row = score[jp:jp + 1, :]
        for v, tile in enumerate(tiles):
            if v * sub > jp:
                beats = row >= tile
            elif (v + 1) * sub - 1 <= jp:
                beats = row > tile
            else:
                beats = (row > tile) | ((row == tile) & (jl > jp - v * sub))
            ranks[v] = jnp.where(beats, ranks[v] + 1.0, ranks[v])
    rank = jnp.concatenate(ranks, axis=0)
    return jnp.where((rank < float(min(SLC_TOP_N, nb))) & (j <= cur), 1.0, 0.0)


def _put_heads(o_ref, g, o, nq=QBLK):
    for i in range(REP // 2):
        pair = jnp.concatenate([o[:, (2 * i) * nq:(2 * i + 1) * nq],
                                o[:, (2 * i + 1) * nq:(2 * i + 2) * nq]], axis=0)
        col = (g * REP + 2 * i) * HEAD_DIM
        o_ref[:, col:col + LANES] = pair.T.astype(o_ref.dtype)


class _Pipeline:
    def __init__(self, k_ref, vt_ref, q_ref, s_ref, p_ref, acc_ref, chunk_of):
        self.k_ref, self.vt_ref, self.q_ref = k_ref, vt_ref, q_ref
        self.s_ref, self.p_ref, self.acc_ref = s_ref, p_ref, acc_ref
        self.chunk_of = chunk_of
        self.groups = range(KV_HEADS)

    def qk(self, g, j):
        start = pl.multiple_of(self.chunk_of(j) * KSTEP, KSTEP)
        depth = self.q_ref.shape[1]
        return _dot(self.k_ref[pl.ds(start, KSTEP), g * LANES:g * LANES + depth], self.q_ref[g])

    def keep(self, g, st, bias):
        if bias is not None:
            st = st + bias
        self.s_ref[g] = st
        return jnp.max(st, axis=0, keepdims=True)

    def softmax(self, g, m, cmax):
        m_new = jnp.maximum(m, cmax)
        a = jnp.exp2(m - m_new)
        self.p_ref[g] = jnp.exp2(self.s_ref[g] - m_new).astype(BF16)
        return m_new, a

    def values(self, g, j, a):
        vt = self.vt_ref[self.chunk_of(j), g * VROWS:(g + 1) * VROWS, :]
        self.acc_ref[g] = a * self.acc_ref[g] + _dot(vt, self.p_ref[g])

    def start(self, bias0, bias1):
        m0 = jnp.full((1, self.s_ref.shape[2]), NEG, F32)
        cmax = [self.keep(g, self.qk(g, 0), bias0) for g in self.groups]
        st1 = [self.qk(g, 1) for g in self.groups]
        sm = [self.softmax(g, m0, cmax[g]) for g in self.groups]
        cmax = tuple(self.keep(g, st1[g], bias1) for g in self.groups)
        for g in self.groups:
            self.acc_ref[g] = jnp.zeros(self.acc_ref.shape[1:], F32)
        return (tuple(s[0] for s in sm), tuple(s[1] for s in sm), cmax)

    def step(self, j, state, last=False, bias=None):
        m, a, cmax = state
        st = None if last else [self.qk(g, j + 2) for g in self.groups]
        for g in self.groups:
            self.values(g, j, a[g])
        sm = [self.softmax(g, m[g], cmax[g]) for g in self.groups]
        if not last:
            cmax = tuple(self.keep(g, st[g], bias) for g in self.groups)
        return (tuple(s[0] for s in sm), tuple(s[1] for s in sm), cmax)

    def finish(self, count, state):
        _, a, _ = state
        out = []
        for g in self.groups:
            self.values(g, count - 1, a[g])
            acc = self.acc_ref[g]
            out.append(acc[:HEAD_DIM] * (1.0 / acc[HEAD_DIM:HEAD_DIM + 1]))
        return out


def _attn_kernel(sink_ref, abias_ref, pbias_ref, cbias_ref, ovt_ref,
                 qa_ref, ka_ref, vat_ref, qn_ref, qnr_ref, gnt_ref, kcmp_ref, vcmpt_ref,
                 ksl_ref, vslt_ref, kw_ref, vwt_ref,
                 oa_ref, ob_ref,
                 qw_ref, sw_ref, pw_ref, accw_ref, qs_ref, ss_ref, ps_ref, accs_ref, sa_ref, pa_ref, sc_ref):
    n = pl.program_id(1)
    groups = range(KV_HEADS)
    width = REP * QBLK
    nb = ovt_ref.shape[0]
    nc = kcmp_ref.shape[1]
    t0 = pl.multiple_of(n * QBLK, QBLK)
    heads = lambda g: [g * REP + r for r in range(REP)]
    parts = range(QBLK // LANES)
    head_t = lambda ref, h, i: ref[i, h * HEAD_DIM:(h + 1) * HEAD_DIM, :]
    cols = lambda ref, g: jnp.concatenate([head_t(ref, h, i) for h in heads(g) for i in parts], axis=1)
    lane_chunks = lambda a: [a[:, r * QBLK:(r + 1) * QBLK] for r in range(REP)]
    gone = lambda cond: jnp.where(cond, 0.0, NEG)

    assert B_WINDOW == 2 * KSTEP
    win = _Pipeline(kw_ref, vwt_ref, qw_ref, sw_ref, pw_ref, accw_ref, lambda j: jnp.maximum(n - j, 0))
    slc = _Pipeline(ksl_ref, vslt_ref, qs_ref, ss_ref, ps_ref, accs_ref,
                    lambda j: jnp.where(j == 0, n, jnp.maximum(j - 1, 0)))
    slc_count = jnp.maximum(n, 1) + 1

    for g in groups:
        qw_ref[g] = cols(qnr_ref, g)
    a_chunk = [n * len(parts) + i for i in parts]
    a_first = [jnp.maximum(c - 1, 0) for c in a_chunk]
    part_cols = lambda i, g: jnp.concatenate([head_t(qa_ref, h, i) for h in heads(g)], axis=1)
    amax = {}
    for i in parts:
        abias = abias_ref[jnp.minimum(a_chunk[i], 1)]
        astart = pl.multiple_of(a_first[i] * A_WINDOW, A_WINDOW)
        for g in groups:
            st = _dot(ka_ref[pl.ds(astart, 2 * A_WINDOW), g * LANES:g * LANES + HEAD_DIM], part_cols(i, g)) + abias
            sa_ref[i, g] = st
            amax[i, g] = jnp.max(st, axis=0, keepdims=True)
    cstart = pl.multiple_of(cbias_ref.shape[0] - nc - n * (QBLK // CMP_STRIDE), 8)
    cbias = cbias_ref[pl.ds(cstart, nc), :]
    cmax = []
    for g in groups:
        s = _dot(kcmp_ref[g, :, :HEAD_DIM], cols(qn_ref, g)) + cbias
        sc_ref[g] = s
        cmax.append(jnp.max(s, axis=0, keepdims=True))
    win_state = win.start(pbias_ref[0], gone(n >= 1))

    x_swa = {}
    for i in parts:
        for g in groups:
            sk = jnp.concatenate([jnp.full((1, A_WINDOW), sink_ref[h] * LOG2E, F32) for h in heads(g)], axis=1)
            m = jnp.maximum(amax[i, g], sk)
            pa_ref[i, g] = jnp.exp2(sa_ref[i, g] - m).astype(BF16)
            x_swa[i, g] = jnp.exp2(sk - m)
    tq = t0 + (lax.broadcasted_iota(jnp.int32, (1, width), 1) & (QBLK - 1))
    sees_any = tq >= CMP_BLOCK - 1
    p_cmp = []
    for g in groups:
        e = jnp.exp2(sc_ref[g] - cmax[g])
        inv = jnp.where(sees_any, 1.0 / jnp.maximum(jnp.sum(e, axis=0, keepdims=True), 1e-30), 0.0)
        p_cmp.append(e * inv)

    win_state = win.step(0, win_state, bias=pbias_ref[1] + gone(n >= 2))

    for i in parts:
        for g in groups:
            vt = jnp.concatenate([vat_ref[a_first[i] + k, g * VROWS:(g + 1) * VROWS, :] for k in range(2)], axis=1)
            o = _dot(vt, pa_ref[i, g])
            o = o[:HEAD_DIM] * (1.0 / (o[HEAD_DIM:HEAD_DIM + 1] + x_swa[i, g]))
            _put_heads(oa_ref.at[i * A_WINDOW:(i + 1) * A_WINDOW, :], g, o, A_WINDOW)
    o_cmp = [_dot(vcmpt_ref[g], p_cmp[g].astype(BF16)) for g in groups]
    imp = []
    for g in groups:
        pc = lane_chunks(p_cmp[g])
        psum = (pc[0] + pc[1]) + (pc[2] + pc[3])
        hi = psum.astype(BF16)
        lo = (psum - hi.astype(F32)).astype(BF16)
        imp.append(_dot(ovt_ref[...], hi) + _dot(ovt_ref[...], lo))
    o_win = win.finish(3, win.step(1, win_state, last=True))

    jb = lax.broadcasted_iota(jnp.int32, (nb, QBLK), 0)
    curb = (t0 + lax.broadcasted_iota(jnp.int32, (nb, QBLK), 1)) >> int(np.log2(SLC_BLOCK))
    sel = lax.cond((t0 + QBLK - 1) // SLC_BLOCK < SLC_TOP_N,
                   lambda: tuple(jnp.where(jb <= curb, 1.0, 0.0) for _ in groups),
                   lambda: tuple(_select_blocks_t(imp[g], t0) for g in groups))

    for g in groups:
        qs_ref[g, :HEAD_DIM, :] = cols(qnr_ref, g)
        mrows = jnp.concatenate([sel[g] - 1.0, jnp.zeros((LANES - MASK_COL - nb, QBLK), F32)], axis=0).astype(BF16)
        for r in range(REP):
            qs_ref[g, MASK_COL:, r * QBLK:(r + 1) * QBLK] = mrows
    slc_state = slc.start(pbias_ref[0], gone(n >= 1))

    no_bias = gone(n >= 0)
    one_step = lambda j, s: slc.step(j, s, bias=no_bias)
    trips = slc_count - 2
    slc_state = lax.fori_loop(0, trips // 2, lambda i, s: one_step(2 * i + 1, one_step(2 * i, s)), slc_state)
    slc_state = lax.cond(trips % 2 == 1, lambda s: one_step(trips - 1, s), lambda s: s, slc_state)

    o_slc = slc.finish(slc_count, slc.step(slc_count - 2, slc_state, last=True))
    for g in groups:
        gate = lambda br: jnp.concatenate(
            [gnt_ref[i, br * Q_HEADS + h:br * Q_HEADS + h + 1, :]
             for h in heads(g) for i in range(gnt_ref.shape[0])], axis=1)
        _put_heads(ob_ref, g, gate(0) * o_cmp[g] + gate(1) * o_slc[g] + gate(2) * o_win[g])


def _attn(sinks, qa, ka, vat, qn, qnr, gnt, kcmp, vcmpt, ksl, vslt, kw, vwt, batch, seq):
    nq = seq // QBLK
    nc = kcmp.shape[2]
    nb = seq // SLC_BLOCK
    width = REP * QBLK
    assert nb <= LANES - MASK_COL and QBLK % SLC_BLOCK == 0 and QBLK % A_WINDOW == 0 and A_WINDOW % LANES == 0
    c_start = np.arange(nc)[None, :] * CMP_STRIDE
    s_start = np.arange(nb)[:, None] * SLC_BLOCK
    ovt = jnp.asarray((c_start < s_start + SLC_BLOCK) & (s_start < c_start + CMP_BLOCK), BF16)
    ql = np.arange(width)[None, :] % QBLK
    masked = lambda visible: np.where(visible, 0.0, NEG)
    kq = np.arange(KSTEP)[:, None] - ql
    pbias = jnp.asarray(np.stack([masked(kq <= 0),
                                  masked(kq > 0)]), F32)
    band = np.arange(2 * A_WINDOW)[:, None]
    qpart = np.arange(REP * A_WINDOW)[None, :] % A_WINDOW
    in_window = lambda dist: (dist >= 0) & (dist < A_WINDOW)
    abias = jnp.asarray(np.stack([masked(in_window(qpart - band)),
                                  masked(in_window(qpart - (band - A_WINDOW)))]), F32)
    per_blk = QBLK // CMP_STRIDE
    d = np.arange(nc + (nq - 1) * per_blk)[:, None] - (nq - 1) * per_blk
    cbias = jnp.asarray(masked(d * CMP_STRIDE + CMP_BLOCK - 1 <= ql), F32)
    q_spec = pl.BlockSpec((None, QBLK // LANES, Q_DIM, LANES), lambda b, n: (b, n, 0, 0))
    g_spec = pl.BlockSpec((None, QBLK // LANES, LANES, LANES), lambda b, n: (b, n, 0, 0))
    kc_spec = pl.BlockSpec((None, KV_HEADS, nc, LANES), lambda b, n: (b, 0, 0, 0))
    vc_spec = pl.BlockSpec((None, KV_HEADS, HEAD_DIM, nc), lambda b, n: (b, 0, 0, 0))
    const = lambda a: pl.BlockSpec(a.shape, lambda b, n: (0,) * a.ndim, pipeline_mode=pl.Buffered(1))
    k_spec = pl.BlockSpec((seq, KV_HEADS * LANES), lambda b, n: (b, 0))
    vt_spec = lambda keys: pl.BlockSpec((None, seq // keys, KV_HEADS * VROWS, keys), lambda b, n: (b, 0, 0, 0))
    o_spec = pl.BlockSpec((QBLK, Q_DIM), lambda b, n: (b * nq + n, 0))
    o_shape = jax.ShapeDtypeStruct((batch * seq, Q_DIM), BF16)
    pipe_scratch = lambda depth: [
        pltpu.VMEM((KV_HEADS, depth, width), BF16),
        pltpu.VMEM((KV_HEADS, KSTEP, width), F32),
        pltpu.VMEM((KV_HEADS, KSTEP, width), BF16),
        pltpu.VMEM((KV_HEADS, VROWS, width), F32)]
    return pl.pallas_call(
        _attn_kernel,
        grid=(batch, nq),
        in_specs=[pl.BlockSpec(memory_space=pltpu.SMEM), const(abias), const(pbias), const(cbias), const(ovt),
                  q_spec, k_spec, vt_spec(A_WINDOW), q_spec, q_spec, g_spec, kc_spec, vc_spec,
                  k_spec, vt_spec(KSTEP), k_spec, vt_spec(KSTEP)],
        out_specs=[o_spec, o_spec],
        out_shape=[o_shape, o_shape],
        scratch_shapes=pipe_scratch(HEAD_DIM) + pipe_scratch(LANES) + [
            pltpu.VMEM((QBLK // A_WINDOW, KV_HEADS, 2 * A_WINDOW, REP * A_WINDOW), F32),
            pltpu.VMEM((QBLK // A_WINDOW, KV_HEADS, 2 * A_WINDOW, REP * A_WINDOW), BF16),
            pltpu.VMEM((KV_HEADS, nc, width), F32)],
        compiler_params=pltpu.CompilerParams(dimension_semantics=("parallel", "arbitrary"),
                                             vmem_limit_bytes=56 * 1024 * 1024),
        name="attn",
    )(sinks, abias, pbias, cbias, ovt, qa, ka, vat, qn, qnr, gnt, kcmp, vcmpt, ksl, vslt, kw, vwt)


def _layer_norm(r, g, b):
    mu = jnp.mean(r, axis=-1, keepdims=True)
    d = r - mu
    var = jnp.mean(d * d, axis=-1, keepdims=True)
    return d * lax.rsqrt(var + LN_EPS) * g + b


POST_PARTS = 2


def _post_kernel(alpha, x_ref, oa_ref, ob_ref, wgm_ref, wpa_ref, wpb_ref, wout_ref, g_ref, b_ref, h_ref):
    d = x_ref.shape[1]
    rows = [slice(i * (x_ref.shape[0] // POST_PARTS), (i + 1) * (x_ref.shape[0] // POST_PARTS))
            for i in range(POST_PARTS)]
    ys = []
    for r in rows:
        xb = x_ref[r, :].astype(BF16)
        pa = _dot(oa_ref[r, :], wpa_ref[...])
        pb = _dot(ob_ref[r, :], wpb_ref[...])
        ys.append(jax.nn.sigmoid(_dot(xb, wgm_ref[:, :d])) * pa + jax.nn.sigmoid(_dot(xb, wgm_ref[:, d:])) * pb)
    for r, y in zip(rows, ys):
        m = _dot(y.astype(BF16), wout_ref[...])
        h_ref[r, :] = _layer_norm(alpha * x_ref[r, :] + m, g_ref[...], b_ref[...])


def _post(x2, oa, ob, w_gm, wpa, wpb, wout, g, b, alpha, tm):
    n, d = x2.shape
    tok = lambda width: pl.BlockSpec((tm, width), lambda i: (i, 0))
    full = lambda a: pl.BlockSpec(a.shape, lambda i: (0, 0))
    return pl.pallas_call(
        functools.partial(_post_kernel, alpha),
        grid=(n // tm,),
        in_specs=[tok(d), tok(Q_DIM), tok(Q_DIM)] + [full(a) for a in (w_gm, wpa, wpb, wout, g, b)],
        out_specs=tok(d),
        out_shape=jax.ShapeDtypeStruct((n, d), F32),
        compiler_params=pltpu.CompilerParams(dimension_semantics=("parallel",),
                                             vmem_limit_bytes=48 * 1024 * 1024),
        name="post",
    )(x2, oa, ob, w_gm, wpa, wpb, wout, g, b)


FFN_CHUNK = 256


def _ffn_kernel(alpha, h_ref, wg_ref, wu_ref, wd_ref, g_ref, b_ref, o_ref):
    h = h_ref[...]
    hb = h.astype(BF16)
    hidden = wg_ref.shape[1]
    acc = jnp.zeros(h.shape, F32)
    for c in range(hidden // FFN_CHUNK):
        sl = slice(c * FFN_CHUNK, (c + 1) * FFN_CHUNK)
        a = jax.nn.silu(_dot(hb, wg_ref[:, sl])) * _dot(hb, wu_ref[:, sl])
        acc = acc + _dot(a.astype(BF16), wd_ref[sl, :])
    o_ref[...] = _layer_norm(alpha * h + acc, g_ref[...], b_ref[...])


def _ffn(h, wg, wu, wd, g, b, alpha, tm):
    n, d = h.shape
    tok = pl.BlockSpec((tm, d), lambda i: (i, 0))
    once = lambda a: pl.BlockSpec(a.shape, lambda i: (0, 0), pipeline_mode=pl.Buffered(1))
    return pl.pallas_call(
        functools.partial(_ffn_kernel, alpha),
        grid=(n // tm,),
        in_specs=[tok] + [once(a) for a in (wg, wu, wd, g, b)],
        out_specs=tok,
        out_shape=jax.ShapeDtypeStruct((n, d), F32),
        compiler_params=pltpu.CompilerParams(dimension_semantics=("parallel",),
                                             vmem_limit_bytes=56 * 1024 * 1024),
        name="ffn",
    )(h, wg, wu, wd, g, b)


def _position_tables(seq):
    half = HEAD_DIM // 2
    inv = ROPE_THETA ** (-jnp.arange(half, dtype=F32) / half)
    ang = jnp.arange(seq).astype(F32)[:, None] * inv[None, :]
    cos, sin = jnp.cos(ang), jnp.sin(ang)
    reps = LANES // HEAD_DIM
    lane = np.arange(LANES)[None, :]
    blk = (np.arange(seq) // SLC_BLOCK)[:, None]
    ke = jnp.asarray(np.where(lane - MASK_COL == blk, -NEG, 0.0), F32)
    return jnp.tile(cos, (1, 2 * reps)), jnp.tile(jnp.concatenate([-sin, sin], axis=1), (1, reps)), ke


def kernel(x, w_in, sinks, cmp_pe_k, cmp_w1_k, cmp_b1_k, cmp_w2_k, cmp_pe_v, cmp_w1_v, cmp_b1_v, cmp_w2_v,
           w_proj_a, w_proj_b, w_out, ln1_g, ln1_b, w_gate, w_up, w_down, ln2_g, ln2_b):
    batch, seq, d = x.shape
    depth = w_in.shape[0]
    alpha = (2 * depth) ** 0.25
    n_main = 2 * Q_DIM + 8 * KV_DIM
    n_gate = 3 * Q_HEADS
    tm = 512
    cos, sin, ke = _position_tables(seq)
    pad_cols = lambda w: jnp.pad(w, ((0, 0), (0, LANES - w.shape[1]))).astype(BF16)
    xt = x.reshape(batch * seq, d)
    for l in range(depth):
        w_main = w_in[l, :, :n_main].astype(BF16)
        w_gn = pad_cols(w_in[l, :, n_main:n_main + n_gate])
        w_gm = w_in[l, :, n_main + n_gate:].astype(BF16)
        (qa, ka, vat, qn, qnr, kc, vc, ksl, vslt, kw, vwt, gnt) = _in_proj(
            xt, w_main, w_gn, cos, sin, ke, batch, seq, tm)
        wk = (cmp_pe_k[l], cmp_w1_k[l].astype(BF16), cmp_b1_k[l][None, :], pad_cols(cmp_w2_k[l]))
        wv = (cmp_pe_v[l], cmp_w1_v[l].astype(BF16), cmp_b1_v[l][None, :], pad_cols(cmp_w2_v[l]))
        kcmp, vcmpt = _compress(kc, vc, wk, wv, batch, seq)
        oa, ob = _attn(sinks[l], qa, ka, vat, qn, qnr, gnt, kcmp, vcmpt, ksl, vslt, kw, vwt, batch, seq)
        h = _post(xt, oa, ob, w_gm, w_proj_a[l].astype(BF16), w_proj_b[l].astype(BF16), w_out[l].astype(BF16),
                  ln1_g[l][None, :], ln1_b[l][None, :], alpha, tm)
        xt = _ffn(h, w_gate[l].astype(BF16), w_up[l].astype(BF16), w_down[l].astype(BF16),
                  ln2_g[l][None, :], ln2_b[l][None, :], alpha, tm)
    return xt.reshape(batch, seq, d)
```

```python
import functools

import jax
import jax.numpy as jnp
import numpy as np
from jax import lax
from jax.experimental import pallas as pl
from jax.experimental.pallas import tpu as pltpu

HEAD_DIM = 64
ROPE_THETA = 10000.0
Q_HEADS = 8
KV_HEADS = 2
REP = Q_HEADS // KV_HEADS
A_WINDOW = 128
B_WINDOW = 512
CMP_BLOCK = 32
CMP_STRIDE = 16
SLC_BLOCK = 64
SLC_TOP_N = 16
LN_EPS = 1e-5
NEG = -1e30
BIG = 1e9
Q_DIM = Q_HEADS * HEAD_DIM
KV_DIM = KV_HEADS * HEAD_DIM
LANES = 128
QBLK = 256
KSTEP = QBLK
LOG2E = 1.4426950408889634
SCALE = HEAD_DIM ** -0.5 * LOG2E
VROWS = HEAD_DIM + 16

F32 = jnp.float32
BF16 = jnp.bfloat16


def _dot(a, b):
    return jnp.dot(a, b, preferred_element_type=F32)


def _inproj_kernel(x_ref, w_ref, wg_ref, cos_ref, sin_ref, ke_ref,
                   qa_ref, ka_ref, vat_ref, qn_ref, qnr_ref, kc_ref, vc_ref,
                   ksl_ref, vslt_ref, kw_ref, vwt_ref, gnt_ref):
    tm = x_ref.shape[0]
    xb = x_ref[...].astype(BF16)
    cos = cos_ref[...]
    sin = sin_ref[...]
    lane = lax.broadcasted_iota(jnp.int32, (tm, LANES), 1)
    first_half = (lane & (HEAD_DIM - 1)) < (HEAD_DIM // 2)
    low = lane < HEAD_DIM

    def rope(z):
        sw = jnp.where(first_half, pltpu.roll(z, LANES - HEAD_DIM // 2, 1), pltpu.roll(z, HEAD_DIM // 2, 1))
        return z * cos + sw * sin

    def proj(c0, n):
        return _dot(xb, w_ref[:, c0:c0 + n])

    def chunks(z):
        return [z[:, c * LANES:(c + 1) * LANES] for c in range(z.shape[1] // LANES)]

    def put_padded(ref, c, z, fill):
        ref[:, (2 * c) * LANES:(2 * c + 1) * LANES] = jnp.where(low, z, fill).astype(ref.dtype)
        ref[:, (2 * c + 1) * LANES:(2 * c + 2) * LANES] = jnp.where(low, pltpu.roll(z, HEAD_DIM, 1), fill).astype(ref.dtype)

    def put_transposed(ref, z):
        for j in range(tm // LANES):
            ref[j] = z[j * LANES:(j + 1) * LANES, :].T.astype(ref.dtype)

    def put_heads_t(ref, c, z):
        for j in range(tm // LANES):
            ref[j, c * LANES:(c + 1) * LANES, :] = z[j * LANES:(j + 1) * LANES, :].T.astype(ref.dtype)

    ones_row = jnp.where(lax.broadcasted_iota(jnp.int32, (VROWS - HEAD_DIM, LANES), 0) == 0, 1.0, 0.0)

    def put_values_t(ref, z):
        keys = ref.shape[2]
        for j in range(tm // keys):
            for i in range(keys // LANES):
                zt = z[j * keys + i * LANES:j * keys + (i + 1) * LANES, :].T
                cols = slice(i * LANES, (i + 1) * LANES)
                for g in range(KV_HEADS):
                    ref[j, g * VROWS:g * VROWS + HEAD_DIM, cols] = zt[g * HEAD_DIM:(g + 1) * HEAD_DIM, :].astype(ref.dtype)
                    ref[j, g * VROWS + HEAD_DIM:(g + 1) * VROWS, cols] = ones_row.astype(ref.dtype)

    for c, zc in enumerate(chunks(proj(0, Q_DIM))):
        put_heads_t(qa_ref, c, rope(zc) * SCALE)
    z = proj(Q_DIM, 2 * KV_DIM)
    put_padded(ka_ref, 0, rope(z[:, :KV_DIM]), 0.0)
    put_values_t(vat_ref, z[:, KV_DIM:])
    c0 = Q_DIM + 2 * KV_DIM
    for c, zc in enumerate(chunks(proj(c0, Q_DIM))):
        put_heads_t(qn_ref, c, zc * SCALE)
        put_heads_t(qnr_ref, c, rope(zc) * SCALE)
    c0 += Q_DIM
    z = proj(c0, 2 * KV_DIM)
    kc_ref[...] = z[:, :KV_DIM]
    vc_ref[...] = z[:, KV_DIM:]
    c0 += 2 * KV_DIM
    z = proj(c0, 2 * KV_DIM)
    put_padded(ksl_ref, 0, rope(z[:, :KV_DIM]), ke_ref[...])
    put_values_t(vslt_ref, z[:, KV_DIM:])
    c0 += 2 * KV_DIM
    z = proj(c0, 2 * KV_DIM)
    put_padded(kw_ref, 0, rope(z[:, :KV_DIM]), 0.0)
    put_values_t(vwt_ref, z[:, KV_DIM:])
    put_transposed(gnt_ref, jax.nn.sigmoid(_dot(xb, wg_ref[...])))


def _in_proj(x2, w_main, w_gn, cos, sin, ke, batch, seq, tm):
    n = x2.shape[0]
    d = x2.shape[1]
    spt = seq // tm
    tok = lambda width: pl.BlockSpec((tm, width), lambda i: (i, 0))
    full = lambda a: pl.BlockSpec(a.shape, lambda i: (0, 0))
    tab = pl.BlockSpec((tm, LANES), lambda i: (i % spt, 0))
    flat = lambda w, dt: (tok(w), jax.ShapeDtypeStruct((n, w), dt))
    trans = lambda r, keys, dt: (pl.BlockSpec((None, tm // keys, r, keys), lambda i: (i // spt, i % spt, 0, 0)),
                                 jax.ShapeDtypeStruct((batch, seq // keys, r, keys), dt))
    outs = [trans(Q_DIM, LANES, BF16), flat(KV_HEADS * LANES, BF16),
            trans(KV_HEADS * VROWS, A_WINDOW, BF16),
            trans(Q_DIM, LANES, BF16), trans(Q_DIM, LANES, BF16),
            flat(KV_DIM, F32), flat(KV_DIM, F32),
            flat(KV_HEADS * LANES, BF16), trans(KV_HEADS * VROWS, KSTEP, BF16),
            flat(KV_HEADS * LANES, BF16), trans(KV_HEADS * VROWS, KSTEP, BF16),
            trans(LANES, LANES, F32)]
    return pl.pallas_call(
        _inproj_kernel,
        grid=(n // tm,),
        in_specs=[tok(d), full(w_main), full(w_gn), tab, tab, tab],
        out_specs=[o[0] for o in outs],
        out_shape=[o[1] for o in outs],
        compiler_params=pltpu.CompilerParams(dimension_semantics=("parallel",),
                                             vmem_limit_bytes=48 * 1024 * 1024),
        name="in_proj",
    )(x2, w_main, w_gn, cos, sin, ke)


def _compress_one(src_ref, pe_ref, w1_ref, b1_ref, w2_ref):
    ratio = CMP_BLOCK // CMP_STRIDE
    nchunk = src_ref.shape[0] // CMP_STRIDE
    hid = w1_ref.shape[1]
    parts = [jnp.zeros((KV_HEADS * nchunk, hid), F32) for _ in range(ratio)]
    for l in range(CMP_STRIDE):
        rows = src_ref[pl.ds(l, nchunk, stride=CMP_STRIDE), :]
        rows = jnp.concatenate([rows[:, g * HEAD_DIM:(g + 1) * HEAD_DIM] for g in range(KV_HEADS)], axis=0)
        for j in range(ratio):
            p = j * CMP_STRIDE + l
            a = (rows + pe_ref[p:p + 1, :]).astype(BF16)
            parts[j] = parts[j] + _dot(a, w1_ref[p * HEAD_DIM:(p + 1) * HEAD_DIM, :])
    out = []
    for g in range(KV_HEADS):
        h = parts[0][g * nchunk:(g + 1) * nchunk]
        for j in range(1, ratio):
            h = h + pltpu.roll(parts[j][g * nchunk:(g + 1) * nchunk], nchunk - j, 0)
        h = jax.nn.gelu(h + b1_ref[...])
        out.append(_dot(h.astype(BF16), w2_ref[...]))
    return out


def _compress_kernel(kc_ref, vc_ref, pek_ref, w1k_ref, b1k_ref, w2k_ref,
                     pev_ref, w1v_ref, b1v_ref, w2v_ref, kcmp_ref, vcmpt_ref):
    for g, kc in enumerate(_compress_one(kc_ref, pek_ref, w1k_ref, b1k_ref, w2k_ref)):
        kcmp_ref[g] = kc.astype(BF16)
    for g, vc in enumerate(_compress_one(vc_ref, pev_ref, w1v_ref, b1v_ref, w2v_ref)):
        vcmpt_ref[g] = vc.T[:HEAD_DIM, :].astype(BF16)


def _compress(kc, vc, wk, wv, batch, seq):
    nchunk = seq // CMP_STRIDE
    src = pl.BlockSpec((seq, KV_DIM), lambda b: (b, 0))
    full = lambda a: pl.BlockSpec(a.shape, lambda b: (0,) * a.ndim)
    return pl.pallas_call(
        _compress_kernel,
        grid=(batch,),
        in_specs=[src, src] + [full(a) for a in wk] + [full(a) for a in wv],
        out_specs=[pl.BlockSpec((None, KV_HEADS, nchunk, LANES), lambda b: (b, 0, 0, 0)),
                   pl.BlockSpec((None, KV_HEADS, HEAD_DIM, nchunk), lambda b: (b, 0, 0, 0))],
        out_shape=[jax.ShapeDtypeStruct((batch, KV_HEADS, nchunk, LANES), BF16),
                   jax.ShapeDtypeStruct((batch, KV_HEADS, HEAD_DIM, nchunk), BF16)],
        compiler_params=pltpu.CompilerParams(dimension_semantics=("parallel",)),
        name="compress",
    )(kc, vc, *wk, *wv)


MASK_COL = HEAD_DIM


def _select_blocks_t(imp, t0):
    nb, cols = imp.shape
    j = lax.broadcasted_iota(jnp.int32, (nb, cols), 0)
    cur = (t0 + lax.broadcasted_iota(jnp.int32, (nb, cols), 1)) >> int(np.log2(SLC_BLOCK))
    forced = (j == 0) | (j == cur) | (j == cur - 1)
    score = jnp.where(forced, BIG, jnp.where(j <= cur, imp, -BIG))
    sub = 8
    tiles = [score[v * sub:(v + 1) * sub] for v in range(nb // sub)]
    ranks = [jnp.zeros((sub, cols), F32) for _ in tiles]
    jl = lax.broadcasted_iota(jnp.int32, (sub, cols), 0)
    for jp in range(nb):
        row = score[jp:jp + 1, :]
        for v, tile in enumerate(tiles):
            if v * sub > jp:
                beats = row >= tile
            elif (v + 1) * sub - 1 <= jp:
                beats = row > tile
            else:
                beats = (row > tile) | ((row == tile) & (jl > jp - v * sub))
            ranks[v] = jnp.where(beats, ranks[v] + 1.0, ranks[v])
    rank = jnp.concatenate(ranks, axis=0)
    return jnp.where((rank < float(min(SLC_TOP_N, nb))) & (j <= cur), 1.0, 0.0)


def _put_heads(o_ref, g, o, nq=QBLK):
    for i in range(REP // 2):
        pair = jnp.concatenate([o[:, (2 * i) * nq:(2 * i + 1) * nq],
                                o[:, (2 * i + 1) * nq:(2 * i + 2) * nq]], axis=0)
        col = (g * REP + 2 * i) * HEAD_DIM
        o_ref[:, col:col + LANES] = pair.T.astype(o_ref.dtype)


class _Pipeline:
    def __init__(self, k_ref, vt_ref, q_ref, s_ref, p_ref, acc_ref, chunk_of):
        self.k_ref, self.vt_ref, self.q_ref = k_ref, vt_ref, q_ref
        self.s_ref, self.p_ref, self.acc_ref = s_ref, p_ref, acc_ref
        self.chunk_of = chunk_of
        self.groups = range(KV_HEADS)

    def qk(self, g, j):
        start = pl.multiple_of(self.chunk_of(j) * KSTEP, KSTEP)
        depth = self.q_ref.shape[1]
        return _dot(self.k_ref[pl.ds(start, KSTEP), g * LANES:g * LANES + depth], self.q_ref[g])

    def keep(self, g, st, bias):
        if bias is not None:
            st = st + bias
        self.s_ref[g] = st
        return jnp.max(st, axis=0, keepdims=True)

    def softmax(self, g, m, cmax):
        m_new = jnp.maximum(m, cmax)
        a = jnp.exp2(m - m_new)
        self.p_ref[g] = jnp.exp2(self.s_ref[g] - m_new).astype(BF16)
        return m_new, a

    def values(self, g, j, a):
        vt = self.vt_ref[self.chunk_of(j), g * VROWS:(g + 1) * VROWS, :]
        self.acc_ref[g] = a * self.acc_ref[g] + _dot(vt, self.p_ref[g])

    def start(self, bias0, bias1):
        m0 = jnp.full((1, self.s_ref.shape[2]), NEG, F32)
        cmax = [self.keep(g, self.qk(g, 0), bias0) for g in self.groups]
        st1 = [self.qk(g, 1) for g in self.groups]
        sm = [self.softmax(g, m0, cmax[g]) for g in self.groups]
        cmax = tuple(self.keep(g, st1[g], bias1) for g in self.groups)
        for g in self.groups:
            self.acc_ref[g] = jnp.zeros(self.acc_ref.shape[1:], F32)
        return (tuple(s[0] for s in sm), tuple(s[1] for s in sm), cmax)

    def step(self, j, state, last=False, bias=None):
        m, a, cmax = state
        st = None if last else [self.qk(g, j + 2) for g in self.groups]
        for g in self.groups:
            self.values(g, j, a[g])
        sm = [self.softmax(g, m[g], cmax[g]) for g in self.groups]
        if not last:
            cmax = tuple(self.keep(g, st[g], bias) for g in self.groups)
        return (tuple(s[0] for s in sm), tuple(s[1] for s in sm), cmax)

    def finish(self, count, state):
        _, a, _ = state
        out = []
        for g in self.groups:
            self.values(g, count - 1, a[g])
            acc = self.acc_ref[g]
            out.append(acc[:HEAD_DIM] * (1.0 / acc[HEAD_DIM:HEAD_DIM + 1]))
        return out


def _attn_kernel(sink_ref, abias_ref, pbias_ref, cbias_ref, ovt_ref,
                 qa_ref, ka_ref, vat_ref, qn_ref, qnr_ref, gnt_ref, kcmp_ref, vcmpt_ref,
                 ksl_ref, vslt_ref, kw_ref, vwt_ref,
                 oa_ref, ob_ref,
                 qw_ref, sw_ref, pw_ref, accw_ref, qs_ref, ss_ref, ps_ref, accs_ref, sa_ref, pa_ref, sc_ref):
    n = pl.program_id(1)
    groups = range(KV_HEADS)
    width = REP * QBLK
    nb = ovt_ref.shape[0]
    nc = kcmp_ref.shape[1]
    t0 = pl.multiple_of(n * QBLK, QBLK)
    heads = lambda g: [g * REP + r for r in range(REP)]
    parts = range(QBLK // LANES)
    head_t = lambda ref, h, i: ref[i, h * HEAD_DIM:(h + 1) * HEAD_DIM, :]
    cols = lambda ref, g: jnp.concatenate([head_t(ref, h, i) for h in heads(g) for i in parts], axis=1)
    lane_chunks = lambda a: [a[:, r * QBLK:(r + 1) * QBLK] for r in range(REP)]
    gone = lambda cond: jnp.where(cond, 0.0, NEG)

    assert B_WINDOW == 2 * KSTEP
    win = _Pipeline(kw_ref, vwt_ref, qw_ref, sw_ref, pw_ref, accw_ref, lambda j: jnp.maximum(n - j, 0))
    slc = _Pipeline(ksl_ref, vslt_ref, qs_ref, ss_ref, ps_ref, accs_ref,
                    lambda j: jnp.where(j == 0, n, jnp.where(n == 0, 1, j - 1)))
    slc_count = jnp.maximum(n, 1) + 1

    for g in groups:
        qw_ref[g] = cols(qnr_ref, g)
    a_chunk = [n * len(parts) + i for i in parts]
    a_first = [jnp.maximum(c - 1, 0) for c in a_chunk]
    part_cols = lambda i, g: jnp.concatenate([head_t(qa_ref, h, i) for h in heads(g)], axis=1)
    amax = {}
    for i in parts:
        abias = abias_ref[jnp.minimum(a_chunk[i], 1)]
        astart = pl.multiple_of(a_first[i] * A_WINDOW, A_WINDOW)
        for g in groups:
            st = _dot(ka_ref[pl.ds(astart, 2 * A_WINDOW), g * LANES:g * LANES + HEAD_DIM], part_cols(i, g)) + abias
            sa_ref[i, g] = st
            amax[i, g] = jnp.max(st, axis=0, keepdims=True)
    cstart = pl.multiple_of(cbias_ref.shape[0] - nc - n * (QBLK // CMP_STRIDE), 8)
    cbias = cbias_ref[pl.ds(cstart, nc), :]
    cmax = []
    for g in groups:
        s = _dot(kcmp_ref[g, :, :HEAD_DIM], cols(qn_ref, g)) + cbias
        sc_ref[g] = s
        cmax.append(jnp.max(s, axis=0, keepdims=True))
    win_state = win.start(pbias_ref[0], gone(n >= 1))

    x_swa = {}
    for i in parts:
        for g in groups:
            sk = jnp.concatenate([jnp.full((1, A_WINDOW), sink_ref[h] * LOG2E, F32) for h in heads(g)], axis=1)
            m = jnp.maximum(amax[i, g], sk)
            pa_ref[i, g] = jnp.exp2(sa_ref[i, g] - m).astype(BF16)
            x_swa[i, g] = jnp.exp2(sk - m)
    tq = t0 + (lax.broadcasted_iota(jnp.int32, (1, width), 1) & (QBLK - 1))
    sees_any = tq >= CMP_BLOCK - 1
    p_cmp = []
    for g in groups:
        e = jnp.exp2(sc_ref[g] - cmax[g])
        inv = jnp.where(sees_any, 1.0 / jnp.maximum(jnp.sum(e, axis=0, keepdims=True), 1e-30), 0.0)
        p_cmp.append(e * inv)

    win_state = win.step(0, win_state, bias=pbias_ref[1] + gone(n >= 2))

    for i in parts:
        for g in groups:
            vt = jnp.concatenate([vat_ref[a_first[i] + k, g * VROWS:(g + 1) * VROWS, :] for k in range(2)], axis=1)
            o = _dot(vt, pa_ref[i, g])
            o = o[:HEAD_DIM] * (1.0 / (o[HEAD_DIM:HEAD_DIM + 1] + x_swa[i, g]))
            _put_heads(oa_ref.at[i * A_WINDOW:(i + 1) * A_WINDOW, :], g, o, A_WINDOW)
    o_cmp = [_dot(vcmpt_ref[g], p_cmp[g].astype(BF16)) for g in groups]
    imp = []
    for g in groups:
        pc = lane_chunks(p_cmp[g])
        psum = (pc[0] + pc[1]) + (pc[2] + pc[3])
        hi = psum.astype(BF16)
        lo = (psum - hi.astype(F32)).astype(BF16)
        imp.append(_dot(ovt_ref[...], hi) + _dot(ovt_ref[...], lo))
    o_win = win.finish(3, win.step(1, win_state, last=True))

    jb = lax.broadcasted_iota(jnp.int32, (nb, QBLK), 0)
    curb = (t0 + lax.broadcasted_iota(jnp.int32, (nb, QBLK), 1)) >> int(np.log2(SLC_BLOCK))
    sel = lax.cond((t0 + QBLK - 1) // SLC_BLOCK < SLC_TOP_N,
                   lambda: tuple(jnp.where(jb <= curb, 1.0, 0.0) for _ in groups),
                   lambda: tuple(_select_blocks_t(imp[g], t0) for g in groups))

    for g in groups:
        qs_ref[g, :HEAD_DIM, :] = cols(qnr_ref, g)
        mrows = jnp.concatenate([sel[g] - 1.0, jnp.zeros((LANES - MASK_COL - nb, QBLK), F32)], axis=0).astype(BF16)
        for r in range(REP):
            qs_ref[g, MASK_COL:, r * QBLK:(r + 1) * QBLK] = mrows
    slc_state = slc.start(pbias_ref[0], None)

    trips = slc_count - 2
    slc_state = lax.fori_loop(0, trips // 2, lambda i, s: slc.step(2 * i + 1, slc.step(2 * i, s)), slc_state)
    slc_state = lax.cond(trips % 2 == 1, lambda s: slc.step(trips - 1, s), lambda s: s, slc_state)

    o_slc = slc.finish(slc_count, slc.step(slc_count - 2, slc_state, last=True))
    for g in groups:
        gate = lambda br: jnp.concatenate(
            [gnt_ref[i, br * Q_HEADS + h:br * Q_HEADS + h + 1, :]
             for h in heads(g) for i in range(gnt_ref.shape[0])], axis=1)
        _put_heads(ob_ref, g, gate(0) * o_cmp[g] + gate(1) * o_slc[g] + gate(2) * o_win[g])


def _attn(sinks, qa, ka, vat, qn, qnr, gnt, kcmp, vcmpt, ksl, vslt, kw, vwt, batch, seq):
    nq = seq // QBLK
    nc = kcmp.shape[2]
    nb = seq // SLC_BLOCK
    width = REP * QBLK
    assert nb <= LANES - MASK_COL and QBLK % SLC_BLOCK == 0 and QBLK % A_WINDOW == 0 and A_WINDOW % LANES == 0
    c_start = np.arange(nc)[None, :] * CMP_STRIDE
    s_start = np.arange(nb)[:, None] * SLC_BLOCK
    ovt = jnp.asarray((c_start < s_start + SLC_BLOCK) & (s_start < c_start + CMP_BLOCK), BF16)
    ql = np.arange(width)[None, :] % QBLK
    masked = lambda visible: np.where(visible, 0.0, NEG)
    kq = np.arange(KSTEP)[:, None] - ql
    pbias = jnp.asarray(np.stack([masked(kq <= 0),
                                  masked(kq > 0)]), F32)
    band = np.arange(2 * A_WINDOW)[:, None]
    qpart = np.arange(REP * A_WINDOW)[None, :] % A_WINDOW
    in_window = lambda dist: (dist >= 0) & (dist < A_WINDOW)
    abias = jnp.asarray(np.stack([masked(in_window(qpart - band)),
                                  masked(in_window(qpart - (band - A_WINDOW)))]), F32)
    per_blk = QBLK // CMP_STRIDE
    d = np.arange(nc + (nq - 1) * per_blk)[:, None] - (nq - 1) * per_blk
    cbias = jnp.asarray(masked(d * CMP_STRIDE + CMP_BLOCK - 1 <= ql), F32)
    q_spec = pl.BlockSpec((None, QBLK // LANES, Q_DIM, LANES), lambda b, n: (b, n, 0, 0))
    g_spec = pl.BlockSpec((None, QBLK // LANES, LANES, LANES), lambda b, n: (b, n, 0, 0))
    kc_spec = pl.BlockSpec((None, KV_HEADS, nc, LANES), lambda b, n: (b, 0, 0, 0))
    vc_spec = pl.BlockSpec((None, KV_HEADS, HEAD_DIM, nc), lambda b, n: (b, 0, 0, 0))
    const = lambda a: pl.BlockSpec(a.shape, lambda b, n: (0,) * a.ndim, pipeline_mode=pl.Buffered(1))
    k_spec = pl.BlockSpec((seq, KV_HEADS * LANES), lambda b, n: (b, 0))
    vt_spec = lambda keys: pl.BlockSpec((None, seq // keys, KV_HEADS * VROWS, keys), lambda b, n: (b, 0, 0, 0))
    o_spec = pl.BlockSpec((QBLK, Q_DIM), lambda b, n: (b * nq + n, 0))
    o_shape = jax.ShapeDtypeStruct((batch * seq, Q_DIM), BF16)
    pipe_scratch = lambda depth: [
        pltpu.VMEM((KV_HEADS, depth, width), BF16),
        pltpu.VMEM((KV_HEADS, KSTEP, width), F32),
        pltpu.VMEM((KV_HEADS, KSTEP, width), BF16),
        pltpu.VMEM((KV_HEADS, VROWS, width), F32)]
    return pl.pallas_call(
        _attn_kernel,
        grid=(batch, nq),
        in_specs=[pl.BlockSpec(memory_space=pltpu.SMEM), const(abias), const(pbias), const(cbias), const(ovt),
                  q_spec, k_spec, vt_spec(A_WINDOW), q_spec, q_spec, g_spec, kc_spec, vc_spec,
                  k_spec, vt_spec(KSTEP), k_spec, vt_spec(KSTEP)],
        out_specs=[o_spec, o_spec],
        out_shape=[o_shape, o_shape],
        scratch_shapes=pipe_scratch(HEAD_DIM) + pipe_scratch(LANES) + [
            pltpu.VMEM((QBLK // A_WINDOW, KV_HEADS, 2 * A_WINDOW, REP * A_WINDOW), F32),
            pltpu.VMEM((QBLK // A_WINDOW, KV_HEADS, 2 * A_WINDOW, REP * A_WINDOW), BF16),
            pltpu.VMEM((KV_HEADS, nc, width), F32)],
        compiler_params=pltpu.CompilerParams(dimension_semantics=("parallel", "arbitrary"),
                                             vmem_limit_bytes=56 * 1024 * 1024),
        name="attn",
    )(sinks, abias, pbias, cbias, ovt, qa, ka, vat, qn, qnr, gnt, kcmp, vcmpt, ksl, vslt, kw, vwt)


def _layer_norm(r, g, b):
    mu = jnp.mean(r, axis=-1, keepdims=True)
    d = r - mu
    var = jnp.mean(d * d, axis=-1, keepdims=True)
    return d * lax.rsqrt(var + LN_EPS) * g + b


POST_PARTS = 2


def _post_kernel(alpha, x_ref, oa_ref, ob_ref, wgm_ref, wpa_ref, wpb_ref, wout_ref, g_ref, b_ref, h_ref):
    d = x_ref.shape[1]
    rows = [slice(i * (x_ref.shape[0] // POST_PARTS), (i + 1) * (x_ref.shape[0] // POST_PARTS))
            for i in range(POST_PARTS)]
    ys = []
    for r in rows:
        xb = x_ref[r, :].astype(BF16)
        pa = _dot(oa_ref[r, :], wpa_ref[...])
        pb = _dot(ob_ref[r, :], wpb_ref[...])
        ys.append(jax.nn.sigmoid(_dot(xb, wgm_ref[:, :d])) * pa + jax.nn.sigmoid(_dot(xb, wgm_ref[:, d:])) * pb)
    for r, y in zip(rows, ys):
        m = _dot(y.astype(BF16), wout_ref[...])
        h_ref[r, :] = _layer_norm(alpha * x_ref[r, :] + m, g_ref[...], b_ref[...])


def _post(x2, oa, ob, w_gm, wpa, wpb, wout, g, b, alpha, tm):
    n, d = x2.shape
    tok = lambda width: pl.BlockSpec((tm, width), lambda i: (i, 0))
    full = lambda a: pl.BlockSpec(a.shape, lambda i: (0, 0))
    return pl.pallas_call(
        functools.partial(_post_kernel, alpha),
        grid=(n // tm,),
        in_specs=[tok(d), tok(Q_DIM), tok(Q_DIM)] + [full(a) for a in (w_gm, wpa, wpb, wout, g, b)],
        out_specs=tok(d),
        out_shape=jax.ShapeDtypeStruct((n, d), F32),
        compiler_params=pltpu.CompilerParams(dimension_semantics=("parallel",),
                                             vmem_limit_bytes=48 * 1024 * 1024),
        name="post",
    )(x2, oa, ob, w_gm, wpa, wpb, wout, g, b)


FFN_CHUNK = 256


def _ffn_kernel(alpha, h_ref, wg_ref, wu_ref, wd_ref, g_ref, b_ref, o_ref):
    h = h_ref[...]
    hb = h.astype(BF16)
    hidden = wg_ref.shape[1]
    acc = jnp.zeros(h.shape, F32)
    for c in range(hidden // FFN_CHUNK):
        sl = slice(c * FFN_CHUNK, (c + 1) * FFN_CHUNK)
        a = jax.nn.silu(_dot(hb, wg_ref[:, sl])) * _dot(hb, wu_ref[:, sl])
        acc = acc + _dot(a.astype(BF16), wd_ref[sl, :])
    o_ref[...] = _layer_norm(alpha * h + acc, g_ref[...], b_ref[...])


def _ffn(h, wg, wu, wd, g, b, alpha, tm):
    n, d = h.shape
    tok = pl.BlockSpec((tm, d), lambda i: (i, 0))
    once = lambda a: pl.BlockSpec(a.shape, lambda i: (0, 0), pipeline_mode=pl.Buffered(1))
    return pl.pallas_call(
        functools.partial(_ffn_kernel, alpha),
        grid=(n // tm,),
        in_specs=[tok] + [once(a) for a in (wg, wu, wd, g, b)],
        out_specs=tok,
        out_shape=jax.ShapeDtypeStruct((n, d), F32),
        compiler_params=pltpu.CompilerParams(dimension_semantics=("parallel",),
                                             vmem_limit_bytes=56 * 1024 * 1024),
        name="ffn",
    )(h, wg, wu, wd, g, b)


def _position_tables(seq):
    half = HEAD_DIM // 2
    inv = ROPE_THETA ** (-jnp.arange(half, dtype=F32) / half)
    ang = jnp.arange(seq).astype(F32)[:, None] * inv[None, :]
    cos, sin = jnp.cos(ang), jnp.sin(ang)
    reps = LANES // HEAD_DIM
    lane = np.arange(LANES)[None, :]
    blk = (np.arange(seq) // SLC_BLOCK)[:, None]
    ke = jnp.asarray(np.where(lane - MASK_COL == blk, -NEG, 0.0), F32)
    return jnp.tile(cos, (1, 2 * reps)), jnp.tile(jnp.concatenate([-sin, sin], axis=1), (1, reps)), ke


def kernel(x, w_in, sinks, cmp_pe_k, cmp_w1_k, cmp_b1_k, cmp_w2_k, cmp_pe_v, cmp_w1_v, cmp_b1_v, cmp_w2_v,
           w_proj_a, w_proj_b, w_out, ln1_g, ln1_b, w_gate, w_up, w_down, ln2_g, ln2_b):
    batch, seq, d = x.shape
    depth = w_in.shape[0]
    alpha = (2 * depth) ** 0.25
    n_main = 2 * Q_DIM + 8 * KV_DIM
    n_gate = 3 * Q_HEADS
    tm = 512
    cos, sin, ke = _position_tables(seq)
    pad_cols = lambda w: jnp.pad(w, ((0, 0), (0, LANES - w.shape[1]))).astype(BF16)
    xt = x.reshape(batch * seq, d)
    for l in range(depth):
        w_main = w_in[l, :, :n_main].astype(BF16)
        w_gn = pad_cols(w_in[l, :, n_main:n_main + n_gate])
        w_gm = w_in[l, :, n_main + n_gate:].astype(BF16)
        (qa, ka, vat, qn, qnr, kc, vc, ksl, vslt, kw, vwt, gnt) = _in_proj(
            xt, w_main, w_gn, cos, sin, ke, batch, seq, tm)
        wk = (cmp_pe_k[l], cmp_w1_k[l].astype(BF16), cmp_b1_k[l][None, :], pad_cols(cmp_w2_k[l]))
        wv = (cmp_pe_v[l], cmp_w1_v[l].astype(BF16), cmp_b1_v[l][None, :], pad_cols(cmp_w2_v[l]))
        kcmp, vcmpt = _compress(kc, vc, wk, wv, batch, seq)
        oa, ob = _attn(sinks[l], qa, ka, vat, qn, qnr, gnt, kcmp, vcmpt, ksl, vslt, kw, vwt, batch, seq)
        h = _post(xt, oa, ob, w_gm, w_proj_a[l].astype(BF16), w_proj_b[l].astype(BF16), w_out[l].astype(BF16),
                  ln1_g[l][None, :], ln1_b[l][None, :], alpha, tm)
        xt = _ffn(h, w_gate[l].astype(BF16), w_up[l].astype(BF16), w_down[l].astype(BF16),
                  ln2_g[l][None, :], ln2_b[l][None, :], alpha, tm)
    return xt.reshape(batch, seq, d)
```

```python
import functools

import jax
import jax.numpy as jnp
import numpy as np
from jax import lax
from jax.experimental import pallas as pl
from jax.experimental.pallas import tpu as pltpu

HEAD_DIM = 64
ROPE_THETA = 10000.0
Q_HEADS = 8
KV_HEADS = 2
REP = Q_HEADS // KV_HEADS
A_WINDOW = 128
B_WINDOW = 512
CMP_BLOCK = 32
CMP_STRIDE = 16
SLC_BLOCK = 64
SLC_TOP_N = 16
LN_EPS = 1e-5
NEG = -1e30
BIG = 1e9
Q_DIM = Q_HEADS * HEAD_DIM
KV_DIM = KV_HEADS * HEAD_DIM
LANES = 128
QBLK = 256
KSTEP = QBLK
LOG2E = 1.4426950408889634
SCALE = HEAD_DIM ** -0.5 * LOG2E
VROWS = HEAD_DIM + 16

F32 = jnp.float32
BF16 = jnp.bfloat16


def _dot(a, b):
    return jnp.dot(a, b, preferred_element_type=F32)


def _inproj_kernel(x_ref, w_ref, wg_ref, cos_ref, sin_ref, cost_ref, sint_ref, ke_ref,
                   qa_ref, ka_ref, vat_ref, qn_ref, qnr_ref, kc_ref, vc_ref,
                   ksl_ref, vslt_ref, kw_ref, vwt_ref, gnt_ref):
    tm = x_ref.shape[0]
    xb = x_ref[...].astype(BF16)
    cos = cos_ref[...]
    sin = sin_ref[...]
    lane = lax.broadcasted_iota(jnp.int32, (tm, LANES), 1)
    first_half = (lane & (HEAD_DIM - 1)) < (HEAD_DIM // 2)
    low = lane < HEAD_DIM

    def rope(z):
        sw = jnp.where(first_half, pltpu.roll(z, LANES - HEAD_DIM // 2, 1), pltpu.roll(z, HEAD_DIM // 2, 1))
        return z * cos + sw * sin

    def proj(c0, n):
        return _dot(xb, w_ref[:, c0:c0 + n])

    def chunks(z):
        return [z[:, c * LANES:(c + 1) * LANES] for c in range(z.shape[1] // LANES)]

    def put_padded(ref, c, z, fill):
        ref[:, (2 * c) * LANES:(2 * c + 1) * LANES] = jnp.where(low, z, fill).astype(ref.dtype)
        ref[:, (2 * c + 1) * LANES:(2 * c + 2) * LANES] = jnp.where(low, pltpu.roll(z, HEAD_DIM, 1), fill).astype(ref.dtype)

    def put_transposed(ref, z):
        for j in range(tm // LANES):
            ref[j] = z[j * LANES:(j + 1) * LANES, :].T.astype(ref.dtype)

    def put_heads_t(c, z, plain_ref, rope_ref):
        half = HEAD_DIM // 2
        for j in range(tm // LANES):
            zt = z[j * LANES:(j + 1) * LANES, :].T * SCALE
            rows = slice(c * LANES, (c + 1) * LANES)
            if plain_ref is not None:
                plain_ref[j, rows, :] = zt.astype(plain_ref.dtype)
            sw = jnp.concatenate([zt[(s ^ 1) * half:((s ^ 1) + 1) * half] for s in range(LANES // half)], axis=0)
            rope_ref[j, rows, :] = (zt * cost_ref[j] + sw * sint_ref[j]).astype(rope_ref.dtype)

    ones_row = jnp.where(lax.broadcasted_iota(jnp.int32, (VROWS - HEAD_DIM, LANES), 0) == 0, 1.0, 0.0)

    def put_values_t(ref, z):
        keys = ref.shape[2]
        for j in range(tm // keys):
            for i in range(keys // LANES):
                zt = z[j * keys + i * LANES:j * keys + (i + 1) * LANES, :].T
                cols = slice(i * LANES, (i + 1) * LANES)
                for g in range(KV_HEADS):
                    ref[j, g * VROWS:g * VROWS + HEAD_DIM, cols] = zt[g * HEAD_DIM:(g + 1) * HEAD_DIM, :].astype(ref.dtype)
                    ref[j, g * VROWS + HEAD_DIM:(g + 1) * VROWS, cols] = ones_row.astype(ref.dtype)

    for c, zc in enumerate(chunks(proj(0, Q_DIM))):
        put_heads_t(c, zc, None, qa_ref)
    z = proj(Q_DIM, 2 * KV_DIM)
    put_padded(ka_ref, 0, rope(z[:, :KV_DIM]), 0.0)
    put_values_t(vat_ref, z[:, KV_DIM:])
    c0 = Q_DIM + 2 * KV_DIM
    for c, zc in enumerate(chunks(proj(c0, Q_DIM))):
        put_heads_t(c, zc, qn_ref, qnr_ref)
    c0 += Q_DIM
    z = proj(c0, 2 * KV_DIM)
    kc_ref[...] = z[:, :KV_DIM]
    vc_ref[...] = z[:, KV_DIM:]
    c0 += 2 * KV_DIM
    z = proj(c0, 2 * KV_DIM)
    put_padded(ksl_ref, 0, rope(z[:, :KV_DIM]), ke_ref[...])
    put_values_t(vslt_ref, z[:, KV_DIM:])
    c0 += 2 * KV_DIM
    z = proj(c0, 2 * KV_DIM)
    put_padded(kw_ref, 0, rope(z[:, :KV_DIM]), 0.0)
    put_values_t(vwt_ref, z[:, KV_DIM:])
    put_transposed(gnt_ref, jax.nn.sigmoid(_dot(xb, wg_ref[...])))


def _in_proj(x2, w_main, w_gn, cos, sin, ke, batch, seq, tm):
    n = x2.shape[0]
    d = x2.shape[1]
    spt = seq // tm
    tok = lambda width: pl.BlockSpec((tm, width), lambda i: (i, 0))
    full = lambda a: pl.BlockSpec(a.shape, lambda i: (0, 0))
    tab = pl.BlockSpec((tm, LANES), lambda i: (i % spt, 0))
    tab_t = pl.BlockSpec((tm // LANES, LANES, LANES), lambda i: (i % spt, 0, 0))
    cos_t, sin_t = (jnp.transpose(t.reshape(seq // LANES, LANES, LANES), (0, 2, 1)) for t in (cos, sin))
    flat = lambda w, dt: (tok(w), jax.ShapeDtypeStruct((n, w), dt))
    trans = lambda r, keys, dt: (pl.BlockSpec((None, tm // keys, r, keys), lambda i: (i // spt, i % spt, 0, 0)),
                                 jax.ShapeDtypeStruct((batch, seq // keys, r, keys), dt))
    outs = [trans(Q_DIM, LANES, BF16), flat(KV_HEADS * LANES, BF16),
            trans(KV_HEADS * VROWS, A_WINDOW, BF16),
            trans(Q_DIM, LANES, BF16), trans(Q_DIM, LANES, BF16),
            flat(KV_DIM, F32), flat(KV_DIM, F32),
            flat(KV_HEADS * LANES, BF16), trans(KV_HEADS * VROWS, KSTEP, BF16),
            flat(KV_HEADS * LANES, BF16), trans(KV_HEADS * VROWS, KSTEP, BF16),
            trans(LANES, LANES, F32)]
    return pl.pallas_call(
        _inproj_kernel,
        grid=(n // tm,),
        in_specs=[tok(d), full(w_main), full(w_gn), tab, tab, tab_t, tab_t, tab],
        out_specs=[o[0] for o in outs],
        out_shape=[o[1] for o in outs],
        compiler_params=pltpu.CompilerParams(dimension_semantics=("parallel",),
                                             vmem_limit_bytes=48 * 1024 * 1024),
        name="in_proj",
    )(x2, w_main, w_gn, cos, sin, cos_t, sin_t, ke)


def _compress_one(src_ref, pe_ref, w1_ref, b1_ref, w2_ref):
    ratio = CMP_BLOCK // CMP_STRIDE
    nchunk = src_ref.shape[0] // CMP_STRIDE
    hid = w1_ref.shape[1]
    parts = [jnp.zeros((KV_HEADS * nchunk, hid), F32) for _ in range(ratio)]
    for l in range(CMP_STRIDE):
        rows = src_ref[pl.ds(l, nchunk, stride=CMP_STRIDE), :]
        rows = jnp.concatenate([rows[:, g * HEAD_DIM:(g + 1) * HEAD_DIM] for g in range(KV_HEADS)], axis=0)
        for j in range(ratio):
            p = j * CMP_STRIDE + l
            a = (rows + pe_ref[p:p + 1, :]).astype(BF16)
            parts[j] = parts[j] + _dot(a, w1_ref[p * HEAD_DIM:(p + 1) * HEAD_DIM, :])
    out = []
    for g in range(KV_HEADS):
        h = parts[0][g * nchunk:(g + 1) * nchunk]
        for j in range(1, ratio):
            h = h + pltpu.roll(parts[j][g * nchunk:(g + 1) * nchunk], nchunk - j, 0)
        h = jax.nn.gelu(h + b1_ref[...])
        out.append(_dot(h.astype(BF16), w2_ref[...]))
    return out


def _compress_kernel(kc_ref, vc_ref, pek_ref, w1k_ref, b1k_ref, w2k_ref,
                     pev_ref, w1v_ref, b1v_ref, w2v_ref, kcmp_ref, vcmpt_ref):
    for g, kc in enumerate(_compress_one(kc_ref, pek_ref, w1k_ref, b1k_ref, w2k_ref)):
        kcmp_ref[g] = kc.astype(BF16)
    for g, vc in enumerate(_compress_one(vc_ref, pev_ref, w1v_ref, b1v_ref, w2v_ref)):
        vcmpt_ref[g] = vc.T[:HEAD_DIM, :].astype(BF16)


def _compress(kc, vc, wk, wv, batch, seq):
    nchunk = seq // CMP_STRIDE
    src = pl.BlockSpec((seq, KV_DIM), lambda b: (b, 0))
    full = lambda a: pl.BlockSpec(a.shape, lambda b: (0,) * a.ndim)
    return pl.pallas_call(
        _compress_kernel,
        grid=(batch,),
        in_specs=[src, src] + [full(a) for a in wk] + [full(a) for a in wv],
        out_specs=[pl.BlockSpec((None, KV_HEADS, nchunk, LANES), lambda b: (b, 0, 0, 0)),
                   pl.BlockSpec((None, KV_HEADS, HEAD_DIM, nchunk), lambda b: (b, 0, 0, 0))],
        out_shape=[jax.ShapeDtypeStruct((batch, KV_HEADS, nchunk, LANES), BF16),
                   jax.ShapeDtypeStruct((batch, KV_HEADS, HEAD_DIM, nchunk), BF16)],
        compiler_params=pltpu.CompilerParams(dimension_semantics=("parallel",)),
        name="compress",
    )(kc, vc, *wk, *wv)


MASK_COL = HEAD_DIM


def _select_blocks_t(imp, t0):
    nb, cols = imp.shape
    j = lax.broadcasted_iota(jnp.int32, (nb, cols), 0)
    cur = (t0 + lax.broadcasted_iota(jnp.int32, (nb, cols), 1)) >> int(np.log2(SLC_BLOCK))
    forced = (j == 0) | (j == cur) | (j == cur - 1)
    score = jnp.where(forced, BIG, jnp.where(j <= cur, imp, -BIG))
    sub = 8
    tiles = [score[v * sub:(v + 1) * sub] for v in range(nb // sub)]
    ranks = [jnp.zeros((sub, cols), F32) for _ in tiles]
    jl = lax.broadcasted_iota(jnp.int32, (sub, cols), 0)
    for jp in range(nb):
        row = score[jp:jp + 1, :]
        for v, tile in enumerate(tiles):
            if v * sub > jp:
                beats = row >= tile
            elif (v + 1) * sub - 1 <= jp:
                beats = row > tile
            else:
                beats = (row > tile) | ((row == tile) & (jl > jp - v * sub))
            ranks[v] = jnp.where(beats, ranks[v] + 1.0, ranks[v])
    rank = jnp.concatenate(ranks, axis=0)
    return jnp.where((rank < float(min(SLC_TOP_N, nb))) & (j <= cur), 1.0, 0.0)


def _put_heads(o_ref, g, o, nq=QBLK):
    for i in range(REP // 2):
        pair = jnp.concatenate([o[:, (2 * i) * nq:(2 * i + 1) * nq],
                                o[:, (2 * i + 1) * nq:(2 * i + 2) * nq]], axis=0)
        col = (g * REP + 2 * i) * HEAD_DIM
        o_ref[:, col:col + LANES] = pair.T.astype(o_ref.dtype)


class _Pipeline:
    def __init__(self, k_ref, vt_ref, q_ref, s_ref, p_ref, acc_ref, chunk_of):
        self.k_ref, self.vt_ref, self.q_ref = k_ref, vt_ref, q_ref
        self.s_ref, self.p_ref, self.acc_ref = s_ref, p_ref, acc_ref
        self.chunk_of = chunk_of
        self.groups = range(KV_HEADS)

    def qk(self, g, j):
        start = pl.multiple_of(self.chunk_of(j) * KSTEP, KSTEP)
        depth = self.q_ref.shape[1]
        return _dot(self.k_ref[pl.ds(start, KSTEP), g * LANES:g * LANES + depth], self.q_ref[g])

    def keep(self, g, st, bias):
        if bias is not None:
            st = st + bias
        self.s_ref[g] = st
        return jnp.max(st, axis=0, keepdims=True)

    def softmax(self, g, m, cmax):
        m_new = jnp.maximum(m, cmax)
        a = jnp.exp2(m - m_new)
        self.p_ref[g] = jnp.exp2(self.s_ref[g] - m_new).astype(BF16)
        return m_new, a

    def values(self, g, j, a):
        vt = self.vt_ref[self.chunk_of(j), g * VROWS:(g + 1) * VROWS, :]
        self.acc_ref[g] = a * self.acc_ref[g] + _dot(vt, self.p_ref[g])

    def start(self, bias0, bias1):
        m0 = jnp.full((1, self.s_ref.shape[2]), NEG, F32)
        cmax = [self.keep(g, self.qk(g, 0), bias0) for g in self.groups]
        st1 = [self.qk(g, 1) for g in self.groups]
        sm = [self.softmax(g, m0, cmax[g]) for g in self.groups]
        cmax = tuple(self.keep(g, st1[g], bias1) for g in self.groups)
        for g in self.groups:
            self.acc_ref[g] = jnp.zeros(self.acc_ref.shape[1:], F32)
        return (tuple(s[0] for s in sm), tuple(s[1] for s in sm), cmax)

    def step(self, j, state, last=False, bias=None):
        m, a, cmax = state
        st = None if last else [self.qk(g, j + 2) for g in self.groups]
        for g in self.groups:
            self.values(g, j, a[g])
        sm = [self.softmax(g, m[g], cmax[g]) for g in self.groups]
        if not last:
            cmax = tuple(self.keep(g, st[g], bias) for g in self.groups)
        return (tuple(s[0] for s in sm), tuple(s[1] for s in sm), cmax)

    def finish(self, count, state):
        _, a, _ = state
        out = []
        for g in self.groups:
            self.values(g, count - 1, a[g])
            acc = self.acc_ref[g]
            out.append(acc[:HEAD_DIM] * (1.0 / acc[HEAD_DIM:HEAD_DIM + 1]))
        return out


def _attn_kernel(sink_ref, abias_ref, pbias_ref, cbias_ref, ovt_ref,
                 qa_ref, ka_ref, vat_ref, qn_ref, qnr_ref, gnt_ref, kcmp_ref, vcmpt_ref,
                 ksl_ref, vslt_ref, kw_ref, vwt_ref,
                 oa_ref, ob_ref,
                 qw_ref, sw_ref, pw_ref, accw_ref, qs_ref, ss_ref, ps_ref, accs_ref, sa_ref, pa_ref, sc_ref):
    n = pl.program_id(1)
    groups = range(KV_HEADS)
    width = REP * QBLK
    nb = ovt_ref.shape[0]
    nc = kcmp_ref.shape[1]
    t0 = pl.multiple_of(n * QBLK, QBLK)
    heads = lambda g: [g * REP + r for r in range(REP)]
    parts = range(QBLK // LANES)
    head_t = lambda ref, h, i: ref[i, h * HEAD_DIM:(h + 1) * HEAD_DIM, :]
    cols = lambda ref, g: jnp.concatenate([head_t(ref, h, i) for h in heads(g) for i in parts], axis=1)
    lane_chunks = lambda a: [a[:, r * QBLK:(r + 1) * QBLK] for r in range(REP)]
    gone = lambda cond: jnp.where(cond, 0.0, NEG)

    assert B_WINDOW == 2 * KSTEP
    win = _Pipeline(kw_ref, vwt_ref, qw_ref, sw_ref, pw_ref, accw_ref, lambda j: jnp.maximum(n - j, 0))
    slc = _Pipeline(ksl_ref, vslt_ref, qs_ref, ss_ref, ps_ref, accs_ref,
                    lambda j: jnp.where(j == 0, n, jnp.maximum(j - 1, 0)))
    slc_count = jnp.maximum(n, 1) + 1

    for g in groups:
        qw_ref[g] = cols(qnr_ref, g)
    a_chunk = [n * len(parts) + i for i in parts]
    a_first = [jnp.maximum(c - 1, 0) for c in a_chunk]
    part_cols = lambda i, g: jnp.concatenate([head_t(qa_ref, h, i) for h in heads(g)], axis=1)
    amax = {}
    for i in parts:
        abias = abias_ref[jnp.minimum(a_chunk[i], 1)]
        astart = pl.multiple_of(a_first[i] * A_WINDOW, A_WINDOW)
        for g in groups:
            st = _dot(ka_ref[pl.ds(astart, 2 * A_WINDOW), g * LANES:g * LANES + HEAD_DIM], part_cols(i, g)) + abias
            sa_ref[i, g] = st
            amax[i, g] = jnp.max(st, axis=0, keepdims=True)
    cstart = pl.multiple_of(cbias_ref.shape[0] - nc - n * (QBLK // CMP_STRIDE), 8)
    cbias = cbias_ref[pl.ds(cstart, nc), :]
    cmax = []
    for g in groups:
        s = _dot(kcmp_ref[g, :, :HEAD_DIM], cols(qn_ref, g)) + cbias
        sc_ref[g] = s
        cmax.append(jnp.max(s, axis=0, keepdims=True))
    win_state = win.start(pbias_ref[0], gone(n >= 1))

    x_swa = {}
    for i in parts:
        for g in groups:
            sk = jnp.concatenate([jnp.full((1, A_WINDOW), sink_ref[h] * LOG2E, F32) for h in heads(g)], axis=1)
            m = jnp.maximum(amax[i, g], sk)
            pa_ref[i, g] = jnp.exp2(sa_ref[i, g] - m).astype(BF16)
            x_swa[i, g] = jnp.exp2(sk - m)
    tq = t0 + (lax.broadcasted_iota(jnp.int32, (1, width), 1) & (QBLK - 1))
    sees_any = tq >= CMP_BLOCK - 1
    p_cmp = []
    for g in groups:
        e = jnp.exp2(sc_ref[g] - cmax[g])
        inv = jnp.where(sees_any, 1.0 / jnp.maximum(jnp.sum(e, axis=0, keepdims=True), 1e-30), 0.0)
        p_cmp.append(e * inv)

    win_state = win.step(0, win_state, bias=pbias_ref[1] + gone(n >= 2))

    for i in parts:
        for g in groups:
            vt = jnp.concatenate([vat_ref[a_first[i] + k, g * VROWS:(g + 1) * VROWS, :] for k in range(2)], axis=1)
            o = _dot(vt, pa_ref[i, g])
            o = o[:HEAD_DIM] * (1.0 / (o[HEAD_DIM:HEAD_DIM + 1] + x_swa[i, g]))
            _put_heads(oa_ref.at[i * A_WINDOW:(i + 1) * A_WINDOW, :], g, o, A_WINDOW)
    o_cmp = [_dot(vcmpt_ref[g], p_cmp[g].astype(BF16)) for g in groups]
    imp = []
    for g in groups:
        pc = lane_chunks(p_cmp[g])
        psum = (pc[0] + pc[1]) + (pc[2] + pc[3])
        hi = psum.astype(BF16)
        lo = (psum - hi.astype(F32)).astype(BF16)
        imp.append(_dot(ovt_ref[...], hi) + _dot(ovt_ref[...], lo))
    o_win = win.finish(3, win.step(1, win_state, last=True))

    jb = lax.broadcasted_iota(jnp.int32, (nb, QBLK), 0)
    curb = (t0 + lax.broadcasted_iota(jnp.int32, (nb, QBLK), 1)) >> int(np.log2(SLC_BLOCK))
    sel = lax.cond((t0 + QBLK - 1) // SLC_BLOCK < SLC_TOP_N,
                   lambda: tuple(jnp.where(jb <= curb, 1.0, 0.0) for _ in groups),
                   lambda: tuple(_select_blocks_t(imp[g], t0) for g in groups))

    for g in groups:
        qs_ref[g, :HEAD_DIM, :] = cols(qnr_ref, g)
        mrows = jnp.concatenate([sel[g] - 1.0, jnp.zeros((LANES - MASK_COL - nb, QBLK), F32)], axis=0).astype(BF16)
        for r in range(REP):
            qs_ref[g, MASK_COL:, r * QBLK:(r + 1) * QBLK] = mrows
    slc_state = slc.start(pbias_ref[0], gone(n >= 1))

    no_bias = gone(n >= 0)
    one_step = lambda j, s: slc.step(j, s, bias=no_bias)
    trips = slc_count - 2
    slc_state = lax.fori_loop(0, trips // 2, lambda i, s: one_step(2 * i + 1, one_step(2 * i, s)), slc_state)
    slc_state = lax.cond(trips % 2 == 1, lambda s: one_step(trips - 1, s), lambda s: s, slc_state)

    o_slc = slc.finish(slc_count, slc.step(slc_count - 2, slc_state, last=True))
    for g in groups:
        gate = lambda br: jnp.concatenate(
            [gnt_ref[i, br * Q_HEADS + h:br * Q_HEADS + h + 1, :]
             for h in heads(g) for i in range(gnt_ref.shape[0])], axis=1)
        _put_heads(ob_ref, g, gate(0) * o_cmp[g] + gate(1) * o_slc[g] + gate(2) * o_win[g])


def _attn(sinks, qa, ka, vat, qn, qnr, gnt, kcmp, vcmpt, ksl, vslt, kw, vwt, batch, seq):
    nq = seq // QBLK
    nc = kcmp.shape[2]
    nb = seq // SLC_BLOCK
    width = REP * QBLK
    assert nb <= LANES - MASK_COL and QBLK % SLC_BLOCK == 0 and QBLK % A_WINDOW == 0 and A_WINDOW % LANES == 0
    c_start = np.arange(nc)[None, :] * CMP_STRIDE
    s_start = np.arange(nb)[:, None] * SLC_BLOCK
    ovt = jnp.asarray((c_start < s_start + SLC_BLOCK) & (s_start < c_start + CMP_BLOCK), BF16)
    ql = np.arange(width)[None, :] % QBLK
    masked = lambda visible: np.where(visible, 0.0, NEG)
    kq = np.arange(KSTEP)[:, None] - ql
    pbias = jnp.asarray(np.stack([masked(kq <= 0),
                                  masked(kq > 0)]), F32)
    band = np.arange(2 * A_WINDOW)[:, None]
    qpart = np.arange(REP * A_WINDOW)[None, :] % A_WINDOW
    in_window = lambda dist: (dist >= 0) & (dist < A_WINDOW)
    abias = jnp.asarray(np.stack([masked(in_window(qpart - band)),
                                  masked(in_window(qpart - (band - A_WINDOW)))]), F32)
    per_blk = QBLK // CMP_STRIDE
    d = np.arange(nc + (nq - 1) * per_blk)[:, None] - (nq - 1) * per_blk
    cbias = jnp.asarray(masked(d * CMP_STRIDE + CMP_BLOCK - 1 <= ql), F32)
    q_spec = pl.BlockSpec((None, QBLK // LANES, Q_DIM, LANES), lambda b, n: (b, n, 0, 0))
    g_spec = pl.BlockSpec((None, QBLK // LANES, LANES, LANES), lambda b, n: (b, n, 0, 0))
    kc_spec = pl.BlockSpec((None, KV_HEADS, nc, LANES), lambda b, n: (b, 0, 0, 0))
    vc_spec = pl.BlockSpec((None, KV_HEADS, HEAD_DIM, nc), lambda b, n: (b, 0, 0, 0))
    const = lambda a: pl.BlockSpec(a.shape, lambda b, n: (0,) * a.ndim, pipeline_mode=pl.Buffered(1))
    k_spec = pl.BlockSpec((seq, KV_HEADS * LANES), lambda b, n: (b, 0))
    vt_spec = lambda keys: pl.BlockSpec((None, seq // keys, KV_HEADS * VROWS, keys), lambda b, n: (b, 0, 0, 0))
    o_spec = pl.BlockSpec((QBLK, Q_DIM), lambda b, n: (b * nq + n, 0))
    o_shape = jax.ShapeDtypeStruct((batch * seq, Q_DIM), BF16)
    pipe_scratch = lambda depth: [
        pltpu.VMEM((KV_HEADS, depth, width), BF16),
        pltpu.VMEM((KV_HEADS, KSTEP, width), F32),
        pltpu.VMEM((KV_HEADS, KSTEP, width), BF16),
        pltpu.VMEM((KV_HEADS, VROWS, width), F32)]
    return pl.pallas_call(
        _attn_kernel,
        grid=(batch, nq),
        in_specs=[pl.BlockSpec(memory_space=pltpu.SMEM), const(abias), const(pbias), const(cbias), const(ovt),
                  q_spec, k_spec, vt_spec(A_WINDOW), q_spec, q_spec, g_spec, kc_spec, vc_spec,
                  k_spec, vt_spec(KSTEP), k_spec, vt_spec(KSTEP)],
        out_specs=[o_spec, o_spec],
        out_shape=[o_shape, o_shape],
        scratch_shapes=pipe_scratch(HEAD_DIM) + pipe_scratch(LANES) + [
            pltpu.VMEM((QBLK // A_WINDOW, KV_HEADS, 2 * A_WINDOW, REP * A_WINDOW), F32),
            pltpu.VMEM((QBLK // A_WINDOW, KV_HEADS, 2 * A_WINDOW, REP * A_WINDOW), BF16),
            pltpu.VMEM((KV_HEADS, nc, width), F32)],
        compiler_params=pltpu.CompilerParams(dimension_semantics=("parallel", "arbitrary"),
                                             vmem_limit_bytes=56 * 1024 * 1024),
        name="attn",
    )(sinks, abias, pbias, cbias, ovt, qa, ka, vat, qn, qnr, gnt, kcmp, vcmpt, ksl, vslt, kw, vwt)


def _layer_norm(r, g, b):
    mu = jnp.mean(r, axis=-1, keepdims=True)
    d = r - mu
    var = jnp.mean(d * d, axis=-1, keepdims=True)
    return d * lax.rsqrt(var + LN_EPS) * g + b


POST_PARTS = 2


def _post_kernel(alpha, x_ref, oa_ref, ob_ref, wgm_ref, wpa_ref, wpb_ref, wout_ref, g_ref, b_ref, h_ref):
    d = x_ref.shape[1]
    rows = [slice(i * (x_ref.shape[0] // POST_PARTS), (i + 1) * (x_ref.shape[0] // POST_PARTS))
            for i in range(POST_PARTS)]
    ys = []
    for r in rows:
        xb = x_ref[r, :].astype(BF16)
        pa = _dot(oa_ref[r, :], wpa_ref[...])
        pb = _dot(ob_ref[r, :], wpb_ref[...])
        ys.append(jax.nn.sigmoid(_dot(xb, wgm_ref[:, :d])) * pa + jax.nn.sigmoid(_dot(xb, wgm_ref[:, d:])) * pb)
    for r, y in zip(rows, ys):
        m = _dot(y.astype(BF16), wout_ref[...])
        h_ref[r, :] = _layer_norm(alpha * x_ref[r, :] + m, g_ref[...], b_ref[...])


def _post(x2, oa, ob, w_gm, wpa, wpb, wout, g, b, alpha, tm):
    n, d = x2.shape
    tok = lambda width: pl.BlockSpec((tm, width), lambda i: (i, 0))
    full = lambda a: pl.BlockSpec(a.shape, lambda i: (0, 0))
    return pl.pallas_call(
        functools.partial(_post_kernel, alpha),
        grid=(n // tm,),
        in_specs=[tok(d), tok(Q_DIM), tok(Q_DIM)] + [full(a) for a in (w_gm, wpa, wpb, wout, g, b)],
        out_specs=tok(d),
        out_shape=jax.ShapeDtypeStruct((n, d), F32),
        compiler_params=pltpu.CompilerParams(dimension_semantics=("parallel",),
                                             vmem_limit_bytes=48 * 1024 * 1024),
        name="post",
    )(x2, oa, ob, w_gm, wpa, wpb, wout, g, b)


FFN_CHUNK = 256


def _ffn_kernel(alpha, h_ref, wg_ref, wu_ref, wd_ref, g_ref, b_ref, o_ref):
    h = h_ref[...]
    hb = h.astype(BF16)
    hidden = wg_ref.shape[1]
    acc = jnp.zeros(h.shape, F32)
    for c in range(hidden // FFN_CHUNK):
        sl = slice(c * FFN_CHUNK, (c + 1) * FFN_CHUNK)
        a = jax.nn.silu(_dot(hb, wg_ref[:, sl])) * _dot(hb, wu_ref[:, sl])
        acc = acc + _dot(a.astype(BF16), wd_ref[sl, :])
    o_ref[...] = _layer_norm(alpha * h + acc, g_ref[...], b_ref[...])


def _ffn(h, wg, wu, wd, g, b, alpha, tm):
    n, d = h.shape
    tok = pl.BlockSpec((tm, d), lambda i: (i, 0))
    once = lambda a: pl.BlockSpec(a.shape, lambda i: (0, 0), pipeline_mode=pl.Buffered(1))
    return pl.pallas_call(
        functools.partial(_ffn_kernel, alpha),
        grid=(n // tm,),
        in_specs=[tok] + [once(a) for a in (wg, wu, wd, g, b)],
        out_specs=tok,
        out_shape=jax.ShapeDtypeStruct((n, d), F32),
        compiler_params=pltpu.CompilerParams(dimension_semantics=("parallel",),
                                             vmem_limit_bytes=56 * 1024 * 1024),
        name="ffn",
    )(h, wg, wu, wd, g, b)


def _position_tables(seq):
    half = HEAD_DIM // 2
    inv = ROPE_THETA ** (-jnp.arange(half, dtype=F32) / half)
    ang = jnp.arange(seq).astype(F32)[:, None] * inv[None, :]
    cos, sin = jnp.cos(ang), jnp.sin(ang)
    reps = LANES // HEAD_DIM
    lane = np.arange(LANES)[None, :]
    blk = (np.arange(seq) // SLC_BLOCK)[:, None]
    ke = jnp.asarray(np.where(lane - MASK_COL == blk, -NEG, 0.0), F32)
    return jnp.tile(cos, (1, 2 * reps)), jnp.tile(jnp.concatenate([-sin, sin], axis=1), (1, reps)), ke


def kernel(x, w_in, sinks, cmp_pe_k, cmp_w1_k, cmp_b1_k, cmp_w2_k, cmp_pe_v, cmp_w1_v, cmp_b1_v, cmp_w2_v,
           w_proj_a, w_proj_b, w_out, ln1_g, ln1_b, w_gate, w_up, w_down, ln2_g, ln2_b):
    batch, seq, d = x.shape
    depth = w_in.shape[0]
    alpha = (2 * depth) ** 0.25
    n_main = 2 * Q_DIM + 8 * KV_DIM
    n_gate = 3 * Q_HEADS
    tm = 512
    cos, sin, ke = _position_tables(seq)
    pad_cols = lambda w: jnp.pad(w, ((0, 0), (0, LANES - w.shape[1]))).astype(BF16)
    xt = x.reshape(batch * seq, d)
    for l in range(depth):
        w_main = w_in[l, :, :n_main].astype(BF16)
        w_gn = pad_cols(w_in[l, :, n_main:n_main + n_gate])
        w_gm = w_in[l, :, n_main + n_gate:].astype(BF16)
        (qa, ka, vat, qn, qnr, kc, vc, ksl, vslt, kw, vwt, gnt) = _in_proj(
            xt, w_main, w_gn, cos, sin, ke, batch, seq, tm)
        wk = (cmp_pe_k[l], cmp_w1_k[l].astype(BF16), cmp_b1_k[l][None, :], pad_cols(cmp_w2_k[l]))
        wv = (cmp_pe_v[l], cmp_w1_v[l].astype(BF16), cmp_b1_v[l][None, :], pad_cols(cmp_w2_v[l]))
        kcmp, vcmpt = _compress(kc, vc, wk, wv, batch, seq)
        oa, ob = _attn(sinks[l], qa, ka, vat, qn, qnr, gnt, kcmp, vcmpt, ksl, vslt, kw, vwt, batch, seq)
        h = _post(xt, oa, ob, w_gm, w_proj_a[l].astype(BF16), w_proj_b[l].astype(BF16), w_out[l].astype(BF16),
                  ln1_g[l][None, :], ln1_b[l][None, :], alpha, tm)
        xt = _ffn(h, w_gate[l].astype(BF16), w_up[l].astype(BF16), w_down[l].astype(BF16),
                  ln2_g[l][None, :], ln2_b[l][None, :], alpha, tm)
    return xt.reshape(batch, seq, d)
```

```python
import functools

import jax
import jax.numpy as jnp
import numpy as np
from jax import lax
from jax.experimental import pallas as pl
from jax.experimental.pallas import tpu as pltpu

HEAD_DIM = 64
ROPE_THETA = 10000.0
Q_HEADS = 8
KV_HEADS = 2
REP = Q_HEADS // KV_HEADS
A_WINDOW = 128
B_WINDOW = 512
CMP_BLOCK = 32
CMP_STRIDE = 16
SLC_BLOCK = 64
SLC_TOP_N = 16
LN_EPS = 1e-5
NEG = -1e30
BIG = 1e9
Q_DIM = Q_HEADS * HEAD_DIM
KV_DIM = KV_HEADS * HEAD_DIM
LANES = 128
QBLK = 256
KSTEP = QBLK
LOG2E = 1.4426950408889634
SCALE = HEAD_DIM ** -0.5 * LOG2E
VROWS = HEAD_DIM + 16

F32 = jnp.float32
BF16 = jnp.bfloat16


def _dot(a, b):
    return jnp.dot(a, b, preferred_element_type=F32)


def _inproj_kernel(x_ref, w_ref, wg_ref, cos_ref, sin_ref, cost_ref, sint_ref, ke_ref,
                   qa_ref, ka_ref, vat_ref, qn_ref, qnr_ref, kc_ref, vc_ref,
                   ksl_ref, vslt_ref, kw_ref, vwt_ref, gnt_ref):
    tm = x_ref.shape[0]
    xb = x_ref[...].astype(BF16)
    cos = cos_ref[...]
    sin = sin_ref[...]
    lane = lax.broadcasted_iota(jnp.int32, (tm, LANES), 1)
    first_half = (lane & (HEAD_DIM - 1)) < (HEAD_DIM // 2)
    low = lane < HEAD_DIM

    def rope(z):
        sw = jnp.where(first_half, pltpu.roll(z, LANES - HEAD_DIM // 2, 1), pltpu.roll(z, HEAD_DIM // 2, 1))
        return z * cos + sw * sin

    def proj(c0, n):
        return _dot(xb, w_ref[:, c0:c0 + n])

    def chunks(z):
        return [z[:, c * LANES:(c + 1) * LANES] for c in range(z.shape[1] // LANES)]

    def put_padded(ref, c, z, fill):
        ref[:, (2 * c) * LANES:(2 * c + 1) * LANES] = jnp.where(low, z, fill).astype(ref.dtype)
        ref[:, (2 * c + 1) * LANES:(2 * c + 2) * LANES] = jnp.where(low, pltpu.roll(z, HEAD_DIM, 1), fill).astype(ref.dtype)

    def put_transposed(ref, z):
        for j in range(tm // LANES):
            ref[j] = z[j * LANES:(j + 1) * LANES, :].T.astype(ref.dtype)

    def put_heads_t(c, z, plain_ref, rope_ref):
        half = HEAD_DIM // 2
        for j in range(tm // LANES):
            zt = z[j * LANES:(j + 1) * LANES, :].T * SCALE
            rows = slice(c * LANES, (c + 1) * LANES)
            if plain_ref is not None:
                plain_ref[j, rows, :] = zt.astype(plain_ref.dtype)
            sw = jnp.concatenate([zt[(s ^ 1) * half:((s ^ 1) + 1) * half] for s in range(LANES // half)], axis=0)
            rope_ref[j, rows, :] = (zt * cost_ref[j] + sw * sint_ref[j]).astype(rope_ref.dtype)

    ones_row = jnp.where(lax.broadcasted_iota(jnp.int32, (VROWS - HEAD_DIM, LANES), 0) == 0, 1.0, 0.0)

    def put_values_t(ref, z):
        keys = ref.shape[2]
        for j in range(tm // keys):
            for i in range(keys // LANES):
                zt = z[j * keys + i * LANES:j * keys + (i + 1) * LANES, :].T
                cols = slice(i * LANES, (i + 1) * LANES)
                for g in range(KV_HEADS):
                    ref[j, g * VROWS:g * VROWS + HEAD_DIM, cols] = zt[g * HEAD_DIM:(g + 1) * HEAD_DIM, :].astype(ref.dtype)
                    ref[j, g * VROWS + HEAD_DIM:(g + 1) * VROWS, cols] = ones_row.astype(ref.dtype)

    for c, zc in enumerate(chunks(proj(0, Q_DIM))):
        put_heads_t(c, zc, None, qa_ref)
    z = proj(Q_DIM, 2 * KV_DIM)
    put_padded(ka_ref, 0, rope(z[:, :KV_DIM]), 0.0)
    put_values_t(vat_ref, z[:, KV_DIM:])
    c0 = Q_DIM + 2 * KV_DIM
    for c, zc in enumerate(chunks(proj(c0, Q_DIM))):
        put_heads_t(c, zc, qn_ref, qnr_ref)
    c0 += Q_DIM
    z = proj(c0, 2 * KV_DIM)
    kc_ref[...] = z[:, :KV_DIM]
    vc_ref[...] = z[:, KV_DIM:]
    c0 += 2 * KV_DIM
    z = proj(c0, 2 * KV_DIM)
    put_padded(ksl_ref, 0, rope(z[:, :KV_DIM]), ke_ref[...])
    put_values_t(vslt_ref, z[:, KV_DIM:])
    c0 += 2 * KV_DIM
    z = proj(c0, 2 * KV_DIM)
    put_padded(kw_ref, 0, rope(z[:, :KV_DIM]), 0.0)
    put_values_t(vwt_ref, z[:, KV_DIM:])
    put_transposed(gnt_ref, jax.nn.sigmoid(_dot(xb, wg_ref[...])))


def _in_proj(x2, w_main, w_gn, cos, sin, ke, batch, seq, tm):
    n = x2.shape[0]
    d = x2.shape[1]
    spt = seq // tm
    tok = lambda width: pl.BlockSpec((tm, width), lambda i: (i, 0))
    full = lambda a: pl.BlockSpec(a.shape, lambda i: (0, 0))
    tab = pl.BlockSpec((tm, LANES), lambda i: (i % spt, 0))
    tab_t = pl.BlockSpec((tm // LANES, LANES, LANES), lambda i: (i % spt, 0, 0))
    cos_t, sin_t = (jnp.transpose(t.reshape(seq // LANES, LANES, LANES), (0, 2, 1)) for t in (cos, sin))
    flat = lambda w, dt: (tok(w), jax.ShapeDtypeStruct((n, w), dt))
    trans = lambda r, keys, dt: (pl.BlockSpec((None, tm // keys, r, keys), lambda i: (i // spt, i % spt, 0, 0)),
                                 jax.ShapeDtypeStruct((batch, seq // keys, r, keys), dt))
    outs = [trans(Q_DIM, LANES, BF16), flat(KV_HEADS * LANES, BF16),
            trans(KV_HEADS * VROWS, A_WINDOW, BF16),
            trans(Q_DIM, LANES, BF16), trans(Q_DIM, LANES, BF16),
            flat(KV_DIM, F32), flat(KV_DIM, F32),
            flat(KV_HEADS * LANES, BF16), trans(KV_HEADS * VROWS, KSTEP, BF16),
            flat(KV_HEADS * LANES, BF16), trans(KV_HEADS * VROWS, KSTEP, BF16),
            trans(LANES, LANES, F32)]
    return pl.pallas_call(
        _inproj_kernel,
        grid=(n // tm,),
        in_specs=[tok(d), full(w_main), full(w_gn), tab, tab, tab_t, tab_t, tab],
        out_specs=[o[0] for o in outs],
        out_shape=[o[1] for o in outs],
        compiler_params=pltpu.CompilerParams(dimension_semantics=("parallel",),
                                             vmem_limit_bytes=48 * 1024 * 1024),
        name="in_proj",
    )(x2, w_main, w_gn, cos, sin, cos_t, sin_t, ke)


CMP_PAIR = 2


def _compress_one(src_ref, pe_ref, w1_ref, b1_ref, w2_ref):
    ratio = CMP_BLOCK // CMP_STRIDE
    nchunk = src_ref.shape[0] // CMP_STRIDE
    parts = [jnp.zeros((nchunk, w1_ref.shape[3]), F32) for _ in range(ratio)]
    for p in range(CMP_STRIDE // CMP_PAIR):
        rows = [src_ref[pl.ds(p * CMP_PAIR + i, nchunk, stride=CMP_STRIDE), :] for i in range(CMP_PAIR)]
        for j in range(ratio):
            pos = j * CMP_STRIDE + p * CMP_PAIR
            a = jnp.concatenate([(rows[i] + pe_ref[pos + i:pos + i + 1, :]).astype(BF16)
                                 for i in range(CMP_PAIR)], axis=1)
            parts[j] = parts[j] + _dot(a, w1_ref[j, p])
    h = parts[0]
    for j in range(1, ratio):
        h = h + pltpu.roll(parts[j], nchunk - j, 0)
    h = jax.nn.gelu(h + b1_ref[...])
    return _dot(h.astype(BF16), w2_ref[...])


def _compress_weights(pe, w1, b1, w2):
    hid = w1.shape[1]
    eye = jnp.eye(KV_HEADS, dtype=F32)
    npair = CMP_STRIDE // CMP_PAIR
    w1 = w1.reshape(CMP_BLOCK // CMP_STRIDE, npair, CMP_PAIR, 1, HEAD_DIM, 1, hid)
    w1 = (w1 * eye[None, None, None, :, None, :, None]).reshape(
        CMP_BLOCK // CMP_STRIDE, npair, CMP_PAIR * KV_DIM, KV_HEADS * hid)
    w2 = jnp.pad(w2, ((0, 0), (0, LANES - w2.shape[1])))
    w2 = (w2[None, :, None, :] * eye[:, None, :, None]).reshape(KV_HEADS * hid, KV_HEADS * LANES)
    return (jnp.tile(pe, (1, KV_HEADS)), w1.astype(BF16), jnp.tile(b1[None, :], (1, KV_HEADS)), w2.astype(BF16))


def _compress_kernel(kc_ref, vc_ref, pek_ref, w1k_ref, b1k_ref, w2k_ref,
                     pev_ref, w1v_ref, b1v_ref, w2v_ref, kcmp_ref, vcmpt_ref):
    kc = _compress_one(kc_ref, pek_ref, w1k_ref, b1k_ref, w2k_ref)
    vc = _compress_one(vc_ref, pev_ref, w1v_ref, b1v_ref, w2v_ref)
    for g in range(KV_HEADS):
        kcmp_ref[g] = kc[:, g * LANES:(g + 1) * LANES].astype(BF16)
        vcmpt_ref[g] = vc[:, g * LANES:(g + 1) * LANES].T[:HEAD_DIM, :].astype(BF16)


def _compress(kc, vc, wk, wv, batch, seq):
    nchunk = seq // CMP_STRIDE
    src = pl.BlockSpec((seq, KV_DIM), lambda b: (b, 0))
    full = lambda a: pl.BlockSpec(a.shape, lambda b: (0,) * a.ndim)
    return pl.pallas_call(
        _compress_kernel,
        grid=(batch,),
        in_specs=[src, src] + [full(a) for a in wk] + [full(a) for a in wv],
        out_specs=[pl.BlockSpec((None, KV_HEADS, nchunk, LANES), lambda b: (b, 0, 0, 0)),
                   pl.BlockSpec((None, KV_HEADS, HEAD_DIM, nchunk), lambda b: (b, 0, 0, 0))],
        out_shape=[jax.ShapeDtypeStruct((batch, KV_HEADS, nchunk, LANES), BF16),
                   jax.ShapeDtypeStruct((batch, KV_HEADS, HEAD_DIM, nchunk), BF16)],
        compiler_params=pltpu.CompilerParams(dimension_semantics=("parallel",)),
        name="compress",
    )(kc, vc, *wk, *wv)


MASK_COL = HEAD_DIM


def _select_blocks_t(imp, t0):
    nb, cols = imp.shape
    j = lax.broadcasted_iota(jnp.int32, (nb, cols), 0)
    cur = (t0 + lax.broadcasted_iota(jnp.int32, (nb, cols), 1)) >> int(np.log2(SLC_BLOCK))
    forced = (j == 0) | (j == cur) | (j == cur - 1)
    score = jnp.where(forced, BIG, jnp.where(j <= cur, imp, -BIG))
    sub = 8
    tiles = [score[v * sub:(v + 1) * sub] for v in range(nb // sub)]
    ranks = [jnp.zeros((sub, cols), F32) for _ in tiles]
    jl = lax.broadcasted_iota(jnp.int32, (sub, cols), 0)
    for jp in range(nb):
        row = score[jp:jp + 1, :]
        for v, tile in enumerate(tiles):
            if v * sub > jp:
                beats = row >= tile
            elif (v + 1) * sub - 1 <= jp:
                beats = row > tile
            else:
                beats = (row > tile) | ((row == tile) & (jl > jp - v * sub))
            ranks[v] = jnp.where(beats, ranks[v] + 1.0, ranks[v])
    rank = jnp.concatenate(ranks, axis=0)
    return jnp.where((rank < float(min(SLC_TOP_N, nb))) & (j <= cur), 1.0, 0.0)


def _put_heads(o_ref, g, o, nq=QBLK):
    for i in range(REP // 2):
        pair = jnp.concatenate([o[:, (2 * i) * nq:(2 * i + 1) * nq],
                                o[:, (2 * i + 1) * nq:(2 * i + 2) * nq]], axis=0)
        col = (g * REP + 2 * i) * HEAD_DIM
        o_ref[:, col:col + LANES] = pair.T.astype(o_ref.dtype)


class _Pipeline:
    def __init__(self, k_ref, vt_ref, q_ref, s_ref, p_ref, acc_ref, chunk_of):
        self.k_ref, self.vt_ref, self.q_ref = k_ref, vt_ref, q_ref
        self.s_ref, self.p_ref, self.acc_ref = s_ref, p_ref, acc_ref
        self.chunk_of = chunk_of
        self.groups = range(KV_HEADS)

    def qk(self, g, j):
        start = pl.multiple_of(self.chunk_of(j) * KSTEP, KSTEP)
        depth = self.q_ref.shape[1]
        return _dot(self.k_ref[pl.ds(start, KSTEP), g * LANES:g * LANES + depth], self.q_ref[g])

    def keep(self, g, st, bias):
        if bias is not None:
            st = st + bias
        self.s_ref[g] = st
        return jnp.max(st, axis=0, keepdims=True)

    def softmax(self, g, m, cmax):
        m_new = jnp.maximum(m, cmax)
        a = jnp.exp2(m - m_new)
        self.p_ref[g] = jnp.exp2(self.s_ref[g] - m_new).astype(BF16)
        return m_new, a

    def values(self, g, j, a):
        vt = self.vt_ref[self.chunk_of(j), g * VROWS:(g + 1) * VROWS, :]
        self.acc_ref[g] = a * self.acc_ref[g] + _dot(vt, self.p_ref[g])

    def start(self, bias0, bias1):
        m0 = jnp.full((1, self.s_ref.shape[2]), NEG, F32)
        cmax = [self.keep(g, self.qk(g, 0), bias0) for g in self.groups]
        st1 = [self.qk(g, 1) for g in self.groups]
        sm = [self.softmax(g, m0, cmax[g]) for g in self.groups]
        cmax = tuple(self.keep(g, st1[g], bias1) for g in self.groups)
        for g in self.groups:
            self.acc_ref[g] = jnp.zeros(self.acc_ref.shape[1:], F32)
        return (tuple(s[0] for s in sm), tuple(s[1] for s in sm), cmax)

    def step(self, j, state, last=False, bias=None):
        m, a, cmax = state
        st = None if last else [self.qk(g, j + 2) for g in self.groups]
        for g in self.groups:
            self.values(g, j, a[g])
        sm = [self.softmax(g, m[g], cmax[g]) for g in self.groups]
        if not last:
            cmax = tuple(self.keep(g, st[g], bias) for g in self.groups)
        return (tuple(s[0] for s in sm), tuple(s[1] for s in sm), cmax)

    def finish(self, count, state):
        _, a, _ = state
        out = []
        for g in self.groups:
            self.values(g, count - 1, a[g])
            acc = self.acc_ref[g]
            out.append(acc[:HEAD_DIM] * (1.0 / acc[HEAD_DIM:HEAD_DIM + 1]))
        return out


def _attn_kernel(sink_ref, abias_ref, pbias_ref, cbias_ref, ovt_ref,
                 qa_ref, ka_ref, vat_ref, qn_ref, qnr_ref, gnt_ref, kcmp_ref, vcmpt_ref,
                 ksl_ref, vslt_ref, kw_ref, vwt_ref,
                 oa_ref, ob_ref,
                 qw_ref, sw_ref, pw_ref, accw_ref, qs_ref, ss_ref, ps_ref, accs_ref, sa_ref, pa_ref, sc_ref):
    n = pl.program_id(1)
    groups = range(KV_HEADS)
    width = REP * QBLK
    nb = ovt_ref.shape[0]
    nc = kcmp_ref.shape[1]
    t0 = pl.multiple_of(n * QBLK, QBLK)
    heads = lambda g: [g * REP + r for r in range(REP)]
    parts = range(QBLK // LANES)
    head_t = lambda ref, h, i: ref[i, h * HEAD_DIM:(h + 1) * HEAD_DIM, :]
    cols = lambda ref, g: jnp.concatenate([head_t(ref, h, i) for h in heads(g) for i in parts], axis=1)
    lane_chunks = lambda a: [a[:, r * QBLK:(r + 1) * QBLK] for r in range(REP)]
    gone = lambda cond: jnp.where(cond, 0.0, NEG)

    assert B_WINDOW == 2 * KSTEP
    win = _Pipeline(kw_ref, vwt_ref, qw_ref, sw_ref, pw_ref, accw_ref, lambda j: jnp.maximum(n - j, 0))
    slc = _Pipeline(ksl_ref, vslt_ref, qs_ref, ss_ref, ps_ref, accs_ref,
                    lambda j: jnp.where(j == 0, n, jnp.maximum(j - 1, 0)))
    slc_count = jnp.maximum(n, 1) + 1

    for g in groups:
        qw_ref[g] = cols(qnr_ref, g)
    a_chunk = [n * len(parts) + i for i in parts]
    a_first = [jnp.maximum(c - 1, 0) for c in a_chunk]
    part_cols = lambda i, g: jnp.concatenate([head_t(qa_ref, h, i) for h in heads(g)], axis=1)
    amax = {}
    for i in parts:
        abias = abias_ref[jnp.minimum(a_chunk[i], 1)]
        astart = pl.multiple_of(a_first[i] * A_WINDOW, A_WINDOW)
        for g in groups:
            st = _dot(ka_ref[pl.ds(astart, 2 * A_WINDOW), g * LANES:g * LANES + HEAD_DIM], part_cols(i, g)) + abias
            sa_ref[i, g] = st
            amax[i, g] = jnp.max(st, axis=0, keepdims=True)
    cstart = pl.multiple_of(cbias_ref.shape[0] - nc - n * (QBLK // CMP_STRIDE), 8)
    cbias = cbias_ref[pl.ds(cstart, nc), :]
    cmax = []
    for g in groups:
        s = _dot(kcmp_ref[g, :, :HEAD_DIM], cols(qn_ref, g)) + cbias
        sc_ref[g] = s
        cmax.append(jnp.max(s, axis=0, keepdims=True))
    win_state = win.start(pbias_ref[0], gone(n >= 1))

    x_swa = {}
    for i in parts:
        for g in groups:
            sk = jnp.concatenate([jnp.full((1, A_WINDOW), sink_ref[h] * LOG2E, F32) for h in heads(g)], axis=1)
            m = jnp.maximum(amax[i, g], sk)
            pa_ref[i, g] = jnp.exp2(sa_ref[i, g] - m).astype(BF16)
            x_swa[i, g] = jnp.exp2(sk - m)
    tq = t0 + (lax.broadcasted_iota(jnp.int32, (1, width), 1) & (QBLK - 1))
    sees_any = tq >= CMP_BLOCK - 1
    p_cmp = []
    for g in groups:
        e = jnp.exp2(sc_ref[g] - cmax[g])
        inv = jnp.where(sees_any, 1.0 / jnp.maximum(jnp.sum(e, axis=0, keepdims=True), 1e-30), 0.0)
        p_cmp.append(e * inv)

    win_state = win.step(0, win_state, bias=pbias_ref[1] + gone(n >= 2))

    for i in parts:
        for g in groups:
            vt = jnp.concatenate([vat_ref[a_first[i] + k, g * VROWS:(g + 1) * VROWS, :] for k in range(2)], axis=1)
            o = _dot(vt, pa_ref[i, g])
            o = o[:HEAD_DIM] * (1.0 / (o[HEAD_DIM:HEAD_DIM + 1] + x_swa[i, g]))
            _put_heads(oa_ref.at[i * A_WINDOW:(i + 1) * A_WINDOW, :], g, o, A_WINDOW)
    o_cmp = [_dot(vcmpt_ref[g], p_cmp[g].astype(BF16)) for g in groups]
    imp = []
    for g in groups:
        pc = lane_chunks(p_cmp[g])
        psum = (pc[0] + pc[1]) + (pc[2] + pc[3])
        hi = psum.astype(BF16)
        lo = (psum - hi.astype(F32)).astype(BF16)
        imp.append(_dot(ovt_ref[...], hi) + _dot(ovt_ref[...], lo))
    o_win = win.finish(3, win.step(1, win_state, last=True))

    jb = lax.broadcasted_iota(jnp.int32, (nb, QBLK), 0)
    curb = (t0 + lax.broadcasted_iota(jnp.int32, (nb, QBLK), 1)) >> int(np.log2(SLC_BLOCK))
    sel = lax.cond((t0 + QBLK - 1) // SLC_BLOCK < SLC_TOP_N,
                   lambda: tuple(jnp.where(jb <= curb, 1.0, 0.0) for _ in groups),
                   lambda: tuple(_select_blocks_t(imp[g], t0) for g in groups))

    for g in groups:
        qs_ref[g, :HEAD_DIM, :] = cols(qnr_ref, g)
        mrows = jnp.concatenate([sel[g] - 1.0, jnp.zeros((LANES - MASK_COL - nb, QBLK), F32)], axis=0).astype(BF16)
        for r in range(REP):
            qs_ref[g, MASK_COL:, r * QBLK:(r + 1) * QBLK] = mrows
    slc_state = slc.start(pbias_ref[0], gone(n >= 1))

    no_bias = gone(n >= 0)
    one_step = lambda j, s: slc.step(j, s, bias=no_bias)
    trips = slc_count - 2
    slc_state = lax.fori_loop(0, trips // 2, lambda i, s: one_step(2 * i + 1, one_step(2 * i, s)), slc_state)
    slc_state = lax.cond(trips % 2 == 1, lambda s: one_step(trips - 1, s), lambda s: s, slc_state)

    o_slc = slc.finish(slc_count, slc.step(slc_count - 2, slc_state, last=True))
    for g in groups:
        gate = lambda br: jnp.concatenate(
            [gnt_ref[i, br * Q_HEADS + h:br * Q_HEADS + h + 1, :]
             for h in heads(g) for i in range(gnt_ref.shape[0])], axis=1)
        _put_heads(ob_ref, g, gate(0) * o_cmp[g] + gate(1) * o_slc[g] + gate(2) * o_win[g])


def _attn(sinks, qa, ka, vat, qn, qnr, gnt, kcmp, vcmpt, ksl, vslt, kw, vwt, batch, seq):
    nq = seq // QBLK
    nc = kcmp.shape[2]
    nb = seq // SLC_BLOCK
    width = REP * QBLK
    assert nb <= LANES - MASK_COL and QBLK % SLC_BLOCK == 0 and QBLK % A_WINDOW == 0 and A_WINDOW % LANES == 0
    c_start = np.arange(nc)[None, :] * CMP_STRIDE
    s_start = np.arange(nb)[:, None] * SLC_BLOCK
    ovt = jnp.asarray((c_start < s_start + SLC_BLOCK) & (s_start < c_start + CMP_BLOCK), BF16)
    ql = np.arange(width)[None, :] % QBLK
    masked = lambda visible: np.where(visible, 0.0, NEG)
    kq = np.arange(KSTEP)[:, None] - ql
    pbias = jnp.asarray(np.stack([masked(kq <= 0),
                                  masked(kq > 0)]), F32)
    band = np.arange(2 * A_WINDOW)[:, None]
    qpart = np.arange(REP * A_WINDOW)[None, :] % A_WINDOW
    in_window = lambda dist: (dist >= 0) & (dist < A_WINDOW)
    abias = jnp.asarray(np.stack([masked(in_window(qpart - band)),
                                  masked(in_window(qpart - (band - A_WINDOW)))]), F32)
    per_blk = QBLK // CMP_STRIDE
    d = np.arange(nc + (nq - 1) * per_blk)[:, None] - (nq - 1) * per_blk
    cbias = jnp.asarray(masked(d * CMP_STRIDE + CMP_BLOCK - 1 <= ql), F32)
    q_spec = pl.BlockSpec((None, QBLK // LANES, Q_DIM, LANES), lambda b, n: (b, n, 0, 0))
    g_spec = pl.BlockSpec((None, QBLK // LANES, LANES, LANES), lambda b, n: (b, n, 0, 0))
    kc_spec = pl.BlockSpec((None, KV_HEADS, nc, LANES), lambda b, n: (b, 0, 0, 0))
    vc_spec = pl.BlockSpec((None, KV_HEADS, HEAD_DIM, nc), lambda b, n: (b, 0, 0, 0))
    const = lambda a: pl.BlockSpec(a.shape, lambda b, n: (0,) * a.ndim, pipeline_mode=pl.Buffered(1))
    k_spec = pl.BlockSpec((seq, KV_HEADS * LANES), lambda b, n: (b, 0))
    vt_spec = lambda keys: pl.BlockSpec((None, seq // keys, KV_HEADS * VROWS, keys), lambda b, n: (b, 0, 0, 0))
    o_spec = pl.BlockSpec((QBLK, Q_DIM), lambda b, n: (b * nq + n, 0))
    o_shape = jax.ShapeDtypeStruct((batch * seq, Q_DIM), BF16)
    pipe_scratch = lambda depth: [
        pltpu.VMEM((KV_HEADS, depth, width), BF16),
        pltpu.VMEM((KV_HEADS, KSTEP, width), F32),
        pltpu.VMEM((KV_HEADS, KSTEP, width), BF16),
        pltpu.VMEM((KV_HEADS, VROWS, width), F32)]
    return pl.pallas_call(
        _attn_kernel,
        grid=(batch, nq),
        in_specs=[pl.BlockSpec(memory_space=pltpu.SMEM), const(abias), const(pbias), const(cbias), const(ovt),
                  q_spec, k_spec, vt_spec(A_WINDOW), q_spec, q_spec, g_spec, kc_spec, vc_spec,
                  k_spec, vt_spec(KSTEP), k_spec, vt_spec(KSTEP)],
        out_specs=[o_spec, o_spec],
        out_shape=[o_shape, o_shape],
        scratch_shapes=pipe_scratch(HEAD_DIM) + pipe_scratch(LANES) + [
            pltpu.VMEM((QBLK // A_WINDOW, KV_HEADS, 2 * A_WINDOW, REP * A_WINDOW), F32),
            pltpu.VMEM((QBLK // A_WINDOW, KV_HEADS, 2 * A_WINDOW, REP * A_WINDOW), BF16),
            pltpu.VMEM((KV_HEADS, nc, width), F32)],
        compiler_params=pltpu.CompilerParams(dimension_semantics=("parallel", "arbitrary"),
                                             vmem_limit_bytes=56 * 1024 * 1024),
        name="attn",
    )(sinks, abias, pbias, cbias, ovt, qa, ka, vat, qn, qnr, gnt, kcmp, vcmpt, ksl, vslt, kw, vwt)


def _layer_norm(r, g, b):
    mu = jnp.mean(r, axis=-1, keepdims=True)
    d = r - mu
    var = jnp.mean(d * d, axis=-1, keepdims=True)
    return d * lax.rsqrt(var + LN_EPS) * g + b


POST_PARTS = 2


def _post_kernel(alpha, x_ref, oa_ref, ob_ref, wgm_ref, wpa_ref, wpb_ref, wout_ref, g_ref, b_ref, h_ref):
    d = x_ref.shape[1]
    rows = [slice(i * (x_ref.shape[0] // POST_PARTS), (i + 1) * (x_ref.shape[0] // POST_PARTS))
            for i in range(POST_PARTS)]
    ys = []
    for r in rows:
        xb = x_ref[r, :].astype(BF16)
        pa = _dot(oa_ref[r, :], wpa_ref[...])
        pb = _dot(ob_ref[r, :], wpb_ref[...])
        ys.append(jax.nn.sigmoid(_dot(xb, wgm_ref[:, :d])) * pa + jax.nn.sigmoid(_dot(xb, wgm_ref[:, d:])) * pb)
    for r, y in zip(rows, ys):
        m = _dot(y.astype(BF16), wout_ref[...])
        h_ref[r, :] = _layer_norm(alpha * x_ref[r, :] + m, g_ref[...], b_ref[...])


def _post(x2, oa, ob, w_gm, wpa, wpb, wout, g, b, alpha, tm):
    n, d = x2.shape
    tok = lambda width: pl.BlockSpec((tm, width), lambda i: (i, 0))
    full = lambda a: pl.BlockSpec(a.shape, lambda i: (0, 0))
    return pl.pallas_call(
        functools.partial(_post_kernel, alpha),
        grid=(n // tm,),
        in_specs=[tok(d), tok(Q_DIM), tok(Q_DIM)] + [full(a) for a in (w_gm, wpa, wpb, wout, g, b)],
        out_specs=tok(d),
        out_shape=jax.ShapeDtypeStruct((n, d), F32),
        compiler_params=pltpu.CompilerParams(dimension_semantics=("parallel",),
                                             vmem_limit_bytes=48 * 1024 * 1024),
        name="post",
    )(x2, oa, ob, w_gm, wpa, wpb, wout, g, b)


FFN_CHUNK = 256


def _ffn_kernel(alpha, h_ref, wg_ref, wu_ref, wd_ref, g_ref, b_ref, o_ref):
    h = h_ref[...]
    hb = h.astype(BF16)
    hidden = wg_ref.shape[1]
    acc = jnp.zeros(h.shape, F32)
    for c in range(hidden // FFN_CHUNK):
        sl = slice(c * FFN_CHUNK, (c + 1) * FFN_CHUNK)
        a = jax.nn.silu(_dot(hb, wg_ref[:, sl])) * _dot(hb, wu_ref[:, sl])
        acc = acc + _dot(a.astype(BF16), wd_ref[sl, :])
    o_ref[...] = _layer_norm(alpha * h + acc, g_ref[...], b_ref[...])


def _ffn(h, wg, wu, wd, g, b, alpha, tm):
    n, d = h.shape
    tok = pl.BlockSpec((tm, d), lambda i: (i, 0))
    once = lambda a: pl.BlockSpec(a.shape, lambda i: (0, 0), pipeline_mode=pl.Buffered(1))
    return pl.pallas_call(
        functools.partial(_ffn_kernel, alpha),
        grid=(n // tm,),
        in_specs=[tok] + [once(a) for a in (wg, wu, wd, g, b)],
        out_specs=tok,
        out_shape=jax.ShapeDtypeStruct((n, d), F32),
        compiler_params=pltpu.CompilerParams(dimension_semantics=("parallel",),
                                             vmem_limit_bytes=56 * 1024 * 1024),
        name="ffn",
    )(h, wg, wu, wd, g, b)


def _position_tables(seq):
    half = HEAD_DIM // 2
    inv = ROPE_THETA ** (-jnp.arange(half, dtype=F32) / half)
    ang = jnp.arange(seq).astype(F32)[:, None] * inv[None, :]
    cos, sin = jnp.cos(ang), jnp.sin(ang)
    reps = LANES // HEAD_DIM
    lane = np.arange(LANES)[None, :]
    blk = (np.arange(seq) // SLC_BLOCK)[:, None]
    ke = jnp.asarray(np.where(lane - MASK_COL == blk, -NEG, 0.0), F32)
    return jnp.tile(cos, (1, 2 * reps)), jnp.tile(jnp.concatenate([-sin, sin], axis=1), (1, reps)), ke


def kernel(x, w_in, sinks, cmp_pe_k, cmp_w1_k, cmp_b1_k, cmp_w2_k, cmp_pe_v, cmp_w1_v, cmp_b1_v, cmp_w2_v,
           w_proj_a, w_proj_b, w_out, ln1_g, ln1_b, w_gate, w_up, w_down, ln2_g, ln2_b):
    batch, seq, d = x.shape
    depth = w_in.shape[0]
    alpha = (2 * depth) ** 0.25
    n_main = 2 * Q_DIM + 8 * KV_DIM
    n_gate = 3 * Q_HEADS
    tm = 512
    cos, sin, ke = _position_tables(seq)
    pad_cols = lambda w: jnp.pad(w, ((0, 0), (0, LANES - w.shape[1]))).astype(BF16)
    xt = x.reshape(batch * seq, d)
    for l in range(depth):
        w_main = w_in[l, :, :n_main].astype(BF16)
        w_gn = pad_cols(w_in[l, :, n_main:n_main + n_gate])
        w_gm = w_in[l, :, n_main + n_gate:].astype(BF16)
        (qa, ka, vat, qn, qnr, kc, vc, ksl, vslt, kw, vwt, gnt) = _in_proj(
            xt, w_main, w_gn, cos, sin, ke, batch, seq, tm)
        wk = _compress_weights(cmp_pe_k[l], cmp_w1_k[l], cmp_b1_k[l], cmp_w2_k[l])
        wv = _compress_weights(cmp_pe_v[l], cmp_w1_v[l], cmp_b1_v[l], cmp_w2_v[l])
        kcmp, vcmpt = _compress(kc, vc, wk, wv, batch, seq)
        oa, ob = _attn(sinks[l], qa, ka, vat, qn, qnr, gnt, kcmp, vcmpt, ksl, vslt, kw, vwt, batch, seq)
        h = _post(xt, oa, ob, w_gm, w_proj_a[l].astype(BF16), w_proj_b[l].astype(BF16), w_out[l].astype(BF16),
                  ln1_g[l][None, :], ln1_b[l][None, :], alpha, tm)
        xt = _ffn(h, w_gate[l].astype(BF16), w_up[l].astype(BF16), w_down[l].astype(BF16),
                  ln2_g[l][None, :], ln2_b[l][None, :], alpha, tm)
    return xt.reshape(batch, seq, d)
```

```python
import functools

import jax
import jax.numpy as jnp
import numpy as np
from jax import lax
from jax.experimental import pallas as pl
from jax.experimental.pallas import tpu as pltpu

HEAD_DIM = 64
ROPE_THETA = 10000.0
Q_HEADS = 8
KV_HEADS = 2
REP = Q_HEADS // KV_HEADS
A_WINDOW = 128
B_WINDOW = 512
CMP_BLOCK = 32
CMP_STRIDE = 16
SLC_BLOCK = 64
SLC_TOP_N = 16
LN_EPS = 1e-5
NEG = -1e30
BIG = 1e9
Q_DIM = Q_HEADS * HEAD_DIM
KV_DIM = KV_HEADS * HEAD_DIM
LANES = 128
QBLK = 256
KSTEP = QBLK
LOG2E = 1.4426950408889634
SCALE = HEAD_DIM ** -0.5 * LOG2E
VROWS = HEAD_DIM + 16

F32 = jnp.float32
BF16 = jnp.bfloat16


def _dot(a, b):
    return jnp.dot(a, b, preferred_element_type=F32)


def _inproj_kernel(x_ref, w_ref, wg_ref, cos_ref, sin_ref, cost_ref, sint_ref, ke_ref,
                   qa_ref, ka_ref, vat_ref, qn_ref, qnr_ref, kc_ref, vc_ref,
                   ksl_ref, vslt_ref, kw_ref, vwt_ref, gnt_ref):
    tm = x_ref.shape[0]
    xb = x_ref[...].astype(BF16)
    cos = cos_ref[...]
    sin = sin_ref[...]
    lane = lax.broadcasted_iota(jnp.int32, (tm, LANES), 1)
    first_half = (lane & (HEAD_DIM - 1)) < (HEAD_DIM // 2)
    low = lane < HEAD_DIM

    def rope(z):
        sw = jnp.where(first_half, pltpu.roll(z, LANES - HEAD_DIM // 2, 1), pltpu.roll(z, HEAD_DIM // 2, 1))
        return z * cos + sw * sin

    def proj(c0, n):
        return _dot(xb, w_ref[:, c0:c0 + n])

    def chunks(z):
        return [z[:, c * LANES:(c + 1) * LANES] for c in range(z.shape[1] // LANES)]

    def put_padded(ref, c, z, fill):
        ref[:, (2 * c) * LANES:(2 * c + 1) * LANES] = jnp.where(low, z, fill).astype(ref.dtype)
        ref[:, (2 * c + 1) * LANES:(2 * c + 2) * LANES] = jnp.where(low, pltpu.roll(z, HEAD_DIM, 1), fill).astype(ref.dtype)

    def put_transposed(ref, z):
        for j in range(tm // LANES):
            ref[j] = z[j * LANES:(j + 1) * LANES, :].T.astype(ref.dtype)

    def put_heads_t(c, z, plain_ref, rope_ref):
        half = HEAD_DIM // 2
        for j in range(tm // LANES):
            zt = z[j * LANES:(j + 1) * LANES, :].T * SCALE
            rows = slice(c * LANES, (c + 1) * LANES)
            if plain_ref is not None:
                plain_ref[j, rows, :] = zt.astype(plain_ref.dtype)
            sw = jnp.concatenate([zt[(s ^ 1) * half:((s ^ 1) + 1) * half] for s in range(LANES // half)], axis=0)
            rope_ref[j, rows, :] = (zt * cost_ref[j] + sw * sint_ref[j]).astype(rope_ref.dtype)

    ones_row = jnp.where(lax.broadcasted_iota(jnp.int32, (VROWS - HEAD_DIM, LANES), 0) == 0, 1.0, 0.0)

    def put_values_t(ref, z):
        keys = ref.shape[2]
        for j in range(tm // keys):
            for i in range(keys // LANES):
                zt = z[j * keys + i * LANES:j * keys + (i + 1) * LANES, :].T
                cols = slice(i * LANES, (i + 1) * LANES)
                for g in range(KV_HEADS):
                    ref[j, g * VROWS:g * VROWS + HEAD_DIM, cols] = zt[g * HEAD_DIM:(g + 1) * HEAD_DIM, :].astype(ref.dtype)
                    ref[j, g * VROWS + HEAD_DIM:(g + 1) * VROWS, cols] = ones_row.astype(ref.dtype)

    for c, zc in enumerate(chunks(proj(0, Q_DIM))):
        put_heads_t(c, zc, None, qa_ref)
    z = proj(Q_DIM, 2 * KV_DIM)
    put_padded(ka_ref, 0, rope(z[:, :KV_DIM]), 0.0)
    put_values_t(vat_ref, z[:, KV_DIM:])
    c0 = Q_DIM + 2 * KV_DIM
    for c, zc in enumerate(chunks(proj(c0, Q_DIM))):
        put_heads_t(c, zc, qn_ref, qnr_ref)
    c0 += Q_DIM
    z = proj(c0, 2 * KV_DIM)
    kc_ref[...] = z[:, :KV_DIM]
    vc_ref[...] = z[:, KV_DIM:]
    c0 += 2 * KV_DIM
    z = proj(c0, 2 * KV_DIM)
    put_padded(ksl_ref, 0, rope(z[:, :KV_DIM]), ke_ref[...])
    put_values_t(vslt_ref, z[:, KV_DIM:])
    c0 += 2 * KV_DIM
    z = proj(c0, 2 * KV_DIM)
    put_padded(kw_ref, 0, rope(z[:, :KV_DIM]), 0.0)
    put_values_t(vwt_ref, z[:, KV_DIM:])
    put_transposed(gnt_ref, jax.nn.sigmoid(_dot(xb, wg_ref[...])))


def _in_proj(x2, w_main, w_gn, cos, sin, ke, batch, seq, tm):
    n = x2.shape[0]
    d = x2.shape[1]
    spt = seq // tm
    tok = lambda width: pl.BlockSpec((tm, width), lambda i: (i, 0))
    full = lambda a: pl.BlockSpec(a.shape, lambda i: (0, 0))
    tab = pl.BlockSpec((tm, LANES), lambda i: (i % spt, 0))
    tab_t = pl.BlockSpec((tm // LANES, LANES, LANES), lambda i: (i % spt, 0, 0))
    cos_t, sin_t = (jnp.transpose(t.reshape(seq // LANES, LANES, LANES), (0, 2, 1)) for t in (cos, sin))
    flat = lambda w, dt: (tok(w), jax.ShapeDtypeStruct((n, w), dt))
    trans = lambda r, keys, dt: (pl.BlockSpec((None, tm // keys, r, keys), lambda i: (i // spt, i % spt, 0, 0)),
                                 jax.ShapeDtypeStruct((batch, seq // keys, r, keys), dt))
    outs = [trans(Q_DIM, LANES, BF16), flat(KV_HEADS * LANES, BF16),
            trans(KV_HEADS * VROWS, A_WINDOW, BF16),
            trans(Q_DIM, LANES, BF16), trans(Q_DIM, LANES, BF16),
            flat(KV_DIM, F32), flat(KV_DIM, F32),
            flat(KV_HEADS * LANES, BF16), trans(KV_HEADS * VROWS, KSTEP, BF16),
            flat(KV_HEADS * LANES, BF16), trans(KV_HEADS * VROWS, KSTEP, BF16),
            trans(LANES, LANES, F32)]
    return pl.pallas_call(
        _inproj_kernel,
        grid=(n // tm,),
        in_specs=[tok(d), full(w_main), full(w_gn), tab, tab, tab_t, tab_t, tab],
        out_specs=[o[0] for o in outs],
        out_shape=[o[1] for o in outs],
        compiler_params=pltpu.CompilerParams(dimension_semantics=("parallel",),
                                             vmem_limit_bytes=48 * 1024 * 1024),
        name="in_proj",
    )(x2, w_main, w_gn, cos, sin, cos_t, sin_t, ke)


CMP_PAIR = 2


def _compress_one(src_ref, pe_ref, w1_ref, b1_ref, w2_ref):
    ratio = CMP_BLOCK // CMP_STRIDE
    nchunk = src_ref.shape[0] // CMP_STRIDE
    both = lambda row: jnp.concatenate([row] * KV_HEADS, axis=1)
    parts = [jnp.zeros((nchunk, w1_ref.shape[3]), F32) for _ in range(ratio)]
    for p in range(CMP_STRIDE // CMP_PAIR):
        rows = [src_ref[pl.ds(p * CMP_PAIR + i, nchunk, stride=CMP_STRIDE), :] for i in range(CMP_PAIR)]
        for j in range(ratio):
            pos = j * CMP_STRIDE + p * CMP_PAIR
            a = jnp.concatenate([(rows[i] + both(pe_ref[pos + i:pos + i + 1, :])).astype(BF16)
                                 for i in range(CMP_PAIR)], axis=1)
            parts[j] = parts[j] + _dot(a, w1_ref[j, p])
    h = parts[0]
    for j in range(1, ratio):
        h = h + pltpu.roll(parts[j], nchunk - j, 0)
    h = jax.nn.gelu(h + both(b1_ref[...]))
    return _dot(h.astype(BF16), w2_ref[...])


def _group_diagonal(w1_ref, w2_ref, w1d_ref, w2d_ref):
    hid = w1_ref.shape[1]
    w1d_ref[...] = jnp.zeros(w1d_ref.shape, w1d_ref.dtype)
    w2d_ref[...] = jnp.zeros(w2d_ref.shape, w2d_ref.dtype)
    for g in range(KV_HEADS):
        w2d_ref[g * hid:(g + 1) * hid, g * LANES:(g + 1) * LANES] = w2_ref[...]
        for j in range(CMP_BLOCK // CMP_STRIDE):
            for p in range(CMP_STRIDE // CMP_PAIR):
                for i in range(CMP_PAIR):
                    pos = j * CMP_STRIDE + p * CMP_PAIR + i
                    row = i * KV_DIM + g * HEAD_DIM
                    w1d_ref[j, p, row:row + HEAD_DIM, g * hid:(g + 1) * hid] = w1_ref[pos * HEAD_DIM:(pos + 1) * HEAD_DIM, :]


def _compress_kernel(kc_ref, vc_ref, pek_ref, w1k_ref, b1k_ref, w2k_ref,
                     pev_ref, w1v_ref, b1v_ref, w2v_ref, kcmp_ref, vcmpt_ref,
                     w1dk_ref, w2dk_ref, w1dv_ref, w2dv_ref):
    @pl.when(pl.program_id(0) == 0)
    def _():
        _group_diagonal(w1k_ref, w2k_ref, w1dk_ref, w2dk_ref)
        _group_diagonal(w1v_ref, w2v_ref, w1dv_ref, w2dv_ref)

    kc = _compress_one(kc_ref, pek_ref, w1dk_ref, b1k_ref, w2dk_ref)
    vc = _compress_one(vc_ref, pev_ref, w1dv_ref, b1v_ref, w2dv_ref)
    for g in range(KV_HEADS):
        kcmp_ref[g] = kc[:, g * LANES:(g + 1) * LANES].astype(BF16)
        vcmpt_ref[g] = vc[:, g * LANES:(g + 1) * LANES].T[:HEAD_DIM, :].astype(BF16)


def _compress(kc, vc, wk, wv, batch, seq):
    nchunk = seq // CMP_STRIDE
    hid = wk[1].shape[1]
    src = pl.BlockSpec((seq, KV_DIM), lambda b: (b, 0))
    full = lambda a: pl.BlockSpec(a.shape, lambda b: (0,) * a.ndim)
    return pl.pallas_call(
        _compress_kernel,
        grid=(batch,),
        in_specs=[src, src] + [full(a) for a in wk] + [full(a) for a in wv],
        out_specs=[pl.BlockSpec((None, KV_HEADS, nchunk, LANES), lambda b: (b, 0, 0, 0)),
                   pl.BlockSpec((None, KV_HEADS, HEAD_DIM, nchunk), lambda b: (b, 0, 0, 0))],
        out_shape=[jax.ShapeDtypeStruct((batch, KV_HEADS, nchunk, LANES), BF16),
                   jax.ShapeDtypeStruct((batch, KV_HEADS, HEAD_DIM, nchunk), BF16)],
        scratch_shapes=[pltpu.VMEM((CMP_BLOCK // CMP_STRIDE, CMP_STRIDE // CMP_PAIR, CMP_PAIR * KV_DIM, KV_HEADS * hid), BF16),
                        pltpu.VMEM((KV_HEADS * hid, KV_HEADS * LANES), BF16)] * 2,
        compiler_params=pltpu.CompilerParams(dimension_semantics=("arbitrary",)),
        name="compress",
    )(kc, vc, *wk, *wv)


MASK_COL = HEAD_DIM


def _select_blocks_t(imp, t0):
    nb, cols = imp.shape
    j = lax.broadcasted_iota(jnp.int32, (nb, cols), 0)
    cur = (t0 + lax.broadcasted_iota(jnp.int32, (nb, cols), 1)) >> int(np.log2(SLC_BLOCK))
    forced = (j == 0) | (j == cur) | (j == cur - 1)
    score = jnp.where(forced, BIG, jnp.where(j <= cur, imp, -BIG))
    sub = 8
    tiles = [score[v * sub:(v + 1) * sub] for v in range(nb // sub)]
    ranks = [jnp.zeros((sub, cols), F32) for _ in tiles]
    jl = lax.broadcasted_iota(jnp.int32, (sub, cols), 0)
    for jp in range(nb):
        row = score[jp:jp + 1, :]
        for v, tile in enumerate(tiles):
            if v * sub > jp:
                beats = row >= tile
            elif (v + 1) * sub - 1 <= jp:
                beats = row > tile
            else:
                beats = (row > tile) | ((row == tile) & (jl > jp - v * sub))
            ranks[v] = jnp.where(beats, ranks[v] + 1.0, ranks[v])
    rank = jnp.concatenate(ranks, axis=0)
    return jnp.where((rank < float(min(SLC_TOP_N, nb))) & (j <= cur), 1.0, 0.0)


def _put_heads(o_ref, g, o, nq=QBLK):
    for i in range(REP // 2):
        pair = jnp.concatenate([o[:, (2 * i) * nq:(2 * i + 1) * nq],
                                o[:, (2 * i + 1) * nq:(2 * i + 2) * nq]], axis=0)
        col = (g * REP + 2 * i) * HEAD_DIM
        o_ref[:, col:col + LANES] = pair.T.astype(o_ref.dtype)


class _Pipeline:
    def __init__(self, k_ref, vt_ref, q_ref, s_ref, p_ref, acc_ref, chunk_of):
        self.k_ref, self.vt_ref, self.q_ref = k_ref, vt_ref, q_ref
        self.s_ref, self.p_ref, self.acc_ref = s_ref, p_ref, acc_ref
        self.chunk_of = chunk_of
        self.groups = range(KV_HEADS)

    def qk(self, g, j):
        start = pl.multiple_of(self.chunk_of(j) * KSTEP, KSTEP)
        depth = self.q_ref.shape[1]
        return _dot(self.k_ref[pl.ds(start, KSTEP), g * LANES:g * LANES + depth], self.q_ref[g])

    def keep(self, g, st, bias):
        if bias is not None:
            st = st + bias
        self.s_ref[g] = st
        return jnp.max(st, axis=0, keepdims=True)

    def softmax(self, g, m, cmax):
        m_new = jnp.maximum(m, cmax)
        a = jnp.exp2(m - m_new)
        self.p_ref[g] = jnp.exp2(self.s_ref[g] - m_new).astype(BF16)
        return m_new, a

    def values(self, g, j, a):
        vt = self.vt_ref[self.chunk_of(j), g * VROWS:(g + 1) * VROWS, :]
        self.acc_ref[g] = a * self.acc_ref[g] + _dot(vt, self.p_ref[g])

    def start(self, bias0, bias1):
        m0 = jnp.full((1, self.s_ref.shape[2]), NEG, F32)
        cmax = [self.keep(g, self.qk(g, 0), bias0) for g in self.groups]
        st1 = [self.qk(g, 1) for g in self.groups]
        sm = [self.softmax(g, m0, cmax[g]) for g in self.groups]
        cmax = tuple(self.keep(g, st1[g], bias1) for g in self.groups)
        for g in self.groups:
            self.acc_ref[g] = jnp.zeros(self.acc_ref.shape[1:], F32)
        return (tuple(s[0] for s in sm), tuple(s[1] for s in sm), cmax)

    def step(self, j, state, last=False, bias=None):
        m, a, cmax = state
        st = None if last else [self.qk(g, j + 2) for g in self.groups]
        for g in self.groups:
            self.values(g, j, a[g])
        sm = [self.softmax(g, m[g], cmax[g]) for g in self.groups]
        if not last:
            cmax = tuple(self.keep(g, st[g], bias) for g in self.groups)
        return (tuple(s[0] for s in sm), tuple(s[1] for s in sm), cmax)

    def finish(self, count, state):
        _, a, _ = state
        out = []
        for g in self.groups:
            self.values(g, count - 1, a[g])
            acc = self.acc_ref[g]
            out.append(acc[:HEAD_DIM] * (1.0 / acc[HEAD_DIM:HEAD_DIM + 1]))
        return out


def _attn_kernel(sink_ref, abias_ref, pbias_ref, cbias_ref, ovt_ref,
                 qa_ref, ka_ref, vat_ref, qn_ref, qnr_ref, gnt_ref, kcmp_ref, vcmpt_ref,
                 ksl_ref, vslt_ref, kw_ref, vwt_ref,
                 oa_ref, ob_ref,
                 qw_ref, sw_ref, pw_ref, accw_ref, qs_ref, ss_ref, ps_ref, accs_ref, sa_ref, pa_ref, sc_ref):
    n = pl.program_id(1)
    groups = range(KV_HEADS)
    width = REP * QBLK
    nb = ovt_ref.shape[0]
    nc = kcmp_ref.shape[1]
    t0 = pl.multiple_of(n * QBLK, QBLK)
    heads = lambda g: [g * REP + r for r in range(REP)]
    parts = range(QBLK // LANES)
    head_t = lambda ref, h, i: ref[i, h * HEAD_DIM:(h + 1) * HEAD_DIM, :]
    cols = lambda ref, g: jnp.concatenate([head_t(ref, h, i) for h in heads(g) for i in parts], axis=1)
    lane_chunks = lambda a: [a[:, r * QBLK:(r + 1) * QBLK] for r in range(REP)]
    gone = lambda cond: jnp.where(cond, 0.0, NEG)

    assert B_WINDOW == 2 * KSTEP
    win = _Pipeline(kw_ref, vwt_ref, qw_ref, sw_ref, pw_ref, accw_ref, lambda j: jnp.maximum(n - j, 0))
    slc = _Pipeline(ksl_ref, vslt_ref, qs_ref, ss_ref, ps_ref, accs_ref,
                    lambda j: jnp.where(j == 0, n, jnp.maximum(j - 1, 0)))
    slc_count = jnp.maximum(n, 1) + 1

    for g in groups:
        qw_ref[g] = cols(qnr_ref, g)
    a_chunk = [n * len(parts) + i for i in parts]
    a_first = [jnp.maximum(c - 1, 0) for c in a_chunk]
    part_cols = lambda i, g: jnp.concatenate([head_t(qa_ref, h, i) for h in heads(g)], axis=1)
    amax = {}
    for i in parts:
        abias = abias_ref[jnp.minimum(a_chunk[i], 1)]
        astart = pl.multiple_of(a_first[i] * A_WINDOW, A_WINDOW)
        for g in groups:
            st = _dot(ka_ref[pl.ds(astart, 2 * A_WINDOW), g * LANES:g * LANES + HEAD_DIM], part_cols(i, g)) + abias
            sa_ref[i, g] = st
            amax[i, g] = jnp.max(st, axis=0, keepdims=True)
    cstart = pl.multiple_of(cbias_ref.shape[0] - nc - n * (QBLK // CMP_STRIDE), 8)
    cbias = cbias_ref[pl.ds(cstart, nc), :]
    cmax = []
    for g in groups:
        s = _dot(kcmp_ref[g, :, :HEAD_DIM], cols(qn_ref, g)) + cbias
        sc_ref[g] = s
        cmax.append(jnp.max(s, axis=0, keepdims=True))
    win_state = win.start(pbias_ref[0], gone(n >= 1))

    x_swa = {}
    for i in parts:
        for g in groups:
            sk = jnp.concatenate([jnp.full((1, A_WINDOW), sink_ref[h] * LOG2E, F32) for h in heads(g)], axis=1)
            m = jnp.maximum(amax[i, g], sk)
            pa_ref[i, g] = jnp.exp2(sa_ref[i, g] - m).astype(BF16)
            x_swa[i, g] = jnp.exp2(sk - m)
    tq = t0 + (lax.broadcasted_iota(jnp.int32, (1, width), 1) & (QBLK - 1))
    sees_any = tq >= CMP_BLOCK - 1
    p_cmp = []
    for g in groups:
        e = jnp.exp2(sc_ref[g] - cmax[g])
        inv = jnp.where(sees_any, 1.0 / jnp.maximum(jnp.sum(e, axis=0, keepdims=True), 1e-30), 0.0)
        p_cmp.append(e * inv)

    win_state = win.step(0, win_state, bias=pbias_ref[1] + gone(n >= 2))

    for i in parts:
        for g in groups:
            vt = jnp.concatenate([vat_ref[a_first[i] + k, g * VROWS:(g + 1) * VROWS, :] for k in range(2)], axis=1)
            o = _dot(vt, pa_ref[i, g])
            o = o[:HEAD_DIM] * (1.0 / (o[HEAD_DIM:HEAD_DIM + 1] + x_swa[i, g]))
            _put_heads(oa_ref.at[i * A_WINDOW:(i + 1) * A_WINDOW, :], g, o, A_WINDOW)
    o_cmp = [_dot(vcmpt_ref[g], p_cmp[g].astype(BF16)) for g in groups]
    imp = []
    for g in groups:
        pc = lane_chunks(p_cmp[g])
        psum = (pc[0] + pc[1]) + (pc[2] + pc[3])
        hi = psum.astype(BF16)
        lo = (psum - hi.astype(F32)).astype(BF16)
        imp.append(_dot(ovt_ref[...], hi) + _dot(ovt_ref[...], lo))
    o_win = win.finish(3, win.step(1, win_state, last=True))

    jb = lax.broadcasted_iota(jnp.int32, (nb, QBLK), 0)
    curb = (t0 + lax.broadcasted_iota(jnp.int32, (nb, QBLK), 1)) >> int(np.log2(SLC_BLOCK))
    sel = lax.cond((t0 + QBLK - 1) // SLC_BLOCK < SLC_TOP_N,
                   lambda: tuple(jnp.where(jb <= curb, 1.0, 0.0) for _ in groups),
                   lambda: tuple(_select_blocks_t(imp[g], t0) for g in groups))

    for g in groups:
        qs_ref[g, :HEAD_DIM, :] = cols(qnr_ref, g)
        mrows = jnp.concatenate([sel[g] - 1.0, jnp.zeros((LANES - MASK_COL - nb, QBLK), F32)], axis=0).astype(BF16)
        for r in range(REP):
            qs_ref[g, MASK_COL:, r * QBLK:(r + 1) * QBLK] = mrows
    slc_state = slc.start(pbias_ref[0], gone(n >= 1))

    no_bias = gone(n >= 0)
    one_step = lambda j, s: slc.step(j, s, bias=no_bias)
    trips = slc_count - 2
    slc_state = lax.fori_loop(0, trips // 2, lambda i, s: one_step(2 * i + 1, one_step(2 * i, s)), slc_state)
    slc_state = lax.cond(trips % 2 == 1, lambda s: one_step(trips - 1, s), lambda s: s, slc_state)

    o_slc = slc.finish(slc_count, slc.step(slc_count - 2, slc_state, last=True))
    for g in groups:
        gate = lambda br: jnp.concatenate(
            [gnt_ref[i, br * Q_HEADS + h:br * Q_HEADS + h + 1, :]
             for h in heads(g) for i in range(gnt_ref.shape[0])], axis=1)
        _put_heads(ob_ref, g, gate(0) * o_cmp[g] + gate(1) * o_slc[g] + gate(2) * o_win[g])


def _attn(sinks, qa, ka, vat, qn, qnr, gnt, kcmp, vcmpt, ksl, vslt, kw, vwt, batch, seq):
    nq = seq // QBLK
    nc = kcmp.shape[2]
    nb = seq // SLC_BLOCK
    width = REP * QBLK
    assert nb <= LANES - MASK_COL and QBLK % SLC_BLOCK == 0 and QBLK % A_WINDOW == 0 and A_WINDOW % LANES == 0
    c_start = np.arange(nc)[None, :] * CMP_STRIDE
    s_start = np.arange(nb)[:, None] * SLC_BLOCK
    ovt = jnp.asarray((c_start < s_start + SLC_BLOCK) & (s_start < c_start + CMP_BLOCK), BF16)
    ql = np.arange(width)[None, :] % QBLK
    masked = lambda visible: np.where(visible, 0.0, NEG)
    kq = np.arange(KSTEP)[:, None] - ql
    pbias = jnp.asarray(np.stack([masked(kq <= 0),
                                  masked(kq > 0)]), F32)
    band = np.arange(2 * A_WINDOW)[:, None]
    qpart = np.arange(REP * A_WINDOW)[None, :] % A_WINDOW
    in_window = lambda dist: (dist >= 0) & (dist < A_WINDOW)
    abias = jnp.asarray(np.stack([masked(in_window(qpart - band)),
                                  masked(in_window(qpart - (band - A_WINDOW)))]), F32)
    per_blk = QBLK // CMP_STRIDE
    d = np.arange(nc + (nq - 1) * per_blk)[:, None] - (nq - 1) * per_blk
    cbias = jnp.asarray(masked(d * CMP_STRIDE + CMP_BLOCK - 1 <= ql), F32)
    q_spec = pl.BlockSpec((None, QBLK // LANES, Q_DIM, LANES), lambda b, n: (b, n, 0, 0))
    g_spec = pl.BlockSpec((None, QBLK // LANES, LANES, LANES), lambda b, n: (b, n, 0, 0))
    kc_spec = pl.BlockSpec((None, KV_HEADS, nc, LANES), lambda b, n: (b, 0, 0, 0))
    vc_spec = pl.BlockSpec((None, KV_HEADS, HEAD_DIM, nc), lambda b, n: (b, 0, 0, 0))
    const = lambda a: pl.BlockSpec(a.shape, lambda b, n: (0,) * a.ndim, pipeline_mode=pl.Buffered(1))
    k_spec = pl.BlockSpec((seq, KV_HEADS * LANES), lambda b, n: (b, 0))
    vt_spec = lambda keys: pl.BlockSpec((None, seq // keys, KV_HEADS * VROWS, keys), lambda b, n: (b, 0, 0, 0))
    o_spec = pl.BlockSpec((QBLK, Q_DIM), lambda b, n: (b * nq + n, 0))
    o_shape = jax.ShapeDtypeStruct((batch * seq, Q_DIM), BF16)
    pipe_scratch = lambda depth: [
        pltpu.VMEM((KV_HEADS, depth, width), BF16),
        pltpu.VMEM((KV_HEADS, KSTEP, width), F32),
        pltpu.VMEM((KV_HEADS, KSTEP, width), BF16),
        pltpu.VMEM((KV_HEADS, VROWS, width), F32)]
    return pl.pallas_call(
        _attn_kernel,
        grid=(batch, nq),
        in_specs=[pl.BlockSpec(memory_space=pltpu.SMEM), const(abias), const(pbias), const(cbias), const(ovt),
                  q_spec, k_spec, vt_spec(A_WINDOW), q_spec, q_spec, g_spec, kc_spec, vc_spec,
                  k_spec, vt_spec(KSTEP), k_spec, vt_spec(KSTEP)],
        out_specs=[o_spec, o_spec],
        out_shape=[o_shape, o_shape],
        scratch_shapes=pipe_scratch(HEAD_DIM) + pipe_scratch(LANES) + [
            pltpu.VMEM((QBLK // A_WINDOW, KV_HEADS, 2 * A_WINDOW, REP * A_WINDOW), F32),
            pltpu.VMEM((QBLK // A_WINDOW, KV_HEADS, 2 * A_WINDOW, REP * A_WINDOW), BF16),
            pltpu.VMEM((KV_HEADS, nc, width), F32)],
        compiler_params=pltpu.CompilerParams(dimension_semantics=("parallel", "arbitrary"),
                                             vmem_limit_bytes=56 * 1024 * 1024),
        name="attn",
    )(sinks, abias, pbias, cbias, ovt, qa, ka, vat, qn, qnr, gnt, kcmp, vcmpt, ksl, vslt, kw, vwt)


def _layer_norm(r, g, b):
    mu = jnp.mean(r, axis=-1, keepdims=True)
    d = r - mu
    var = jnp.mean(d * d, axis=-1, keepdims=True)
    return d * lax.rsqrt(var + LN_EPS) * g + b


POST_PARTS = 2


def _post_kernel(alpha, x_ref, oa_ref, ob_ref, wgm_ref, wpa_ref, wpb_ref, wout_ref, g_ref, b_ref, h_ref):
    d = x_ref.shape[1]
    rows = [slice(i * (x_ref.shape[0] // POST_PARTS), (i + 1) * (x_ref.shape[0] // POST_PARTS))
            for i in range(POST_PARTS)]
    ys = []
    for r in rows:
        xb = x_ref[r, :].astype(BF16)
        pa = _dot(oa_ref[r, :], wpa_ref[...])
        pb = _dot(ob_ref[r, :], wpb_ref[...])
        ys.append(jax.nn.sigmoid(_dot(xb, wgm_ref[:, :d])) * pa + jax.nn.sigmoid(_dot(xb, wgm_ref[:, d:])) * pb)
    for r, y in zip(rows, ys):
        m = _dot(y.astype(BF16), wout_ref[...])
        h_ref[r, :] = _layer_norm(alpha * x_ref[r, :] + m, g_ref[...], b_ref[...])


def _post(x2, oa, ob, w_gm, wpa, wpb, wout, g, b, alpha, tm):
    n, d = x2.shape
    tok = lambda width: pl.BlockSpec((tm, width), lambda i: (i, 0))
    full = lambda a: pl.BlockSpec(a.shape, lambda i: (0, 0))
    return pl.pallas_call(
        functools.partial(_post_kernel, alpha),
        grid=(n // tm,),
        in_specs=[tok(d), tok(Q_DIM), tok(Q_DIM)] + [full(a) for a in (w_gm, wpa, wpb, wout, g, b)],
        out_specs=tok(d),
        out_shape=jax.ShapeDtypeStruct((n, d), F32),
        compiler_params=pltpu.CompilerParams(dimension_semantics=("parallel",),
                                             vmem_limit_bytes=48 * 1024 * 1024),
        name="post",
    )(x2, oa, ob, w_gm, wpa, wpb, wout, g, b)


FFN_CHUNK = 256


def _ffn_kernel(alpha, h_ref, wg_ref, wu_ref, wd_ref, g_ref, b_ref, o_ref):
    h = h_ref[...]
    hb = h.astype(BF16)
    hidden = wg_ref.shape[1]
    acc = jnp.zeros(h.shape, F32)
    for c in range(hidden // FFN_CHUNK):
        sl = slice(c * FFN_CHUNK, (c + 1) * FFN_CHUNK)
        a = jax.nn.silu(_dot(hb, wg_ref[:, sl])) * _dot(hb, wu_ref[:, sl])
        acc = acc + _dot(a.astype(BF16), wd_ref[sl, :])
    o_ref[...] = _layer_norm(alpha * h + acc, g_ref[...], b_ref[...])


def _ffn(h, wg, wu, wd, g, b, alpha, tm):
    n, d = h.shape
    tok = pl.BlockSpec((tm, d), lambda i: (i, 0))
    once = lambda a: pl.BlockSpec(a.shape, lambda i: (0, 0), pipeline_mode=pl.Buffered(1))
    return pl.pallas_call(
        functools.partial(_ffn_kernel, alpha),
        grid=(n // tm,),
        in_specs=[tok] + [once(a) for a in (wg, wu, wd, g, b)],
        out_specs=tok,
        out_shape=jax.ShapeDtypeStruct((n, d), F32),
        compiler_params=pltpu.CompilerParams(dimension_semantics=("parallel",),
                                             vmem_limit_bytes=56 * 1024 * 1024),
        name="ffn",
    )(h, wg, wu, wd, g, b)


def _position_tables(seq):
    half = HEAD_DIM // 2
    inv = ROPE_THETA ** (-jnp.arange(half, dtype=F32) / half)
    ang = jnp.arange(seq).astype(F32)[:, None] * inv[None, :]
    cos, sin = jnp.cos(ang), jnp.sin(ang)
    reps = LANES // HEAD_DIM
    lane = np.arange(LANES)[None, :]
    blk = (np.arange(seq) // SLC_BLOCK)[:, None]
    ke = jnp.asarray(np.where(lane - MASK_COL == blk, -NEG, 0.0), F32)
    return jnp.tile(cos, (1, 2 * reps)), jnp.tile(jnp.concatenate([-sin, sin], axis=1), (1, reps)), ke


def kernel(x, w_in, sinks, cmp_pe_k, cmp_w1_k, cmp_b1_k, cmp_w2_k, cmp_pe_v, cmp_w1_v, cmp_b1_v, cmp_w2_v,
           w_proj_a, w_proj_b, w_out, ln1_g, ln1_b, w_gate, w_up, w_down, ln2_g, ln2_b):
    batch, seq, d = x.shape
    depth = w_in.shape[0]
    alpha = (2 * depth) ** 0.25
    n_main = 2 * Q_DIM + 8 * KV_DIM
    n_gate = 3 * Q_HEADS
    tm = 512
    cos, sin, ke = _position_tables(seq)
    pad_cols = lambda w: jnp.pad(w, ((0, 0), (0, LANES - w.shape[1]))).astype(BF16)
    xt = x.reshape(batch * seq, d)
    for l in range(depth):
        w_main = w_in[l, :, :n_main].astype(BF16)
        w_gn = pad_cols(w_in[l, :, n_main:n_main + n_gate])
        w_gm = w_in[l, :, n_main + n_gate:].astype(BF16)
        (qa, ka, vat, qn, qnr, kc, vc, ksl, vslt, kw, vwt, gnt) = _in_proj(
            xt, w_main, w_gn, cos, sin, ke, batch, seq, tm)
        wk = (cmp_pe_k[l], cmp_w1_k[l].astype(BF16), cmp_b1_k[l][None, :], pad_cols(cmp_w2_k[l]))
        wv = (cmp_pe_v[l], cmp_w1_v[l].astype(BF16), cmp_b1_v[l][None, :], pad_cols(cmp_w2_v[l]))
        kcmp, vcmpt = _compress(kc, vc, wk, wv, batch, seq)
        oa, ob = _attn(sinks[l], qa, ka, vat, qn, qnr, gnt, kcmp, vcmpt, ksl, vslt, kw, vwt, batch, seq)
        h = _post(xt, oa, ob, w_gm, w_proj_a[l].astype(BF16), w_proj_b[l].astype(BF16), w_out[l].astype(BF16),
                  ln1_g[l][None, :], ln1_b[l][None, :], alpha, tm)
        xt = _ffn(h, w_gate[l].astype(BF16), w_up[l].astype(BF16), w_down[l].astype(BF16),
                  ln2_g[l][None, :], ln2_b[l][None, :], alpha, tm)
    return xt.reshape(batch, seq, d)
```

```python
import functools

import jax
import jax.numpy as jnp
import numpy as np
from jax import lax
from jax.experimental import pallas as pl
from jax.experimental.pallas import tpu as pltpu

HEAD_DIM = 64
ROPE_THETA = 10000.0
Q_HEADS = 8
KV_HEADS = 2
REP = Q_HEADS // KV_HEADS
A_WINDOW = 128
B_WINDOW = 512
CMP_BLOCK = 32
CMP_STRIDE = 16
SLC_BLOCK = 64
SLC_TOP_N = 16
LN_EPS = 1e-5
NEG = -1e30
BIG = 1e9
Q_DIM = Q_HEADS * HEAD_DIM
KV_DIM = KV_HEADS * HEAD_DIM
LANES = 128
QBLK = 256
KSTEP = QBLK
LOG2E = 1.4426950408889634
SCALE = HEAD_DIM ** -0.5 * LOG2E
VROWS = HEAD_DIM + 16
ROW_TILE = 512
MIB = 1024 * 1024
VMEM_PROJ = 48 * MIB
VMEM_WIDE = 56 * MIB

F32 = jnp.float32
BF16 = jnp.bfloat16


def _dot(a, b):
    return jnp.dot(a, b, preferred_element_type=F32)


def _inproj_kernel(x_ref, w_ref, wg_ref, cos_ref, sin_ref, cost_ref, sint_ref, ke_ref,
                   qa_ref, ka_ref, vat_ref, qn_ref, qnr_ref, kc_ref, vc_ref,
                   ksl_ref, vslt_ref, kw_ref, vwt_ref, gnt_ref):
    tm = x_ref.shape[0]
    xb = x_ref[...].astype(BF16)
    cos = cos_ref[...]
    sin = sin_ref[...]
    lane = lax.broadcasted_iota(jnp.int32, (tm, LANES), 1)
    first_half = (lane & (HEAD_DIM - 1)) < (HEAD_DIM // 2)
    low = lane < HEAD_DIM

    def rope(z):
        sw = jnp.where(first_half, pltpu.roll(z, LANES - HEAD_DIM // 2, 1), pltpu.roll(z, HEAD_DIM // 2, 1))
        return z * cos + sw * sin

    def proj(c0, n):
        return _dot(xb, w_ref[:, c0:c0 + n])

    def chunks(z):
        return [z[:, c * LANES:(c + 1) * LANES] for c in range(z.shape[1] // LANES)]

    def put_padded(ref, c, z, fill):
        ref[:, (2 * c) * LANES:(2 * c + 1) * LANES] = jnp.where(low, z, fill).astype(ref.dtype)
        ref[:, (2 * c + 1) * LANES:(2 * c + 2) * LANES] = jnp.where(low, pltpu.roll(z, HEAD_DIM, 1), fill).astype(ref.dtype)

    def put_transposed(ref, z):
        for j in range(tm // LANES):
            ref[j] = z[j * LANES:(j + 1) * LANES, :].T.astype(ref.dtype)

    def put_heads_t(c, z, plain_ref, rope_ref):
        half = HEAD_DIM // 2
        for j in range(tm // LANES):
            zt = z[j * LANES:(j + 1) * LANES, :].T * SCALE
            rows = slice(c * LANES, (c + 1) * LANES)
            if plain_ref is not None:
                plain_ref[j, rows, :] = zt.astype(plain_ref.dtype)
            sw = jnp.concatenate([zt[(s ^ 1) * half:((s ^ 1) + 1) * half] for s in range(LANES // half)], axis=0)
            rope_ref[j, rows, :] = (zt * cost_ref[j] + sw * sint_ref[j]).astype(rope_ref.dtype)

    ones_row = jnp.where(lax.broadcasted_iota(jnp.int32, (VROWS - HEAD_DIM, LANES), 0) == 0, 1.0, 0.0)

    def put_values_t(ref, z):
        keys = ref.shape[2]
        for j in range(tm // keys):
            for i in range(keys // LANES):
                zt = z[j * keys + i * LANES:j * keys + (i + 1) * LANES, :].T
                cols = slice(i * LANES, (i + 1) * LANES)
                for g in range(KV_HEADS):
                    ref[j, g * VROWS:g * VROWS + HEAD_DIM, cols] = zt[g * HEAD_DIM:(g + 1) * HEAD_DIM, :].astype(ref.dtype)
                    ref[j, g * VROWS + HEAD_DIM:(g + 1) * VROWS, cols] = ones_row.astype(ref.dtype)

    for c, zc in enumerate(chunks(proj(0, Q_DIM))):
        put_heads_t(c, zc, None, qa_ref)
    z = proj(Q_DIM, 2 * KV_DIM)
    put_padded(ka_ref, 0, rope(z[:, :KV_DIM]), 0.0)
    put_values_t(vat_ref, z[:, KV_DIM:])
    c0 = Q_DIM + 2 * KV_DIM
    for c, zc in enumerate(chunks(proj(c0, Q_DIM))):
        put_heads_t(c, zc, qn_ref, qnr_ref)
    c0 += Q_DIM
    z = proj(c0, 2 * KV_DIM)
    kc_ref[...] = z[:, :KV_DIM]
    vc_ref[...] = z[:, KV_DIM:]
    c0 += 2 * KV_DIM
    z = proj(c0, 2 * KV_DIM)
    put_padded(ksl_ref, 0, rope(z[:, :KV_DIM]), ke_ref[...])
    put_values_t(vslt_ref, z[:, KV_DIM:])
    c0 += 2 * KV_DIM
    z = proj(c0, 2 * KV_DIM)
    put_padded(kw_ref, 0, rope(z[:, :KV_DIM]), 0.0)
    put_values_t(vwt_ref, z[:, KV_DIM:])
    put_transposed(gnt_ref, jax.nn.sigmoid(_dot(xb, wg_ref[...])))


def _in_proj(x2, w_main, w_gn, cos, sin, ke, batch, seq, tm):
    n = x2.shape[0]
    d = x2.shape[1]
    spt = seq // tm
    tok = lambda width: pl.BlockSpec((tm, width), lambda i: (i, 0))
    full = lambda a: pl.BlockSpec(a.shape, lambda i: (0, 0))
    tab = pl.BlockSpec((tm, LANES), lambda i: (i % spt, 0))
    tab_t = pl.BlockSpec((tm // LANES, LANES, LANES), lambda i: (i % spt, 0, 0))
    cos_t, sin_t = (jnp.transpose(t.reshape(seq // LANES, LANES, LANES), (0, 2, 1)) for t in (cos, sin))
    flat = lambda w, dt: (tok(w), jax.ShapeDtypeStruct((n, w), dt))
    trans = lambda r, keys, dt: (pl.BlockSpec((None, tm // keys, r, keys), lambda i: (i // spt, i % spt, 0, 0)),
                                 jax.ShapeDtypeStruct((batch, seq // keys, r, keys), dt))
    outs = [trans(Q_DIM, LANES, BF16), flat(KV_HEADS * LANES, BF16),
            trans(KV_HEADS * VROWS, A_WINDOW, BF16),
            trans(Q_DIM, LANES, BF16), trans(Q_DIM, LANES, BF16),
            flat(KV_DIM, F32), flat(KV_DIM, F32),
            flat(KV_HEADS * LANES, BF16), trans(KV_HEADS * VROWS, KSTEP, BF16),
            flat(KV_HEADS * LANES, BF16), trans(KV_HEADS * VROWS, KSTEP, BF16),
            trans(LANES, LANES, F32)]
    return pl.pallas_call(
        _inproj_kernel,
        grid=(n // tm,),
        in_specs=[tok(d), full(w_main), full(w_gn), tab, tab, tab_t, tab_t, tab],
        out_specs=[o[0] for o in outs],
        out_shape=[o[1] for o in outs],
        compiler_params=pltpu.CompilerParams(dimension_semantics=("parallel",),
                                             vmem_limit_bytes=VMEM_PROJ),
        name="in_proj",
    )(x2, w_main, w_gn, cos, sin, cos_t, sin_t, ke)


CMP_PAIR = 2


def _compress_one(src_ref, pe_ref, w1_ref, b1_ref, w2_ref):
    ratio = CMP_BLOCK // CMP_STRIDE
    nchunk = src_ref.shape[0] // CMP_STRIDE
    both = lambda row: jnp.concatenate([row] * KV_HEADS, axis=1)
    parts = [jnp.zeros((nchunk, w1_ref.shape[3]), F32) for _ in range(ratio)]
    for p in range(CMP_STRIDE // CMP_PAIR):
        rows = [src_ref[pl.ds(p * CMP_PAIR + i, nchunk, stride=CMP_STRIDE), :] for i in range(CMP_PAIR)]
        for j in range(ratio):
            pos = j * CMP_STRIDE + p * CMP_PAIR
            a = jnp.concatenate([(rows[i] + both(pe_ref[pos + i:pos + i + 1, :])).astype(BF16)
                                 for i in range(CMP_PAIR)], axis=1)
            parts[j] = parts[j] + _dot(a, w1_ref[j, p])
    h = parts[0]
    for j in range(1, ratio):
        h = h + pltpu.roll(parts[j], nchunk - j, 0)
    h = jax.nn.gelu(h + both(b1_ref[...]))
    return _dot(h.astype(BF16), w2_ref[...])


def _group_diagonal(w1_ref, w2_ref, w1d_ref, w2d_ref):
    hid = w1_ref.shape[1]
    w1d_ref[...] = jnp.zeros(w1d_ref.shape, w1d_ref.dtype)
    w2d_ref[...] = jnp.zeros(w2d_ref.shape, w2d_ref.dtype)
    for g in range(KV_HEADS):
        w2d_ref[g * hid:(g + 1) * hid, g * LANES:(g + 1) * LANES] = w2_ref[...]
        for j in range(CMP_BLOCK // CMP_STRIDE):
            for p in range(CMP_STRIDE // CMP_PAIR):
                for i in range(CMP_PAIR):
                    pos = j * CMP_STRIDE + p * CMP_PAIR + i
                    row = i * KV_DIM + g * HEAD_DIM
                    w1d_ref[j, p, row:row + HEAD_DIM, g * hid:(g + 1) * hid] = w1_ref[pos * HEAD_DIM:(pos + 1) * HEAD_DIM, :]


def _compress_kernel(kc_ref, vc_ref, pek_ref, w1k_ref, b1k_ref, w2k_ref,
                     pev_ref, w1v_ref, b1v_ref, w2v_ref, kcmp_ref, vcmpt_ref,
                     w1dk_ref, w2dk_ref, w1dv_ref, w2dv_ref):
    @pl.when(pl.program_id(0) == 0)
    def _():
        _group_diagonal(w1k_ref, w2k_ref, w1dk_ref, w2dk_ref)
        _group_diagonal(w1v_ref, w2v_ref, w1dv_ref, w2dv_ref)

    kc = _compress_one(kc_ref, pek_ref, w1dk_ref, b1k_ref, w2dk_ref)
    vc = _compress_one(vc_ref, pev_ref, w1dv_ref, b1v_ref, w2dv_ref)
    for g in range(KV_HEADS):
        kcmp_ref[g] = kc[:, g * LANES:(g + 1) * LANES].astype(BF16)
        vcmpt_ref[g] = vc[:, g * LANES:(g + 1) * LANES].T[:HEAD_DIM, :].astype(BF16)


def _compress(kc, vc, wk, wv, batch, seq):
    nchunk = seq // CMP_STRIDE
    hid = wk[1].shape[1]
    src = pl.BlockSpec((seq, KV_DIM), lambda b: (b, 0))
    full = lambda a: pl.BlockSpec(a.shape, lambda b: (0,) * a.ndim)
    return pl.pallas_call(
        _compress_kernel,
        grid=(batch,),
        in_specs=[src, src] + [full(a) for a in wk] + [full(a) for a in wv],
        out_specs=[pl.BlockSpec((None, KV_HEADS, nchunk, LANES), lambda b: (b, 0, 0, 0)),
                   pl.BlockSpec((None, KV_HEADS, HEAD_DIM, nchunk), lambda b: (b, 0, 0, 0))],
        out_shape=[jax.ShapeDtypeStruct((batch, KV_HEADS, nchunk, LANES), BF16),
                   jax.ShapeDtypeStruct((batch, KV_HEADS, HEAD_DIM, nchunk), BF16)],
        scratch_shapes=[pltpu.VMEM((CMP_BLOCK // CMP_STRIDE, CMP_STRIDE // CMP_PAIR, CMP_PAIR * KV_DIM, KV_HEADS * hid), BF16),
                        pltpu.VMEM((KV_HEADS * hid, KV_HEADS * LANES), BF16)] * 2,
        compiler_params=pltpu.CompilerParams(dimension_semantics=("arbitrary",)),
        name="compress",
    )(kc, vc, *wk, *wv)


MASK_COL = HEAD_DIM


def _select_blocks_t(imp, t0):
    nb, cols = imp.shape
    j = lax.broadcasted_iota(jnp.int32, (nb, cols), 0)
    cur = (t0 + lax.broadcasted_iota(jnp.int32, (nb, cols), 1)) >> int(np.log2(SLC_BLOCK))
    forced = (j == 0) | (j == cur) | (j == cur - 1)
    score = jnp.where(forced, BIG, jnp.where(j <= cur, imp, -BIG))
    sub = 8
    tiles = [score[v * sub:(v + 1) * sub] for v in range(nb // sub)]
    ranks = [jnp.zeros((sub, cols), F32) for _ in tiles]
    jl = lax.broadcasted_iota(jnp.int32, (sub, cols), 0)
    for jp in range(nb):
        row = score[jp:jp + 1, :]
        for v, tile in enumerate(tiles):
            if v * sub > jp:
                beats = row >= tile
            elif (v + 1) * sub - 1 <= jp:
                beats = row > tile
            else:
                beats = (row > tile) | ((row == tile) & (jl > jp - v * sub))
            ranks[v] = jnp.where(beats, ranks[v] + 1.0, ranks[v])
    rank = jnp.concatenate(ranks, axis=0)
    return jnp.where((rank < float(min(SLC_TOP_N, nb))) & (j <= cur), 1.0, 0.0)


def _put_heads(o_ref, g, o, nq=QBLK):
    for i in range(REP // 2):
        pair = jnp.concatenate([o[:, (2 * i) * nq:(2 * i + 1) * nq],
                                o[:, (2 * i + 1) * nq:(2 * i + 2) * nq]], axis=0)
        col = (g * REP + 2 * i) * HEAD_DIM
        o_ref[:, col:col + LANES] = pair.T.astype(o_ref.dtype)


class _Pipeline:
    def __init__(self, k_ref, vt_ref, q_ref, s_ref, p_ref, acc_ref, chunk_of):
        self.k_ref, self.vt_ref, self.q_ref = k_ref, vt_ref, q_ref
        self.s_ref, self.p_ref, self.acc_ref = s_ref, p_ref, acc_ref
        self.chunk_of = chunk_of
        self.groups = range(KV_HEADS)

    def qk(self, g, j):
        start = pl.multiple_of(self.chunk_of(j) * KSTEP, KSTEP)
        depth = self.q_ref.shape[1]
        return _dot(self.k_ref[pl.ds(start, KSTEP), g * LANES:g * LANES + depth], self.q_ref[g])

    def keep(self, g, st, bias):
        if bias is not None:
            st = st + bias
        self.s_ref[g] = st
        return jnp.max(st, axis=0, keepdims=True)

    def softmax(self, g, m, cmax):
        m_new = jnp.maximum(m, cmax)
        a = jnp.exp2(m - m_new)
        self.p_ref[g] = jnp.exp2(self.s_ref[g] - m_new).astype(BF16)
        return m_new, a

    def values(self, g, j, a):
        vt = self.vt_ref[self.chunk_of(j), g * VROWS:(g + 1) * VROWS, :]
        self.acc_ref[g] = a * self.acc_ref[g] + _dot(vt, self.p_ref[g])

    def start(self, bias0, bias1):
        m0 = jnp.full((1, self.s_ref.shape[2]), NEG, F32)
        cmax = [self.keep(g, self.qk(g, 0), bias0) for g in self.groups]
        st1 = [self.qk(g, 1) for g in self.groups]
        sm = [self.softmax(g, m0, cmax[g]) for g in self.groups]
        cmax = tuple(self.keep(g, st1[g], bias1) for g in self.groups)
        for g in self.groups:
            self.acc_ref[g] = jnp.zeros(self.acc_ref.shape[1:], F32)
        return (tuple(s[0] for s in sm), tuple(s[1] for s in sm), cmax)

    def step(self, j, state, last=False, bias=None):
        m, a, cmax = state
        st = None if last else [self.qk(g, j + 2) for g in self.groups]
        for g in self.groups:
            self.values(g, j, a[g])
        sm = [self.softmax(g, m[g], cmax[g]) for g in self.groups]
        if not last:
            cmax = tuple(self.keep(g, st[g], bias) for g in self.groups)
        return (tuple(s[0] for s in sm), tuple(s[1] for s in sm), cmax)

    def finish(self, count, state):
        _, a, _ = state
        out = []
        for g in self.groups:
            self.values(g, count - 1, a[g])
            acc = self.acc_ref[g]
            out.append(acc[:HEAD_DIM] * (1.0 / acc[HEAD_DIM:HEAD_DIM + 1]))
        return out


def _attn_kernel(sink_ref, abias_ref, pbias_ref, cbias_ref, ovt_ref,
                 qa_ref, ka_ref, vat_ref, qn_ref, qnr_ref, gnt_ref, kcmp_ref, vcmpt_ref,
                 ksl_ref, vslt_ref, kw_ref, vwt_ref,
                 oa_ref, ob_ref,
                 qw_ref, sw_ref, pw_ref, accw_ref, qs_ref, ss_ref, ps_ref, accs_ref, sa_ref, pa_ref, sc_ref):
    n = pl.program_id(1)
    groups = range(KV_HEADS)
    width = REP * QBLK
    nb = ovt_ref.shape[0]
    nc = kcmp_ref.shape[1]
    t0 = pl.multiple_of(n * QBLK, QBLK)
    heads = lambda g: [g * REP + r for r in range(REP)]
    parts = range(QBLK // LANES)
    head_t = lambda ref, h, i: ref[i, h * HEAD_DIM:(h + 1) * HEAD_DIM, :]
    cols = lambda ref, g: jnp.concatenate([head_t(ref, h, i) for h in heads(g) for i in parts], axis=1)
    lane_chunks = lambda a: [a[:, r * QBLK:(r + 1) * QBLK] for r in range(REP)]
    gone = lambda cond: jnp.where(cond, 0.0, NEG)

    assert B_WINDOW == 2 * KSTEP
    win = _Pipeline(kw_ref, vwt_ref, qw_ref, sw_ref, pw_ref, accw_ref, lambda j: jnp.maximum(n - j, 0))
    slc = _Pipeline(ksl_ref, vslt_ref, qs_ref, ss_ref, ps_ref, accs_ref,
                    lambda j: jnp.where(j == 0, n, jnp.maximum(j - 1, 0)))
    slc_count = jnp.maximum(n, 1) + 1

    for g in groups:
        qw_ref[g] = cols(qnr_ref, g)
    a_chunk = [n * len(parts) + i for i in parts]
    a_first = [jnp.maximum(c - 1, 0) for c in a_chunk]
    part_cols = lambda i, g: jnp.concatenate([head_t(qa_ref, h, i) for h in heads(g)], axis=1)
    amax = {}
    for i in parts:
        abias = abias_ref[jnp.minimum(a_chunk[i], 1)]
        astart = pl.multiple_of(a_first[i] * A_WINDOW, A_WINDOW)
        for g in groups:
            st = _dot(ka_ref[pl.ds(astart, 2 * A_WINDOW), g * LANES:g * LANES + HEAD_DIM], part_cols(i, g)) + abias
            sa_ref[i, g] = st
            amax[i, g] = jnp.max(st, axis=0, keepdims=True)
    cstart = pl.multiple_of(cbias_ref.shape[0] - nc - n * (QBLK // CMP_STRIDE), 8)
    cbias = cbias_ref[pl.ds(cstart, nc), :]
    cmax = []
    for g in groups:
        s = _dot(kcmp_ref[g, :, :HEAD_DIM], cols(qn_ref, g)) + cbias
        sc_ref[g] = s
        cmax.append(jnp.max(s, axis=0, keepdims=True))
    win_state = win.start(pbias_ref[0], gone(n >= 1))

    x_swa = {}
    for i in parts:
        for g in groups:
            sk = jnp.concatenate([jnp.full((1, A_WINDOW), sink_ref[h] * LOG2E, F32) for h in heads(g)], axis=1)
            m = jnp.maximum(amax[i, g], sk)
            pa_ref[i, g] = jnp.exp2(sa_ref[i, g] - m).astype(BF16)
            x_swa[i, g] = jnp.exp2(sk - m)
    tq = t0 + (lax.broadcasted_iota(jnp.int32, (1, width), 1) & (QBLK - 1))
    sees_any = tq >= CMP_BLOCK - 1
    p_cmp = []
    for g in groups:
        e = jnp.exp2(sc_ref[g] - cmax[g])
        inv = jnp.where(sees_any, 1.0 / jnp.maximum(jnp.sum(e, axis=0, keepdims=True), 1e-30), 0.0)
        p_cmp.append(e * inv)

    win_state = win.step(0, win_state, bias=pbias_ref[1] + gone(n >= 2))

    for i in parts:
        for g in groups:
            vt = jnp.concatenate([vat_ref[a_first[i] + k, g * VROWS:(g + 1) * VROWS, :] for k in range(2)], axis=1)
            o = _dot(vt, pa_ref[i, g])
            o = o[:HEAD_DIM] * (1.0 / (o[HEAD_DIM:HEAD_DIM + 1] + x_swa[i, g]))
            _put_heads(oa_ref.at[i * A_WINDOW:(i + 1) * A_WINDOW, :], g, o, A_WINDOW)
    o_cmp = [_dot(vcmpt_ref[g], p_cmp[g].astype(BF16)) for g in groups]
    imp = []
    for g in groups:
        pc = lane_chunks(p_cmp[g])
        psum = (pc[0] + pc[1]) + (pc[2] + pc[3])
        hi = psum.astype(BF16)
        lo = (psum - hi.astype(F32)).astype(BF16)
        imp.append(_dot(ovt_ref[...], hi) + _dot(ovt_ref[...], lo))
    o_win = win.finish(3, win.step(1, win_state, last=True))

    jb = lax.broadcasted_iota(jnp.int32, (nb, QBLK), 0)
    curb = (t0 + lax.broadcasted_iota(jnp.int32, (nb, QBLK), 1)) >> int(np.log2(SLC_BLOCK))
    sel = lax.cond((t0 + QBLK - 1) // SLC_BLOCK < SLC_TOP_N,
                   lambda: tuple(jnp.where(jb <= curb, 1.0, 0.0) for _ in groups),
                   lambda: tuple(_select_blocks_t(imp[g], t0) for g in groups))

    for g in groups:
        qs_ref[g, :HEAD_DIM, :] = cols(qnr_ref, g)
        mrows = jnp.concatenate([sel[g] - 1.0, jnp.zeros((LANES - MASK_COL - nb, QBLK), F32)], axis=0).astype(BF16)
        for r in range(REP):
            qs_ref[g, MASK_COL:, r * QBLK:(r + 1) * QBLK] = mrows
    slc_state = slc.start(pbias_ref[0], gone(n >= 1))

    no_bias = gone(n >= 0)
    one_step = lambda j, s: slc.step(j, s, bias=no_bias)
    trips = slc_count - 2
    slc_state = lax.fori_loop(0, trips // 2, lambda i, s: one_step(2 * i + 1, one_step(2 * i, s)), slc_state)
    slc_state = lax.cond(trips % 2 == 1, lambda s: one_step(trips - 1, s), lambda s: s, slc_state)

    o_slc = slc.finish(slc_count, slc.step(slc_count - 2, slc_state, last=True))
    for g in groups:
        gate = lambda br: jnp.concatenate(
            [gnt_ref[i, br * Q_HEADS + h:br * Q_HEADS + h + 1, :]
             for h in heads(g) for i in range(gnt_ref.shape[0])], axis=1)
        _put_heads(ob_ref, g, gate(0) * o_cmp[g] + gate(1) * o_slc[g] + gate(2) * o_win[g])


def _attn(sinks, qa, ka, vat, qn, qnr, gnt, kcmp, vcmpt, ksl, vslt, kw, vwt, batch, seq):
    nq = seq // QBLK
    nc = kcmp.shape[2]
    nb = seq // SLC_BLOCK
    width = REP * QBLK
    assert nb <= LANES - MASK_COL and QBLK % SLC_BLOCK == 0 and QBLK % A_WINDOW == 0 and A_WINDOW % LANES == 0
    c_start = np.arange(nc)[None, :] * CMP_STRIDE
    s_start = np.arange(nb)[:, None] * SLC_BLOCK
    ovt = jnp.asarray((c_start < s_start + SLC_BLOCK) & (s_start < c_start + CMP_BLOCK), BF16)
    ql = np.arange(width)[None, :] % QBLK
    masked = lambda visible: np.where(visible, 0.0, NEG)
    kq = np.arange(KSTEP)[:, None] - ql
    pbias = jnp.asarray(np.stack([masked(kq <= 0),
                                  masked(kq > 0)]), F32)
    band = np.arange(2 * A_WINDOW)[:, None]
    qpart = np.arange(REP * A_WINDOW)[None, :] % A_WINDOW
    in_window = lambda dist: (dist >= 0) & (dist < A_WINDOW)
    abias = jnp.asarray(np.stack([masked(in_window(qpart - band)),
                                  masked(in_window(qpart - (band - A_WINDOW)))]), F32)
    per_blk = QBLK // CMP_STRIDE
    d = np.arange(nc + (nq - 1) * per_blk)[:, None] - (nq - 1) * per_blk
    cbias = jnp.asarray(masked(d * CMP_STRIDE + CMP_BLOCK - 1 <= ql), F32)
    q_spec = pl.BlockSpec((None, QBLK // LANES, Q_DIM, LANES), lambda b, n: (b, n, 0, 0))
    g_spec = pl.BlockSpec((None, QBLK // LANES, LANES, LANES), lambda b, n: (b, n, 0, 0))
    kc_spec = pl.BlockSpec((None, KV_HEADS, nc, LANES), lambda b, n: (b, 0, 0, 0))
    vc_spec = pl.BlockSpec((None, KV_HEADS, HEAD_DIM, nc), lambda b, n: (b, 0, 0, 0))
    const = lambda a: pl.BlockSpec(a.shape, lambda b, n: (0,) * a.ndim, pipeline_mode=pl.Buffered(1))
    k_spec = pl.BlockSpec((seq, KV_HEADS * LANES), lambda b, n: (b, 0))
    vt_spec = lambda keys: pl.BlockSpec((None, seq // keys, KV_HEADS * VROWS, keys), lambda b, n: (b, 0, 0, 0))
    o_spec = pl.BlockSpec((QBLK, Q_DIM), lambda b, n: (b * nq + n, 0))
    o_shape = jax.ShapeDtypeStruct((batch * seq, Q_DIM), BF16)
    pipe_scratch = lambda depth: [
        pltpu.VMEM((KV_HEADS, depth, width), BF16),
        pltpu.VMEM((KV_HEADS, KSTEP, width), F32),
        pltpu.VMEM((KV_HEADS, KSTEP, width), BF16),
        pltpu.VMEM((KV_HEADS, VROWS, width), F32)]
    return pl.pallas_call(
        _attn_kernel,
        grid=(batch, nq),
        in_specs=[pl.BlockSpec(memory_space=pltpu.SMEM), const(abias), const(pbias), const(cbias), const(ovt),
                  q_spec, k_spec, vt_spec(A_WINDOW), q_spec, q_spec, g_spec, kc_spec, vc_spec,
                  k_spec, vt_spec(KSTEP), k_spec, vt_spec(KSTEP)],
        out_specs=[o_spec, o_spec],
        out_shape=[o_shape, o_shape],
        scratch_shapes=pipe_scratch(HEAD_DIM) + pipe_scratch(LANES) + [
            pltpu.VMEM((QBLK // A_WINDOW, KV_HEADS, 2 * A_WINDOW, REP * A_WINDOW), F32),
            pltpu.VMEM((QBLK // A_WINDOW, KV_HEADS, 2 * A_WINDOW, REP * A_WINDOW), BF16),
            pltpu.VMEM((KV_HEADS, nc, width), F32)],
        compiler_params=pltpu.CompilerParams(dimension_semantics=("parallel", "arbitrary"),
                                             vmem_limit_bytes=VMEM_WIDE),
        name="attn",
    )(sinks, abias, pbias, cbias, ovt, qa, ka, vat, qn, qnr, gnt, kcmp, vcmpt, ksl, vslt, kw, vwt)


def _layer_norm(r, g, b):
    mu = jnp.mean(r, axis=-1, keepdims=True)
    d = r - mu
    var = jnp.mean(d * d, axis=-1, keepdims=True)
    return d * lax.rsqrt(var + LN_EPS) * g + b


POST_PARTS = 2


def _post_kernel(alpha, x_ref, oa_ref, ob_ref, wgm_ref, wpa_ref, wpb_ref, wout_ref, g_ref, b_ref, h_ref):
    d = x_ref.shape[1]
    rows = [slice(i * (x_ref.shape[0] // POST_PARTS), (i + 1) * (x_ref.shape[0] // POST_PARTS))
            for i in range(POST_PARTS)]
    ys = []
    for r in rows:
        xb = x_ref[r, :].astype(BF16)
        pa = _dot(oa_ref[r, :], wpa_ref[...])
        pb = _dot(ob_ref[r, :], wpb_ref[...])
        ys.append(jax.nn.sigmoid(_dot(xb, wgm_ref[:, :d])) * pa + jax.nn.sigmoid(_dot(xb, wgm_ref[:, d:])) * pb)
    for r, y in zip(rows, ys):
        m = _dot(y.astype(BF16), wout_ref[...])
        h_ref[r, :] = _layer_norm(alpha * x_ref[r, :] + m, g_ref[...], b_ref[...])


def _post(x2, oa, ob, w_gm, wpa, wpb, wout, g, b, alpha, tm):
    n, d = x2.shape
    tok = lambda width: pl.BlockSpec((tm, width), lambda i: (i, 0))
    full = lambda a: pl.BlockSpec(a.shape, lambda i: (0, 0))
    return pl.pallas_call(
        functools.partial(_post_kernel, alpha),
        grid=(n // tm,),
        in_specs=[tok(d), tok(Q_DIM), tok(Q_DIM)] + [full(a) for a in (w_gm, wpa, wpb, wout, g, b)],
        out_specs=tok(d),
        out_shape=jax.ShapeDtypeStruct((n, d), F32),
        compiler_params=pltpu.CompilerParams(dimension_semantics=("parallel",),
                                             vmem_limit_bytes=VMEM_PROJ),
        name="post",
    )(x2, oa, ob, w_gm, wpa, wpb, wout, g, b)


FFN_CHUNK = 256


def _ffn_kernel(alpha, h_ref, wg_ref, wu_ref, wd_ref, g_ref, b_ref, o_ref):
    h = h_ref[...]
    hb = h.astype(BF16)
    hidden = wg_ref.shape[1]
    acc = jnp.zeros(h.shape, F32)
    for c in range(hidden // FFN_CHUNK):
        sl = slice(c * FFN_CHUNK, (c + 1) * FFN_CHUNK)
        a = jax.nn.silu(_dot(hb, wg_ref[:, sl])) * _dot(hb, wu_ref[:, sl])
        acc = acc + _dot(a.astype(BF16), wd_ref[sl, :])
    o_ref[...] = _layer_norm(alpha * h + acc, g_ref[...], b_ref[...])


def _ffn(h, wg, wu, wd, g, b, alpha, tm):
    n, d = h.shape
    tok = pl.BlockSpec((tm, d), lambda i: (i, 0))
    once = lambda a: pl.BlockSpec(a.shape, lambda i: (0, 0), pipeline_mode=pl.Buffered(1))
    return pl.pallas_call(
        functools.partial(_ffn_kernel, alpha),
        grid=(n // tm,),
        in_specs=[tok] + [once(a) for a in (wg, wu, wd, g, b)],
        out_specs=tok,
        out_shape=jax.ShapeDtypeStruct((n, d), F32),
        compiler_params=pltpu.CompilerParams(dimension_semantics=("parallel",),
                                             vmem_limit_bytes=VMEM_WIDE),
        name="ffn",
    )(h, wg, wu, wd, g, b)


def _position_tables(seq):
    half = HEAD_DIM // 2
    inv = ROPE_THETA ** (-jnp.arange(half, dtype=F32) / half)
    ang = jnp.arange(seq).astype(F32)[:, None] * inv[None, :]
    cos, sin = jnp.cos(ang), jnp.sin(ang)
    reps = LANES // HEAD_DIM
    lane = np.arange(LANES)[None, :]
    blk = (np.arange(seq) // SLC_BLOCK)[:, None]
    ke = jnp.asarray(np.where(lane - MASK_COL == blk, -NEG, 0.0), F32)
    return jnp.tile(cos, (1, 2 * reps)), jnp.tile(jnp.concatenate([-sin, sin], axis=1), (1, reps)), ke


def kernel(x, w_in, sinks, cmp_pe_k, cmp_w1_k, cmp_b1_k, cmp_w2_k, cmp_pe_v, cmp_w1_v, cmp_b1_v, cmp_w2_v,
           w_proj_a, w_proj_b, w_out, ln1_g, ln1_b, w_gate, w_up, w_down, ln2_g, ln2_b):
    batch, seq, d = x.shape
    depth = w_in.shape[0]
    alpha = (2 * depth) ** 0.25
    n_main = 2 * Q_DIM + 8 * KV_DIM
    n_gate = 3 * Q_HEADS
    tm = ROW_TILE
    cos, sin, ke = _position_tables(seq)
    pad_cols = lambda w: jnp.pad(w, ((0, 0), (0, LANES - w.shape[1]))).astype(BF16)
    xt = x.reshape(batch * seq, d)
    for l in range(depth):
        w_main = w_in[l, :, :n_main].astype(BF16)
        w_gn = pad_cols(w_in[l, :, n_main:n_main + n_gate])
        w_gm = w_in[l, :, n_main + n_gate:].astype(BF16)
        (qa, ka, vat, qn, qnr, kc, vc, ksl, vslt, kw, vwt, gnt) = _in_proj(
            xt, w_main, w_gn, cos, sin, ke, batch, seq, tm)
        wk = (cmp_pe_k[l], cmp_w1_k[l].astype(BF16), cmp_b1_k[l][None, :], pad_cols(cmp_w2_k[l]))
        wv = (cmp_pe_v[l], cmp_w1_v[l].astype(BF16), cmp_b1_v[l][None, :], pad_cols(cmp_w2_v[l]))
        kcmp, vcmpt = _compress(kc, vc, wk, wv, batch, seq)
        oa, ob = _attn(sinks[l], qa, ka, vat, qn, qnr, gnt, kcmp, vcmpt, ksl, vslt, kw, vwt, batch, seq)
        h = _post(xt, oa, ob, w_gm, w_proj_a[l].astype(BF16), w_proj_b[l].astype(BF16), w_out[l].astype(BF16),
                  ln1_g[l][None, :], ln1_b[l][None, :], alpha, tm)
        xt = _ffn(h, w_gate[l].astype(BF16), w_up[l].astype(BF16), w_down[l].astype(BF16),
                  ln2_g[l][None, :], ln2_b[l][None, :], alpha, tm)
    return xt.reshape(batch, seq, d)
```

```python
import functools

import jax
import jax.numpy as jnp
import numpy as np
from jax import lax
from jax.experimental import pallas as pl
from jax.experimental.pallas import tpu as pltpu

HEAD_DIM = 64
ROPE_THETA = 10000.0
Q_HEADS = 8
KV_HEADS = 2
REP = Q_HEADS // KV_HEADS
A_WINDOW = 128
B_WINDOW = 512
CMP_BLOCK = 32
CMP_STRIDE = 16
SLC_BLOCK = 64
SLC_TOP_N = 16
LN_EPS = 1e-5
NEG = -1e30
BIG = 1e9
Q_DIM = Q_HEADS * HEAD_DIM
KV_DIM = KV_HEADS * HEAD_DIM
LANES = 128
QBLK = 256
KSTEP = QBLK
LOG2E = 1.4426950408889634
SCALE = HEAD_DIM ** -0.5 * LOG2E
VROWS = HEAD_DIM + 16
ROW_TILE = 512
MIB = 1024 * 1024
VMEM_PROJ = 48 * MIB
VMEM_WIDE = 56 * MIB

F32 = jnp.float32
BF16 = jnp.bfloat16


def _dot(a, b):
    return jnp.dot(a, b, preferred_element_type=F32)


def _inproj_kernel(x_ref, w32_ref, wg32_ref, cos_ref, sin_ref, cost_ref, sint_ref, ke_ref,
                   qa_ref, ka_ref, vat_ref, qn_ref, qnr_ref, kc_ref, vc_ref,
                   ksl_ref, vslt_ref, kw_ref, vwt_ref, gnt_ref, w_ref, wg_ref):
    @pl.when(pl.program_id(0) == 0)
    def _():
        w_ref[...] = w32_ref[...].astype(BF16)
        wg_ref[...] = wg32_ref[...].astype(BF16)

    tm = x_ref.shape[0]
    xb = x_ref[...].astype(BF16)
    cos = cos_ref[...]
    sin = sin_ref[...]
    lane = lax.broadcasted_iota(jnp.int32, (tm, LANES), 1)
    first_half = (lane & (HEAD_DIM - 1)) < (HEAD_DIM // 2)
    low = lane < HEAD_DIM

    def rope(z):
        sw = jnp.where(first_half, pltpu.roll(z, LANES - HEAD_DIM // 2, 1), pltpu.roll(z, HEAD_DIM // 2, 1))
        return z * cos + sw * sin

    def proj(c0, n):
        return _dot(xb, w_ref[:, c0:c0 + n])

    def chunks(z):
        return [z[:, c * LANES:(c + 1) * LANES] for c in range(z.shape[1] // LANES)]

    def put_padded(ref, c, z, fill):
        ref[:, (2 * c) * LANES:(2 * c + 1) * LANES] = jnp.where(low, z, fill).astype(ref.dtype)
        ref[:, (2 * c + 1) * LANES:(2 * c + 2) * LANES] = jnp.where(low, pltpu.roll(z, HEAD_DIM, 1), fill).astype(ref.dtype)

    def put_transposed(ref, z):
        for j in range(tm // LANES):
            ref[j] = z[j * LANES:(j + 1) * LANES, :].T.astype(ref.dtype)

    def put_heads_t(c, z, plain_ref, rope_ref):
        half = HEAD_DIM // 2
        for j in range(tm // LANES):
            zt = z[j * LANES:(j + 1) * LANES, :].T * SCALE
            rows = slice(c * LANES, (c + 1) * LANES)
            if plain_ref is not None:
                plain_ref[j, rows, :] = zt.astype(plain_ref.dtype)
            sw = jnp.concatenate([zt[(s ^ 1) * half:((s ^ 1) + 1) * half] for s in range(LANES // half)], axis=0)
            rope_ref[j, rows, :] = (zt * cost_ref[j] + sw * sint_ref[j]).astype(rope_ref.dtype)

    ones_row = jnp.where(lax.broadcasted_iota(jnp.int32, (VROWS - HEAD_DIM, LANES), 0) == 0, 1.0, 0.0)

    def put_values_t(ref, z):
        keys = ref.shape[2]
        for j in range(tm // keys):
            for i in range(keys // LANES):
                zt = z[j * keys + i * LANES:j * keys + (i + 1) * LANES, :].T
                cols = slice(i * LANES, (i + 1) * LANES)
                for g in range(KV_HEADS):
                    ref[j, g * VROWS:g * VROWS + HEAD_DIM, cols] = zt[g * HEAD_DIM:(g + 1) * HEAD_DIM, :].astype(ref.dtype)
                    ref[j, g * VROWS + HEAD_DIM:(g + 1) * VROWS, cols] = ones_row.astype(ref.dtype)

    for c, zc in enumerate(chunks(proj(0, Q_DIM))):
        put_heads_t(c, zc, None, qa_ref)
    z = proj(Q_DIM, 2 * KV_DIM)
    put_padded(ka_ref, 0, rope(z[:, :KV_DIM]), 0.0)
    put_values_t(vat_ref, z[:, KV_DIM:])
    c0 = Q_DIM + 2 * KV_DIM
    for c, zc in enumerate(chunks(proj(c0, Q_DIM))):
        put_heads_t(c, zc, qn_ref, qnr_ref)
    c0 += Q_DIM
    z = proj(c0, 2 * KV_DIM)
    kc_ref[...] = z[:, :KV_DIM]
    vc_ref[...] = z[:, KV_DIM:]
    c0 += 2 * KV_DIM
    z = proj(c0, 2 * KV_DIM)
    put_padded(ksl_ref, 0, rope(z[:, :KV_DIM]), ke_ref[...])
    put_values_t(vslt_ref, z[:, KV_DIM:])
    c0 += 2 * KV_DIM
    z = proj(c0, 2 * KV_DIM)
    put_padded(kw_ref, 0, rope(z[:, :KV_DIM]), 0.0)
    put_values_t(vwt_ref, z[:, KV_DIM:])
    put_transposed(gnt_ref, jax.nn.sigmoid(_dot(xb, wg_ref[...])))


def _in_proj(x2, w_in, layer, n_main, cos, sin, ke, batch, seq, tm):
    n = x2.shape[0]
    d = x2.shape[1]
    spt = seq // tm
    tok = lambda width: pl.BlockSpec((tm, width), lambda i: (i, 0))
    assert n_main % LANES == 0 and n_main + LANES <= w_in.shape[2]
    once = pl.Buffered(1)
    w_main = pl.BlockSpec((None, d, n_main), lambda i: (layer, 0, 0), pipeline_mode=once)
    w_gate = pl.BlockSpec((None, d, LANES), lambda i: (layer, 0, n_main // LANES), pipeline_mode=once)
    tab = pl.BlockSpec((tm, LANES), lambda i: (i % spt, 0))
    tab_t = pl.BlockSpec((tm // LANES, LANES, LANES), lambda i: (i % spt, 0, 0))
    cos_t, sin_t = (jnp.transpose(t.reshape(seq // LANES, LANES, LANES), (0, 2, 1)) for t in (cos, sin))
    flat = lambda w, dt: (tok(w), jax.ShapeDtypeStruct((n, w), dt))
    trans = lambda r, keys, dt: (pl.BlockSpec((None, tm // keys, r, keys), lambda i: (i // spt, i % spt, 0, 0)),
                                 jax.ShapeDtypeStruct((batch, seq // keys, r, keys), dt))
    outs = [trans(Q_DIM, LANES, BF16), flat(KV_HEADS * LANES, BF16),
            trans(KV_HEADS * VROWS, A_WINDOW, BF16),
            trans(Q_DIM, LANES, BF16), trans(Q_DIM, LANES, BF16),
            flat(KV_DIM, F32), flat(KV_DIM, F32),
            flat(KV_HEADS * LANES, BF16), trans(KV_HEADS * VROWS, KSTEP, BF16),
            flat(KV_HEADS * LANES, BF16), trans(KV_HEADS * VROWS, KSTEP, BF16),
            trans(LANES, LANES, F32)]
    return pl.pallas_call(
        _inproj_kernel,
        grid=(n // tm,),
        in_specs=[tok(d), w_main, w_gate, tab, tab, tab_t, tab_t, tab],
        out_specs=[o[0] for o in outs],
        out_shape=[o[1] for o in outs],
        scratch_shapes=[pltpu.VMEM((d, n_main), BF16), pltpu.VMEM((d, LANES), BF16)],
        compiler_params=pltpu.CompilerParams(dimension_semantics=("arbitrary",),
                                             vmem_limit_bytes=VMEM_PROJ),
        name="in_proj",
    )(x2, w_in, w_in, cos, sin, cos_t, sin_t, ke)


CMP_PAIR = 2


def _compress_one(src_ref, pe_ref, w1_ref, b1_ref, w2_ref):
    ratio = CMP_BLOCK // CMP_STRIDE
    nchunk = src_ref.shape[0] // CMP_STRIDE
    both = lambda row: jnp.concatenate([row] * KV_HEADS, axis=1)
    parts = [jnp.zeros((nchunk, w1_ref.shape[3]), F32) for _ in range(ratio)]
    for p in range(CMP_STRIDE // CMP_PAIR):
        rows = [src_ref[pl.ds(p * CMP_PAIR + i, nchunk, stride=CMP_STRIDE), :] for i in range(CMP_PAIR)]
        for j in range(ratio):
            pos = j * CMP_STRIDE + p * CMP_PAIR
            a = jnp.concatenate([(rows[i] + both(pe_ref[pos + i:pos + i + 1, :])).astype(BF16)
                                 for i in range(CMP_PAIR)], axis=1)
            parts[j] = parts[j] + _dot(a, w1_ref[j, p])
    h = parts[0]
    for j in range(1, ratio):
        h = h + pltpu.roll(parts[j], nchunk - j, 0)
    h = jax.nn.gelu(h + both(b1_ref[...]))
    return _dot(h.astype(BF16), w2_ref[...])


def _group_diagonal(w1_ref, w2_ref, w1d_ref, w2d_ref):
    hid = w1_ref.shape[1]
    w1d_ref[...] = jnp.zeros(w1d_ref.shape, w1d_ref.dtype)
    w2d_ref[...] = jnp.zeros(w2d_ref.shape, w2d_ref.dtype)
    for g in range(KV_HEADS):
        w2d_ref[g * hid:(g + 1) * hid, g * LANES:(g + 1) * LANES] = w2_ref[...]
        for j in range(CMP_BLOCK // CMP_STRIDE):
            for p in range(CMP_STRIDE // CMP_PAIR):
                for i in range(CMP_PAIR):
                    pos = j * CMP_STRIDE + p * CMP_PAIR + i
                    row = i * KV_DIM + g * HEAD_DIM
                    w1d_ref[j, p, row:row + HEAD_DIM, g * hid:(g + 1) * hid] = w1_ref[pos * HEAD_DIM:(pos + 1) * HEAD_DIM, :]


def _compress_kernel(kc_ref, vc_ref, pek_ref, w1k_ref, b1k_ref, w2k_ref,
                     pev_ref, w1v_ref, b1v_ref, w2v_ref, kcmp_ref, vcmpt_ref,
                     w1dk_ref, w2dk_ref, w1dv_ref, w2dv_ref):
    @pl.when(pl.program_id(0) == 0)
    def _():
        _group_diagonal(w1k_ref, w2k_ref, w1dk_ref, w2dk_ref)
        _group_diagonal(w1v_ref, w2v_ref, w1dv_ref, w2dv_ref)

    kc = _compress_one(kc_ref, pek_ref, w1dk_ref, b1k_ref, w2dk_ref)
    vc = _compress_one(vc_ref, pev_ref, w1dv_ref, b1v_ref, w2dv_ref)
    for g in range(KV_HEADS):
        kcmp_ref[g] = kc[:, g * LANES:(g + 1) * LANES].astype(BF16)
        vcmpt_ref[g] = vc[:, g * LANES:(g + 1) * LANES].T[:HEAD_DIM, :].astype(BF16)


def _compress(kc, vc, wk, wv, batch, seq):
    nchunk = seq // CMP_STRIDE
    hid = wk[1].shape[1]
    src = pl.BlockSpec((seq, KV_DIM), lambda b: (b, 0))
    full = lambda a: pl.BlockSpec(a.shape, lambda b: (0,) * a.ndim)
    return pl.pallas_call(
        _compress_kernel,
        grid=(batch,),
        in_specs=[src, src] + [full(a) for a in wk] + [full(a) for a in wv],
        out_specs=[pl.BlockSpec((None, KV_HEADS, nchunk, LANES), lambda b: (b, 0, 0, 0)),
                   pl.BlockSpec((None, KV_HEADS, HEAD_DIM, nchunk), lambda b: (b, 0, 0, 0))],
        out_shape=[jax.ShapeDtypeStruct((batch, KV_HEADS, nchunk, LANES), BF16),
                   jax.ShapeDtypeStruct((batch, KV_HEADS, HEAD_DIM, nchunk), BF16)],
        scratch_shapes=[pltpu.VMEM((CMP_BLOCK // CMP_STRIDE, CMP_STRIDE // CMP_PAIR, CMP_PAIR * KV_DIM, KV_HEADS * hid), BF16),
                        pltpu.VMEM((KV_HEADS * hid, KV_HEADS * LANES), BF16)] * 2,
        compiler_params=pltpu.CompilerParams(dimension_semantics=("arbitrary",)),
        name="compress",
    )(kc, vc, *wk, *wv)


MASK_COL = HEAD_DIM


def _select_blocks_t(imp, t0):
    nb, cols = imp.shape
    j = lax.broadcasted_iota(jnp.int32, (nb, cols), 0)
    cur = (t0 + lax.broadcasted_iota(jnp.int32, (nb, cols), 1)) >> int(np.log2(SLC_BLOCK))
    forced = (j == 0) | (j == cur) | (j == cur - 1)
    score = jnp.where(forced, BIG, jnp.where(j <= cur, imp, -BIG))
    sub = 8
    tiles = [score[v * sub:(v + 1) * sub] for v in range(nb // sub)]
    ranks = [jnp.zeros((sub, cols), F32) for _ in tiles]
    jl = lax.broadcasted_iota(jnp.int32, (sub, cols), 0)
    for jp in range(nb):
        row = score[jp:jp + 1, :]
        for v, tile in enumerate(tiles):
            if v * sub > jp:
                beats = row >= tile
            elif (v + 1) * sub - 1 <= jp:
                beats = row > tile
            else:
                beats = (row > tile) | ((row == tile) & (jl > jp - v * sub))
            ranks[v] = jnp.where(beats, ranks[v] + 1.0, ranks[v])
    rank = jnp.concatenate(ranks, axis=0)
    return jnp.where((rank < float(min(SLC_TOP_N, nb))) & (j <= cur), 1.0, 0.0)


def _put_heads(o_ref, g, o, nq=QBLK):
    for i in range(REP // 2):
        pair = jnp.concatenate([o[:, (2 * i) * nq:(2 * i + 1) * nq],
                                o[:, (2 * i + 1) * nq:(2 * i + 2) * nq]], axis=0)
        col = (g * REP + 2 * i) * HEAD_DIM
        o_ref[:, col:col + LANES] = pair.T.astype(o_ref.dtype)


class _Pipeline:
    def __init__(self, k_ref, vt_ref, q_ref, s_ref, p_ref, acc_ref, chunk_of):
        self.k_ref, self.vt_ref, self.q_ref = k_ref, vt_ref, q_ref
        self.s_ref, self.p_ref, self.acc_ref = s_ref, p_ref, acc_ref
        self.chunk_of = chunk_of
        self.groups = range(KV_HEADS)

    def qk(self, g, j):
        start = pl.multiple_of(self.chunk_of(j) * KSTEP, KSTEP)
        depth = self.q_ref.shape[1]
        return _dot(self.k_ref[pl.ds(start, KSTEP), g * LANES:g * LANES + depth], self.q_ref[g])

    def keep(self, g, st, bias):
        if bias is not None:
            st = st + bias
        self.s_ref[g] = st
        return jnp.max(st, axis=0, keepdims=True)

    def softmax(self, g, m, cmax):
        m_new = jnp.maximum(m, cmax)
        a = jnp.exp2(m - m_new)
        self.p_ref[g] = jnp.exp2(self.s_ref[g] - m_new).astype(BF16)
        return m_new, a

    def values(self, g, j, a):
        vt = self.vt_ref[self.chunk_of(j), g * VROWS:(g + 1) * VROWS, :]
        self.acc_ref[g] = a * self.acc_ref[g] + _dot(vt, self.p_ref[g])

    def start(self, bias0, bias1):
        m0 = jnp.full((1, self.s_ref.shape[2]), NEG, F32)
        cmax = [self.keep(g, self.qk(g, 0), bias0) for g in self.groups]
        st1 = [self.qk(g, 1) for g in self.groups]
        sm = [self.softmax(g, m0, cmax[g]) for g in self.groups]
        cmax = tuple(self.keep(g, st1[g], bias1) for g in self.groups)
        for g in self.groups:
            self.acc_ref[g] = jnp.zeros(self.acc_ref.shape[1:], F32)
        return (tuple(s[0] for s in sm), tuple(s[1] for s in sm), cmax)

    def step(self, j, state, last=False, bias=None):
        m, a, cmax = state
        st = None if last else [self.qk(g, j + 2) for g in self.groups]
        for g in self.groups:
            self.values(g, j, a[g])
        sm = [self.softmax(g, m[g], cmax[g]) for g in self.groups]
        if not last:
            cmax = tuple(self.keep(g, st[g], bias) for g in self.groups)
        return (tuple(s[0] for s in sm), tuple(s[1] for s in sm), cmax)

    def finish(self, count, state):
        _, a, _ = state
        out = []
        for g in self.groups:
            self.values(g, count - 1, a[g])
            acc = self.acc_ref[g]
            out.append(acc[:HEAD_DIM] * (1.0 / acc[HEAD_DIM:HEAD_DIM + 1]))
        return out


def _attn_kernel(sink_ref, abias_ref, pbias_ref, cbias_ref, ovt_ref,
                 qa_ref, ka_ref, vat_ref, qn_ref, qnr_ref, gnt_ref, kcmp_ref, vcmpt_ref,
                 ksl_ref, vslt_ref, kw_ref, vwt_ref,
                 oa_ref, ob_ref,
                 qw_ref, sw_ref, pw_ref, accw_ref, qs_ref, ss_ref, ps_ref, accs_ref, sa_ref, pa_ref, sc_ref):
    n = pl.program_id(1)
    groups = range(KV_HEADS)
    width = REP * QBLK
    nb = ovt_ref.shape[0]
    nc = kcmp_ref.shape[1]
    t0 = pl.multiple_of(n * QBLK, QBLK)
    heads = lambda g: [g * REP + r for r in range(REP)]
    parts = range(QBLK // LANES)
    head_t = lambda ref, h, i: ref[i, h * HEAD_DIM:(h + 1) * HEAD_DIM, :]
    cols = lambda ref, g: jnp.concatenate([head_t(ref, h, i) for h in heads(g) for i in parts], axis=1)
    lane_chunks = lambda a: [a[:, r * QBLK:(r + 1) * QBLK] for r in range(REP)]
    gone = lambda cond: jnp.where(cond, 0.0, NEG)

    assert B_WINDOW == 2 * KSTEP
    win = _Pipeline(kw_ref, vwt_ref, qw_ref, sw_ref, pw_ref, accw_ref, lambda j: jnp.maximum(n - j, 0))
    slc = _Pipeline(ksl_ref, vslt_ref, qs_ref, ss_ref, ps_ref, accs_ref,
                    lambda j: jnp.where(j == 0, n, jnp.maximum(j - 1, 0)))
    slc_count = jnp.maximum(n, 1) + 1

    for g in groups:
        qw_ref[g] = cols(qnr_ref, g)
    a_chunk = [n * len(parts) + i for i in parts]
    a_first = [jnp.maximum(c - 1, 0) for c in a_chunk]
    part_cols = lambda i, g: jnp.concatenate([head_t(qa_ref, h, i) for h in heads(g)], axis=1)
    amax = {}
    for i in parts:
        abias = abias_ref[jnp.minimum(a_chunk[i], 1)]
        astart = pl.multiple_of(a_first[i] * A_WINDOW, A_WINDOW)
        for g in groups:
            st = _dot(ka_ref[pl.ds(astart, 2 * A_WINDOW), g * LANES:g * LANES + HEAD_DIM], part_cols(i, g)) + abias
            sa_ref[i, g] = st
            amax[i, g] = jnp.max(st, axis=0, keepdims=True)
    cstart = pl.multiple_of(cbias_ref.shape[0] - nc - n * (QBLK // CMP_STRIDE), 8)
    cbias = cbias_ref[pl.ds(cstart, nc), :]
    cmax = []
    for g in groups:
        s = _dot(kcmp_ref[g, :, :HEAD_DIM], cols(qn_ref, g)) + cbias
        sc_ref[g] = s
        cmax.append(jnp.max(s, axis=0, keepdims=True))
    win_state = win.start(pbias_ref[0], gone(n >= 1))

    x_swa = {}
    for i in parts:
        for g in groups:
            sk = jnp.concatenate([jnp.full((1, A_WINDOW), sink_ref[h] * LOG2E, F32) for h in heads(g)], axis=1)
            m = jnp.maximum(amax[i, g], sk)
            pa_ref[i, g] = jnp.exp2(sa_ref[i, g] - m).astype(BF16)
            x_swa[i, g] = jnp.exp2(sk - m)
    tq = t0 + (lax.broadcasted_iota(jnp.int32, (1, width), 1) & (QBLK - 1))
    sees_any = tq >= CMP_BLOCK - 1
    p_cmp = []
    for g in groups:
        e = jnp.exp2(sc_ref[g] - cmax[g])
        inv = jnp.where(sees_any, 1.0 / jnp.maximum(jnp.sum(e, axis=0, keepdims=True), 1e-30), 0.0)
        p_cmp.append(e * inv)

    win_state = win.step(0, win_state, bias=pbias_ref[1] + gone(n >= 2))

    for i in parts:
        for g in groups:
            vt = jnp.concatenate([vat_ref[a_first[i] + k, g * VROWS:(g + 1) * VROWS, :] for k in range(2)], axis=1)
            o = _dot(vt, pa_ref[i, g])
            o = o[:HEAD_DIM] * (1.0 / (o[HEAD_DIM:HEAD_DIM + 1] + x_swa[i, g]))
            _put_heads(oa_ref.at[i * A_WINDOW:(i + 1) * A_WINDOW, :], g, o, A_WINDOW)
    o_cmp = [_dot(vcmpt_ref[g], p_cmp[g].astype(BF16)) for g in groups]
    imp = []
    for g in groups:
        pc = lane_chunks(p_cmp[g])
        psum = (pc[0] + pc[1]) + (pc[2] + pc[3])
        hi = psum.astype(BF16)
        lo = (psum - hi.astype(F32)).astype(BF16)
        imp.append(_dot(ovt_ref[...], hi) + _dot(ovt_ref[...], lo))
    o_win = win.finish(3, win.step(1, win_state, last=True))

    jb = lax.broadcasted_iota(jnp.int32, (nb, QBLK), 0)
    curb = (t0 + lax.broadcasted_iota(jnp.int32, (nb, QBLK), 1)) >> int(np.log2(SLC_BLOCK))
    sel = lax.cond((t0 + QBLK - 1) // SLC_BLOCK < SLC_TOP_N,
                   lambda: tuple(jnp.where(jb <= curb, 1.0, 0.0) for _ in groups),
                   lambda: tuple(_select_blocks_t(imp[g], t0) for g in groups))

    for g in groups:
        qs_ref[g, :HEAD_DIM, :] = cols(qnr_ref, g)
        mrows = jnp.concatenate([sel[g] - 1.0, jnp.zeros((LANES - MASK_COL - nb, QBLK), F32)], axis=0).astype(BF16)
        for r in range(REP):
            qs_ref[g, MASK_COL:, r * QBLK:(r + 1) * QBLK] = mrows
    slc_state = slc.start(pbias_ref[0], gone(n >= 1))

    no_bias = gone(n >= 0)
    one_step = lambda j, s: slc.step(j, s, bias=no_bias)
    trips = slc_count - 2
    slc_state = lax.fori_loop(0, trips // 2, lambda i, s: one_step(2 * i + 1, one_step(2 * i, s)), slc_state)
    slc_state = lax.cond(trips % 2 == 1, lambda s: one_step(trips - 1, s), lambda s: s, slc_state)

    o_slc = slc.finish(slc_count, slc.step(slc_count - 2, slc_state, last=True))
    for g in groups:
        gate = lambda br: jnp.concatenate(
            [gnt_ref[i, br * Q_HEADS + h:br * Q_HEADS + h + 1, :]
             for h in heads(g) for i in range(gnt_ref.shape[0])], axis=1)
        _put_heads(ob_ref, g, gate(0) * o_cmp[g] + gate(1) * o_slc[g] + gate(2) * o_win[g])


def _attn(sinks, qa, ka, vat, qn, qnr, gnt, kcmp, vcmpt, ksl, vslt, kw, vwt, batch, seq):
    nq = seq // QBLK
    nc = kcmp.shape[2]
    nb = seq // SLC_BLOCK
    width = REP * QBLK
    assert nb <= LANES - MASK_COL and QBLK % SLC_BLOCK == 0 and QBLK % A_WINDOW == 0 and A_WINDOW % LANES == 0
    c_start = np.arange(nc)[None, :] * CMP_STRIDE
    s_start = np.arange(nb)[:, None] * SLC_BLOCK
    ovt = jnp.asarray((c_start < s_start + SLC_BLOCK) & (s_start < c_start + CMP_BLOCK), BF16)
    ql = np.arange(width)[None, :] % QBLK
    masked = lambda visible: np.where(visible, 0.0, NEG)
    kq = np.arange(KSTEP)[:, None] - ql
    pbias = jnp.asarray(np.stack([masked(kq <= 0),
                                  masked(kq > 0)]), F32)
    band = np.arange(2 * A_WINDOW)[:, None]
    qpart = np.arange(REP * A_WINDOW)[None, :] % A_WINDOW
    in_window = lambda dist: (dist >= 0) & (dist < A_WINDOW)
    abias = jnp.asarray(np.stack([masked(in_window(qpart - band)),
                                  masked(in_window(qpart - (band - A_WINDOW)))]), F32)
    per_blk = QBLK // CMP_STRIDE
    d = np.arange(nc + (nq - 1) * per_blk)[:, None] - (nq - 1) * per_blk
    cbias = jnp.asarray(masked(d * CMP_STRIDE + CMP_BLOCK - 1 <= ql), F32)
    q_spec = pl.BlockSpec((None, QBLK // LANES, Q_DIM, LANES), lambda b, n: (b, n, 0, 0))
    g_spec = pl.BlockSpec((None, QBLK // LANES, LANES, LANES), lambda b, n: (b, n, 0, 0))
    kc_spec = pl.BlockSpec((None, KV_HEADS, nc, LANES), lambda b, n: (b, 0, 0, 0))
    vc_spec = pl.BlockSpec((None, KV_HEADS, HEAD_DIM, nc), lambda b, n: (b, 0, 0, 0))
    const = lambda a: pl.BlockSpec(a.shape, lambda b, n: (0,) * a.ndim, pipeline_mode=pl.Buffered(1))
    k_spec = pl.BlockSpec((seq, KV_HEADS * LANES), lambda b, n: (b, 0))
    vt_spec = lambda keys: pl.BlockSpec((None, seq // keys, KV_HEADS * VROWS, keys), lambda b, n: (b, 0, 0, 0))
    o_spec = pl.BlockSpec((QBLK, Q_DIM), lambda b, n: (b * nq + n, 0))
    o_shape = jax.ShapeDtypeStruct((batch * seq, Q_DIM), BF16)
    pipe_scratch = lambda depth: [
        pltpu.VMEM((KV_HEADS, depth, width), BF16),
        pltpu.VMEM((KV_HEADS, KSTEP, width), F32),
        pltpu.VMEM((KV_HEADS, KSTEP, width), BF16),
        pltpu.VMEM((KV_HEADS, VROWS, width), F32)]
    return pl.pallas_call(
        _attn_kernel,
        grid=(batch, nq),
        in_specs=[pl.BlockSpec(memory_space=pltpu.SMEM), const(abias), const(pbias), const(cbias), const(ovt),
                  q_spec, k_spec, vt_spec(A_WINDOW), q_spec, q_spec, g_spec, kc_spec, vc_spec,
                  k_spec, vt_spec(KSTEP), k_spec, vt_spec(KSTEP)],
        out_specs=[o_spec, o_spec],
        out_shape=[o_shape, o_shape],
        scratch_shapes=pipe_scratch(HEAD_DIM) + pipe_scratch(LANES) + [
            pltpu.VMEM((QBLK // A_WINDOW, KV_HEADS, 2 * A_WINDOW, REP * A_WINDOW), F32),
            pltpu.VMEM((QBLK // A_WINDOW, KV_HEADS, 2 * A_WINDOW, REP * A_WINDOW), BF16),
            pltpu.VMEM((KV_HEADS, nc, width), F32)],
        compiler_params=pltpu.CompilerParams(dimension_semantics=("parallel", "arbitrary"),
                                             vmem_limit_bytes=VMEM_WIDE),
        name="attn",
    )(sinks, abias, pbias, cbias, ovt, qa, ka, vat, qn, qnr, gnt, kcmp, vcmpt, ksl, vslt, kw, vwt)


def _layer_norm(r, g, b):
    mu = jnp.mean(r, axis=-1, keepdims=True)
    d = r - mu
    var = jnp.mean(d * d, axis=-1, keepdims=True)
    return d * lax.rsqrt(var + LN_EPS) * g + b


POST_PARTS = 2


def _post_kernel(alpha, x_ref, oa_ref, ob_ref, wgm_ref, wpa_ref, wpb_ref, wout_ref, g_ref, b_ref, h_ref):
    d = x_ref.shape[1]
    rows = [slice(i * (x_ref.shape[0] // POST_PARTS), (i + 1) * (x_ref.shape[0] // POST_PARTS))
            for i in range(POST_PARTS)]
    ys = []
    for r in rows:
        xb = x_ref[r, :].astype(BF16)
        pa = _dot(oa_ref[r, :], wpa_ref[...])
        pb = _dot(ob_ref[r, :], wpb_ref[...])
        ys.append(jax.nn.sigmoid(_dot(xb, wgm_ref[:, :d])) * pa + jax.nn.sigmoid(_dot(xb, wgm_ref[:, d:])) * pb)
    for r, y in zip(rows, ys):
        m = _dot(y.astype(BF16), wout_ref[...])
        h_ref[r, :] = _layer_norm(alpha * x_ref[r, :] + m, g_ref[...], b_ref[...])


def _post(x2, oa, ob, w_gm, wpa, wpb, wout, g, b, alpha, tm):
    n, d = x2.shape
    tok = lambda width: pl.BlockSpec((tm, width), lambda i: (i, 0))
    full = lambda a: pl.BlockSpec(a.shape, lambda i: (0, 0))
    return pl.pallas_call(
        functools.partial(_post_kernel, alpha),
        grid=(n // tm,),
        in_specs=[tok(d), tok(Q_DIM), tok(Q_DIM)] + [full(a) for a in (w_gm, wpa, wpb, wout, g, b)],
        out_specs=tok(d),
        out_shape=jax.ShapeDtypeStruct((n, d), F32),
        compiler_params=pltpu.CompilerParams(dimension_semantics=("parallel",),
                                             vmem_limit_bytes=VMEM_PROJ),
        name="post",
    )(x2, oa, ob, w_gm, wpa, wpb, wout, g, b)


FFN_CHUNK = 256


def _ffn_kernel(alpha, h_ref, wg_ref, wu_ref, wd_ref, g_ref, b_ref, o_ref):
    h = h_ref[...]
    hb = h.astype(BF16)
    hidden = wg_ref.shape[1]
    acc = jnp.zeros(h.shape, F32)
    for c in range(hidden // FFN_CHUNK):
        sl = slice(c * FFN_CHUNK, (c + 1) * FFN_CHUNK)
        a = jax.nn.silu(_dot(hb, wg_ref[:, sl])) * _dot(hb, wu_ref[:, sl])
        acc = acc + _dot(a.astype(BF16), wd_ref[sl, :])
    o_ref[...] = _layer_norm(alpha * h + acc, g_ref[...], b_ref[...])


def _ffn(h, wg, wu, wd, g, b, alpha, tm):
    n, d = h.shape
    tok = pl.BlockSpec((tm, d), lambda i: (i, 0))
    once = lambda a: pl.BlockSpec(a.shape, lambda i: (0, 0), pipeline_mode=pl.Buffered(1))
    return pl.pallas_call(
        functools.partial(_ffn_kernel, alpha),
        grid=(n // tm,),
        in_specs=[tok] + [once(a) for a in (wg, wu, wd, g, b)],
        out_specs=tok,
        out_shape=jax.ShapeDtypeStruct((n, d), F32),
        compiler_params=pltpu.CompilerParams(dimension_semantics=("parallel",),
                                             vmem_limit_bytes=VMEM_WIDE),
        name="ffn",
    )(h, wg, wu, wd, g, b)


def _position_tables(seq):
    half = HEAD_DIM // 2
    inv = ROPE_THETA ** (-jnp.arange(half, dtype=F32) / half)
    ang = jnp.arange(seq).astype(F32)[:, None] * inv[None, :]
    cos, sin = jnp.cos(ang), jnp.sin(ang)
    reps = LANES // HEAD_DIM
    lane = np.arange(LANES)[None, :]
    blk = (np.arange(seq) // SLC_BLOCK)[:, None]
    ke = jnp.asarray(np.where(lane - MASK_COL == blk, -NEG, 0.0), F32)
    return jnp.tile(cos, (1, 2 * reps)), jnp.tile(jnp.concatenate([-sin, sin], axis=1), (1, reps)), ke


def kernel(x, w_in, sinks, cmp_pe_k, cmp_w1_k, cmp_b1_k, cmp_w2_k, cmp_pe_v, cmp_w1_v, cmp_b1_v, cmp_w2_v,
           w_proj_a, w_proj_b, w_out, ln1_g, ln1_b, w_gate, w_up, w_down, ln2_g, ln2_b):
    batch, seq, d = x.shape
    depth = w_in.shape[0]
    alpha = (2 * depth) ** 0.25
    n_main = 2 * Q_DIM + 8 * KV_DIM
    n_gate = 3 * Q_HEADS
    tm = ROW_TILE
    cos, sin, ke = _position_tables(seq)
    pad_cols = lambda w: jnp.pad(w, ((0, 0), (0, LANES - w.shape[1]))).astype(BF16)
    xt = x.reshape(batch * seq, d)
    for l in range(depth):
        w_gm = w_in[l, :, n_main + n_gate:].astype(BF16)
        (qa, ka, vat, qn, qnr, kc, vc, ksl, vslt, kw, vwt, gnt) = _in_proj(
            xt, w_in, l, n_main, cos, sin, ke, batch, seq, tm)
        wk = (cmp_pe_k[l], cmp_w1_k[l].astype(BF16), cmp_b1_k[l][None, :], pad_cols(cmp_w2_k[l]))
        wv = (cmp_pe_v[l], cmp_w1_v[l].astype(BF16), cmp_b1_v[l][None, :], pad_cols(cmp_w2_v[l]))
        kcmp, vcmpt = _compress(kc, vc, wk, wv, batch, seq)
        oa, ob = _attn(sinks[l], qa, ka, vat, qn, qnr, gnt, kcmp, vcmpt, ksl, vslt, kw, vwt, batch, seq)
        h = _post(xt, oa, ob, w_gm, w_proj_a[l].astype(BF16), w_proj_b[l].astype(BF16), w_out[l].astype(BF16),
                  ln1_g[l][None, :], ln1_b[l][None, :], alpha, tm)
        xt = _ffn(h, w_gate[l].astype(BF16), w_up[l].astype(BF16), w_down[l].astype(BF16),
                  ln2_g[l][None, :], ln2_b[l][None, :], alpha, tm)
    return xt.reshape(batch, seq, d)
```

```python
import functools

import jax
import jax.numpy as jnp
import numpy as np
from jax import lax
from jax.experimental import pallas as pl
from jax.experimental.pallas import tpu as pltpu

HEAD_DIM = 64
ROPE_THETA = 10000.0
Q_HEADS = 8
KV_HEADS = 2
REP = Q_HEADS // KV_HEADS
A_WINDOW = 128
B_WINDOW = 512
CMP_BLOCK = 32
CMP_STRIDE = 16
SLC_BLOCK = 64
SLC_TOP_N = 16
LN_EPS = 1e-5
NEG = -1e30
BIG = 1e9
Q_DIM = Q_HEADS * HEAD_DIM
KV_DIM = KV_HEADS * HEAD_DIM
LANES = 128
QBLK = 256
KSTEP = QBLK
LOG2E = 1.4426950408889634
SCALE = HEAD_DIM ** -0.5 * LOG2E
VROWS = HEAD_DIM + 16
ROW_TILE = 1024
MIB = 1024 * 1024
VMEM_PROJ = 48 * MIB
VMEM_WIDE = 56 * MIB

F32 = jnp.float32
BF16 = jnp.bfloat16


def _dot(a, b):
    return jnp.dot(a, b, preferred_element_type=F32)


def _inproj_kernel(x_ref, w_ref, wg_ref, cos_ref, sin_ref, cost_ref, sint_ref, ke_ref,
                   qa_ref, ka_ref, vat_ref, qn_ref, qnr_ref, kc_ref, vc_ref,
                   ksl_ref, vslt_ref, kw_ref, vwt_ref, gnt_ref):
    tm = x_ref.shape[0]
    xb = x_ref[...].astype(BF16)
    cos = cos_ref[...]
    sin = sin_ref[...]
    lane = lax.broadcasted_iota(jnp.int32, (tm, LANES), 1)
    first_half = (lane & (HEAD_DIM - 1)) < (HEAD_DIM // 2)
    low = lane < HEAD_DIM

    def rope(z):
        sw = jnp.where(first_half, pltpu.roll(z, LANES - HEAD_DIM // 2, 1), pltpu.roll(z, HEAD_DIM // 2, 1))
        return z * cos + sw * sin

    def proj(c0, n):
        return _dot(xb, w_ref[:, c0:c0 + n])

    def chunks(z):
        return [z[:, c * LANES:(c + 1) * LANES] for c in range(z.shape[1] // LANES)]

    def put_padded(ref, c, z, fill):
        ref[:, (2 * c) * LANES:(2 * c + 1) * LANES] = jnp.where(low, z, fill).astype(ref.dtype)
        ref[:, (2 * c + 1) * LANES:(2 * c + 2) * LANES] = jnp.where(low, pltpu.roll(z, HEAD_DIM, 1), fill).astype(ref.dtype)

    def put_transposed(ref, z):
        for j in range(tm // LANES):
            ref[j] = z[j * LANES:(j + 1) * LANES, :].T.astype(ref.dtype)

    def put_heads_t(c, z, plain_ref, rope_ref):
        half = HEAD_DIM // 2
        for j in range(tm // LANES):
            zt = z[j * LANES:(j + 1) * LANES, :].T * SCALE
            rows = slice(c * LANES, (c + 1) * LANES)
            if plain_ref is not None:
                plain_ref[j, rows, :] = zt.astype(plain_ref.dtype)
            sw = jnp.concatenate([zt[(s ^ 1) * half:((s ^ 1) + 1) * half] for s in range(LANES // half)], axis=0)
            rope_ref[j, rows, :] = (zt * cost_ref[j] + sw * sint_ref[j]).astype(rope_ref.dtype)

    ones_row = jnp.where(lax.broadcasted_iota(jnp.int32, (VROWS - HEAD_DIM, LANES), 0) == 0, 1.0, 0.0)

    def put_values_t(ref, z):
        keys = ref.shape[2]
        for j in range(tm // keys):
            for i in range(keys // LANES):
                zt = z[j * keys + i * LANES:j * keys + (i + 1) * LANES, :].T
                cols = slice(i * LANES, (i + 1) * LANES)
                for g in range(KV_HEADS):
                    ref[j, g * VROWS:g * VROWS + HEAD_DIM, cols] = zt[g * HEAD_DIM:(g + 1) * HEAD_DIM, :].astype(ref.dtype)
                    ref[j, g * VROWS + HEAD_DIM:(g + 1) * VROWS, cols] = ones_row.astype(ref.dtype)

    for c, zc in enumerate(chunks(proj(0, Q_DIM))):
        put_heads_t(c, zc, None, qa_ref)
    z = proj(Q_DIM, 2 * KV_DIM)
    put_padded(ka_ref, 0, rope(z[:, :KV_DIM]), 0.0)
    put_values_t(vat_ref, z[:, KV_DIM:])
    c0 = Q_DIM + 2 * KV_DIM
    for c, zc in enumerate(chunks(proj(c0, Q_DIM))):
        put_heads_t(c, zc, qn_ref, qnr_ref)
    c0 += Q_DIM
    z = proj(c0, 2 * KV_DIM)
    kc_ref[...] = z[:, :KV_DIM]
    vc_ref[...] = z[:, KV_DIM:]
    c0 += 2 * KV_DIM
    z = proj(c0, 2 * KV_DIM)
    put_padded(ksl_ref, 0, rope(z[:, :KV_DIM]), ke_ref[...])
    put_values_t(vslt_ref, z[:, KV_DIM:])
    c0 += 2 * KV_DIM
    z = proj(c0, 2 * KV_DIM)
    put_padded(kw_ref, 0, rope(z[:, :KV_DIM]), 0.0)
    put_values_t(vwt_ref, z[:, KV_DIM:])
    put_transposed(gnt_ref, jax.nn.sigmoid(_dot(xb, wg_ref[...])))


def _in_proj(x2, w_main, w_gn, cos, sin, ke, batch, seq, tm):
    n = x2.shape[0]
    d = x2.shape[1]
    spt = seq // tm
    tok = lambda width: pl.BlockSpec((tm, width), lambda i: (i, 0))
    full = lambda a: pl.BlockSpec(a.shape, lambda i: (0, 0))
    tab = pl.BlockSpec((tm, LANES), lambda i: (i % spt, 0))
    tab_t = pl.BlockSpec((tm // LANES, LANES, LANES), lambda i: (i % spt, 0, 0))
    cos_t, sin_t = (jnp.transpose(t.reshape(seq // LANES, LANES, LANES), (0, 2, 1)) for t in (cos, sin))
    flat = lambda w, dt: (tok(w), jax.ShapeDtypeStruct((n, w), dt))
    trans = lambda r, keys, dt: (pl.BlockSpec((None, tm // keys, r, keys), lambda i: (i // spt, i % spt, 0, 0)),
                                 jax.ShapeDtypeStruct((batch, seq // keys, r, keys), dt))
    outs = [trans(Q_DIM, LANES, BF16), flat(KV_HEADS * LANES, BF16),
            trans(KV_HEADS * VROWS, A_WINDOW, BF16),
            trans(Q_DIM, LANES, BF16), trans(Q_DIM, LANES, BF16),
            flat(KV_DIM, F32), flat(KV_DIM, F32),
            flat(KV_HEADS * LANES, BF16), trans(KV_HEADS * VROWS, KSTEP, BF16),
            flat(KV_HEADS * LANES, BF16), trans(KV_HEADS * VROWS, KSTEP, BF16),
            trans(LANES, LANES, F32)]
    return pl.pallas_call(
        _inproj_kernel,
        grid=(n // tm,),
        in_specs=[tok(d), full(w_main), full(w_gn), tab, tab, tab_t, tab_t, tab],
        out_specs=[o[0] for o in outs],
        out_shape=[o[1] for o in outs],
        compiler_params=pltpu.CompilerParams(dimension_semantics=("parallel",),
                                             vmem_limit_bytes=VMEM_PROJ),
        name="in_proj",
    )(x2, w_main, w_gn, cos, sin, cos_t, sin_t, ke)


CMP_PAIR = 2


def _compress_one(src_ref, pe_ref, w1_ref, b1_ref, w2_ref):
    ratio = CMP_BLOCK // CMP_STRIDE
    nchunk = src_ref.shape[0] // CMP_STRIDE
    both = lambda row: jnp.concatenate([row] * KV_HEADS, axis=1)
    parts = [jnp.zeros((nchunk, w1_ref.shape[3]), F32) for _ in range(ratio)]
    for p in range(CMP_STRIDE // CMP_PAIR):
        rows = [src_ref[pl.ds(p * CMP_PAIR + i, nchunk, stride=CMP_STRIDE), :] for i in range(CMP_PAIR)]
        for j in range(ratio):
            pos = j * CMP_STRIDE + p * CMP_PAIR
            a = jnp.concatenate([(rows[i] + both(pe_ref[pos + i:pos + i + 1, :])).astype(BF16)
                                 for i in range(CMP_PAIR)], axis=1)
            parts[j] = parts[j] + _dot(a, w1_ref[j, p])
    h = parts[0]
    for j in range(1, ratio):
        h = h + pltpu.roll(parts[j], nchunk - j, 0)
    h = jax.nn.gelu(h + both(b1_ref[...]))
    return _dot(h.astype(BF16), w2_ref[...])


def _group_diagonal(w1_ref, w2_ref, w1d_ref, w2d_ref):
    hid = w1_ref.shape[1]
    w1d_ref[...] = jnp.zeros(w1d_ref.shape, w1d_ref.dtype)
    w2d_ref[...] = jnp.zeros(w2d_ref.shape, w2d_ref.dtype)
    for g in range(KV_HEADS):
        w2d_ref[g * hid:(g + 1) * hid, g * LANES:(g + 1) * LANES] = w2_ref[...]
        for j in range(CMP_BLOCK // CMP_STRIDE):
            for p in range(CMP_STRIDE // CMP_PAIR):
                for i in range(CMP_PAIR):
                    pos = j * CMP_STRIDE + p * CMP_PAIR + i
                    row = i * KV_DIM + g * HEAD_DIM
                    w1d_ref[j, p, row:row + HEAD_DIM, g * hid:(g + 1) * hid] = w1_ref[pos * HEAD_DIM:(pos + 1) * HEAD_DIM, :]


def _compress_kernel(kc_ref, vc_ref, pek_ref, w1k_ref, b1k_ref, w2k_ref,
                     pev_ref, w1v_ref, b1v_ref, w2v_ref, kcmp_ref, vcmpt_ref,
                     w1dk_ref, w2dk_ref, w1dv_ref, w2dv_ref):
    @pl.when(pl.program_id(0) == 0)
    def _():
        _group_diagonal(w1k_ref, w2k_ref, w1dk_ref, w2dk_ref)
        _group_diagonal(w1v_ref, w2v_ref, w1dv_ref, w2dv_ref)

    kc = _compress_one(kc_ref, pek_ref, w1dk_ref, b1k_ref, w2dk_ref)
    vc = _compress_one(vc_ref, pev_ref, w1dv_ref, b1v_ref, w2dv_ref)
    for g in range(KV_HEADS):
        kcmp_ref[g] = kc[:, g * LANES:(g + 1) * LANES].astype(BF16)
        vcmpt_ref[g] = vc[:, g * LANES:(g + 1) * LANES].T[:HEAD_DIM, :].astype(BF16)


def _compress(kc, vc, wk, wv, batch, seq):
    nchunk = seq // CMP_STRIDE
    hid = wk[1].shape[1]
    src = pl.BlockSpec((seq, KV_DIM), lambda b: (b, 0))
    full = lambda a: pl.BlockSpec(a.shape, lambda b: (0,) * a.ndim)
    return pl.pallas_call(
        _compress_kernel,
        grid=(batch,),
        in_specs=[src, src] + [full(a) for a in wk] + [full(a) for a in wv],
        out_specs=[pl.BlockSpec((None, KV_HEADS, nchunk, LANES), lambda b: (b, 0, 0, 0)),
                   pl.BlockSpec((None, KV_HEADS, HEAD_DIM, nchunk), lambda b: (b, 0, 0, 0))],
        out_shape=[jax.ShapeDtypeStruct((batch, KV_HEADS, nchunk, LANES), BF16),
                   jax.ShapeDtypeStruct((batch, KV_HEADS, HEAD_DIM, nchunk), BF16)],
        scratch_shapes=[pltpu.VMEM((CMP_BLOCK // CMP_STRIDE, CMP_STRIDE // CMP_PAIR, CMP_PAIR * KV_DIM, KV_HEADS * hid), BF16),
                        pltpu.VMEM((KV_HEADS * hid, KV_HEADS * LANES), BF16)] * 2,
        compiler_params=pltpu.CompilerParams(dimension_semantics=("arbitrary",)),
        name="compress",
    )(kc, vc, *wk, *wv)


MASK_COL = HEAD_DIM


def _select_blocks_t(imp, t0):
    nb, cols = imp.shape
    j = lax.broadcasted_iota(jnp.int32, (nb, cols), 0)
    cur = (t0 + lax.broadcasted_iota(jnp.int32, (nb, cols), 1)) >> int(np.log2(SLC_BLOCK))
    forced = (j == 0) | (j == cur) | (j == cur - 1)
    score = jnp.where(forced, BIG, jnp.where(j <= cur, imp, -BIG))
    sub = 8
    tiles = [score[v * sub:(v + 1) * sub] for v in range(nb // sub)]
    ranks = [jnp.zeros((sub, cols), F32) for _ in tiles]
    jl = lax.broadcasted_iota(jnp.int32, (sub, cols), 0)
    for jp in range(nb):
        row = score[jp:jp + 1, :]
        for v, tile in enumerate(tiles):
            if v * sub > jp:
                beats = row >= tile
            elif (v + 1) * sub - 1 <= jp:
                beats = row > tile
            else:
                beats = (row > tile) | ((row == tile) & (jl > jp - v * sub))
            ranks[v] = jnp.where(beats, ranks[v] + 1.0, ranks[v])
    rank = jnp.concatenate(ranks, axis=0)
    return jnp.where((rank < float(min(SLC_TOP_N, nb))) & (j <= cur), 1.0, 0.0)


def _put_heads(o_ref, g, o, nq=QBLK):
    for i in range(REP // 2):
        pair = jnp.concatenate([o[:, (2 * i) * nq:(2 * i + 1) * nq],
                                o[:, (2 * i + 1) * nq:(2 * i + 2) * nq]], axis=0)
        col = (g * REP + 2 * i) * HEAD_DIM
        o_ref[:, col:col + LANES] = pair.T.astype(o_ref.dtype)


class _Pipeline:
    def __init__(self, k_ref, vt_ref, q_ref, s_ref, p_ref, acc_ref, chunk_of):
        self.k_ref, self.vt_ref, self.q_ref = k_ref, vt_ref, q_ref
        self.s_ref, self.p_ref, self.acc_ref = s_ref, p_ref, acc_ref
        self.chunk_of = chunk_of
        self.groups = range(KV_HEADS)

    def qk(self, g, j):
        start = pl.multiple_of(self.chunk_of(j) * KSTEP, KSTEP)
        depth = self.q_ref.shape[1]
        return _dot(self.k_ref[pl.ds(start, KSTEP), g * LANES:g * LANES + depth], self.q_ref[g])

    def keep(self, g, st, bias):
        if bias is not None:
            st = st + bias
        self.s_ref[g] = st
        return jnp.max(st, axis=0, keepdims=True)

    def softmax(self, g, m, cmax):
        m_new = jnp.maximum(m, cmax)
        a = jnp.exp2(m - m_new)
        self.p_ref[g] = jnp.exp2(self.s_ref[g] - m_new).astype(BF16)
        return m_new, a

    def values(self, g, j, a):
        vt = self.vt_ref[self.chunk_of(j), g * VROWS:(g + 1) * VROWS, :]
        self.acc_ref[g] = a * self.acc_ref[g] + _dot(vt, self.p_ref[g])

    def start(self, bias0, bias1):
        m0 = jnp.full((1, self.s_ref.shape[2]), NEG, F32)
        cmax = [self.keep(g, self.qk(g, 0), bias0) for g in self.groups]
        st1 = [self.qk(g, 1) for g in self.groups]
        sm = [self.softmax(g, m0, cmax[g]) for g in self.groups]
        cmax = tuple(self.keep(g, st1[g], bias1) for g in self.groups)
        for g in self.groups:
            self.acc_ref[g] = jnp.zeros(self.acc_ref.shape[1:], F32)
        return (tuple(s[0] for s in sm), tuple(s[1] for s in sm), cmax)

    def step(self, j, state, last=False, bias=None):
        m, a, cmax = state
        st = None if last else [self.qk(g, j + 2) for g in self.groups]
        for g in self.groups:
            self.values(g, j, a[g])
        sm = [self.softmax(g, m[g], cmax[g]) for g in self.groups]
        if not last:
            cmax = tuple(self.keep(g, st[g], bias) for g in self.groups)
        return (tuple(s[0] for s in sm), tuple(s[1] for s in sm), cmax)

    def finish(self, count, state):
        _, a, _ = state
        out = []
        for g in self.groups:
            self.values(g, count - 1, a[g])
            acc = self.acc_ref[g]
            out.append(acc[:HEAD_DIM] * (1.0 / acc[HEAD_DIM:HEAD_DIM + 1]))
        return out


def _attn_kernel(sink_ref, abias_ref, pbias_ref, cbias_ref, ovt_ref,
                 qa_ref, ka_ref, vat_ref, qn_ref, qnr_ref, gnt_ref, kcmp_ref, vcmpt_ref,
                 ksl_ref, vslt_ref, kw_ref, vwt_ref,
                 oa_ref, ob_ref,
                 qw_ref, sw_ref, pw_ref, accw_ref, qs_ref, ss_ref, ps_ref, accs_ref, sa_ref, pa_ref, sc_ref):
    n = pl.program_id(1)
    groups = range(KV_HEADS)
    width = REP * QBLK
    nb = ovt_ref.shape[0]
    nc = kcmp_ref.shape[1]
    t0 = pl.multiple_of(n * QBLK, QBLK)
    heads = lambda g: [g * REP + r for r in range(REP)]
    parts = range(QBLK // LANES)
    head_t = lambda ref, h, i: ref[i, h * HEAD_DIM:(h + 1) * HEAD_DIM, :]
    cols = lambda ref, g: jnp.concatenate([head_t(ref, h, i) for h in heads(g) for i in parts], axis=1)
    lane_chunks = lambda a: [a[:, r * QBLK:(r + 1) * QBLK] for r in range(REP)]
    gone = lambda cond: jnp.where(cond, 0.0, NEG)

    assert B_WINDOW == 2 * KSTEP
    win = _Pipeline(kw_ref, vwt_ref, qw_ref, sw_ref, pw_ref, accw_ref, lambda j: jnp.maximum(n - j, 0))
    slc = _Pipeline(ksl_ref, vslt_ref, qs_ref, ss_ref, ps_ref, accs_ref,
                    lambda j: jnp.where(j == 0, n, jnp.maximum(j - 1, 0)))
    slc_count = jnp.maximum(n, 1) + 1

    for g in groups:
        qw_ref[g] = cols(qnr_ref, g)
    a_chunk = [n * len(parts) + i for i in parts]
    a_first = [jnp.maximum(c - 1, 0) for c in a_chunk]
    part_cols = lambda i, g: jnp.concatenate([head_t(qa_ref, h, i) for h in heads(g)], axis=1)
    amax = {}
    for i in parts:
        abias = abias_ref[jnp.minimum(a_chunk[i], 1)]
        astart = pl.multiple_of(a_first[i] * A_WINDOW, A_WINDOW)
        for g in groups:
            st = _dot(ka_ref[pl.ds(astart, 2 * A_WINDOW), g * LANES:g * LANES + HEAD_DIM], part_cols(i, g)) + abias
            sa_ref[i, g] = st
            amax[i, g] = jnp.max(st, axis=0, keepdims=True)
    cstart = pl.multiple_of(cbias_ref.shape[0] - nc - n * (QBLK // CMP_STRIDE), 8)
    cbias = cbias_ref[pl.ds(cstart, nc), :]
    cmax = []
    for g in groups:
        s = _dot(kcmp_ref[g, :, :HEAD_DIM], cols(qn_ref, g)) + cbias
        sc_ref[g] = s
        cmax.append(jnp.max(s, axis=0, keepdims=True))
    win_state = win.start(pbias_ref[0], gone(n >= 1))

    x_swa = {}
    for i in parts:
        for g in groups:
            sk = jnp.concatenate([jnp.full((1, A_WINDOW), sink_ref[h] * LOG2E, F32) for h in heads(g)], axis=1)
            m = jnp.maximum(amax[i, g], sk)
            pa_ref[i, g] = jnp.exp2(sa_ref[i, g] - m).astype(BF16)
            x_swa[i, g] = jnp.exp2(sk - m)
    tq = t0 + (lax.broadcasted_iota(jnp.int32, (1, width), 1) & (QBLK - 1))
    sees_any = tq >= CMP_BLOCK - 1
    p_cmp = []
    for g in groups:
        e = jnp.exp2(sc_ref[g] - cmax[g])
        inv = jnp.where(sees_any, 1.0 / jnp.maximum(jnp.sum(e, axis=0, keepdims=True), 1e-30), 0.0)
        p_cmp.append(e * inv)

    win_state = win.step(0, win_state, bias=pbias_ref[1] + gone(n >= 2))

    for i in parts:
        for g in groups:
            vt = jnp.concatenate([vat_ref[a_first[i] + k, g * VROWS:(g + 1) * VROWS, :] for k in range(2)], axis=1)
            o = _dot(vt, pa_ref[i, g])
            o = o[:HEAD_DIM] * (1.0 / (o[HEAD_DIM:HEAD_DIM + 1] + x_swa[i, g]))
            _put_heads(oa_ref.at[i * A_WINDOW:(i + 1) * A_WINDOW, :], g, o, A_WINDOW)
    o_cmp = [_dot(vcmpt_ref[g], p_cmp[g].astype(BF16)) for g in groups]
    imp = []
    for g in groups:
        pc = lane_chunks(p_cmp[g])
        psum = (pc[0] + pc[1]) + (pc[2] + pc[3])
        hi = psum.astype(BF16)
        lo = (psum - hi.astype(F32)).astype(BF16)
        imp.append(_dot(ovt_ref[...], hi) + _dot(ovt_ref[...], lo))
    o_win = win.finish(3, win.step(1, win_state, last=True))

    jb = lax.broadcasted_iota(jnp.int32, (nb, QBLK), 0)
    curb = (t0 + lax.broadcasted_iota(jnp.int32, (nb, QBLK), 1)) >> int(np.log2(SLC_BLOCK))
    sel = lax.cond((t0 + QBLK - 1) // SLC_BLOCK < SLC_TOP_N,
                   lambda: tuple(jnp.where(jb <= curb, 1.0, 0.0) for _ in groups),
                   lambda: tuple(_select_blocks_t(imp[g], t0) for g in groups))

    for g in groups:
        qs_ref[g, :HEAD_DIM, :] = cols(qnr_ref, g)
        mrows = jnp.concatenate([sel[g] - 1.0, jnp.zeros((LANES - MASK_COL - nb, QBLK), F32)], axis=0).astype(BF16)
        for r in range(REP):
            qs_ref[g, MASK_COL:, r * QBLK:(r + 1) * QBLK] = mrows
    slc_state = slc.start(pbias_ref[0], gone(n >= 1))

    no_bias = gone(n >= 0)
    one_step = lambda j, s: slc.step(j, s, bias=no_bias)
    trips = slc_count - 2
    slc_state = lax.fori_loop(0, trips // 2, lambda i, s: one_step(2 * i + 1, one_step(2 * i, s)), slc_state)
    slc_state = lax.cond(trips % 2 == 1, lambda s: one_step(trips - 1, s), lambda s: s, slc_state)

    o_slc = slc.finish(slc_count, slc.step(slc_count - 2, slc_state, last=True))
    for g in groups:
        gate = lambda br: jnp.concatenate(
            [gnt_ref[i, br * Q_HEADS + h:br * Q_HEADS + h + 1, :]
             for h in heads(g) for i in range(gnt_ref.shape[0])], axis=1)
        _put_heads(ob_ref, g, gate(0) * o_cmp[g] + gate(1) * o_slc[g] + gate(2) * o_win[g])


def _attn(sinks, qa, ka, vat, qn, qnr, gnt, kcmp, vcmpt, ksl, vslt, kw, vwt, batch, seq):
    nq = seq // QBLK
    nc = kcmp.shape[2]
    nb = seq // SLC_BLOCK
    width = REP * QBLK
    assert nb <= LANES - MASK_COL and QBLK % SLC_BLOCK == 0 and QBLK % A_WINDOW == 0 and A_WINDOW % LANES == 0
    c_start = np.arange(nc)[None, :] * CMP_STRIDE
    s_start = np.arange(nb)[:, None] * SLC_BLOCK
    ovt = jnp.asarray((c_start < s_start + SLC_BLOCK) & (s_start < c_start + CMP_BLOCK), BF16)
    ql = np.arange(width)[None, :] % QBLK
    masked = lambda visible: np.where(visible, 0.0, NEG)
    kq = np.arange(KSTEP)[:, None] - ql
    pbias = jnp.asarray(np.stack([masked(kq <= 0),
                                  masked(kq > 0)]), F32)
    band = np.arange(2 * A_WINDOW)[:, None]
    qpart = np.arange(REP * A_WINDOW)[None, :] % A_WINDOW
    in_window = lambda dist: (dist >= 0) & (dist < A_WINDOW)
    abias = jnp.asarray(np.stack([masked(in_window(qpart - band)),
                                  masked(in_window(qpart - (band - A_WINDOW)))]), F32)
    per_blk = QBLK // CMP_STRIDE
    d = np.arange(nc + (nq - 1) * per_blk)[:, None] - (nq - 1) * per_blk
    cbias = jnp.asarray(masked(d * CMP_STRIDE + CMP_BLOCK - 1 <= ql), F32)
    q_spec = pl.BlockSpec((None, QBLK // LANES, Q_DIM, LANES), lambda b, n: (b, n, 0, 0))
    g_spec = pl.BlockSpec((None, QBLK // LANES, LANES, LANES), lambda b, n: (b, n, 0, 0))
    kc_spec = pl.BlockSpec((None, KV_HEADS, nc, LANES), lambda b, n: (b, 0, 0, 0))
    vc_spec = pl.BlockSpec((None, KV_HEADS, HEAD_DIM, nc), lambda b, n: (b, 0, 0, 0))
    const = lambda a: pl.BlockSpec(a.shape, lambda b, n: (0,) * a.ndim, pipeline_mode=pl.Buffered(1))
    k_spec = pl.BlockSpec((seq, KV_HEADS * LANES), lambda b, n: (b, 0))
    vt_spec = lambda keys: pl.BlockSpec((None, seq // keys, KV_HEADS * VROWS, keys), lambda b, n: (b, 0, 0, 0))
    o_spec = pl.BlockSpec((QBLK, Q_DIM), lambda b, n: (b * nq + n, 0))
    o_shape = jax.ShapeDtypeStruct((batch * seq, Q_DIM), BF16)
    pipe_scratch = lambda depth: [
        pltpu.VMEM((KV_HEADS, depth, width), BF16),
        pltpu.VMEM((KV_HEADS, KSTEP, width), F32),
        pltpu.VMEM((KV_HEADS, KSTEP, width), BF16),
        pltpu.VMEM((KV_HEADS, VROWS, width), F32)]
    return pl.pallas_call(
        _attn_kernel,
        grid=(batch, nq),
        in_specs=[pl.BlockSpec(memory_space=pltpu.SMEM), const(abias), const(pbias), const(cbias), const(ovt),
                  q_spec, k_spec, vt_spec(A_WINDOW), q_spec, q_spec, g_spec, kc_spec, vc_spec,
                  k_spec, vt_spec(KSTEP), k_spec, vt_spec(KSTEP)],
        out_specs=[o_spec, o_spec],
        out_shape=[o_shape, o_shape],
        scratch_shapes=pipe_scratch(HEAD_DIM) + pipe_scratch(LANES) + [
            pltpu.VMEM((QBLK // A_WINDOW, KV_HEADS, 2 * A_WINDOW, REP * A_WINDOW), F32),
            pltpu.VMEM((QBLK // A_WINDOW, KV_HEADS, 2 * A_WINDOW, REP * A_WINDOW), BF16),
            pltpu.VMEM((KV_HEADS, nc, width), F32)],
        compiler_params=pltpu.CompilerParams(dimension_semantics=("parallel", "arbitrary"),
                                             vmem_limit_bytes=VMEM_WIDE),
        name="attn",
    )(sinks, abias, pbias, cbias, ovt, qa, ka, vat, qn, qnr, gnt, kcmp, vcmpt, ksl, vslt, kw, vwt)


def _layer_norm(r, g, b):
    mu = jnp.mean(r, axis=-1, keepdims=True)
    d = r - mu
    var = jnp.mean(d * d, axis=-1, keepdims=True)
    return d * lax.rsqrt(var + LN_EPS) * g + b


POST_PARTS = 4


def _post_kernel(alpha, x_ref, oa_ref, ob_ref, wgm_ref, wpa_ref, wpb_ref, wout_ref, g_ref, b_ref, h_ref):
    d = x_ref.shape[1]
    rows = [slice(i * (x_ref.shape[0] // POST_PARTS), (i + 1) * (x_ref.shape[0] // POST_PARTS))
            for i in range(POST_PARTS)]
    ys = []
    for r in rows:
        xb = x_ref[r, :].astype(BF16)
        pa = _dot(oa_ref[r, :], wpa_ref[...])
        pb = _dot(ob_ref[r, :], wpb_ref[...])
        ys.append(jax.nn.sigmoid(_dot(xb, wgm_ref[:, :d])) * pa + jax.nn.sigmoid(_dot(xb, wgm_ref[:, d:])) * pb)
    for r, y in zip(rows, ys):
        m = _dot(y.astype(BF16), wout_ref[...])
        h_ref[r, :] = _layer_norm(alpha * x_ref[r, :] + m, g_ref[...], b_ref[...])


def _post(x2, oa, ob, w_gm, wpa, wpb, wout, g, b, alpha, tm):
    n, d = x2.shape
    tok = lambda width: pl.BlockSpec((tm, width), lambda i: (i, 0))
    full = lambda a: pl.BlockSpec(a.shape, lambda i: (0, 0), pipeline_mode=pl.Buffered(1))
    return pl.pallas_call(
        functools.partial(_post_kernel, alpha),
        grid=(n // tm,),
        in_specs=[tok(d), tok(Q_DIM), tok(Q_DIM)] + [full(a) for a in (w_gm, wpa, wpb, wout, g, b)],
        out_specs=tok(d),
        out_shape=jax.ShapeDtypeStruct((n, d), F32),
        compiler_params=pltpu.CompilerParams(dimension_semantics=("parallel",),
                                             vmem_limit_bytes=VMEM_PROJ),
        name="post",
    )(x2, oa, ob, w_gm, wpa, wpb, wout, g, b)


FFN_CHUNK = 256


FFN_PARTS = 2


def _ffn_kernel(alpha, h_ref, wg_ref, wu_ref, wd_ref, g_ref, b_ref, o_ref):
    hidden = wg_ref.shape[1]
    rows = h_ref.shape[0] // FFN_PARTS
    for p in range(FFN_PARTS):
        r = slice(p * rows, (p + 1) * rows)
        h = h_ref[r, :]
        hb = h.astype(BF16)
        acc = jnp.zeros(h.shape, F32)
        for c in range(hidden // FFN_CHUNK):
            sl = slice(c * FFN_CHUNK, (c + 1) * FFN_CHUNK)
            a = jax.nn.silu(_dot(hb, wg_ref[:, sl])) * _dot(hb, wu_ref[:, sl])
            acc = acc + _dot(a.astype(BF16), wd_ref[sl, :])
        o_ref[r, :] = _layer_norm(alpha * h + acc, g_ref[...], b_ref[...])


def _ffn(h, wg, wu, wd, g, b, alpha, tm):
    n, d = h.shape
    tok = pl.BlockSpec((tm, d), lambda i: (i, 0))
    once = lambda a: pl.BlockSpec(a.shape, lambda i: (0, 0), pipeline_mode=pl.Buffered(1))
    return pl.pallas_call(
        functools.partial(_ffn_kernel, alpha),
        grid=(n // tm,),
        in_specs=[tok] + [once(a) for a in (wg, wu, wd, g, b)],
        out_specs=tok,
        out_shape=jax.ShapeDtypeStruct((n, d), F32),
        compiler_params=pltpu.CompilerParams(dimension_semantics=("parallel",),
                                             vmem_limit_bytes=VMEM_WIDE),
        name="ffn",
    )(h, wg, wu, wd, g, b)


def _position_tables(seq):
    half = HEAD_DIM // 2
    inv = ROPE_THETA ** (-jnp.arange(half, dtype=F32) / half)
    ang = jnp.arange(seq).astype(F32)[:, None] * inv[None, :]
    cos, sin = jnp.cos(ang), jnp.sin(ang)
    reps = LANES // HEAD_DIM
    lane = np.arange(LANES)[None, :]
    blk = (np.arange(seq) // SLC_BLOCK)[:, None]
    ke = jnp.asarray(np.where(lane - MASK_COL == blk, -NEG, 0.0), F32)
    return jnp.tile(cos, (1, 2 * reps)), jnp.tile(jnp.concatenate([-sin, sin], axis=1), (1, reps)), ke


def kernel(x, w_in, sinks, cmp_pe_k, cmp_w1_k, cmp_b1_k, cmp_w2_k, cmp_pe_v, cmp_w1_v, cmp_b1_v, cmp_w2_v,
           w_proj_a, w_proj_b, w_out, ln1_g, ln1_b, w_gate, w_up, w_down, ln2_g, ln2_b):
    batch, seq, d = x.shape
    depth = w_in.shape[0]
    alpha = (2 * depth) ** 0.25
    n_main = 2 * Q_DIM + 8 * KV_DIM
    n_gate = 3 * Q_HEADS
    tm = ROW_TILE
    cos, sin, ke = _position_tables(seq)
    pad_cols = lambda w: jnp.pad(w, ((0, 0), (0, LANES - w.shape[1]))).astype(BF16)
    xt = x.reshape(batch * seq, d)
    for l in range(depth):
        w_main = w_in[l, :, :n_main].astype(BF16)
        w_gn = pad_cols(w_in[l, :, n_main:n_main + n_gate])
        w_gm = w_in[l, :, n_main + n_gate:].astype(BF16)
        (qa, ka, vat, qn, qnr, kc, vc, ksl, vslt, kw, vwt, gnt) = _in_proj(
            xt, w_main, w_gn, cos, sin, ke, batch, seq, tm)
        wk = (cmp_pe_k[l], cmp_w1_k[l].astype(BF16), cmp_b1_k[l][None, :], pad_cols(cmp_w2_k[l]))
        wv = (cmp_pe_v[l], cmp_w1_v[l].astype(BF16), cmp_b1_v[l][None, :], pad_cols(cmp_w2_v[l]))
        kcmp, vcmpt = _compress(kc, vc, wk, wv, batch, seq)
        oa, ob = _attn(sinks[l], qa, ka, vat, qn, qnr, gnt, kcmp, vcmpt, ksl, vslt, kw, vwt, batch, seq)
        h = _post(xt, oa, ob, w_gm, w_proj_a[l].astype(BF16), w_proj_b[l].astype(BF16), w_out[l].astype(BF16),
                  ln1_g[l][None, :], ln1_b[l][None, :], alpha, tm)
        xt = _ffn(h, w_gate[l].astype(BF16), w_up[l].astype(BF16), w_down[l].astype(BF16),
                  ln2_g[l][None, :], ln2_b[l][None, :], alpha, tm)
    return xt.reshape(batch, seq, d)
```

```python
import functools

import jax
import jax.numpy as jnp
import numpy as np
from jax import lax
from jax.experimental import pallas as pl
from jax.experimental.pallas import tpu as pltpu

HEAD_DIM = 64
ROPE_THETA = 10000.0
Q_HEADS = 8
KV_HEADS = 2
REP = Q_HEADS // KV_HEADS
A_WINDOW = 128
B_WINDOW = 512
CMP_BLOCK = 32
CMP_STRIDE = 16
SLC_BLOCK = 64
SLC_TOP_N = 16
LN_EPS = 1e-5
NEG = -1e30
BIG = 1e9
Q_DIM = Q_HEADS * HEAD_DIM
KV_DIM = KV_HEADS * HEAD_DIM
LANES = 128
QBLK = 256
KSTEP = QBLK
LOG2E = 1.4426950408889634
SCALE = HEAD_DIM ** -0.5 * LOG2E
VROWS = HEAD_DIM + 16
ROW_TILE = 1024
MIB = 1024 * 1024
VMEM_PROJ = 48 * MIB
VMEM_WIDE = 56 * MIB

F32 = jnp.float32
BF16 = jnp.bfloat16


def _dot(a, b):
    return jnp.dot(a, b, preferred_element_type=F32)


def _inproj_kernel(x_ref, w_ref, wg_ref, cos_ref, sin_ref, cost_ref, sint_ref, ke_ref,
                   qa_ref, ka_ref, vat_ref, qn_ref, qnr_ref, kc_ref, vc_ref,
                   ksl_ref, vslt_ref, kw_ref, vwt_ref, gnt_ref):
    tm = x_ref.shape[0]
    xb = x_ref[...].astype(BF16)
    cos = cos_ref[...]
    sin = sin_ref[...]
    lane = lax.broadcasted_iota(jnp.int32, (tm, LANES), 1)
    first_half = (lane & (HEAD_DIM - 1)) < (HEAD_DIM // 2)
    low = lane < HEAD_DIM

    def rope(z):
        sw = jnp.where(first_half, pltpu.roll(z, LANES - HEAD_DIM // 2, 1), pltpu.roll(z, HEAD_DIM // 2, 1))
        return z * cos + sw * sin

    def proj(c0, n):
        return _dot(xb, w_ref[:, c0:c0 + n])

    def chunks(z):
        return [z[:, c * LANES:(c + 1) * LANES] for c in range(z.shape[1] // LANES)]

    def put_padded(ref, c, z, fill):
        ref[:, (2 * c) * LANES:(2 * c + 1) * LANES] = jnp.where(low, z, fill).astype(ref.dtype)
        ref[:, (2 * c + 1) * LANES:(2 * c + 2) * LANES] = jnp.where(low, pltpu.roll(z, HEAD_DIM, 1), fill).astype(ref.dtype)

    def put_transposed(ref, z):
        for j in range(tm // LANES):
            ref[j] = z[j * LANES:(j + 1) * LANES, :].T.astype(ref.dtype)

    def put_heads_t(c, z, plain_ref, rope_ref):
        half = HEAD_DIM // 2
        for j in range(tm // LANES):
            zt = z[j * LANES:(j + 1) * LANES, :].T * SCALE
            rows = slice(c * LANES, (c + 1) * LANES)
            if plain_ref is not None:
                plain_ref[j, rows, :] = zt.astype(plain_ref.dtype)
            sw = jnp.concatenate([zt[(s ^ 1) * half:((s ^ 1) + 1) * half] for s in range(LANES // half)], axis=0)
            rope_ref[j, rows, :] = (zt * cost_ref[j] + sw * sint_ref[j]).astype(rope_ref.dtype)

    ones_row = jnp.where(lax.broadcasted_iota(jnp.int32, (VROWS - HEAD_DIM, LANES), 0) == 0, 1.0, 0.0)

    def put_values_t(ref, z):
        keys = ref.shape[2]
        for j in range(tm // keys):
            for i in range(keys // LANES):
                zt = z[j * keys + i * LANES:j * keys + (i + 1) * LANES, :].T
                cols = slice(i * LANES, (i + 1) * LANES)
                for g in range(KV_HEADS):
                    ref[j, g * VROWS:g * VROWS + HEAD_DIM, cols] = zt[g * HEAD_DIM:(g + 1) * HEAD_DIM, :].astype(ref.dtype)
                    ref[j, g * VROWS + HEAD_DIM:(g + 1) * VROWS, cols] = ones_row.astype(ref.dtype)

    for c, zc in enumerate(chunks(proj(0, Q_DIM))):
        put_heads_t(c, zc, None, qa_ref)
    z = proj(Q_DIM, 2 * KV_DIM)
    put_padded(ka_ref, 0, rope(z[:, :KV_DIM]), 0.0)
    put_values_t(vat_ref, z[:, KV_DIM:])
    c0 = Q_DIM + 2 * KV_DIM
    for c, zc in enumerate(chunks(proj(c0, Q_DIM))):
        put_heads_t(c, zc, qn_ref, qnr_ref)
    c0 += Q_DIM
    z = proj(c0, 2 * KV_DIM)
    kc_ref[...] = z[:, :KV_DIM]
    vc_ref[...] = z[:, KV_DIM:]
    c0 += 2 * KV_DIM
    z = proj(c0, 2 * KV_DIM)
    put_padded(ksl_ref, 0, rope(z[:, :KV_DIM]), ke_ref[...])
    put_values_t(vslt_ref, z[:, KV_DIM:])
    c0 += 2 * KV_DIM
    z = proj(c0, 2 * KV_DIM)
    put_padded(kw_ref, 0, rope(z[:, :KV_DIM]), 0.0)
    put_values_t(vwt_ref, z[:, KV_DIM:])
    put_transposed(gnt_ref, jax.nn.sigmoid(_dot(xb, wg_ref[...])))


def _in_proj(x2, w_main, w_gn, cos, sin, ke, batch, seq, tm):
    n = x2.shape[0]
    d = x2.shape[1]
    spt = seq // tm
    tok = lambda width: pl.BlockSpec((tm, width), lambda i: (i, 0))
    full = lambda a: pl.BlockSpec(a.shape, lambda i: (0, 0))
    tab = pl.BlockSpec((tm, LANES), lambda i: (i % spt, 0))
    tab_t = pl.BlockSpec((tm // LANES, LANES, LANES), lambda i: (i % spt, 0, 0))
    cos_t, sin_t = (jnp.transpose(t.reshape(seq // LANES, LANES, LANES), (0, 2, 1)) for t in (cos, sin))
    flat = lambda w, dt: (tok(w), jax.ShapeDtypeStruct((n, w), dt))
    trans = lambda r, keys, dt: (pl.BlockSpec((None, tm // keys, r, keys), lambda i: (i // spt, i % spt, 0, 0)),
                                 jax.ShapeDtypeStruct((batch, seq // keys, r, keys), dt))
    outs = [trans(Q_DIM, LANES, BF16), flat(KV_HEADS * LANES, BF16),
            trans(KV_HEADS * VROWS, A_WINDOW, BF16),
            trans(Q_DIM, LANES, BF16), trans(Q_DIM, LANES, BF16),
            flat(KV_DIM, F32), flat(KV_DIM, F32),
            flat(KV_HEADS * LANES, BF16), trans(KV_HEADS * VROWS, KSTEP, BF16),
            flat(KV_HEADS * LANES, BF16), trans(KV_HEADS * VROWS, KSTEP, BF16),
            trans(LANES, LANES, F32)]
    return pl.pallas_call(
        _inproj_kernel,
        grid=(n // tm,),
        in_specs=[tok(d), full(w_main), full(w_gn), tab, tab, tab_t, tab_t, tab],
        out_specs=[o[0] for o in outs],
        out_shape=[o[1] for o in outs],
        compiler_params=pltpu.CompilerParams(dimension_semantics=("parallel",),
                                             vmem_limit_bytes=VMEM_PROJ),
        name="in_proj",
    )(x2, w_main, w_gn, cos, sin, cos_t, sin_t, ke)


CMP_PAIR = 2


def _compress_one(src_ref, pe_ref, w1_ref, b1_ref, w2_ref):
    ratio = CMP_BLOCK // CMP_STRIDE
    nchunk = src_ref.shape[0] // CMP_STRIDE
    both = lambda row: jnp.concatenate([row] * KV_HEADS, axis=1)
    parts = [jnp.zeros((nchunk, w1_ref.shape[3]), F32) for _ in range(ratio)]
    for p in range(CMP_STRIDE // CMP_PAIR):
        rows = [src_ref[pl.ds(p * CMP_PAIR + i, nchunk, stride=CMP_STRIDE), :] for i in range(CMP_PAIR)]
        for j in range(ratio):
            pos = j * CMP_STRIDE + p * CMP_PAIR
            a = jnp.concatenate([(rows[i] + both(pe_ref[pos + i:pos + i + 1, :])).astype(BF16)
                                 for i in range(CMP_PAIR)], axis=1)
            parts[j] = parts[j] + _dot(a, w1_ref[j, p])
    h = parts[0]
    for j in range(1, ratio):
        h = h + pltpu.roll(parts[j], nchunk - j, 0)
    h = jax.nn.gelu(h + both(b1_ref[...]))
    return _dot(h.astype(BF16), w2_ref[...])


def _group_diagonal(w1_ref, w2_ref, w1d_ref, w2d_ref):
    hid = w1_ref.shape[1]
    w1d_ref[...] = jnp.zeros(w1d_ref.shape, w1d_ref.dtype)
    w2d_ref[...] = jnp.zeros(w2d_ref.shape, w2d_ref.dtype)
    for g in range(KV_HEADS):
        w2d_ref[g * hid:(g + 1) * hid, g * LANES:(g + 1) * LANES] = w2_ref[...]
        for j in range(CMP_BLOCK // CMP_STRIDE):
            for p in range(CMP_STRIDE // CMP_PAIR):
                for i in range(CMP_PAIR):
                    pos = j * CMP_STRIDE + p * CMP_PAIR + i
                    row = i * KV_DIM + g * HEAD_DIM
                    w1d_ref[j, p, row:row + HEAD_DIM, g * hid:(g + 1) * hid] = w1_ref[pos * HEAD_DIM:(pos + 1) * HEAD_DIM, :]


def _compress_kernel(kc_ref, vc_ref, pek_ref, w1k_ref, b1k_ref, w2k_ref,
                     pev_ref, w1v_ref, b1v_ref, w2v_ref, kcmp_ref, vcmpt_ref,
                     w1dk_ref, w2dk_ref, w1dv_ref, w2dv_ref):
    @pl.when(pl.program_id(0) == 0)
    def _():
        _group_diagonal(w1k_ref, w2k_ref, w1dk_ref, w2dk_ref)
        _group_diagonal(w1v_ref, w2v_ref, w1dv_ref, w2dv_ref)

    kc = _compress_one(kc_ref, pek_ref, w1dk_ref, b1k_ref, w2dk_ref)
    vc = _compress_one(vc_ref, pev_ref, w1dv_ref, b1v_ref, w2dv_ref)
    for g in range(KV_HEADS):
        kcmp_ref[g] = kc[:, g * LANES:(g + 1) * LANES].astype(BF16)
        vcmpt_ref[g] = vc[:, g * LANES:(g + 1) * LANES].T[:HEAD_DIM, :].astype(BF16)


def _compress(kc, vc, wk, wv, batch, seq):
    nchunk = seq // CMP_STRIDE
    hid = wk[1].shape[1]
    src = pl.BlockSpec((seq, KV_DIM), lambda b: (b, 0))
    full = lambda a: pl.BlockSpec(a.shape, lambda b: (0,) * a.ndim)
    return pl.pallas_call(
        _compress_kernel,
        grid=(batch,),
        in_specs=[src, src] + [full(a) for a in wk] + [full(a) for a in wv],
        out_specs=[pl.BlockSpec((None, KV_HEADS, nchunk, LANES), lambda b: (b, 0, 0, 0)),
                   pl.BlockSpec((None, KV_HEADS, HEAD_DIM, nchunk), lambda b: (b, 0, 0, 0))],
        out_shape=[jax.ShapeDtypeStruct((batch, KV_HEADS, nchunk, LANES), BF16),
                   jax.ShapeDtypeStruct((batch, KV_HEADS, HEAD_DIM, nchunk), BF16)],
        scratch_shapes=[pltpu.VMEM((CMP_BLOCK // CMP_STRIDE, CMP_STRIDE // CMP_PAIR, CMP_PAIR * KV_DIM, KV_HEADS * hid), BF16),
                        pltpu.VMEM((KV_HEADS * hid, KV_HEADS * LANES), BF16)] * 2,
        compiler_params=pltpu.CompilerParams(dimension_semantics=("arbitrary",)),
        name="compress",
    )(kc, vc, *wk, *wv)


MASK_COL = HEAD_DIM


def _select_blocks_t(imp, t0):
    nb, cols = imp.shape
    j = lax.broadcasted_iota(jnp.int32, (nb, cols), 0)
    cur = (t0 + lax.broadcasted_iota(jnp.int32, (nb, cols), 1)) >> int(np.log2(SLC_BLOCK))
    forced = (j == 0) | (j == cur) | (j == cur - 1)
    score = jnp.where(forced, BIG, jnp.where(j <= cur, imp, -BIG))
    sub = 8
    tiles = [score[v * sub:(v + 1) * sub] for v in range(nb // sub)]
    ranks = [jnp.zeros((sub, cols), F32) for _ in tiles]
    jl = lax.broadcasted_iota(jnp.int32, (sub, cols), 0)
    for jp in range(nb):
        row = score[jp:jp + 1, :]
        for v, tile in enumerate(tiles):
            if v * sub > jp:
                beats = row >= tile
            elif (v + 1) * sub - 1 <= jp:
                beats = row > tile
            else:
                beats = (row > tile) | ((row == tile) & (jl > jp - v * sub))
            ranks[v] = jnp.where(beats, ranks[v] + 1.0, ranks[v])
    rank = jnp.concatenate(ranks, axis=0)
    return jnp.where((rank < float(min(SLC_TOP_N, nb))) & (j <= cur), 1.0, 0.0)


def _put_heads(o_ref, g, o, nq=QBLK):
    for i in range(REP // 2):
        pair = jnp.concatenate([o[:, (2 * i) * nq:(2 * i + 1) * nq],
                                o[:, (2 * i + 1) * nq:(2 * i + 2) * nq]], axis=0)
        col = (g * REP + 2 * i) * HEAD_DIM
        o_ref[:, col:col + LANES] = pair.T.astype(o_ref.dtype)


class _Pipeline:
    def __init__(self, k_ref, vt_ref, q_ref, s_ref, p_ref, acc_ref, chunk_of):
        self.k_ref, self.vt_ref, self.q_ref = k_ref, vt_ref, q_ref
        self.s_ref, self.p_ref, self.acc_ref = s_ref, p_ref, acc_ref
        self.chunk_of = chunk_of
        self.groups = range(KV_HEADS)

    def qk(self, g, j):
        start = pl.multiple_of(self.chunk_of(j) * KSTEP, KSTEP)
        depth = self.q_ref.shape[1]
        return _dot(self.k_ref[pl.ds(start, KSTEP), g * LANES:g * LANES + depth], self.q_ref[g])

    def keep(self, g, st, bias):
        if bias is not None:
            st = st + bias
        self.s_ref[g] = st
        return jnp.max(st, axis=0, keepdims=True)

    def softmax(self, g, m, cmax):
        m_new = jnp.maximum(m, cmax)
        a = jnp.exp2(m - m_new)
        self.p_ref[g] = jnp.exp2(self.s_ref[g] - m_new).astype(BF16)
        return m_new, a

    def values(self, g, j, a):
        vt = self.vt_ref[self.chunk_of(j), g * VROWS:(g + 1) * VROWS, :]
        self.acc_ref[g] = a * self.acc_ref[g] + _dot(vt, self.p_ref[g])

    def start(self, bias0, bias1):
        m0 = jnp.full((1, self.s_ref.shape[2]), NEG, F32)
        cmax = [self.keep(g, self.qk(g, 0), bias0) for g in self.groups]
        st1 = [self.qk(g, 1) for g in self.groups]
        sm = [self.softmax(g, m0, cmax[g]) for g in self.groups]
        cmax = tuple(self.keep(g, st1[g], bias1) for g in self.groups)
        for g in self.groups:
            self.acc_ref[g] = jnp.zeros(self.acc_ref.shape[1:], F32)
        return (tuple(s[0] for s in sm), tuple(s[1] for s in sm), cmax)

    def step(self, j, state, last=False, bias=None):
        m, a, cmax = state
        st = None if last else [self.qk(g, j + 2) for g in self.groups]
        for g in self.groups:
            self.values(g, j, a[g])
        sm = [self.softmax(g, m[g], cmax[g]) for g in self.groups]
        if not last:
            cmax = tuple(self.keep(g, st[g], bias) for g in self.groups)
        return (tuple(s[0] for s in sm), tuple(s[1] for s in sm), cmax)

    def finish(self, count, state):
        _, a, _ = state
        out = []
        for g in self.groups:
            self.values(g, count - 1, a[g])
            acc = self.acc_ref[g]
            out.append(acc[:HEAD_DIM] * (1.0 / acc[HEAD_DIM:HEAD_DIM + 1]))
        return out


ATTN_BLOCKS = 2


def _attn_kernel(sink_ref, abias_ref, pbias_ref, cbias_ref, ovt_ref,
                 qa_ref, ka_ref, vat_ref, qn_ref, qnr_ref, gnt_ref, kcmp_ref, vcmpt_ref,
                 ksl_ref, vslt_ref, kw_ref, vwt_ref, oa_ref, ob_ref, *scratch):
    tiles = QBLK // LANES
    for i in range(ATTN_BLOCKS):
        tok = lambda ref: ref.at[pl.ds(i * tiles, tiles)]
        out = lambda ref: ref.at[pl.ds(i * QBLK, QBLK), :]
        _attn_block(pl.program_id(1) * ATTN_BLOCKS + i, sink_ref, abias_ref, pbias_ref, cbias_ref, ovt_ref,
                    tok(qa_ref), ka_ref, vat_ref, tok(qn_ref), tok(qnr_ref), tok(gnt_ref), kcmp_ref, vcmpt_ref,
                    ksl_ref, vslt_ref, kw_ref, vwt_ref, out(oa_ref), out(ob_ref), *scratch)


def _attn_block(n, sink_ref, abias_ref, pbias_ref, cbias_ref, ovt_ref,
                qa_ref, ka_ref, vat_ref, qn_ref, qnr_ref, gnt_ref, kcmp_ref, vcmpt_ref,
                ksl_ref, vslt_ref, kw_ref, vwt_ref,
                oa_ref, ob_ref,
                qw_ref, sw_ref, pw_ref, accw_ref, qs_ref, ss_ref, ps_ref, accs_ref, sa_ref, pa_ref, sc_ref):
    groups = range(KV_HEADS)
    width = REP * QBLK
    nb = ovt_ref.shape[0]
    nc = kcmp_ref.shape[1]
    t0 = pl.multiple_of(n * QBLK, QBLK)
    heads = lambda g: [g * REP + r for r in range(REP)]
    parts = range(QBLK // LANES)
    head_t = lambda ref, h, i: ref[i, h * HEAD_DIM:(h + 1) * HEAD_DIM, :]
    cols = lambda ref, g: jnp.concatenate([head_t(ref, h, i) for h in heads(g) for i in parts], axis=1)
    lane_chunks = lambda a: [a[:, r * QBLK:(r + 1) * QBLK] for r in range(REP)]
    gone = lambda cond: jnp.where(cond, 0.0, NEG)

    assert B_WINDOW == 2 * KSTEP
    win = _Pipeline(kw_ref, vwt_ref, qw_ref, sw_ref, pw_ref, accw_ref, lambda j: jnp.maximum(n - j, 0))
    slc = _Pipeline(ksl_ref, vslt_ref, qs_ref, ss_ref, ps_ref, accs_ref,
                    lambda j: jnp.where(j == 0, n, jnp.maximum(j - 1, 0)))
    slc_count = jnp.maximum(n, 1) + 1

    for g in groups:
        qw_ref[g] = cols(qnr_ref, g)
    a_chunk = [n * len(parts) + i for i in parts]
    a_first = [jnp.maximum(c - 1, 0) for c in a_chunk]
    part_cols = lambda i, g: jnp.concatenate([head_t(qa_ref, h, i) for h in heads(g)], axis=1)
    amax = {}
    for i in parts:
        abias = abias_ref[jnp.minimum(a_chunk[i], 1)]
        astart = pl.multiple_of(a_first[i] * A_WINDOW, A_WINDOW)
        for g in groups:
            st = _dot(ka_ref[pl.ds(astart, 2 * A_WINDOW), g * LANES:g * LANES + HEAD_DIM], part_cols(i, g)) + abias
            sa_ref[i, g] = st
            amax[i, g] = jnp.max(st, axis=0, keepdims=True)
    cstart = pl.multiple_of(cbias_ref.shape[0] - nc - n * (QBLK // CMP_STRIDE), 8)
    cbias = cbias_ref[pl.ds(cstart, nc), :]
    cmax = []
    for g in groups:
        s = _dot(kcmp_ref[g, :, :HEAD_DIM], cols(qn_ref, g)) + cbias
        sc_ref[g] = s
        cmax.append(jnp.max(s, axis=0, keepdims=True))
    win_state = win.start(pbias_ref[0], gone(n >= 1))

    x_swa = {}
    for i in parts:
        for g in groups:
            sk = jnp.concatenate([jnp.full((1, A_WINDOW), sink_ref[h] * LOG2E, F32) for h in heads(g)], axis=1)
            m = jnp.maximum(amax[i, g], sk)
            pa_ref[i, g] = jnp.exp2(sa_ref[i, g] - m).astype(BF16)
            x_swa[i, g] = jnp.exp2(sk - m)
    tq = t0 + (lax.broadcasted_iota(jnp.int32, (1, width), 1) & (QBLK - 1))
    sees_any = tq >= CMP_BLOCK - 1
    p_cmp = []
    for g in groups:
        e = jnp.exp2(sc_ref[g] - cmax[g])
        inv = jnp.where(sees_any, 1.0 / jnp.maximum(jnp.sum(e, axis=0, keepdims=True), 1e-30), 0.0)
        p_cmp.append(e * inv)

    win_state = win.step(0, win_state, bias=pbias_ref[1] + gone(n >= 2))

    for i in parts:
        for g in groups:
            vt = jnp.concatenate([vat_ref[a_first[i] + k, g * VROWS:(g + 1) * VROWS, :] for k in range(2)], axis=1)
            o = _dot(vt, pa_ref[i, g])
            o = o[:HEAD_DIM] * (1.0 / (o[HEAD_DIM:HEAD_DIM + 1] + x_swa[i, g]))
            _put_heads(oa_ref.at[i * A_WINDOW:(i + 1) * A_WINDOW, :], g, o, A_WINDOW)
    o_cmp = [_dot(vcmpt_ref[g], p_cmp[g].astype(BF16)) for g in groups]
    imp = []
    for g in groups:
        pc = lane_chunks(p_cmp[g])
        psum = (pc[0] + pc[1]) + (pc[2] + pc[3])
        hi = psum.astype(BF16)
        lo = (psum - hi.astype(F32)).astype(BF16)
        imp.append(_dot(ovt_ref[...], hi) + _dot(ovt_ref[...], lo))
    o_win = win.finish(3, win.step(1, win_state, last=True))

    jb = lax.broadcasted_iota(jnp.int32, (nb, QBLK), 0)
    curb = (t0 + lax.broadcasted_iota(jnp.int32, (nb, QBLK), 1)) >> int(np.log2(SLC_BLOCK))
    sel = lax.cond((t0 + QBLK - 1) // SLC_BLOCK < SLC_TOP_N,
                   lambda: tuple(jnp.where(jb <= curb, 1.0, 0.0) for _ in groups),
                   lambda: tuple(_select_blocks_t(imp[g], t0) for g in groups))

    for g in groups:
        qs_ref[g, :HEAD_DIM, :] = cols(qnr_ref, g)
        mrows = jnp.concatenate([sel[g] - 1.0, jnp.zeros((LANES - MASK_COL - nb, QBLK), F32)], axis=0).astype(BF16)
        for r in range(REP):
            qs_ref[g, MASK_COL:, r * QBLK:(r + 1) * QBLK] = mrows
    slc_state = slc.start(pbias_ref[0], gone(n >= 1))

    no_bias = gone(n >= 0)
    one_step = lambda j, s: slc.step(j, s, bias=no_bias)
    trips = slc_count - 2
    slc_state = lax.fori_loop(0, trips // 2, lambda i, s: one_step(2 * i + 1, one_step(2 * i, s)), slc_state)
    slc_state = lax.cond(trips % 2 == 1, lambda s: one_step(trips - 1, s), lambda s: s, slc_state)

    o_slc = slc.finish(slc_count, slc.step(slc_count - 2, slc_state, last=True))
    for g in groups:
        gate = lambda br: jnp.concatenate(
            [gnt_ref[i, br * Q_HEADS + h:br * Q_HEADS + h + 1, :]
             for h in heads(g) for i in range(gnt_ref.shape[0])], axis=1)
        _put_heads(ob_ref, g, gate(0) * o_cmp[g] + gate(1) * o_slc[g] + gate(2) * o_win[g])


def _attn(sinks, qa, ka, vat, qn, qnr, gnt, kcmp, vcmpt, ksl, vslt, kw, vwt, batch, seq):
    nq = seq // QBLK
    nc = kcmp.shape[2]
    nb = seq // SLC_BLOCK
    width = REP * QBLK
    assert nb <= LANES - MASK_COL and QBLK % SLC_BLOCK == 0 and QBLK % A_WINDOW == 0 and A_WINDOW % LANES == 0
    c_start = np.arange(nc)[None, :] * CMP_STRIDE
    s_start = np.arange(nb)[:, None] * SLC_BLOCK
    ovt = jnp.asarray((c_start < s_start + SLC_BLOCK) & (s_start < c_start + CMP_BLOCK), BF16)
    ql = np.arange(width)[None, :] % QBLK
    masked = lambda visible: np.where(visible, 0.0, NEG)
    kq = np.arange(KSTEP)[:, None] - ql
    pbias = jnp.asarray(np.stack([masked(kq <= 0),
                                  masked(kq > 0)]), F32)
    band = np.arange(2 * A_WINDOW)[:, None]
    qpart = np.arange(REP * A_WINDOW)[None, :] % A_WINDOW
    in_window = lambda dist: (dist >= 0) & (dist < A_WINDOW)
    abias = jnp.asarray(np.stack([masked(in_window(qpart - band)),
                                  masked(in_window(qpart - (band - A_WINDOW)))]), F32)
    per_blk = QBLK // CMP_STRIDE
    d = np.arange(nc + (nq - 1) * per_blk)[:, None] - (nq - 1) * per_blk
    cbias = jnp.asarray(masked(d * CMP_STRIDE + CMP_BLOCK - 1 <= ql), F32)
    assert nq % ATTN_BLOCKS == 0
    steps = nq // ATTN_BLOCKS
    tiles = ATTN_BLOCKS * QBLK // LANES
    q_spec = pl.BlockSpec((None, tiles, Q_DIM, LANES), lambda b, n: (b, n, 0, 0))
    g_spec = pl.BlockSpec((None, tiles, LANES, LANES), lambda b, n: (b, n, 0, 0))
    kc_spec = pl.BlockSpec((None, KV_HEADS, nc, LANES), lambda b, n: (b, 0, 0, 0))
    vc_spec = pl.BlockSpec((None, KV_HEADS, HEAD_DIM, nc), lambda b, n: (b, 0, 0, 0))
    const = lambda a: pl.BlockSpec(a.shape, lambda b, n: (0,) * a.ndim, pipeline_mode=pl.Buffered(1))
    k_spec = pl.BlockSpec((seq, KV_HEADS * LANES), lambda b, n: (b, 0))
    vt_spec = lambda keys: pl.BlockSpec((None, seq // keys, KV_HEADS * VROWS, keys), lambda b, n: (b, 0, 0, 0))
    o_spec = pl.BlockSpec((ATTN_BLOCKS * QBLK, Q_DIM), lambda b, n: (b * steps + n, 0))
    o_shape = jax.ShapeDtypeStruct((batch * seq, Q_DIM), BF16)
    pipe_scratch = lambda depth: [
        pltpu.VMEM((KV_HEADS, depth, width), BF16),
        pltpu.VMEM((KV_HEADS, KSTEP, width), F32),
        pltpu.VMEM((KV_HEADS, KSTEP, width), BF16),
        pltpu.VMEM((KV_HEADS, VROWS, width), F32)]
    return pl.pallas_call(
        _attn_kernel,
        grid=(batch, steps),
        in_specs=[pl.BlockSpec(memory_space=pltpu.SMEM), const(abias), const(pbias), const(cbias), const(ovt),
                  q_spec, k_spec, vt_spec(A_WINDOW), q_spec, q_spec, g_spec, kc_spec, vc_spec,
                  k_spec, vt_spec(KSTEP), k_spec, vt_spec(KSTEP)],
        out_specs=[o_spec, o_spec],
        out_shape=[o_shape, o_shape],
        scratch_shapes=pipe_scratch(HEAD_DIM) + pipe_scratch(LANES) + [
            pltpu.VMEM((QBLK // A_WINDOW, KV_HEADS, 2 * A_WINDOW, REP * A_WINDOW), F32),
            pltpu.VMEM((QBLK // A_WINDOW, KV_HEADS, 2 * A_WINDOW, REP * A_WINDOW), BF16),
            pltpu.VMEM((KV_HEADS, nc, width), F32)],
        compiler_params=pltpu.CompilerParams(dimension_semantics=("parallel", "arbitrary"),
                                             vmem_limit_bytes=VMEM_WIDE),
        name="attn",
    )(sinks, abias, pbias, cbias, ovt, qa, ka, vat, qn, qnr, gnt, kcmp, vcmpt, ksl, vslt, kw, vwt)


def _layer_norm(r, g, b):
    mu = jnp.mean(r, axis=-1, keepdims=True)
    d = r - mu
    var = jnp.mean(d * d, axis=-1, keepdims=True)
    return d * lax.rsqrt(var + LN_EPS) * g + b


POST_PARTS = 4


def _post_kernel(alpha, x_ref, oa_ref, ob_ref, wgm_ref, wpa_ref, wpb_ref, wout_ref, g_ref, b_ref, h_ref):
    d = x_ref.shape[1]
    rows = [slice(i * (x_ref.shape[0] // POST_PARTS), (i + 1) * (x_ref.shape[0] // POST_PARTS))
            for i in range(POST_PARTS)]
    ys = []
    for r in rows:
        xb = x_ref[r, :].astype(BF16)
        pa = _dot(oa_ref[r, :], wpa_ref[...])
        pb = _dot(ob_ref[r, :], wpb_ref[...])
        ys.append(jax.nn.sigmoid(_dot(xb, wgm_ref[:, :d])) * pa + jax.nn.sigmoid(_dot(xb, wgm_ref[:, d:])) * pb)
    for r, y in zip(rows, ys):
        m = _dot(y.astype(BF16), wout_ref[...])
        h_ref[r, :] = _layer_norm(alpha * x_ref[r, :] + m, g_ref[...], b_ref[...])


def _post(x2, oa, ob, w_gm, wpa, wpb, wout, g, b, alpha, tm):
    n, d = x2.shape
    tok = lambda width: pl.BlockSpec((tm, width), lambda i: (i, 0))
    full = lambda a: pl.BlockSpec(a.shape, lambda i: (0, 0), pipeline_mode=pl.Buffered(1))
    return pl.pallas_call(
        functools.partial(_post_kernel, alpha),
        grid=(n // tm,),
        in_specs=[tok(d), tok(Q_DIM), tok(Q_DIM)] + [full(a) for a in (w_gm, wpa, wpb, wout, g, b)],
        out_specs=tok(d),
        out_shape=jax.ShapeDtypeStruct((n, d), F32),
        compiler_params=pltpu.CompilerParams(dimension_semantics=("parallel",),
                                             vmem_limit_bytes=VMEM_PROJ),
        name="post",
    )(x2, oa, ob, w_gm, wpa, wpb, wout, g, b)


FFN_CHUNK = 256


FFN_PARTS = 2


def _ffn_kernel(alpha, h_ref, wg_ref, wu_ref, wd_ref, g_ref, b_ref, o_ref):
    hidden = wg_ref.shape[1]
    rows = h_ref.shape[0] // FFN_PARTS
    for p in range(FFN_PARTS):
        r = slice(p * rows, (p + 1) * rows)
        h = h_ref[r, :]
        hb = h.astype(BF16)
        acc = jnp.zeros(h.shape, F32)
        for c in range(hidden // FFN_CHUNK):
            sl = slice(c * FFN_CHUNK, (c + 1) * FFN_CHUNK)
            a = jax.nn.silu(_dot(hb, wg_ref[:, sl])) * _dot(hb, wu_ref[:, sl])
            acc = acc + _dot(a.astype(BF16), wd_ref[sl, :])
        o_ref[r, :] = _layer_norm(alpha * h + acc, g_ref[...], b_ref[...])


def _ffn(h, wg, wu, wd, g, b, alpha, tm):
    n, d = h.shape
    tok = pl.BlockSpec((tm, d), lambda i: (i, 0))
    once = lambda a: pl.BlockSpec(a.shape, lambda i: (0, 0), pipeline_mode=pl.Buffered(1))
    return pl.pallas_call(
        functools.partial(_ffn_kernel, alpha),
        grid=(n // tm,),
        in_specs=[tok] + [once(a) for a in (wg, wu, wd, g, b)],
        out_specs=tok,
        out_shape=jax.ShapeDtypeStruct((n, d), F32),
        compiler_params=pltpu.CompilerParams(dimension_semantics=("parallel",),
                                             vmem_limit_bytes=VMEM_WIDE),
        name="ffn",
    )(h, wg, wu, wd, g, b)


def _position_tables(seq):
    half = HEAD_DIM // 2
    inv = ROPE_THETA ** (-jnp.arange(half, dtype=F32) / half)
    ang = jnp.arange(seq).astype(F32)[:, None] * inv[None, :]
    cos, sin = jnp.cos(ang), jnp.sin(ang)
    reps = LANES // HEAD_DIM
    lane = np.arange(LANES)[None, :]
    blk = (np.arange(seq) // SLC_BLOCK)[:, None]
    ke = jnp.asarray(np.where(lane - MASK_COL == blk, -NEG, 0.0), F32)
    return jnp.tile(cos, (1, 2 * reps)), jnp.tile(jnp.concatenate([-sin, sin], axis=1), (1, reps)), ke


def kernel(x, w_in, sinks, cmp_pe_k, cmp_w1_k, cmp_b1_k, cmp_w2_k, cmp_pe_v, cmp_w1_v, cmp_b1_v, cmp_w2_v,
           w_proj_a, w_proj_b, w_out, ln1_g, ln1_b, w_gate, w_up, w_down, ln2_g, ln2_b):
    batch, seq, d = x.shape
    depth = w_in.shape[0]
    alpha = (2 * depth) ** 0.25
    n_main = 2 * Q_DIM + 8 * KV_DIM
    n_gate = 3 * Q_HEADS
    tm = ROW_TILE
    cos, sin, ke = _position_tables(seq)
    pad_cols = lambda w: jnp.pad(w, ((0, 0), (0, LANES - w.shape[1]))).astype(BF16)
    xt = x.reshape(batch * seq, d)
    for l in range(depth):
        w_main = w_in[l, :, :n_main].astype(BF16)
        w_gn = pad_cols(w_in[l, :, n_main:n_main + n_gate])
        w_gm = w_in[l, :, n_main + n_gate:].astype(BF16)
        (qa, ka, vat, qn, qnr, kc, vc, ksl, vslt, kw, vwt, gnt) = _in_proj(
            xt, w_main, w_gn, cos, sin, ke, batch, seq, tm)
        wk = (cmp_pe_k[l], cmp_w1_k[l].astype(BF16), cmp_b1_k[l][None, :], pad_cols(cmp_w2_k[l]))
        wv = (cmp_pe_v[l], cmp_w1_v[l].astype(BF16), cmp_b1_v[l][None, :], pad_cols(cmp_w2_v[l]))
        kcmp, vcmpt = _compress(kc, vc, wk, wv, batch, seq)
        oa, ob = _attn(sinks[l], qa, ka, vat, qn, qnr, gnt, kcmp, vcmpt, ksl, vslt, kw, vwt, batch, seq)
        h = _post(xt, oa, ob, w_gm, w_proj_a[l].astype(BF16), w_proj_b[l].astype(BF16), w_out[l].astype(BF16),
                  ln1_g[l][None, :], ln1_b[l][None, :], alpha, tm)
        xt = _ffn(h, w_gate[l].astype(BF16), w_up[l].astype(BF16), w_down[l].astype(BF16),
                  ln2_g[l][None, :], ln2_b[l][None, :], alpha, tm)
    return xt.reshape(batch, seq, d)
```

```python
import functools

import jax
import jax.numpy as jnp
import numpy as np
from jax import lax
from jax.experimental import pallas as pl
from jax.experimental.pallas import tpu as pltpu

HEAD_DIM = 64
ROPE_THETA = 10000.0
Q_HEADS = 8
KV_HEADS = 2
REP = Q_HEADS // KV_HEADS
A_WINDOW = 128
B_WINDOW = 512
CMP_BLOCK = 32
CMP_STRIDE = 16
SLC_BLOCK = 64
SLC_TOP_N = 16
LN_EPS = 1e-5
NEG = -1e30
BIG = 1e9
Q_DIM = Q_HEADS * HEAD_DIM
KV_DIM = KV_HEADS * HEAD_DIM
LANES = 128
QBLK = 256
KSTEP = QBLK
LOG2E = 1.4426950408889634
SCALE = HEAD_DIM ** -0.5 * LOG2E
VROWS = HEAD_DIM + 16
ROW_TILE = 1024
MIB = 1024 * 1024
VMEM_PROJ = 48 * MIB
VMEM_WIDE = 56 * MIB

F32 = jnp.float32
BF16 = jnp.bfloat16


def _dot(a, b):
    return jnp.dot(a, b, preferred_element_type=F32)


def _inproj_kernel(x_ref, w_ref, wg_ref, cos_ref, sin_ref, cost_ref, sint_ref, ke_ref,
                   qa_ref, ka_ref, vat_ref, qn_ref, qnr_ref, kc_ref, vc_ref,
                   ksl_ref, vslt_ref, kw_ref, vwt_ref, gnt_ref):
    tm = x_ref.shape[0]
    xb = x_ref[...].astype(BF16)
    cos = cos_ref[...]
    sin = sin_ref[...]
    lane = lax.broadcasted_iota(jnp.int32, (tm, LANES), 1)
    first_half = (lane & (HEAD_DIM - 1)) < (HEAD_DIM // 2)
    low = lane < HEAD_DIM

    def rope(z):
        sw = jnp.where(first_half, pltpu.roll(z, LANES - HEAD_DIM // 2, 1), pltpu.roll(z, HEAD_DIM // 2, 1))
        return z * cos + sw * sin

    def proj(c0, n):
        return _dot(xb, w_ref[:, c0:c0 + n])

    def chunks(z):
        return [z[:, c * LANES:(c + 1) * LANES] for c in range(z.shape[1] // LANES)]

    def put_padded(ref, c, z, fill):
        ref[:, (2 * c) * LANES:(2 * c + 1) * LANES] = jnp.where(low, z, fill).astype(ref.dtype)
        ref[:, (2 * c + 1) * LANES:(2 * c + 2) * LANES] = jnp.where(low, pltpu.roll(z, HEAD_DIM, 1), fill).astype(ref.dtype)

    def put_transposed(ref, z):
        for j in range(tm // LANES):
            ref[j] = z[j * LANES:(j + 1) * LANES, :].T.astype(ref.dtype)

    def put_heads_t(c, z, plain_ref, rope_ref):
        half = HEAD_DIM // 2
        for j in range(tm // LANES):
            zt = z[j * LANES:(j + 1) * LANES, :].T * SCALE
            rows = slice(c * LANES, (c + 1) * LANES)
            if plain_ref is not None:
                plain_ref[j, rows, :] = zt.astype(plain_ref.dtype)
            sw = jnp.concatenate([zt[(s ^ 1) * half:((s ^ 1) + 1) * half] for s in range(LANES // half)], axis=0)
            rope_ref[j, rows, :] = (zt * cost_ref[j] + sw * sint_ref[j]).astype(rope_ref.dtype)

    ones_row = jnp.where(lax.broadcasted_iota(jnp.int32, (VROWS - HEAD_DIM, LANES), 0) == 0, 1.0, 0.0)

    def put_values_t(ref, z):
        keys = ref.shape[2]
        for j in range(tm // keys):
            for i in range(keys // LANES):
                zt = z[j * keys + i * LANES:j * keys + (i + 1) * LANES, :].T
                cols = slice(i * LANES, (i + 1) * LANES)
                for g in range(KV_HEADS):
                    ref[j, g * VROWS:g * VROWS + HEAD_DIM, cols] = zt[g * HEAD_DIM:(g + 1) * HEAD_DIM, :].astype(ref.dtype)
                    ref[j, g * VROWS + HEAD_DIM:(g + 1) * VROWS, cols] = ones_row.astype(ref.dtype)

    for c, zc in enumerate(chunks(proj(0, Q_DIM))):
        put_heads_t(c, zc, None, qa_ref)
    z = proj(Q_DIM, 2 * KV_DIM)
    put_padded(ka_ref, 0, rope(z[:, :KV_DIM]), 0.0)
    put_values_t(vat_ref, z[:, KV_DIM:])
    c0 = Q_DIM + 2 * KV_DIM
    for c, zc in enumerate(chunks(proj(c0, Q_DIM))):
        put_heads_t(c, zc, qn_ref, qnr_ref)
    c0 += Q_DIM
    z = proj(c0, 2 * KV_DIM)
    kc_ref[...] = z[:, :KV_DIM]
    vc_ref[...] = z[:, KV_DIM:]
    c0 += 2 * KV_DIM
    z = proj(c0, 2 * KV_DIM)
    put_padded(ksl_ref, 0, rope(z[:, :KV_DIM]), ke_ref[...])
    put_values_t(vslt_ref, z[:, KV_DIM:])
    c0 += 2 * KV_DIM
    z = proj(c0, 2 * KV_DIM)
    put_padded(kw_ref, 0, rope(z[:, :KV_DIM]), 0.0)
    put_values_t(vwt_ref, z[:, KV_DIM:])
    put_transposed(gnt_ref, jax.nn.sigmoid(_dot(xb, wg_ref[...])))


def _in_proj(x2, w_main, w_gn, cos, sin, ke, batch, seq, tm):
    n = x2.shape[0]
    d = x2.shape[1]
    spt = seq // tm
    tok = lambda width: pl.BlockSpec((tm, width), lambda i: (i, 0))
    full = lambda a: pl.BlockSpec(a.shape, lambda i: (0, 0))
    tab = pl.BlockSpec((tm, LANES), lambda i: (i % spt, 0))
    tab_t = pl.BlockSpec((tm // LANES, LANES, LANES), lambda i: (i % spt, 0, 0))
    cos_t, sin_t = (jnp.transpose(t.reshape(seq // LANES, LANES, LANES), (0, 2, 1)) for t in (cos, sin))
    flat = lambda w, dt: (tok(w), jax.ShapeDtypeStruct((n, w), dt))
    trans = lambda r, keys, dt: (pl.BlockSpec((None, tm // keys, r, keys), lambda i: (i // spt, i % spt, 0, 0)),
                                 jax.ShapeDtypeStruct((batch, seq // keys, r, keys), dt))
    outs = [trans(Q_DIM, LANES, BF16), flat(KV_HEADS * LANES, BF16),
            trans(KV_HEADS * VROWS, A_WINDOW, BF16),
            trans(Q_DIM, LANES, BF16), trans(Q_DIM, LANES, BF16),
            flat(KV_DIM, F32), flat(KV_DIM, F32),
            flat(KV_HEADS * LANES, BF16), trans(KV_HEADS * VROWS, KSTEP, BF16),
            flat(KV_HEADS * LANES, BF16), trans(KV_HEADS * VROWS, KSTEP, BF16),
            trans(LANES, LANES, F32)]
    return pl.pallas_call(
        _inproj_kernel,
        grid=(n // tm,),
        in_specs=[tok(d), full(w_main), full(w_gn), tab, tab, tab_t, tab_t, tab],
        out_specs=[o[0] for o in outs],
        out_shape=[o[1] for o in outs],
        compiler_params=pltpu.CompilerParams(dimension_semantics=("parallel",),
                                             vmem_limit_bytes=VMEM_PROJ),
        name="in_proj",
    )(x2, w_main, w_gn, cos, sin, cos_t, sin_t, ke)


CMP_PAIR = 2


def _compress_one(src_ref, pe_ref, w1_ref, b1_ref, w2_ref):
    ratio = CMP_BLOCK // CMP_STRIDE
    nchunk = src_ref.shape[0] // CMP_STRIDE
    both = lambda row: jnp.concatenate([row] * KV_HEADS, axis=1)
    parts = [jnp.zeros((nchunk, w1_ref.shape[3]), F32) for _ in range(ratio)]
    for p in range(CMP_STRIDE // CMP_PAIR):
        rows = [src_ref[pl.ds(p * CMP_PAIR + i, nchunk, stride=CMP_STRIDE), :] for i in range(CMP_PAIR)]
        for j in range(ratio):
            pos = j * CMP_STRIDE + p * CMP_PAIR
            a = jnp.concatenate([(rows[i] + both(pe_ref[pos + i:pos + i + 1, :])).astype(BF16)
                                 for i in range(CMP_PAIR)], axis=1)
            parts[j] = parts[j] + _dot(a, w1_ref[j, p])
    h = parts[0]
    for j in range(1, ratio):
        h = h + pltpu.roll(parts[j], nchunk - j, 0)
    h = jax.nn.gelu(h + both(b1_ref[...]))
    return _dot(h.astype(BF16), w2_ref[...])


def _group_diagonal(w1_ref, w2_ref, w1d_ref, w2d_ref):
    hid = w1_ref.shape[1]
    w1d_ref[...] = jnp.zeros(w1d_ref.shape, w1d_ref.dtype)
    w2d_ref[...] = jnp.zeros(w2d_ref.shape, w2d_ref.dtype)
    for g in range(KV_HEADS):
        w2d_ref[g * hid:(g + 1) * hid, g * LANES:(g + 1) * LANES] = w2_ref[...]
        for j in range(CMP_BLOCK // CMP_STRIDE):
            for p in range(CMP_STRIDE // CMP_PAIR):
                for i in range(CMP_PAIR):
                    pos = j * CMP_STRIDE + p * CMP_PAIR + i
                    row = i * KV_DIM + g * HEAD_DIM
                    w1d_ref[j, p, row:row + HEAD_DIM, g * hid:(g + 1) * hid] = w1_ref[pos * HEAD_DIM:(pos + 1) * HEAD_DIM, :]


def _compress_kernel(kc_ref, vc_ref, pek_ref, w1k_ref, b1k_ref, w2k_ref,
                     pev_ref, w1v_ref, b1v_ref, w2v_ref, kcmp_ref, vcmpt_ref,
                     w1dk_ref, w2dk_ref, w1dv_ref, w2dv_ref):
    @pl.when(pl.program_id(0) == 0)
    def _():
        _group_diagonal(w1k_ref, w2k_ref, w1dk_ref, w2dk_ref)
        _group_diagonal(w1v_ref, w2v_ref, w1dv_ref, w2dv_ref)

    kc = _compress_one(kc_ref, pek_ref, w1dk_ref, b1k_ref, w2dk_ref)
    vc = _compress_one(vc_ref, pev_ref, w1dv_ref, b1v_ref, w2dv_ref)
    for g in range(KV_HEADS):
        kcmp_ref[g] = kc[:, g * LANES:(g + 1) * LANES].astype(BF16)
        vcmpt_ref[g] = vc[:, g * LANES:(g + 1) * LANES].T[:HEAD_DIM, :].astype(BF16)


def _compress(kc, vc, wk, wv, batch, seq):
    nchunk = seq // CMP_STRIDE
    hid = wk[1].shape[1]
    src = pl.BlockSpec((seq, KV_DIM), lambda b: (b, 0))
    full = lambda a: pl.BlockSpec(a.shape, lambda b: (0,) * a.ndim)
    return pl.pallas_call(
        _compress_kernel,
        grid=(batch,),
        in_specs=[src, src] + [full(a) for a in wk] + [full(a) for a in wv],
        out_specs=[pl.BlockSpec((None, KV_HEADS, nchunk, LANES), lambda b: (b, 0, 0, 0)),
                   pl.BlockSpec((None, KV_HEADS, HEAD_DIM, nchunk), lambda b: (b, 0, 0, 0))],
        out_shape=[jax.ShapeDtypeStruct((batch, KV_HEADS, nchunk, LANES), BF16),
                   jax.ShapeDtypeStruct((batch, KV_HEADS, HEAD_DIM, nchunk), BF16)],
        scratch_shapes=[pltpu.VMEM((CMP_BLOCK // CMP_STRIDE, CMP_STRIDE // CMP_PAIR, CMP_PAIR * KV_DIM, KV_HEADS * hid), BF16),
                        pltpu.VMEM((KV_HEADS * hid, KV_HEADS * LANES), BF16)] * 2,
        compiler_params=pltpu.CompilerParams(dimension_semantics=("arbitrary",)),
        name="compress",
    )(kc, vc, *wk, *wv)


MASK_COL = HEAD_DIM


def _select_blocks_t(imp, t0):
    nb, cols = imp.shape
    j = lax.broadcasted_iota(jnp.int32, (nb, cols), 0)
    cur = (t0 + lax.broadcasted_iota(jnp.int32, (nb, cols), 1)) >> int(np.log2(SLC_BLOCK))
    forced = (j == 0) | (j == cur) | (j == cur - 1)
    score = jnp.where(forced, BIG, jnp.where(j <= cur, imp, -BIG))
    sub = 8
    tiles = [score[v * sub:(v + 1) * sub] for v in range(nb // sub)]
    ranks = [jnp.zeros((sub, cols), F32) for _ in tiles]
    jl = lax.broadcasted_iota(jnp.int32, (sub, cols), 0)
    for jp in range(nb):
        row = score[jp:jp + 1, :]
        for v, tile in enumerate(tiles):
            if v * sub > jp:
                beats = row >= tile
            elif (v + 1) * sub - 1 <= jp:
                beats = row > tile
            else:
                beats = (row > tile) | ((row == tile) & (jl > jp - v * sub))
            ranks[v] = jnp.where(beats, ranks[v] + 1.0, ranks[v])
    rank = jnp.concatenate(ranks, axis=0)
    return jnp.where((rank < float(min(SLC_TOP_N, nb))) & (j <= cur), 1.0, 0.0)


def _put_heads(o_ref, g, o, nq=QBLK):
    for i in range(REP // 2):
        pair = jnp.concatenate([o[:, (2 * i) * nq:(2 * i + 1) * nq],
                                o[:, (2 * i + 1) * nq:(2 * i + 2) * nq]], axis=0)
        col = (g * REP + 2 * i) * HEAD_DIM
        o_ref[:, col:col + LANES] = pair.T.astype(o_ref.dtype)


class _Pipeline:
    def __init__(self, k_ref, vt_ref, q_ref, s_ref, p_ref, acc_ref, chunk_of):
        self.k_ref, self.vt_ref, self.q_ref = k_ref, vt_ref, q_ref
        self.s_ref, self.p_ref, self.acc_ref = s_ref, p_ref, acc_ref
        self.chunk_of = chunk_of
        self.groups = range(KV_HEADS)

    def qk(self, g, j):
        start = pl.multiple_of(self.chunk_of(j) * KSTEP, KSTEP)
        depth = self.q_ref.shape[1]
        return _dot(self.k_ref[pl.ds(start, KSTEP), g * LANES:g * LANES + depth], self.q_ref[g])

    def keep(self, g, st, bias):
        if bias is not None:
            st = st + bias
        self.s_ref[g] = st
        return jnp.max(st, axis=0, keepdims=True)

    def softmax(self, g, m, cmax):
        m_new = jnp.maximum(m, cmax)
        a = jnp.exp2(m - m_new)
        self.p_ref[g] = jnp.exp2(self.s_ref[g] - m_new).astype(BF16)
        return m_new, a

    def values(self, g, j, a):
        vt = self.vt_ref[self.chunk_of(j), g * VROWS:(g + 1) * VROWS, :]
        self.acc_ref[g] = a * self.acc_ref[g] + _dot(vt, self.p_ref[g])

    def start(self, bias0, bias1):
        m0 = jnp.full((1, self.s_ref.shape[2]), NEG, F32)
        cmax = [self.keep(g, self.qk(g, 0), bias0) for g in self.groups]
        st1 = [self.qk(g, 1) for g in self.groups]
        sm = [self.softmax(g, m0, cmax[g]) for g in self.groups]
        cmax = tuple(self.keep(g, st1[g], bias1) for g in self.groups)
        for g in self.groups:
            self.acc_ref[g] = jnp.zeros(self.acc_ref.shape[1:], F32)
        return (tuple(s[0] for s in sm), tuple(s[1] for s in sm), cmax)

    def step(self, j, state, last=False, bias=None):
        m, a, cmax = state
        st = None if last else [self.qk(g, j + 2) for g in self.groups]
        for g in self.groups:
            self.values(g, j, a[g])
        sm = [self.softmax(g, m[g], cmax[g]) for g in self.groups]
        if not last:
            cmax = tuple(self.keep(g, st[g], bias) for g in self.groups)
        return (tuple(s[0] for s in sm), tuple(s[1] for s in sm), cmax)

    def finish(self, count, state):
        _, a, _ = state
        out = []
        for g in self.groups:
            self.values(g, count - 1, a[g])
            acc = self.acc_ref[g]
            out.append(acc[:HEAD_DIM] * (1.0 / acc[HEAD_DIM:HEAD_DIM + 1]))
        return out


ATTN_BLOCKS = 4


def _attn_kernel(sink_ref, abias_ref, pbias_ref, cbias_ref, ovt_ref,
                 qa_ref, ka_ref, vat_ref, qn_ref, qnr_ref, gnt_ref, kcmp_ref, vcmpt_ref,
                 ksl_ref, vslt_ref, kw_ref, vwt_ref, oa_ref, ob_ref, *scratch):
    tiles = QBLK // LANES
    for i in range(ATTN_BLOCKS):
        tok = lambda ref: ref.at[pl.ds(i * tiles, tiles)]
        out = lambda ref: ref.at[pl.ds(i * QBLK, QBLK), :]
        _attn_block(pl.program_id(1) * ATTN_BLOCKS + i, sink_ref, abias_ref, pbias_ref, cbias_ref, ovt_ref,
                    tok(qa_ref), ka_ref, vat_ref, tok(qn_ref), tok(qnr_ref), tok(gnt_ref), kcmp_ref, vcmpt_ref,
                    ksl_ref, vslt_ref, kw_ref, vwt_ref, out(oa_ref), out(ob_ref), *scratch)


def _attn_block(n, sink_ref, abias_ref, pbias_ref, cbias_ref, ovt_ref,
                qa_ref, ka_ref, vat_ref, qn_ref, qnr_ref, gnt_ref, kcmp_ref, vcmpt_ref,
                ksl_ref, vslt_ref, kw_ref, vwt_ref,
                oa_ref, ob_ref,
                qw_ref, sw_ref, pw_ref, accw_ref, qs_ref, ss_ref, ps_ref, accs_ref, sa_ref, pa_ref, sc_ref):
    groups = range(KV_HEADS)
    width = REP * QBLK
    nb = ovt_ref.shape[0]
    nc = kcmp_ref.shape[1]
    t0 = pl.multiple_of(n * QBLK, QBLK)
    heads = lambda g: [g * REP + r for r in range(REP)]
    parts = range(QBLK // LANES)
    head_t = lambda ref, h, i: ref[i, h * HEAD_DIM:(h + 1) * HEAD_DIM, :]
    cols = lambda ref, g: jnp.concatenate([head_t(ref, h, i) for h in heads(g) for i in parts], axis=1)
    lane_chunks = lambda a: [a[:, r * QBLK:(r + 1) * QBLK] for r in range(REP)]
    gone = lambda cond: jnp.where(cond, 0.0, NEG)

    assert B_WINDOW == 2 * KSTEP
    win = _Pipeline(kw_ref, vwt_ref, qw_ref, sw_ref, pw_ref, accw_ref, lambda j: jnp.maximum(n - j, 0))
    slc = _Pipeline(ksl_ref, vslt_ref, qs_ref, ss_ref, ps_ref, accs_ref,
                    lambda j: jnp.where(j == 0, n, jnp.maximum(j - 1, 0)))
    slc_count = jnp.maximum(n, 1) + 1

    for g in groups:
        qw_ref[g] = cols(qnr_ref, g)
    a_chunk = [n * len(parts) + i for i in parts]
    a_first = [jnp.maximum(c - 1, 0) for c in a_chunk]
    part_cols = lambda i, g: jnp.concatenate([head_t(qa_ref, h, i) for h in heads(g)], axis=1)
    amax = {}
    for i in parts:
        abias = abias_ref[jnp.minimum(a_chunk[i], 1)]
        astart = pl.multiple_of(a_first[i] * A_WINDOW, A_WINDOW)
        for g in groups:
            st = _dot(ka_ref[pl.ds(astart, 2 * A_WINDOW), g * LANES:g * LANES + HEAD_DIM], part_cols(i, g)) + abias
            sa_ref[i, g] = st
            amax[i, g] = jnp.max(st, axis=0, keepdims=True)
    cstart = pl.multiple_of(cbias_ref.shape[0] - nc - n * (QBLK // CMP_STRIDE), 8)
    cbias = cbias_ref[pl.ds(cstart, nc), :]
    cmax = []
    for g in groups:
        s = _dot(kcmp_ref[g, :, :HEAD_DIM], cols(qn_ref, g)) + cbias
        sc_ref[g] = s
        cmax.append(jnp.max(s, axis=0, keepdims=True))
    win_state = win.start(pbias_ref[0], gone(n >= 1))

    x_swa = {}
    for i in parts:
        for g in groups:
            sk = jnp.concatenate([jnp.full((1, A_WINDOW), sink_ref[h] * LOG2E, F32) for h in heads(g)], axis=1)
            m = jnp.maximum(amax[i, g], sk)
            pa_ref[i, g] = jnp.exp2(sa_ref[i, g] - m).astype(BF16)
            x_swa[i, g] = jnp.exp2(sk - m)
    tq = t0 + (lax.broadcasted_iota(jnp.int32, (1, width), 1) & (QBLK - 1))
    sees_any = tq >= CMP_BLOCK - 1
    p_cmp = []
    for g in groups:
        e = jnp.exp2(sc_ref[g] - cmax[g])
        inv = jnp.where(sees_any, 1.0 / jnp.maximum(jnp.sum(e, axis=0, keepdims=True), 1e-30), 0.0)
        p_cmp.append(e * inv)

    win_state = win.step(0, win_state, bias=pbias_ref[1] + gone(n >= 2))

    for i in parts:
        for g in groups:
            vt = jnp.concatenate([vat_ref[a_first[i] + k, g * VROWS:(g + 1) * VROWS, :] for k in range(2)], axis=1)
            o = _dot(vt, pa_ref[i, g])
            o = o[:HEAD_DIM] * (1.0 / (o[HEAD_DIM:HEAD_DIM + 1] + x_swa[i, g]))
            _put_heads(oa_ref.at[i * A_WINDOW:(i + 1) * A_WINDOW, :], g, o, A_WINDOW)
    o_cmp = [_dot(vcmpt_ref[g], p_cmp[g].astype(BF16)) for g in groups]
    imp = []
    for g in groups:
        pc = lane_chunks(p_cmp[g])
        psum = (pc[0] + pc[1]) + (pc[2] + pc[3])
        hi = psum.astype(BF16)
        lo = (psum - hi.astype(F32)).astype(BF16)
        imp.append(_dot(ovt_ref[...], hi) + _dot(ovt_ref[...], lo))
    o_win = win.finish(3, win.step(1, win_state, last=True))

    jb = lax.broadcasted_iota(jnp.int32, (nb, QBLK), 0)
    curb = (t0 + lax.broadcasted_iota(jnp.int32, (nb, QBLK), 1)) >> int(np.log2(SLC_BLOCK))
    sel = lax.cond((t0 + QBLK - 1) // SLC_BLOCK < SLC_TOP_N,
                   lambda: tuple(jnp.where(jb <= curb, 1.0, 0.0) for _ in groups),
                   lambda: tuple(_select_blocks_t(imp[g], t0) for g in groups))

    for g in groups:
        qs_ref[g, :HEAD_DIM, :] = cols(qnr_ref, g)
        mrows = jnp.concatenate([sel[g] - 1.0, jnp.zeros((LANES - MASK_COL - nb, QBLK), F32)], axis=0).astype(BF16)
        for r in range(REP):
            qs_ref[g, MASK_COL:, r * QBLK:(r + 1) * QBLK] = mrows
    slc_state = slc.start(pbias_ref[0], gone(n >= 1))

    no_bias = gone(n >= 0)
    one_step = lambda j, s: slc.step(j, s, bias=no_bias)
    trips = slc_count - 2
    slc_state = lax.fori_loop(0, trips // 2, lambda i, s: one_step(2 * i + 1, one_step(2 * i, s)), slc_state)
    slc_state = lax.cond(trips % 2 == 1, lambda s: one_step(trips - 1, s), lambda s: s, slc_state)

    o_slc = slc.finish(slc_count, slc.step(slc_count - 2, slc_state, last=True))
    for g in groups:
        gate = lambda br: jnp.concatenate(
            [gnt_ref[i, br * Q_HEADS + h:br * Q_HEADS + h + 1, :]
             for h in heads(g) for i in range(gnt_ref.shape[0])], axis=1)
        _put_heads(ob_ref, g, gate(0) * o_cmp[g] + gate(1) * o_slc[g] + gate(2) * o_win[g])


def _attn(sinks, qa, ka, vat, qn, qnr, gnt, kcmp, vcmpt, ksl, vslt, kw, vwt, batch, seq):
    nq = seq // QBLK
    nc = kcmp.shape[2]
    nb = seq // SLC_BLOCK
    width = REP * QBLK
    assert nb <= LANES - MASK_COL and QBLK % SLC_BLOCK == 0 and QBLK % A_WINDOW == 0 and A_WINDOW % LANES == 0
    c_start = np.arange(nc)[None, :] * CMP_STRIDE
    s_start = np.arange(nb)[:, None] * SLC_BLOCK
    ovt = jnp.asarray((c_start < s_start + SLC_BLOCK) & (s_start < c_start + CMP_BLOCK), BF16)
    ql = np.arange(width)[None, :] % QBLK
    masked = lambda visible: np.where(visible, 0.0, NEG)
    kq = np.arange(KSTEP)[:, None] - ql
    pbias = jnp.asarray(np.stack([masked(kq <= 0),
                                  masked(kq > 0)]), F32)
    band = np.arange(2 * A_WINDOW)[:, None]
    qpart = np.arange(REP * A_WINDOW)[None, :] % A_WINDOW
    in_window = lambda dist: (dist >= 0) & (dist < A_WINDOW)
    abias = jnp.asarray(np.stack([masked(in_window(qpart - band)),
                                  masked(in_window(qpart - (band - A_WINDOW)))]), F32)
    per_blk = QBLK // CMP_STRIDE
    d = np.arange(nc + (nq - 1) * per_blk)[:, None] - (nq - 1) * per_blk
    cbias = jnp.asarray(masked(d * CMP_STRIDE + CMP_BLOCK - 1 <= ql), F32)
    assert nq % ATTN_BLOCKS == 0
    steps = nq // ATTN_BLOCKS
    tiles = ATTN_BLOCKS * QBLK // LANES
    q_spec = pl.BlockSpec((None, tiles, Q_DIM, LANES), lambda b, n: (b, n, 0, 0))
    g_spec = pl.BlockSpec((None, tiles, LANES, LANES), lambda b, n: (b, n, 0, 0))
    kc_spec = pl.BlockSpec((None, KV_HEADS, nc, LANES), lambda b, n: (b, 0, 0, 0))
    vc_spec = pl.BlockSpec((None, KV_HEADS, HEAD_DIM, nc), lambda b, n: (b, 0, 0, 0))
    const = lambda a: pl.BlockSpec(a.shape, lambda b, n: (0,) * a.ndim, pipeline_mode=pl.Buffered(1))
    k_spec = pl.BlockSpec((seq, KV_HEADS * LANES), lambda b, n: (b, 0))
    vt_spec = lambda keys: pl.BlockSpec((None, seq // keys, KV_HEADS * VROWS, keys), lambda b, n: (b, 0, 0, 0))
    o_spec = pl.BlockSpec((ATTN_BLOCKS * QBLK, Q_DIM), lambda b, n: (b * steps + n, 0))
    o_shape = jax.ShapeDtypeStruct((batch * seq, Q_DIM), BF16)
    pipe_scratch = lambda depth: [
        pltpu.VMEM((KV_HEADS, depth, width), BF16),
        pltpu.VMEM((KV_HEADS, KSTEP, width), F32),
        pltpu.VMEM((KV_HEADS, KSTEP, width), BF16),
        pltpu.VMEM((KV_HEADS, VROWS, width), F32)]
    return pl.pallas_call(
        _attn_kernel,
        grid=(batch, steps),
        in_specs=[pl.BlockSpec(memory_space=pltpu.SMEM), const(abias), const(pbias), const(cbias), const(ovt),
                  q_spec, k_spec, vt_spec(A_WINDOW), q_spec, q_spec, g_spec, kc_spec, vc_spec,
                  k_spec, vt_spec(KSTEP), k_spec, vt_spec(KSTEP)],
        out_specs=[o_spec, o_spec],
        out_shape=[o_shape, o_shape],
        scratch_shapes=pipe_scratch(HEAD_DIM) + pipe_scratch(LANES) + [
            pltpu.VMEM((QBLK // A_WINDOW, KV_HEADS, 2 * A_WINDOW, REP * A_WINDOW), F32),
            pltpu.VMEM((QBLK // A_WINDOW, KV_HEADS, 2 * A_WINDOW, REP * A_WINDOW), BF16),
            pltpu.VMEM((KV_HEADS, nc, width), F32)],
        compiler_params=pltpu.CompilerParams(dimension_semantics=("parallel", "arbitrary"),
                                             vmem_limit_bytes=VMEM_WIDE),
        name="attn",
    )(sinks, abias, pbias, cbias, ovt, qa, ka, vat, qn, qnr, gnt, kcmp, vcmpt, ksl, vslt, kw, vwt)


def _layer_norm(r, g, b):
    mu = jnp.mean(r, axis=-1, keepdims=True)
    d = r - mu
    var = jnp.mean(d * d, axis=-1, keepdims=True)
    return d * lax.rsqrt(var + LN_EPS) * g + b


POST_PARTS = 4


def _post_kernel(alpha, x_ref, oa_ref, ob_ref, wgm_ref, wpa_ref, wpb_ref, wout_ref, g_ref, b_ref, h_ref):
    d = x_ref.shape[1]
    rows = [slice(i * (x_ref.shape[0] // POST_PARTS), (i + 1) * (x_ref.shape[0] // POST_PARTS))
            for i in range(POST_PARTS)]
    ys = []
    for r in rows:
        xb = x_ref[r, :].astype(BF16)
        pa = _dot(oa_ref[r, :], wpa_ref[...])
        pb = _dot(ob_ref[r, :], wpb_ref[...])
        ys.append(jax.nn.sigmoid(_dot(xb, wgm_ref[:, :d])) * pa + jax.nn.sigmoid(_dot(xb, wgm_ref[:, d:])) * pb)
    for r, y in zip(rows, ys):
        m = _dot(y.astype(BF16), wout_ref[...])
        h_ref[r, :] = _layer_norm(alpha * x_ref[r, :] + m, g_ref[...], b_ref[...])


def _post(x2, oa, ob, w_gm, wpa, wpb, wout, g, b, alpha, tm):
    n, d = x2.shape
    tok = lambda width: pl.BlockSpec((tm, width), lambda i: (i, 0))
    full = lambda a: pl.BlockSpec(a.shape, lambda i: (0, 0), pipeline_mode=pl.Buffered(1))
    return pl.pallas_call(
        functools.partial(_post_kernel, alpha),
        grid=(n // tm,),
        in_specs=[tok(d), tok(Q_DIM), tok(Q_DIM)] + [full(a) for a in (w_gm, wpa, wpb, wout, g, b)],
        out_specs=tok(d),
        out_shape=jax.ShapeDtypeStruct((n, d), F32),
        compiler_params=pltpu.CompilerParams(dimension_semantics=("parallel",),
                                             vmem_limit_bytes=VMEM_PROJ),
        name="post",
    )(x2, oa, ob, w_gm, wpa, wpb, wout, g, b)


FFN_CHUNK = 256


FFN_PARTS = 2


def _ffn_kernel(alpha, h_ref, wg_ref, wu_ref, wd_ref, g_ref, b_ref, o_ref):
    hidden = wg_ref.shape[1]
    rows = h_ref.shape[0] // FFN_PARTS
    for p in range(FFN_PARTS):
        r = slice(p * rows, (p + 1) * rows)
        h = h_ref[r, :]
        hb = h.astype(BF16)
        acc = jnp.zeros(h.shape, F32)
        for c in range(hidden // FFN_CHUNK):
            sl = slice(c * FFN_CHUNK, (c + 1) * FFN_CHUNK)
            a = jax.nn.silu(_dot(hb, wg_ref[:, sl])) * _dot(hb, wu_ref[:, sl])
            acc = acc + _dot(a.astype(BF16), wd_ref[sl, :])
        o_ref[r, :] = _layer_norm(alpha * h + acc, g_ref[...], b_ref[...])


def _ffn(h, wg, wu, wd, g, b, alpha, tm):
    n, d = h.shape
    tok = pl.BlockSpec((tm, d), lambda i: (i, 0))
    once = lambda a: pl.BlockSpec(a.shape, lambda i: (0, 0), pipeline_mode=pl.Buffered(1))
    return pl.pallas_call(
        functools.partial(_ffn_kernel, alpha),
        grid=(n // tm,),
        in_specs=[tok] + [once(a) for a in (wg, wu, wd, g, b)],
        out_specs=tok,
        out_shape=jax.ShapeDtypeStruct((n, d), F32),
        compiler_params=pltpu.CompilerParams(dimension_semantics=("parallel",),
                                             vmem_limit_bytes=VMEM_WIDE),
        name="ffn",
    )(h, wg, wu, wd, g, b)


def _position_tables(seq):
    half = HEAD_DIM // 2
    inv = ROPE_THETA ** (-jnp.arange(half, dtype=F32) / half)
    ang = jnp.arange(seq).astype(F32)[:, None] * inv[None, :]
    cos, sin = jnp.cos(ang), jnp.sin(ang)
    reps = LANES // HEAD_DIM
    lane = np.arange(LANES)[None, :]
    blk = (np.arange(seq) // SLC_BLOCK)[:, None]
    ke = jnp.asarray(np.where(lane - MASK_COL == blk, -NEG, 0.0), F32)
    return jnp.tile(cos, (1, 2 * reps)), jnp.tile(jnp.concatenate([-sin, sin], axis=1), (1, reps)), ke


def kernel(x, w_in, sinks, cmp_pe_k, cmp_w1_k, cmp_b1_k, cmp_w2_k, cmp_pe_v, cmp_w1_v, cmp_b1_v, cmp_w2_v,
           w_proj_a, w_proj_b, w_out, ln1_g, ln1_b, w_gate, w_up, w_down, ln2_g, ln2_b):
    batch, seq, d = x.shape
    depth = w_in.shape[0]
    alpha = (2 * depth) ** 0.25
    n_main = 2 * Q_DIM + 8 * KV_DIM
    n_gate = 3 * Q_HEADS
    tm = ROW_TILE
    cos, sin, ke = _position_tables(seq)
    pad_cols = lambda w: jnp.pad(w, ((0, 0), (0, LANES - w.shape[1]))).astype(BF16)
    xt = x.reshape(batch * seq, d)
    for l in range(depth):
        w_main = w_in[l, :, :n_main].astype(BF16)
        w_gn = pad_cols(w_in[l, :, n_main:n_main + n_gate])
        w_gm = w_in[l, :, n_main + n_gate:].astype(BF16)
        (qa, ka, vat, qn, qnr, kc, vc, ksl, vslt, kw, vwt, gnt) = _in_proj(
            xt, w_main, w_gn, cos, sin, ke, batch, seq, tm)
        wk = (cmp_pe_k[l], cmp_w1_k[l].astype(BF16), cmp_b1_k[l][None, :], pad_cols(cmp_w2_k[l]))
        wv = (cmp_pe_v[l], cmp_w1_v[l].astype(BF16), cmp_b1_v[l][None, :], pad_cols(cmp_w2_v[l]))
        kcmp, vcmpt = _compress(kc, vc, wk, wv, batch, seq)
        oa, ob = _attn(sinks[l], qa, ka, vat, qn, qnr, gnt, kcmp, vcmpt, ksl, vslt, kw, vwt, batch, seq)
        h = _post(xt, oa, ob, w_gm, w_proj_a[l].astype(BF16), w_proj_b[l].astype(BF16), w_out[l].astype(BF16),
                  ln1_g[l][None, :], ln1_b[l][None, :], alpha, tm)
        xt = _ffn(h, w_gate[l].astype(BF16), w_up[l].astype(BF16), w_down[l].astype(BF16),
                  ln2_g[l][None, :], ln2_b[l][None, :], alpha, tm)
    return xt.reshape(batch, seq, d)
```

```python
import functools

import jax
import jax.numpy as jnp
import numpy as np
from jax import lax
from jax.experimental import pallas as pl
from jax.experimental.pallas import tpu as pltpu

HEAD_DIM = 64
ROPE_THETA = 10000.0
Q_HEADS = 8
KV_HEADS = 2
REP = Q_HEADS // KV_HEADS
A_WINDOW = 128
B_WINDOW = 512
CMP_BLOCK = 32
CMP_STRIDE = 16
SLC_BLOCK = 64
SLC_TOP_N = 16
LN_EPS = 1e-5
NEG = -1e30
BIG = 1e9
Q_DIM = Q_HEADS * HEAD_DIM
KV_DIM = KV_HEADS * HEAD_DIM
LANES = 128
QBLK = 256
KSTEP = QBLK
LOG2E = 1.4426950408889634
SCALE = HEAD_DIM ** -0.5 * LOG2E
VROWS = HEAD_DIM + 16
ROW_TILE = 1024
MIB = 1024 * 1024
VMEM_PROJ = 48 * MIB
VMEM_WIDE = 56 * MIB

F32 = jnp.float32
BF16 = jnp.bfloat16


def _dot(a, b):
    return jnp.dot(a, b, preferred_element_type=F32)


def _inproj_kernel(x_ref, w_ref, wg_ref, cos_ref, sin_ref, cost_ref, sint_ref, ke_ref,
                   qa_ref, ka_ref, vat_ref, qn_ref, qnr_ref, kc_ref, vc_ref,
                   ksl_ref, vslt_ref, kw_ref, vwt_ref, gnt_ref):
    tm = x_ref.shape[0]
    xb = x_ref[...].astype(BF16)
    cos = cos_ref[...]
    sin = sin_ref[...]
    lane = lax.broadcasted_iota(jnp.int32, (tm, LANES), 1)
    first_half = (lane & (HEAD_DIM - 1)) < (HEAD_DIM // 2)
    low = lane < HEAD_DIM

    def rope(z):
        sw = jnp.where(first_half, pltpu.roll(z, LANES - HEAD_DIM // 2, 1), pltpu.roll(z, HEAD_DIM // 2, 1))
        return z * cos + sw * sin

    def proj(c0, n):
        return _dot(xb, w_ref[:, c0:c0 + n])

    def chunks(z):
        return [z[:, c * LANES:(c + 1) * LANES] for c in range(z.shape[1] // LANES)]

    def put_padded(ref, c, z, fill):
        ref[:, (2 * c) * LANES:(2 * c + 1) * LANES] = jnp.where(low, z, fill).astype(ref.dtype)
        ref[:, (2 * c + 1) * LANES:(2 * c + 2) * LANES] = jnp.where(low, pltpu.roll(z, HEAD_DIM, 1), fill).astype(ref.dtype)

    def put_transposed(ref, z):
        for j in range(tm // LANES):
            ref[j] = z[j * LANES:(j + 1) * LANES, :].T.astype(ref.dtype)

    def put_heads_t(c, z, plain_ref, rope_ref):
        half = HEAD_DIM // 2
        for j in range(tm // LANES):
            zt = z[j * LANES:(j + 1) * LANES, :].T * SCALE
            rows = slice(c * LANES, (c + 1) * LANES)
            if plain_ref is not None:
                plain_ref[j, rows, :] = zt.astype(plain_ref.dtype)
            sw = jnp.concatenate([zt[(s ^ 1) * half:((s ^ 1) + 1) * half] for s in range(LANES // half)], axis=0)
            rope_ref[j, rows, :] = (zt * cost_ref[j] + sw * sint_ref[j]).astype(rope_ref.dtype)

    ones_row = jnp.where(lax.broadcasted_iota(jnp.int32, (VROWS - HEAD_DIM, LANES), 0) == 0, 1.0, 0.0)

    def put_values_t(ref, z):
        keys = ref.shape[2]
        for j in range(tm // keys):
            for i in range(keys // LANES):
                zt = z[j * keys + i * LANES:j * keys + (i + 1) * LANES, :].T
                cols = slice(i * LANES, (i + 1) * LANES)
                for g in range(KV_HEADS):
                    ref[j, g * VROWS:g * VROWS + HEAD_DIM, cols] = zt[g * HEAD_DIM:(g + 1) * HEAD_DIM, :].astype(ref.dtype)
                    ref[j, g * VROWS + HEAD_DIM:(g + 1) * VROWS, cols] = ones_row.astype(ref.dtype)

    for c, zc in enumerate(chunks(proj(0, Q_DIM))):
        put_heads_t(c, zc, None, qa_ref)
    z = proj(Q_DIM, 2 * KV_DIM)
    put_padded(ka_ref, 0, rope(z[:, :KV_DIM]), 0.0)
    put_values_t(vat_ref, z[:, KV_DIM:])
    c0 = Q_DIM + 2 * KV_DIM
    for c, zc in enumerate(chunks(proj(c0, Q_DIM))):
        put_heads_t(c, zc, qn_ref, qnr_ref)
    c0 += Q_DIM
    z = proj(c0, 2 * KV_DIM)
    kc_ref[...] = z[:, :KV_DIM]
    vc_ref[...] = z[:, KV_DIM:]
    c0 += 2 * KV_DIM
    z = proj(c0, 2 * KV_DIM)
    put_padded(ksl_ref, 0, rope(z[:, :KV_DIM]), ke_ref[...])
    put_values_t(vslt_ref, z[:, KV_DIM:])
    c0 += 2 * KV_DIM
    z = proj(c0, 2 * KV_DIM)
    put_padded(kw_ref, 0, rope(z[:, :KV_DIM]), 0.0)
    put_values_t(vwt_ref, z[:, KV_DIM:])
    put_transposed(gnt_ref, jax.nn.sigmoid(_dot(xb, wg_ref[...])))


def _in_proj(x2, w_main, w_gn, cos, sin, ke, batch, seq, tm):
    n = x2.shape[0]
    d = x2.shape[1]
    spt = seq // tm
    tok = lambda width: pl.BlockSpec((tm, width), lambda i: (i, 0))
    full = lambda a: pl.BlockSpec(a.shape, lambda i: (0, 0))
    tab = pl.BlockSpec((tm, LANES), lambda i: (i % spt, 0))
    tab_t = pl.BlockSpec((tm // LANES, LANES, LANES), lambda i: (i % spt, 0, 0))
    cos_t, sin_t = (jnp.transpose(t.reshape(seq // LANES, LANES, LANES), (0, 2, 1)) for t in (cos, sin))
    flat = lambda w, dt: (tok(w), jax.ShapeDtypeStruct((n, w), dt))
    trans = lambda r, keys, dt: (pl.BlockSpec((None, tm // keys, r, keys), lambda i: (i // spt, i % spt, 0, 0)),
                                 jax.ShapeDtypeStruct((batch, seq // keys, r, keys), dt))
    outs = [trans(Q_DIM, LANES, BF16), flat(KV_HEADS * LANES, BF16),
            trans(KV_HEADS * VROWS, A_WINDOW, BF16),
            trans(Q_DIM, LANES, BF16), trans(Q_DIM, LANES, BF16),
            flat(KV_DIM, F32), flat(KV_DIM, F32),
            flat(KV_HEADS * LANES, BF16), trans(KV_HEADS * VROWS, KSTEP, BF16),
            flat(KV_HEADS * LANES, BF16), trans(KV_HEADS * VROWS, KSTEP, BF16),
            trans(LANES, LANES, F32)]
    return pl.pallas_call(
        _inproj_kernel,
        grid=(n // tm,),
        in_specs=[tok(d), full(w_main), full(w_gn), tab, tab, tab_t, tab_t, tab],
        out_specs=[o[0] for o in outs],
        out_shape=[o[1] for o in outs],
        compiler_params=pltpu.CompilerParams(dimension_semantics=("parallel",),
                                             vmem_limit_bytes=VMEM_PROJ),
        name="in_proj",
    )(x2, w_main, w_gn, cos, sin, cos_t, sin_t, ke)


CMP_PAIR = 2


def _compress_one(src_ref, pe_ref, w1_ref, b1_ref, w2_ref):
    ratio = CMP_BLOCK // CMP_STRIDE
    nchunk = src_ref.shape[0] // CMP_STRIDE
    both = lambda row: jnp.concatenate([row] * KV_HEADS, axis=1)
    parts = [jnp.zeros((nchunk, w1_ref.shape[3]), F32) for _ in range(ratio)]
    for p in range(CMP_STRIDE // CMP_PAIR):
        rows = [src_ref[pl.ds(p * CMP_PAIR + i, nchunk, stride=CMP_STRIDE), :] for i in range(CMP_PAIR)]
        for j in range(ratio):
            pos = j * CMP_STRIDE + p * CMP_PAIR
            a = jnp.concatenate([(rows[i] + both(pe_ref[pos + i:pos + i + 1, :])).astype(BF16)
                                 for i in range(CMP_PAIR)], axis=1)
            parts[j] = parts[j] + _dot(a, w1_ref[j, p])
    h = parts[0]
    for j in range(1, ratio):
        h = h + pltpu.roll(parts[j], nchunk - j, 0)
    h = jax.nn.gelu(h + both(b1_ref[...]))
    return _dot(h.astype(BF16), w2_ref[...])


def _group_diagonal(w1_ref, w2_ref, w1d_ref, w2d_ref):
    hid = w1_ref.shape[1]
    w1d_ref[...] = jnp.zeros(w1d_ref.shape, w1d_ref.dtype)
    w2d_ref[...] = jnp.zeros(w2d_ref.shape, w2d_ref.dtype)
    for g in range(KV_HEADS):
        w2d_ref[g * hid:(g + 1) * hid, g * LANES:(g + 1) * LANES] = w2_ref[...]
        for j in range(CMP_BLOCK // CMP_STRIDE):
            for p in range(CMP_STRIDE // CMP_PAIR):
                for i in range(CMP_PAIR):
                    pos = j * CMP_STRIDE + p * CMP_PAIR + i
                    row = i * KV_DIM + g * HEAD_DIM
                    w1d_ref[j, p, row:row + HEAD_DIM, g * hid:(g + 1) * hid] = w1_ref[pos * HEAD_DIM:(pos + 1) * HEAD_DIM, :]


def _compress_kernel(kc_ref, vc_ref, pek_ref, w1k_ref, b1k_ref, w2k_ref,
                     pev_ref, w1v_ref, b1v_ref, w2v_ref, kcmp_ref, vcmpt_ref,
                     w1dk_ref, w2dk_ref, w1dv_ref, w2dv_ref):
    @pl.when(pl.program_id(0) == 0)
    def _():
        _group_diagonal(w1k_ref, w2k_ref, w1dk_ref, w2dk_ref)
        _group_diagonal(w1v_ref, w2v_ref, w1dv_ref, w2dv_ref)

    kc = _compress_one(kc_ref, pek_ref, w1dk_ref, b1k_ref, w2dk_ref)
    vc = _compress_one(vc_ref, pev_ref, w1dv_ref, b1v_ref, w2dv_ref)
    for g in range(KV_HEADS):
        kcmp_ref[g] = kc[:, g * LANES:(g + 1) * LANES].astype(BF16)
        vcmpt_ref[g] = vc[:, g * LANES:(g + 1) * LANES].T[:HEAD_DIM, :].astype(BF16)


def _compress(kc, vc, wk, wv, batch, seq):
    nchunk = seq // CMP_STRIDE
    hid = wk[1].shape[1]
    src = pl.BlockSpec((seq, KV_DIM), lambda b: (b, 0))
    full = lambda a: pl.BlockSpec(a.shape, lambda b: (0,) * a.ndim)
    return pl.pallas_call(
        _compress_kernel,
        grid=(batch,),
        in_specs=[src, src] + [full(a) for a in wk] + [full(a) for a in wv],
        out_specs=[pl.BlockSpec((None, KV_HEADS, nchunk, LANES), lambda b: (b, 0, 0, 0)),
                   pl.BlockSpec((None, KV_HEADS, HEAD_DIM, nchunk), lambda b: (b, 0, 0, 0))],
        out_shape=[jax.ShapeDtypeStruct((batch, KV_HEADS, nchunk, LANES), BF16),
                   jax.ShapeDtypeStruct((batch, KV_HEADS, HEAD_DIM, nchunk), BF16)],
        scratch_shapes=[pltpu.VMEM((CMP_BLOCK // CMP_STRIDE, CMP_STRIDE // CMP_PAIR, CMP_PAIR * KV_DIM, KV_HEADS * hid), BF16),
                        pltpu.VMEM((KV_HEADS * hid, KV_HEADS * LANES), BF16)] * 2,
        compiler_params=pltpu.CompilerParams(dimension_semantics=("arbitrary",)),
        name="compress",
    )(kc, vc, *wk, *wv)


MASK_COL = HEAD_DIM


def _select_blocks_t(imp, t0):
    nb, cols = imp.shape
    j = lax.broadcasted_iota(jnp.int32, (nb, cols), 0)
    cur = (t0 + lax.broadcasted_iota(jnp.int32, (nb, cols), 1)) >> int(np.log2(SLC_BLOCK))
    forced = (j == 0) | (j == cur) | (j == cur - 1)
    score = jnp.where(forced, BIG, jnp.where(j <= cur, imp, -BIG))
    sub = 8
    tiles = [score[v * sub:(v + 1) * sub] for v in range(nb // sub)]
    ranks = [jnp.zeros((sub, cols), F32) for _ in tiles]
    jl = lax.broadcasted_iota(jnp.int32, (sub, cols), 0)
    for jp in range(nb):
        row = score[jp:jp + 1, :]
        for v, tile in enumerate(tiles):
            if v * sub > jp:
                beats = row >= tile
            elif (v + 1) * sub - 1 <= jp:
                beats = row > tile
            else:
                beats = (row > tile) | ((row == tile) & (jl > jp - v * sub))
            ranks[v] = jnp.where(beats, ranks[v] + 1.0, ranks[v])
    rank = jnp.concatenate(ranks, axis=0)
    return jnp.where((rank < float(min(SLC_TOP_N, nb))) & (j <= cur), 1.0, 0.0)


def _put_heads(o_ref, g, o, nq=QBLK):
    for i in range(REP // 2):
        pair = jnp.concatenate([o[:, (2 * i) * nq:(2 * i + 1) * nq],
                                o[:, (2 * i + 1) * nq:(2 * i + 2) * nq]], axis=0)
        col = (g * REP + 2 * i) * HEAD_DIM
        o_ref[:, col:col + LANES] = pair.T.astype(o_ref.dtype)


class _Pipeline:
    def __init__(self, k_ref, vt_ref, q_ref, s_ref, p_ref, acc_ref, chunk_of):
        self.k_ref, self.vt_ref, self.q_ref = k_ref, vt_ref, q_ref
        self.s_ref, self.p_ref, self.acc_ref = s_ref, p_ref, acc_ref
        self.chunk_of = chunk_of
        self.groups = range(KV_HEADS)

    def qk(self, g, j):
        start = pl.multiple_of(self.chunk_of(j) * KSTEP, KSTEP)
        depth = self.q_ref.shape[1]
        return _dot(self.k_ref[pl.ds(start, KSTEP), g * LANES:g * LANES + depth], self.q_ref[g])

    def keep(self, g, st, bias):
        if bias is not None:
            st = st + bias
        self.s_ref[g] = st
        return jnp.max(st, axis=0, keepdims=True)

    def softmax(self, g, m, cmax):
        m_new = jnp.maximum(m, cmax)
        a = jnp.exp2(m - m_new)
        self.p_ref[g] = jnp.exp2(self.s_ref[g] - m_new).astype(BF16)
        return m_new, a

    def values(self, g, j, a):
        vt = self.vt_ref[self.chunk_of(j), g * VROWS:(g + 1) * VROWS, :]
        self.acc_ref[g] = a * self.acc_ref[g] + _dot(vt, self.p_ref[g])

    def start(self, bias0, bias1):
        m0 = jnp.full((1, self.s_ref.shape[2]), NEG, F32)
        cmax = [self.keep(g, self.qk(g, 0), bias0) for g in self.groups]
        st1 = [self.qk(g, 1) for g in self.groups]
        sm = [self.softmax(g, m0, cmax[g]) for g in self.groups]
        cmax = tuple(self.keep(g, st1[g], bias1) for g in self.groups)
        for g in self.groups:
            self.acc_ref[g] = jnp.zeros(self.acc_ref.shape[1:], F32)
        return (tuple(s[0] for s in sm), tuple(s[1] for s in sm), cmax)

    def step(self, j, state, last=False, bias=None):
        m, a, cmax = state
        st = None if last else [self.qk(g, j + 2) for g in self.groups]
        for g in self.groups:
            self.values(g, j, a[g])
        sm = [self.softmax(g, m[g], cmax[g]) for g in self.groups]
        if not last:
            cmax = tuple(self.keep(g, st[g], bias) for g in self.groups)
        return (tuple(s[0] for s in sm), tuple(s[1] for s in sm), cmax)

    def finish(self, count, state):
        _, a, _ = state
        out = []
        for g in self.groups:
            self.values(g, count - 1, a[g])
            acc = self.acc_ref[g]
            out.append(acc[:HEAD_DIM] * (1.0 / acc[HEAD_DIM:HEAD_DIM + 1]))
        return out


ATTN_BLOCKS = 2


def _attn_kernel(sink_ref, abias_ref, pbias_ref, cbias_ref, ovt_ref,
                 qa_ref, ka_ref, vat_ref, qn_ref, qnr_ref, gnt_ref, kcmp_ref, vcmpt_ref,
                 ksl_ref, vslt_ref, kw_ref, vwt_ref, oa_ref, ob_ref, *scratch):
    tiles = QBLK // LANES
    for i in range(ATTN_BLOCKS):
        tok = lambda ref: ref.at[pl.ds(i * tiles, tiles)]
        out = lambda ref: ref.at[pl.ds(i * QBLK, QBLK), :]
        _attn_block(pl.program_id(1) * ATTN_BLOCKS + i, sink_ref, abias_ref, pbias_ref, cbias_ref, ovt_ref,
                    tok(qa_ref), ka_ref, vat_ref, tok(qn_ref), tok(qnr_ref), tok(gnt_ref), kcmp_ref, vcmpt_ref,
                    ksl_ref, vslt_ref, kw_ref, vwt_ref, out(oa_ref), out(ob_ref), *scratch)


def _attn_block(n, sink_ref, abias_ref, pbias_ref, cbias_ref, ovt_ref,
                qa_ref, ka_ref, vat_ref, qn_ref, qnr_ref, gnt_ref, kcmp_ref, vcmpt_ref,
                ksl_ref, vslt_ref, kw_ref, vwt_ref,
                oa_ref, ob_ref,
                qw_ref, sw_ref, pw_ref, accw_ref, qs_ref, ss_ref, ps_ref, accs_ref, sa_ref, pa_ref, sc_ref):
    groups = range(KV_HEADS)
    width = REP * QBLK
    nb = ovt_ref.shape[0]
    nc = kcmp_ref.shape[1]
    t0 = pl.multiple_of(n * QBLK, QBLK)
    heads = lambda g: [g * REP + r for r in range(REP)]
    parts = range(QBLK // LANES)
    head_t = lambda ref, h, i: ref[i, h * HEAD_DIM:(h + 1) * HEAD_DIM, :]
    cols = lambda ref, g: jnp.concatenate([head_t(ref, h, i) for h in heads(g) for i in parts], axis=1)
    lane_chunks = lambda a: [a[:, r * QBLK:(r + 1) * QBLK] for r in range(REP)]
    gone = lambda cond: jnp.where(cond, 0.0, NEG)

    assert B_WINDOW == 2 * KSTEP
    win = _Pipeline(kw_ref, vwt_ref, qw_ref, sw_ref, pw_ref, accw_ref, lambda j: jnp.maximum(n - j, 0))
    slc = _Pipeline(ksl_ref, vslt_ref, qs_ref, ss_ref, ps_ref, accs_ref,
                    lambda j: jnp.where(j == 0, n, jnp.maximum(j - 1, 0)))
    slc_count = jnp.maximum(n, 1) + 1

    for g in groups:
        qw_ref[g] = cols(qnr_ref, g)
    a_chunk = [n * len(parts) + i for i in parts]
    a_first = [jnp.maximum(c - 1, 0) for c in a_chunk]
    part_cols = lambda i, g: jnp.concatenate([head_t(qa_ref, h, i) for h in heads(g)], axis=1)
    amax = {}
    for i in parts:
        abias = abias_ref[jnp.minimum(a_chunk[i], 1)]
        astart = pl.multiple_of(a_first[i] * A_WINDOW, A_WINDOW)
        for g in groups:
            st = _dot(ka_ref[pl.ds(astart, 2 * A_WINDOW), g * LANES:g * LANES + HEAD_DIM], part_cols(i, g)) + abias
            sa_ref[i, g] = st
            amax[i, g] = jnp.max(st, axis=0, keepdims=True)
    cstart = pl.multiple_of(cbias_ref.shape[0] - nc - n * (QBLK // CMP_STRIDE), 8)
    cbias = cbias_ref[pl.ds(cstart, nc), :]
    cmax = []
    for g in groups:
        s = _dot(kcmp_ref[g, :, :HEAD_DIM], cols(qn_ref, g)) + cbias
        sc_ref[g] = s
        cmax.append(jnp.max(s, axis=0, keepdims=True))
    win_state = win.start(pbias_ref[0], gone(n >= 1))

    x_swa = {}
    for i in parts:
        for g in groups:
            sk = jnp.concatenate([jnp.full((1, A_WINDOW), sink_ref[h] * LOG2E, F32) for h in heads(g)], axis=1)
            m = jnp.maximum(amax[i, g], sk)
            pa_ref[i, g] = jnp.exp2(sa_ref[i, g] - m).astype(BF16)
            x_swa[i, g] = jnp.exp2(sk - m)
    tq = t0 + (lax.broadcasted_iota(jnp.int32, (1, width), 1) & (QBLK - 1))
    sees_any = tq >= CMP_BLOCK - 1
    p_cmp = []
    for g in groups:
        e = jnp.exp2(sc_ref[g] - cmax[g])
        inv = jnp.where(sees_any, 1.0 / jnp.maximum(jnp.sum(e, axis=0, keepdims=True), 1e-30), 0.0)
        p_cmp.append(e * inv)

    win_state = win.step(0, win_state, bias=pbias_ref[1] + gone(n >= 2))

    for i in parts:
        for g in groups:
            vt = jnp.concatenate([vat_ref[a_first[i] + k, g * VROWS:(g + 1) * VROWS, :] for k in range(2)], axis=1)
            o = _dot(vt, pa_ref[i, g])
            o = o[:HEAD_DIM] * (1.0 / (o[HEAD_DIM:HEAD_DIM + 1] + x_swa[i, g]))
            _put_heads(oa_ref.at[i * A_WINDOW:(i + 1) * A_WINDOW, :], g, o, A_WINDOW)
    o_cmp = [_dot(vcmpt_ref[g], p_cmp[g].astype(BF16)) for g in groups]
    imp = []
    for g in groups:
        pc = lane_chunks(p_cmp[g])
        psum = (pc[0] + pc[1]) + (pc[2] + pc[3])
        hi = psum.astype(BF16)
        lo = (psum - hi.astype(F32)).astype(BF16)
        imp.append(_dot(ovt_ref[...], hi) + _dot(ovt_ref[...], lo))
    o_win = win.finish(3, win.step(1, win_state, last=True))

    jb = lax.broadcasted_iota(jnp.int32, (nb, QBLK), 0)
    curb = (t0 + lax.broadcasted_iota(jnp.int32, (nb, QBLK), 1)) >> int(np.log2(SLC_BLOCK))
    sel = lax.cond((t0 + QBLK - 1) // SLC_BLOCK < SLC_TOP_N,
                   lambda: tuple(jnp.where(jb <= curb, 1.0, 0.0) for _ in groups),
                   lambda: tuple(_select_blocks_t(imp[g], t0) for g in groups))

    for g in groups:
        qs_ref[g, :HEAD_DIM, :] = cols(qnr_ref, g)
        mrows = jnp.concatenate([sel[g] - 1.0, jnp.zeros((LANES - MASK_COL - nb, QBLK), F32)], axis=0).astype(BF16)
        for r in range(REP):
            qs_ref[g, MASK_COL:, r * QBLK:(r + 1) * QBLK] = mrows
    slc_state = slc.start(pbias_ref[0], gone(n >= 1))

    no_bias = gone(n >= 0)
    one_step = lambda j, s: slc.step(j, s, bias=no_bias)
    trips = slc_count - 2
    slc_state = lax.fori_loop(0, trips // 2, lambda i, s: one_step(2 * i + 1, one_step(2 * i, s)), slc_state)
    slc_state = lax.cond(trips % 2 == 1, lambda s: one_step(trips - 1, s), lambda s: s, slc_state)

    o_slc = slc.finish(slc_count, slc.step(slc_count - 2, slc_state, last=True))
    for g in groups:
        gate = lambda br: jnp.concatenate(
            [gnt_ref[i, br * Q_HEADS + h:br * Q_HEADS + h + 1, :]
             for h in heads(g) for i in range(gnt_ref.shape[0])], axis=1)
        _put_heads(ob_ref, g, gate(0) * o_cmp[g] + gate(1) * o_slc[g] + gate(2) * o_win[g])


def _attn(sinks, qa, ka, vat, qn, qnr, gnt, kcmp, vcmpt, ksl, vslt, kw, vwt, batch, seq):
    nq = seq // QBLK
    nc = kcmp.shape[2]
    nb = seq // SLC_BLOCK
    width = REP * QBLK
    assert nb <= LANES - MASK_COL and QBLK % SLC_BLOCK == 0 and QBLK % A_WINDOW == 0 and A_WINDOW % LANES == 0
    c_start = np.arange(nc)[None, :] * CMP_STRIDE
    s_start = np.arange(nb)[:, None] * SLC_BLOCK
    ovt = jnp.asarray((c_start < s_start + SLC_BLOCK) & (s_start < c_start + CMP_BLOCK), BF16)
    ql = np.arange(width)[None, :] % QBLK
    masked = lambda visible: np.where(visible, 0.0, NEG)
    kq = np.arange(KSTEP)[:, None] - ql
    pbias = jnp.asarray(np.stack([masked(kq <= 0),
                                  masked(kq > 0)]), F32)
    band = np.arange(2 * A_WINDOW)[:, None]
    qpart = np.arange(REP * A_WINDOW)[None, :] % A_WINDOW
    in_window = lambda dist: (dist >= 0) & (dist < A_WINDOW)
    abias = jnp.asarray(np.stack([masked(in_window(qpart - band)),
                                  masked(in_window(qpart - (band - A_WINDOW)))]), F32)
    per_blk = QBLK // CMP_STRIDE
    d = np.arange(nc + (nq - 1) * per_blk)[:, None] - (nq - 1) * per_blk
    cbias = jnp.asarray(masked(d * CMP_STRIDE + CMP_BLOCK - 1 <= ql), F32)
    assert nq % ATTN_BLOCKS == 0
    steps = nq // ATTN_BLOCKS
    tiles = ATTN_BLOCKS * QBLK // LANES
    q_spec = pl.BlockSpec((None, tiles, Q_DIM, LANES), lambda b, n: (b, n, 0, 0))
    g_spec = pl.BlockSpec((None, tiles, LANES, LANES), lambda b, n: (b, n, 0, 0))
    kc_spec = pl.BlockSpec((None, KV_HEADS, nc, LANES), lambda b, n: (b, 0, 0, 0))
    vc_spec = pl.BlockSpec((None, KV_HEADS, HEAD_DIM, nc), lambda b, n: (b, 0, 0, 0))
    const = lambda a: pl.BlockSpec(a.shape, lambda b, n: (0,) * a.ndim, pipeline_mode=pl.Buffered(1))
    k_spec = pl.BlockSpec((seq, KV_HEADS * LANES), lambda b, n: (b, 0))
    vt_spec = lambda keys: pl.BlockSpec((None, seq // keys, KV_HEADS * VROWS, keys), lambda b, n: (b, 0, 0, 0))
    o_spec = pl.BlockSpec((ATTN_BLOCKS * QBLK, Q_DIM), lambda b, n: (b * steps + n, 0))
    o_shape = jax.ShapeDtypeStruct((batch * seq, Q_DIM), BF16)
    pipe_scratch = lambda depth: [
        pltpu.VMEM((KV_HEADS, depth, width), BF16),
        pltpu.VMEM((KV_HEADS, KSTEP, width), F32),
        pltpu.VMEM((KV_HEADS, KSTEP, width), BF16),
        pltpu.VMEM((KV_HEADS, VROWS, width), F32)]
    return pl.pallas_call(
        _attn_kernel,
        grid=(batch, steps),
        in_specs=[pl.BlockSpec(memory_space=pltpu.SMEM), const(abias), const(pbias), const(cbias), const(ovt),
                  q_spec, k_spec, vt_spec(A_WINDOW), q_spec, q_spec, g_spec, kc_spec, vc_spec,
                  k_spec, vt_spec(KSTEP), k_spec, vt_spec(KSTEP)],
        out_specs=[o_spec, o_spec],
        out_shape=[o_shape, o_shape],
        scratch_shapes=pipe_scratch(HEAD_DIM) + pipe_scratch(LANES) + [
            pltpu.VMEM((QBLK // A_WINDOW, KV_HEADS, 2 * A_WINDOW, REP * A_WINDOW), F32),
            pltpu.VMEM((QBLK // A_WINDOW, KV_HEADS, 2 * A_WINDOW, REP * A_WINDOW), BF16),
            pltpu.VMEM((KV_HEADS, nc, width), F32)],
        compiler_params=pltpu.CompilerParams(dimension_semantics=("parallel", "arbitrary"),
                                             vmem_limit_bytes=VMEM_WIDE),
        name="attn",
    )(sinks, abias, pbias, cbias, ovt, qa, ka, vat, qn, qnr, gnt, kcmp, vcmpt, ksl, vslt, kw, vwt)


def _layer_norm(r, g, b):
    mu = jnp.mean(r, axis=-1, keepdims=True)
    d = r - mu
    var = jnp.mean(d * d, axis=-1, keepdims=True)
    return d * lax.rsqrt(var + LN_EPS) * g + b


POST_PARTS = 4


def _post_kernel(alpha, x_ref, oa_ref, ob_ref, wgm_ref, wpa_ref, wpb_ref, wout_ref, g_ref, b_ref, h_ref):
    d = x_ref.shape[1]
    rows = [slice(i * (x_ref.shape[0] // POST_PARTS), (i + 1) * (x_ref.shape[0] // POST_PARTS))
            for i in range(POST_PARTS)]
    ys = []
    for r in rows:
        xb = x_ref[r, :].astype(BF16)
        pa = _dot(oa_ref[r, :], wpa_ref[...])
        pb = _dot(ob_ref[r, :], wpb_ref[...])
        ys.append(jax.nn.sigmoid(_dot(xb, wgm_ref[:, :d])) * pa + jax.nn.sigmoid(_dot(xb, wgm_ref[:, d:])) * pb)
    for r, y in zip(rows, ys):
        m = _dot(y.astype(BF16), wout_ref[...])
        h_ref[r, :] = _layer_norm(alpha * x_ref[r, :] + m, g_ref[...], b_ref[...])


def _post(x2, oa, ob, w_gm, wpa, wpb, wout, g, b, alpha, tm):
    n, d = x2.shape
    tok = lambda width: pl.BlockSpec((tm, width), lambda i: (i, 0))
    full = lambda a: pl.BlockSpec(a.shape, lambda i: (0, 0), pipeline_mode=pl.Buffered(1))
    return pl.pallas_call(
        functools.partial(_post_kernel, alpha),
        grid=(n // tm,),
        in_specs=[tok(d), tok(Q_DIM), tok(Q_DIM)] + [full(a) for a in (w_gm, wpa, wpb, wout, g, b)],
        out_specs=tok(d),
        out_shape=jax.ShapeDtypeStruct((n, d), F32),
        compiler_params=pltpu.CompilerParams(dimension_semantics=("parallel",),
                                             vmem_limit_bytes=VMEM_PROJ),
        name="post",
    )(x2, oa, ob, w_gm, wpa, wpb, wout, g, b)


FFN_CHUNK = 256


FFN_PARTS = 2


def _ffn_kernel(alpha, h_ref, wg_ref, wu_ref, wd_ref, g_ref, b_ref, o_ref):
    hidden = wg_ref.shape[1]
    rows = h_ref.shape[0] // FFN_PARTS
    for p in range(FFN_PARTS):
        r = slice(p * rows, (p + 1) * rows)
        h = h_ref[r, :]
        hb = h.astype(BF16)
        acc = jnp.zeros(h.shape, F32)
        for c in range(hidden // FFN_CHUNK):
            sl = slice(c * FFN_CHUNK, (c + 1) * FFN_CHUNK)
            a = jax.nn.silu(_dot(hb, wg_ref[:, sl])) * _dot(hb, wu_ref[:, sl])
            acc = acc + _dot(a.astype(BF16), wd_ref[sl, :])
        o_ref[r, :] = _layer_norm(alpha * h + acc, g_ref[...], b_ref[...])


def _ffn(h, wg, wu, wd, g, b, alpha, tm):
    n, d = h.shape
    tok = pl.BlockSpec((tm, d), lambda i: (i, 0))
    once = lambda a: pl.BlockSpec(a.shape, lambda i: (0, 0), pipeline_mode=pl.Buffered(1))
    return pl.pallas_call(
        functools.partial(_ffn_kernel, alpha),
        grid=(n // tm,),
        in_specs=[tok] + [once(a) for a in (wg, wu, wd, g, b)],
        out_specs=tok,
        out_shape=jax.ShapeDtypeStruct((n, d), F32),
        compiler_params=pltpu.CompilerParams(dimension_semantics=("parallel",),
                                             vmem_limit_bytes=VMEM_WIDE),
        name="ffn",
    )(h, wg, wu, wd, g, b)


def _tail_kernel(alpha, x_ref, oa_ref, ob_ref, wgm_ref, wpa_ref, wpb_ref, wout_ref, g1_ref, b1_ref,
                 wg_ref, wu_ref, wd_ref, g2_ref, b2_ref, o_ref, h_ref):
    _post_kernel(alpha, x_ref, oa_ref, ob_ref, wgm_ref, wpa_ref, wpb_ref, wout_ref, g1_ref, b1_ref, h_ref)
    _ffn_kernel(alpha, h_ref, wg_ref, wu_ref, wd_ref, g2_ref, b2_ref, o_ref)


def _tail(x2, oa, ob, post_w, ffn_w, alpha, tm):
    n, d = x2.shape
    tok = lambda width: pl.BlockSpec((tm, width), lambda i: (i, 0))
    once = lambda a: pl.BlockSpec(a.shape, lambda i: (0, 0), pipeline_mode=pl.Buffered(1))
    return pl.pallas_call(
        functools.partial(_tail_kernel, alpha),
        grid=(n // tm,),
        in_specs=[tok(d), tok(Q_DIM), tok(Q_DIM)] + [once(a) for a in post_w + ffn_w],
        out_specs=tok(d),
        out_shape=jax.ShapeDtypeStruct((n, d), F32),
        scratch_shapes=[pltpu.VMEM((tm, d), F32)],
        compiler_params=pltpu.CompilerParams(dimension_semantics=("parallel",),
                                             vmem_limit_bytes=VMEM_WIDE),
        name="tail",
    )(x2, oa, ob, *post_w, *ffn_w)


def _position_tables(seq):
    half = HEAD_DIM // 2
    inv = ROPE_THETA ** (-jnp.arange(half, dtype=F32) / half)
    ang = jnp.arange(seq).astype(F32)[:, None] * inv[None, :]
    cos, sin = jnp.cos(ang), jnp.sin(ang)
    reps = LANES // HEAD_DIM
    lane = np.arange(LANES)[None, :]
    blk = (np.arange(seq) // SLC_BLOCK)[:, None]
    ke = jnp.asarray(np.where(lane - MASK_COL == blk, -NEG, 0.0), F32)
    return jnp.tile(cos, (1, 2 * reps)), jnp.tile(jnp.concatenate([-sin, sin], axis=1), (1, reps)), ke


def kernel(x, w_in, sinks, cmp_pe_k, cmp_w1_k, cmp_b1_k, cmp_w2_k, cmp_pe_v, cmp_w1_v, cmp_b1_v, cmp_w2_v,
           w_proj_a, w_proj_b, w_out, ln1_g, ln1_b, w_gate, w_up, w_down, ln2_g, ln2_b):
    batch, seq, d = x.shape
    depth = w_in.shape[0]
    alpha = (2 * depth) ** 0.25
    n_main = 2 * Q_DIM + 8 * KV_DIM
    n_gate = 3 * Q_HEADS
    tm = ROW_TILE
    cos, sin, ke = _position_tables(seq)
    pad_cols = lambda w: jnp.pad(w, ((0, 0), (0, LANES - w.shape[1]))).astype(BF16)
    xt = x.reshape(batch * seq, d)
    for l in range(depth):
        w_main = w_in[l, :, :n_main].astype(BF16)
        w_gn = pad_cols(w_in[l, :, n_main:n_main + n_gate])
        w_gm = w_in[l, :, n_main + n_gate:].astype(BF16)
        (qa, ka, vat, qn, qnr, kc, vc, ksl, vslt, kw, vwt, gnt) = _in_proj(
            xt, w_main, w_gn, cos, sin, ke, batch, seq, tm)
        wk = (cmp_pe_k[l], cmp_w1_k[l].astype(BF16), cmp_b1_k[l][None, :], pad_cols(cmp_w2_k[l]))
        wv = (cmp_pe_v[l], cmp_w1_v[l].astype(BF16), cmp_b1_v[l][None, :], pad_cols(cmp_w2_v[l]))
        kcmp, vcmpt = _compress(kc, vc, wk, wv, batch, seq)
        oa, ob = _attn(sinks[l], qa, ka, vat, qn, qnr, gnt, kcmp, vcmpt, ksl, vslt, kw, vwt, batch, seq)
        post_w = (w_gm, w_proj_a[l].astype(BF16), w_proj_b[l].astype(BF16), w_out[l].astype(BF16),
                  ln1_g[l][None, :], ln1_b[l][None, :])
        ffn_w = (w_gate[l].astype(BF16), w_up[l].astype(BF16), w_down[l].astype(BF16),
                 ln2_g[l][None, :], ln2_b[l][None, :])
        xt = _tail(xt, oa, ob, post_w, ffn_w, alpha, tm)
    return xt.reshape(batch, seq, d)
```

```python
import functools

import jax
import jax.numpy as jnp
import numpy as np
from jax import lax
from jax.experimental import pallas as pl
from jax.experimental.pallas import tpu as pltpu

HEAD_DIM = 64
ROPE_THETA = 10000.0
Q_HEADS = 8
KV_HEADS = 2
REP = Q_HEADS // KV_HEADS
A_WINDOW = 128
B_WINDOW = 512
CMP_BLOCK = 32
CMP_STRIDE = 16
SLC_BLOCK = 64
SLC_TOP_N = 16
LN_EPS = 1e-5
NEG = -1e30
BIG = 1e9
Q_DIM = Q_HEADS * HEAD_DIM
KV_DIM = KV_HEADS * HEAD_DIM
LANES = 128
QBLK = 256
KSTEP = QBLK
LOG2E = 1.4426950408889634
SCALE = HEAD_DIM ** -0.5 * LOG2E
VROWS = HEAD_DIM + 16
ROW_TILE = 1024
MIB = 1024 * 1024
VMEM_PROJ = 48 * MIB
VMEM_WIDE = 56 * MIB

F32 = jnp.float32
BF16 = jnp.bfloat16


def _dot(a, b):
    return jnp.dot(a, b, preferred_element_type=F32)


def _inproj_kernel(x_ref, w_ref, wg_ref, cos_ref, sin_ref, cost_ref, sint_ref, ke_ref,
                   qa_ref, ka_ref, vat_ref, qn_ref, qnr_ref, kc_ref, vc_ref,
                   ksl_ref, vslt_ref, kw_ref, vwt_ref, gnt_ref):
    tm = x_ref.shape[0]
    xb = x_ref[...].astype(BF16)
    cos = cos_ref[...]
    sin = sin_ref[...]
    lane = lax.broadcasted_iota(jnp.int32, (tm, LANES), 1)
    first_half = (lane & (HEAD_DIM - 1)) < (HEAD_DIM // 2)
    low = lane < HEAD_DIM

    def rope(z):
        sw = jnp.where(first_half, pltpu.roll(z, LANES - HEAD_DIM // 2, 1), pltpu.roll(z, HEAD_DIM // 2, 1))
        return z * cos + sw * sin

    def proj(c0, n):
        return _dot(xb, w_ref[:, c0:c0 + n])

    def chunks(z):
        return [z[:, c * LANES:(c + 1) * LANES] for c in range(z.shape[1] // LANES)]

    def put_padded(ref, c, z, fill):
        ref[:, (2 * c) * LANES:(2 * c + 1) * LANES] = jnp.where(low, z, fill).astype(ref.dtype)
        ref[:, (2 * c + 1) * LANES:(2 * c + 2) * LANES] = jnp.where(low, pltpu.roll(z, HEAD_DIM, 1), fill).astype(ref.dtype)

    def put_transposed(ref, z):
        for j in range(tm // LANES):
            ref[j] = z[j * LANES:(j + 1) * LANES, :].T.astype(ref.dtype)

    def put_heads_t(c, z, plain_ref, rope_ref):
        half = HEAD_DIM // 2
        for j in range(tm // LANES):
            zt = z[j * LANES:(j + 1) * LANES, :].T * SCALE
            rows = slice(c * LANES, (c + 1) * LANES)
            if plain_ref is not None:
                plain_ref[j, rows, :] = zt.astype(plain_ref.dtype)
            sw = jnp.concatenate([zt[(s ^ 1) * half:((s ^ 1) + 1) * half] for s in range(LANES // half)], axis=0)
            rope_ref[j, rows, :] = (zt * cost_ref[j] + sw * sint_ref[j]).astype(rope_ref.dtype)

    ones_row = jnp.where(lax.broadcasted_iota(jnp.int32, (VROWS - HEAD_DIM, LANES), 0) == 0, 1.0, 0.0)

    def put_values_t(ref, z):
        keys = ref.shape[2]
        for j in range(tm // keys):
            for i in range(keys // LANES):
                zt = z[j * keys + i * LANES:j * keys + (i + 1) * LANES, :].T
                cols = slice(i * LANES, (i + 1) * LANES)
                for g in range(KV_HEADS):
                    ref[j, g * VROWS:g * VROWS + HEAD_DIM, cols] = zt[g * HEAD_DIM:(g + 1) * HEAD_DIM, :].astype(ref.dtype)
                    ref[j, g * VROWS + HEAD_DIM:(g + 1) * VROWS, cols] = ones_row.astype(ref.dtype)

    for c, zc in enumerate(chunks(proj(0, Q_DIM))):
        put_heads_t(c, zc, None, qa_ref)
    z = proj(Q_DIM, 2 * KV_DIM)
    put_padded(ka_ref, 0, rope(z[:, :KV_DIM]), 0.0)
    put_values_t(vat_ref, z[:, KV_DIM:])
    c0 = Q_DIM + 2 * KV_DIM
    for c, zc in enumerate(chunks(proj(c0, Q_DIM))):
        put_heads_t(c, zc, qn_ref, qnr_ref)
    c0 += Q_DIM
    z = proj(c0, 2 * KV_DIM)
    kc_ref[...] = z[:, :KV_DIM]
    vc_ref[...] = z[:, KV_DIM:]
    c0 += 2 * KV_DIM
    z = proj(c0, 2 * KV_DIM)
    put_padded(ksl_ref, 0, rope(z[:, :KV_DIM]), ke_ref[...])
    put_values_t(vslt_ref, z[:, KV_DIM:])
    c0 += 2 * KV_DIM
    z = proj(c0, 2 * KV_DIM)
    put_padded(kw_ref, 0, rope(z[:, :KV_DIM]), 0.0)
    put_values_t(vwt_ref, z[:, KV_DIM:])
    put_transposed(gnt_ref, jax.nn.sigmoid(_dot(xb, wg_ref[...])))


def _in_proj(x2, w_main, w_gn, cos, sin, ke, batch, seq, tm):
    n = x2.shape[0]
    d = x2.shape[1]
    spt = seq // tm
    tok = lambda width: pl.BlockSpec((tm, width), lambda i: (i, 0))
    full = lambda a: pl.BlockSpec(a.shape, lambda i: (0, 0))
    tab = pl.BlockSpec((tm, LANES), lambda i: (i % spt, 0))
    tab_t = pl.BlockSpec((tm // LANES, LANES, LANES), lambda i: (i % spt, 0, 0))
    cos_t, sin_t = (jnp.transpose(t.reshape(seq // LANES, LANES, LANES), (0, 2, 1)) for t in (cos, sin))
    flat = lambda w, dt: (tok(w), jax.ShapeDtypeStruct((n, w), dt))
    trans = lambda r, keys, dt: (pl.BlockSpec((None, tm // keys, r, keys), lambda i: (i // spt, i % spt, 0, 0)),
                                 jax.ShapeDtypeStruct((batch, seq // keys, r, keys), dt))
    outs = [trans(Q_DIM, LANES, BF16), flat(KV_HEADS * LANES, BF16),
            trans(KV_HEADS * VROWS, A_WINDOW, BF16),
            trans(Q_DIM, LANES, BF16), trans(Q_DIM, LANES, BF16),
            flat(KV_DIM, F32), flat(KV_DIM, F32),
            flat(KV_HEADS * LANES, BF16), trans(KV_HEADS * VROWS, KSTEP, BF16),
            flat(KV_HEADS * LANES, BF16), trans(KV_HEADS * VROWS, KSTEP, BF16),
            trans(LANES, LANES, F32)]
    return pl.pallas_call(
        _inproj_kernel,
        grid=(n // tm,),
        in_specs=[tok(d), full(w_main), full(w_gn), tab, tab, tab_t, tab_t, tab],
        out_specs=[o[0] for o in outs],
        out_shape=[o[1] for o in outs],
        compiler_params=pltpu.CompilerParams(dimension_semantics=("parallel",),
                                             vmem_limit_bytes=VMEM_PROJ),
        name="in_proj",
    )(x2, w_main, w_gn, cos, sin, cos_t, sin_t, ke)


CMP_PAIR = 2


def _compress_one(src_ref, pe_ref, w1_ref, b1_ref, w2_ref):
    ratio = CMP_BLOCK // CMP_STRIDE
    nchunk = src_ref.shape[0] // CMP_STRIDE
    both = lambda row: jnp.concatenate([row] * KV_HEADS, axis=1)
    parts = [jnp.zeros((nchunk, w1_ref.shape[3]), F32) for _ in range(ratio)]
    for p in range(CMP_STRIDE // CMP_PAIR):
        rows = [src_ref[pl.ds(p * CMP_PAIR + i, nchunk, stride=CMP_STRIDE), :] for i in range(CMP_PAIR)]
        for j in range(ratio):
            pos = j * CMP_STRIDE + p * CMP_PAIR
            a = jnp.concatenate([(rows[i] + both(pe_ref[pos + i:pos + i + 1, :])).astype(BF16)
                                 for i in range(CMP_PAIR)], axis=1)
            parts[j] = parts[j] + _dot(a, w1_ref[j, p])
    h = parts[0]
    for j in range(1, ratio):
        h = h + pltpu.roll(parts[j], nchunk - j, 0)
    h = jax.nn.gelu(h + both(b1_ref[...]))
    return _dot(h.astype(BF16), w2_ref[...])


def _group_diagonal(w1_ref, w2_ref, w1d_ref, w2d_ref):
    hid = w1_ref.shape[1]
    w1d_ref[...] = jnp.zeros(w1d_ref.shape, w1d_ref.dtype)
    w2d_ref[...] = jnp.zeros(w2d_ref.shape, w2d_ref.dtype)
    for g in range(KV_HEADS):
        w2d_ref[g * hid:(g + 1) * hid, g * LANES:(g + 1) * LANES] = w2_ref[...]
        for j in range(CMP_BLOCK // CMP_STRIDE):
            for p in range(CMP_STRIDE // CMP_PAIR):
                for i in range(CMP_PAIR):
                    pos = j * CMP_STRIDE + p * CMP_PAIR + i
                    row = i * KV_DIM + g * HEAD_DIM
                    w1d_ref[j, p, row:row + HEAD_DIM, g * hid:(g + 1) * hid] = w1_ref[pos * HEAD_DIM:(pos + 1) * HEAD_DIM, :]


def _compress_kernel(kc_ref, vc_ref, pek_ref, w1k_ref, b1k_ref, w2k_ref,
                     pev_ref, w1v_ref, b1v_ref, w2v_ref, kcmp_ref, vcmpt_ref,
                     w1dk_ref, w2dk_ref, w1dv_ref, w2dv_ref):
    @pl.when(pl.program_id(0) == 0)
    def _():
        _group_diagonal(w1k_ref, w2k_ref, w1dk_ref, w2dk_ref)
        _group_diagonal(w1v_ref, w2v_ref, w1dv_ref, w2dv_ref)

    kc = _compress_one(kc_ref, pek_ref, w1dk_ref, b1k_ref, w2dk_ref)
    vc = _compress_one(vc_ref, pev_ref, w1dv_ref, b1v_ref, w2dv_ref)
    for g in range(KV_HEADS):
        kcmp_ref[g] = kc[:, g * LANES:(g + 1) * LANES].astype(BF16)
        vcmpt_ref[g] = vc[:, g * LANES:(g + 1) * LANES].T[:HEAD_DIM, :].astype(BF16)


def _compress(kc, vc, wk, wv, batch, seq):
    nchunk = seq // CMP_STRIDE
    hid = wk[1].shape[1]
    src = pl.BlockSpec((seq, KV_DIM), lambda b: (b, 0))
    full = lambda a: pl.BlockSpec(a.shape, lambda b: (0,) * a.ndim)
    return pl.pallas_call(
        _compress_kernel,
        grid=(batch,),
        in_specs=[src, src] + [full(a) for a in wk] + [full(a) for a in wv],
        out_specs=[pl.BlockSpec((None, KV_HEADS, nchunk, LANES), lambda b: (b, 0, 0, 0)),
                   pl.BlockSpec((None, KV_HEADS, HEAD_DIM, nchunk), lambda b: (b, 0, 0, 0))],
        out_shape=[jax.ShapeDtypeStruct((batch, KV_HEADS, nchunk, LANES), BF16),
                   jax.ShapeDtypeStruct((batch, KV_HEADS, HEAD_DIM, nchunk), BF16)],
        scratch_shapes=[pltpu.VMEM((CMP_BLOCK // CMP_STRIDE, CMP_STRIDE // CMP_PAIR, CMP_PAIR * KV_DIM, KV_HEADS * hid), BF16),
                        pltpu.VMEM((KV_HEADS * hid, KV_HEADS * LANES), BF16)] * 2,
        compiler_params=pltpu.CompilerParams(dimension_semantics=("arbitrary",)),
        name="compress",
    )(kc, vc, *wk, *wv)


MASK_COL = HEAD_DIM


def _select_blocks_t(imp, t0):
    nb, cols = imp.shape
    j = lax.broadcasted_iota(jnp.int32, (nb, cols), 0)
    cur = (t0 + lax.broadcasted_iota(jnp.int32, (nb, cols), 1)) >> int(np.log2(SLC_BLOCK))
    forced = (j == 0) | (j == cur) | (j == cur - 1)
    score = jnp.where(forced, BIG, jnp.where(j <= cur, imp, -BIG))
    sub = 8
    tiles = [score[v * sub:(v + 1) * sub] for v in range(nb // sub)]
    ranks = [jnp.zeros((sub, cols), F32) for _ in tiles]
    jl = lax.broadcasted_iota(jnp.int32, (sub, cols), 0)
    for jp in range(nb):
        row = score[jp:jp + 1, :]
        for v, tile in enumerate(tiles):
            if v * sub > jp:
                beats = row >= tile
            elif (v + 1) * sub - 1 <= jp:
                beats = row > tile
            else:
                beats = (row > tile) | ((row == tile) & (jl > jp - v * sub))
            ranks[v] = jnp.where(beats, ranks[v] + 1.0, ranks[v])
    rank = jnp.concatenate(ranks, axis=0)
    return jnp.where((rank < float(min(SLC_TOP_N, nb))) & (j <= cur), 1.0, 0.0)


def _put_heads(o_ref, g, o, nq=QBLK):
    for i in range(REP // 2):
        pair = jnp.concatenate([o[:, (2 * i) * nq:(2 * i + 1) * nq],
                                o[:, (2 * i + 1) * nq:(2 * i + 2) * nq]], axis=0)
        col = (g * REP + 2 * i) * HEAD_DIM
        o_ref[:, col:col + LANES] = pair.T.astype(o_ref.dtype)


class _Pipeline:
    def __init__(self, k_ref, vt_ref, q_ref, s_ref, p_ref, acc_ref, chunk_of):
        self.k_ref, self.vt_ref, self.q_ref = k_ref, vt_ref, q_ref
        self.s_ref, self.p_ref, self.acc_ref = s_ref, p_ref, acc_ref
        self.chunk_of = chunk_of
        self.groups = range(KV_HEADS)

    def qk(self, g, j):
        start = pl.multiple_of(self.chunk_of(j) * KSTEP, KSTEP)
        depth = self.q_ref.shape[1]
        return _dot(self.k_ref[pl.ds(start, KSTEP), g * LANES:g * LANES + depth], self.q_ref[g])

    def keep(self, g, st, bias):
        if bias is not None:
            st = st + bias
        self.s_ref[g] = st
        return jnp.max(st, axis=0, keepdims=True)

    def softmax(self, g, m, cmax):
        m_new = jnp.maximum(m, cmax)
        a = jnp.exp2(m - m_new)
        self.p_ref[g] = jnp.exp2(self.s_ref[g] - m_new).astype(BF16)
        return m_new, a

    def values(self, g, j, a):
        vt = self.vt_ref[self.chunk_of(j), g * VROWS:(g + 1) * VROWS, :]
        self.acc_ref[g] = a * self.acc_ref[g] + _dot(vt, self.p_ref[g])

    def start(self, bias0, bias1):
        m0 = jnp.full((1, self.s_ref.shape[2]), NEG, F32)
        cmax = [self.keep(g, self.qk(g, 0), bias0) for g in self.groups]
        st1 = [self.qk(g, 1) for g in self.groups]
        sm = [self.softmax(g, m0, cmax[g]) for g in self.groups]
        cmax = tuple(self.keep(g, st1[g], bias1) for g in self.groups)
        for g in self.groups:
            self.acc_ref[g] = jnp.zeros(self.acc_ref.shape[1:], F32)
        return (tuple(s[0] for s in sm), tuple(s[1] for s in sm), cmax)

    def step(self, j, state, last=False, bias=None):
        m, a, cmax = state
        st = None if last else [self.qk(g, j + 2) for g in self.groups]
        for g in self.groups:
            self.values(g, j, a[g])
        sm = [self.softmax(g, m[g], cmax[g]) for g in self.groups]
        if not last:
            cmax = tuple(self.keep(g, st[g], bias) for g in self.groups)
        return (tuple(s[0] for s in sm), tuple(s[1] for s in sm), cmax)

    def finish(self, count, state):
        _, a, _ = state
        out = []
        for g in self.groups:
            self.values(g, count - 1, a[g])
            acc = self.acc_ref[g]
            out.append(acc[:HEAD_DIM] * (1.0 / acc[HEAD_DIM:HEAD_DIM + 1]))
        return out


ATTN_BLOCKS = 2


def _attn_kernel(sink_ref, abias_ref, pbias_ref, cbias_ref, ovt_ref,
                 qa_ref, ka_ref, vat_ref, qn_ref, qnr_ref, gnt_ref, kcmp_ref, vcmpt_ref,
                 ksl_ref, vslt_ref, kw_ref, vwt_ref, oa_ref, ob_ref, *scratch):
    tiles = QBLK // LANES
    for i in range(ATTN_BLOCKS):
        tok = lambda ref: ref.at[pl.ds(i * tiles, tiles)]
        out = lambda ref: ref.at[pl.ds(i * QBLK, QBLK), :]
        _attn_block(pl.program_id(1) * ATTN_BLOCKS + i, sink_ref, abias_ref, pbias_ref, cbias_ref, ovt_ref,
                    tok(qa_ref), ka_ref, vat_ref, tok(qn_ref), tok(qnr_ref), tok(gnt_ref), kcmp_ref, vcmpt_ref,
                    ksl_ref, vslt_ref, kw_ref, vwt_ref, out(oa_ref), out(ob_ref), *scratch)


def _attn_block(n, sink_ref, abias_ref, pbias_ref, cbias_ref, ovt_ref,
                qa_ref, ka_ref, vat_ref, qn_ref, qnr_ref, gnt_ref, kcmp_ref, vcmpt_ref,
                ksl_ref, vslt_ref, kw_ref, vwt_ref,
                oa_ref, ob_ref,
                qw_ref, sw_ref, pw_ref, accw_ref, qs_ref, ss_ref, ps_ref, accs_ref, sa_ref, pa_ref, sc_ref):
    groups = range(KV_HEADS)
    width = REP * QBLK
    nb = ovt_ref.shape[0]
    nc = kcmp_ref.shape[1]
    t0 = pl.multiple_of(n * QBLK, QBLK)
    heads = lambda g: [g * REP + r for r in range(REP)]
    parts = range(QBLK // LANES)
    head_t = lambda ref, h, i: ref[i, h * HEAD_DIM:(h + 1) * HEAD_DIM, :]
    cols = lambda ref, g: jnp.concatenate([head_t(ref, h, i) for h in heads(g) for i in parts], axis=1)
    lane_chunks = lambda a: [a[:, r * QBLK:(r + 1) * QBLK] for r in range(REP)]
    gone = lambda cond: jnp.where(cond, 0.0, NEG)

    assert B_WINDOW == 2 * KSTEP
    win = _Pipeline(kw_ref, vwt_ref, qw_ref, sw_ref, pw_ref, accw_ref, lambda j: jnp.maximum(n - j, 0))
    slc = _Pipeline(ksl_ref, vslt_ref, qs_ref, ss_ref, ps_ref, accs_ref,
                    lambda j: jnp.where(j == 0, n, jnp.maximum(j - 1, 0)))
    slc_count = jnp.maximum(n, 1) + 1

    for g in groups:
        qw_ref[g] = cols(qnr_ref, g)
    a_chunk = [n * len(parts) + i for i in parts]
    a_first = [jnp.maximum(c - 1, 0) for c in a_chunk]
    part_cols = lambda i, g: jnp.concatenate([head_t(qa_ref, h, i) for h in heads(g)], axis=1)
    amax = {}
    for i in parts:
        abias = abias_ref[jnp.minimum(a_chunk[i], 1)]
        astart = pl.multiple_of(a_first[i] * A_WINDOW, A_WINDOW)
        for g in groups:
            st = _dot(ka_ref[pl.ds(astart, 2 * A_WINDOW), g * LANES:g * LANES + HEAD_DIM], part_cols(i, g)) + abias
            sa_ref[i, g] = st
            amax[i, g] = jnp.max(st, axis=0, keepdims=True)
    cstart = pl.multiple_of(cbias_ref.shape[0] - nc - n * (QBLK // CMP_STRIDE), 8)
    cbias = cbias_ref[pl.ds(cstart, nc), :]
    cmax = []
    for g in groups:
        s = _dot(kcmp_ref[g, :, :HEAD_DIM], cols(qn_ref, g)) + cbias
        sc_ref[g] = s
        cmax.append(jnp.max(s, axis=0, keepdims=True))
    win_state = win.start(pbias_ref[0], gone(n >= 1))

    x_swa = {}
    for i in parts:
        for g in groups:
            sk = jnp.concatenate([jnp.full((1, A_WINDOW), sink_ref[h] * LOG2E, F32) for h in heads(g)], axis=1)
            m = jnp.maximum(amax[i, g], sk)
            pa_ref[i, g] = jnp.exp2(sa_ref[i, g] - m).astype(BF16)
            x_swa[i, g] = jnp.exp2(sk - m)
    tq = t0 + (lax.broadcasted_iota(jnp.int32, (1, width), 1) & (QBLK - 1))
    sees_any = tq >= CMP_BLOCK - 1
    p_cmp = []
    for g in groups:
        e = jnp.exp2(sc_ref[g] - cmax[g])
        inv = jnp.where(sees_any, 1.0 / jnp.maximum(jnp.sum(e, axis=0, keepdims=True), 1e-30), 0.0)
        p_cmp.append(e * inv)

    win_state = win.step(0, win_state, bias=pbias_ref[1] + gone(n >= 2))

    for i in parts:
        for g in groups:
            vt = jnp.concatenate([vat_ref[a_first[i] + k, g * VROWS:(g + 1) * VROWS, :] for k in range(2)], axis=1)
            o = _dot(vt, pa_ref[i, g])
            o = o[:HEAD_DIM] * (1.0 / (o[HEAD_DIM:HEAD_DIM + 1] + x_swa[i, g]))
            _put_heads(oa_ref.at[i * A_WINDOW:(i + 1) * A_WINDOW, :], g, o, A_WINDOW)
    o_cmp = [_dot(vcmpt_ref[g], p_cmp[g].astype(BF16)) for g in groups]
    imp = []
    for g in groups:
        pc = lane_chunks(p_cmp[g])
        psum = (pc[0] + pc[1]) + (pc[2] + pc[3])
        hi = psum.astype(BF16)
        lo = (psum - hi.astype(F32)).astype(BF16)
        imp.append(_dot(ovt_ref[...], hi) + _dot(ovt_ref[...], lo))
    o_win = win.finish(3, win.step(1, win_state, last=True))

    jb = lax.broadcasted_iota(jnp.int32, (nb, QBLK), 0)
    curb = (t0 + lax.broadcasted_iota(jnp.int32, (nb, QBLK), 1)) >> int(np.log2(SLC_BLOCK))
    sel = lax.cond((t0 + QBLK - 1) // SLC_BLOCK < SLC_TOP_N,
                   lambda: tuple(jnp.where(jb <= curb, 1.0, 0.0) for _ in groups),
                   lambda: tuple(_select_blocks_t(imp[g], t0) for g in groups))

    for g in groups:
        qs_ref[g, :HEAD_DIM, :] = cols(qnr_ref, g)
        mrows = jnp.concatenate([sel[g] - 1.0, jnp.zeros((LANES - MASK_COL - nb, QBLK), F32)], axis=0).astype(BF16)
        for r in range(REP):
            qs_ref[g, MASK_COL:, r * QBLK:(r + 1) * QBLK] = mrows
    slc_state = slc.start(pbias_ref[0], gone(n >= 1))

    no_bias = gone(n >= 0)
    one_step = lambda j, s: slc.step(j, s, bias=no_bias)
    trips = slc_count - 2
    slc_state = lax.fori_loop(0, trips // 2, lambda i, s: one_step(2 * i + 1, one_step(2 * i, s)), slc_state)
    slc_state = lax.cond(trips % 2 == 1, lambda s: one_step(trips - 1, s), lambda s: s, slc_state)

    o_slc = slc.finish(slc_count, slc.step(slc_count - 2, slc_state, last=True))
    for g in groups:
        gate = lambda br: jnp.concatenate(
            [gnt_ref[i, br * Q_HEADS + h:br * Q_HEADS + h + 1, :]
             for h in heads(g) for i in range(gnt_ref.shape[0])], axis=1)
        _put_heads(ob_ref, g, gate(0) * o_cmp[g] + gate(1) * o_slc[g] + gate(2) * o_win[g])


def _attn(sinks, qa, ka, vat, qn, qnr, gnt, kcmp, vcmpt, ksl, vslt, kw, vwt, batch, seq):
    nq = seq // QBLK
    nc = kcmp.shape[2]
    nb = seq // SLC_BLOCK
    width = REP * QBLK
    assert nb <= LANES - MASK_COL and QBLK % SLC_BLOCK == 0 and QBLK % A_WINDOW == 0 and A_WINDOW % LANES == 0
    c_start = np.arange(nc)[None, :] * CMP_STRIDE
    s_start = np.arange(nb)[:, None] * SLC_BLOCK
    ovt = jnp.asarray((c_start < s_start + SLC_BLOCK) & (s_start < c_start + CMP_BLOCK), BF16)
    ql = np.arange(width)[None, :] % QBLK
    masked = lambda visible: np.where(visible, 0.0, NEG)
    kq = np.arange(KSTEP)[:, None] - ql
    pbias = jnp.asarray(np.stack([masked(kq <= 0),
                                  masked(kq > 0)]), F32)
    band = np.arange(2 * A_WINDOW)[:, None]
    qpart = np.arange(REP * A_WINDOW)[None, :] % A_WINDOW
    in_window = lambda dist: (dist >= 0) & (dist < A_WINDOW)
    abias = jnp.asarray(np.stack([masked(in_window(qpart - band)),
                                  masked(in_window(qpart - (band - A_WINDOW)))]), F32)
    per_blk = QBLK // CMP_STRIDE
    d = np.arange(nc + (nq - 1) * per_blk)[:, None] - (nq - 1) * per_blk
    cbias = jnp.asarray(masked(d * CMP_STRIDE + CMP_BLOCK - 1 <= ql), F32)
    assert nq % ATTN_BLOCKS == 0
    steps = nq // ATTN_BLOCKS
    tiles = ATTN_BLOCKS * QBLK // LANES
    q_spec = pl.BlockSpec((None, tiles, Q_DIM, LANES), lambda b, n: (b, n, 0, 0))
    g_spec = pl.BlockSpec((None, tiles, LANES, LANES), lambda b, n: (b, n, 0, 0))
    kc_spec = pl.BlockSpec((None, KV_HEADS, nc, LANES), lambda b, n: (b, 0, 0, 0))
    vc_spec = pl.BlockSpec((None, KV_HEADS, HEAD_DIM, nc), lambda b, n: (b, 0, 0, 0))
    const = lambda a: pl.BlockSpec(a.shape, lambda b, n: (0,) * a.ndim, pipeline_mode=pl.Buffered(1))
    k_spec = pl.BlockSpec((seq, KV_HEADS * LANES), lambda b, n: (b, 0))
    vt_spec = lambda keys: pl.BlockSpec((None, seq // keys, KV_HEADS * VROWS, keys), lambda b, n: (b, 0, 0, 0))
    o_spec = pl.BlockSpec((ATTN_BLOCKS * QBLK, Q_DIM), lambda b, n: (b * steps + n, 0))
    o_shape = jax.ShapeDtypeStruct((batch * seq, Q_DIM), BF16)
    pipe_scratch = lambda depth: [
        pltpu.VMEM((KV_HEADS, depth, width), BF16),
        pltpu.VMEM((KV_HEADS, KSTEP, width), F32),
        pltpu.VMEM((KV_HEADS, KSTEP, width), BF16),
        pltpu.VMEM((KV_HEADS, VROWS, width), F32)]
    return pl.pallas_call(
        _attn_kernel,
        grid=(batch, steps),
        in_specs=[pl.BlockSpec(memory_space=pltpu.SMEM), const(abias), const(pbias), const(cbias), const(ovt),
                  q_spec, k_spec, vt_spec(A_WINDOW), q_spec, q_spec, g_spec, kc_spec, vc_spec,
                  k_spec, vt_spec(KSTEP), k_spec, vt_spec(KSTEP)],
        out_specs=[o_spec, o_spec],
        out_shape=[o_shape, o_shape],
        scratch_shapes=pipe_scratch(HEAD_DIM) + pipe_scratch(LANES) + [
            pltpu.VMEM((QBLK // A_WINDOW, KV_HEADS, 2 * A_WINDOW, REP * A_WINDOW), F32),
            pltpu.VMEM((QBLK // A_WINDOW, KV_HEADS, 2 * A_WINDOW, REP * A_WINDOW), BF16),
            pltpu.VMEM((KV_HEADS, nc, width), F32)],
        compiler_params=pltpu.CompilerParams(dimension_semantics=("parallel", "arbitrary"),
                                             vmem_limit_bytes=VMEM_WIDE),
        name="attn",
    )(sinks, abias, pbias, cbias, ovt, qa, ka, vat, qn, qnr, gnt, kcmp, vcmpt, ksl, vslt, kw, vwt)


def _layer_norm(r, g, b):
    mu = jnp.mean(r, axis=-1, keepdims=True)
    d = r - mu
    var = jnp.mean(d * d, axis=-1, keepdims=True)
    return d * lax.rsqrt(var + LN_EPS) * g + b


POST_PARTS = 4


def _post_kernel(alpha, x_ref, oa_ref, ob_ref, wgm_ref, wpa_ref, wpb_ref, wout_ref, g_ref, b_ref, h_ref):
    d = x_ref.shape[1]
    rows = [slice(i * (x_ref.shape[0] // POST_PARTS), (i + 1) * (x_ref.shape[0] // POST_PARTS))
            for i in range(POST_PARTS)]
    ys = []
    for r in rows:
        xb = x_ref[r, :].astype(BF16)
        pa = _dot(oa_ref[r, :], wpa_ref[...])
        pb = _dot(ob_ref[r, :], wpb_ref[...])
        ys.append(jax.nn.sigmoid(_dot(xb, wgm_ref[:, :d])) * pa + jax.nn.sigmoid(_dot(xb, wgm_ref[:, d:])) * pb)
    for r, y in zip(rows, ys):
        m = _dot(y.astype(BF16), wout_ref[...])
        h_ref[r, :] = _layer_norm(alpha * x_ref[r, :] + m, g_ref[...], b_ref[...])


FFN_CHUNK = 256


FFN_PARTS = 2


def _ffn_kernel(alpha, h_ref, wg_ref, wu_ref, wd_ref, g_ref, b_ref, o_ref):
    hidden = wg_ref.shape[1]
    rows = h_ref.shape[0] // FFN_PARTS
    for p in range(FFN_PARTS):
        r = slice(p * rows, (p + 1) * rows)
        h = h_ref[r, :]
        hb = h.astype(BF16)
        acc = jnp.zeros(h.shape, F32)
        for c in range(hidden // FFN_CHUNK):
            sl = slice(c * FFN_CHUNK, (c + 1) * FFN_CHUNK)
            a = jax.nn.silu(_dot(hb, wg_ref[:, sl])) * _dot(hb, wu_ref[:, sl])
            acc = acc + _dot(a.astype(BF16), wd_ref[sl, :])
        o_ref[r, :] = _layer_norm(alpha * h + acc, g_ref[...], b_ref[...])


def _tail_kernel(alpha, x_ref, oa_ref, ob_ref, wgm_ref, wpa_ref, wpb_ref, wout_ref, g1_ref, b1_ref,
                 wg_ref, wu_ref, wd_ref, g2_ref, b2_ref, o_ref, h_ref):
    _post_kernel(alpha, x_ref, oa_ref, ob_ref, wgm_ref, wpa_ref, wpb_ref, wout_ref, g1_ref, b1_ref, h_ref)
    _ffn_kernel(alpha, h_ref, wg_ref, wu_ref, wd_ref, g2_ref, b2_ref, o_ref)


def _tail(x2, oa, ob, post_w, ffn_w, alpha, tm):
    n, d = x2.shape
    tok = lambda width: pl.BlockSpec((tm, width), lambda i: (i, 0))
    once = lambda a: pl.BlockSpec(a.shape, lambda i: (0, 0), pipeline_mode=pl.Buffered(1))
    return pl.pallas_call(
        functools.partial(_tail_kernel, alpha),
        grid=(n // tm,),
        in_specs=[tok(d), tok(Q_DIM), tok(Q_DIM)] + [once(a) for a in post_w + ffn_w],
        out_specs=tok(d),
        out_shape=jax.ShapeDtypeStruct((n, d), F32),
        scratch_shapes=[pltpu.VMEM((tm, d), F32)],
        compiler_params=pltpu.CompilerParams(dimension_semantics=("parallel",),
                                             vmem_limit_bytes=VMEM_WIDE),
        name="tail",
    )(x2, oa, ob, *post_w, *ffn_w)


def _position_tables(seq):
    half = HEAD_DIM // 2
    inv = ROPE_THETA ** (-jnp.arange(half, dtype=F32) / half)
    ang = jnp.arange(seq).astype(F32)[:, None] * inv[None, :]
    cos, sin = jnp.cos(ang), jnp.sin(ang)
    reps = LANES // HEAD_DIM
    lane = np.arange(LANES)[None, :]
    blk = (np.arange(seq) // SLC_BLOCK)[:, None]
    ke = jnp.asarray(np.where(lane - MASK_COL == blk, -NEG, 0.0), F32)
    return jnp.tile(cos, (1, 2 * reps)), jnp.tile(jnp.concatenate([-sin, sin], axis=1), (1, reps)), ke


def kernel(x, w_in, sinks, cmp_pe_k, cmp_w1_k, cmp_b1_k, cmp_w2_k, cmp_pe_v, cmp_w1_v, cmp_b1_v, cmp_w2_v,
           w_proj_a, w_proj_b, w_out, ln1_g, ln1_b, w_gate, w_up, w_down, ln2_g, ln2_b):
    batch, seq, d = x.shape
    depth = w_in.shape[0]
    alpha = (2 * depth) ** 0.25
    n_main = 2 * Q_DIM + 8 * KV_DIM
    n_gate = 3 * Q_HEADS
    tm = ROW_TILE
    cos, sin, ke = _position_tables(seq)
    pad_cols = lambda w: jnp.pad(w, ((0, 0), (0, LANES - w.shape[1]))).astype(BF16)
    xt = x.reshape(batch * seq, d)
    for l in range(depth):
        w_main = w_in[l, :, :n_main].astype(BF16)
        w_gn = pad_cols(w_in[l, :, n_main:n_main + n_gate])
        w_gm = w_in[l, :, n_main + n_gate:].astype(BF16)
        (qa, ka, vat, qn, qnr, kc, vc, ksl, vslt, kw, vwt, gnt) = _in_proj(
            xt, w_main, w_gn, cos, sin, ke, batch, seq, tm)
        wk = (cmp_pe_k[l], cmp_w1_k[l].astype(BF16), cmp_b1_k[l][None, :], pad_cols(cmp_w2_k[l]))
        wv = (cmp_pe_v[l], cmp_w1_v[l].astype(BF16), cmp_b1_v[l][None, :], pad_cols(cmp_w2_v[l]))
        kcmp, vcmpt = _compress(kc, vc, wk, wv, batch, seq)
        oa, ob = _attn(sinks[l], qa, ka, vat, qn, qnr, gnt, kcmp, vcmpt, ksl, vslt, kw, vwt, batch, seq)
        post_w = (w_gm, w_proj_a[l].astype(BF16), w_proj_b[l].astype(BF16), w_out[l].astype(BF16),
                  ln1_g[l][None, :], ln1_b[l][None, :])
        ffn_w = (w_gate[l].astype(BF16), w_up[l].astype(BF16), w_down[l].astype(BF16),
                 ln2_g[l][None, :], ln2_b[l][None, :])
        xt = _tail(xt, oa, ob, post_w, ffn_w, alpha, tm)
    return xt.reshape(batch, seq, d)
```

```python
import functools

import jax
import jax.numpy as jnp
import numpy as np
from jax import lax
from jax.experimental import pallas as pl
from jax.experimental.pallas import tpu as pltpu

HEAD_DIM = 64
ROPE_THETA = 10000.0
Q_HEADS = 8
KV_HEADS = 2
REP = Q_HEADS // KV_HEADS
A_WINDOW = 128
B_WINDOW = 512
CMP_BLOCK = 32
CMP_STRIDE = 16
SLC_BLOCK = 64
SLC_TOP_N = 16
LN_EPS = 1e-5
NEG = -1e30
BIG = 1e9
Q_DIM = Q_HEADS * HEAD_DIM
KV_DIM = KV_HEADS * HEAD_DIM
LANES = 128
QBLK = 256
KSTEP = QBLK
LOG2E = 1.4426950408889634
SCALE = HEAD_DIM ** -0.5 * LOG2E
VROWS = HEAD_DIM + 16
ROW_TILE = 1024
MIB = 1024 * 1024
VMEM_PROJ = 48 * MIB
VMEM_WIDE = 56 * MIB

F32 = jnp.float32
BF16 = jnp.bfloat16


def _dot(a, b):
    return jnp.dot(a, b, preferred_element_type=F32)


def _inproj_kernel(x_ref, w_ref, wg_ref, cos_ref, sin_ref, cost_ref, sint_ref, ke_ref,
                   qa_ref, ka_ref, vat_ref, qn_ref, qnr_ref, kc_ref, vc_ref,
                   ksl_ref, vslt_ref, kw_ref, vwt_ref, gnt_ref):
    tm = x_ref.shape[0]
    xb = x_ref[...].astype(BF16)
    cos = cos_ref[...]
    sin = sin_ref[...]
    lane = lax.broadcasted_iota(jnp.int32, (tm, LANES), 1)
    first_half = (lane & (HEAD_DIM - 1)) < (HEAD_DIM // 2)
    low = lane < HEAD_DIM

    def rope(z):
        sw = jnp.where(first_half, pltpu.roll(z, LANES - HEAD_DIM // 2, 1), pltpu.roll(z, HEAD_DIM // 2, 1))
        return z * cos + sw * sin

    def proj(c0, n):
        return _dot(xb, w_ref[:, c0:c0 + n])

    def chunks(z):
        return [z[:, c * LANES:(c + 1) * LANES] for c in range(z.shape[1] // LANES)]

    def put_padded(ref, c, z, fill):
        ref[:, (2 * c) * LANES:(2 * c + 1) * LANES] = jnp.where(low, z, fill).astype(ref.dtype)
        ref[:, (2 * c + 1) * LANES:(2 * c + 2) * LANES] = jnp.where(low, pltpu.roll(z, HEAD_DIM, 1), fill).astype(ref.dtype)

    def put_transposed(ref, z):
        for j in range(tm // LANES):
            ref[j] = z[j * LANES:(j + 1) * LANES, :].T.astype(ref.dtype)

    def put_heads_t(c, z, plain_ref, rope_ref):
        half = HEAD_DIM // 2
        for j in range(tm // LANES):
            zt = z[j * LANES:(j + 1) * LANES, :].T * SCALE
            rows = slice(c * LANES, (c + 1) * LANES)
            if plain_ref is not None:
                plain_ref[j, rows, :] = zt.astype(plain_ref.dtype)
            sw = jnp.concatenate([zt[(s ^ 1) * half:((s ^ 1) + 1) * half] for s in range(LANES // half)], axis=0)
            rope_ref[j, rows, :] = (zt * cost_ref[j] + sw * sint_ref[j]).astype(rope_ref.dtype)

    ones_row = jnp.where(lax.broadcasted_iota(jnp.int32, (VROWS - HEAD_DIM, LANES), 0) == 0, 1.0, 0.0)

    def put_values_t(ref, z):
        keys = ref.shape[2]
        for j in range(tm // keys):
            for i in range(keys // LANES):
                zt = z[j * keys + i * LANES:j * keys + (i + 1) * LANES, :].T
                cols = slice(i * LANES, (i + 1) * LANES)
                for g in range(KV_HEADS):
                    ref[j, g * VROWS:g * VROWS + HEAD_DIM, cols] = zt[g * HEAD_DIM:(g + 1) * HEAD_DIM, :].astype(ref.dtype)
                    ref[j, g * VROWS + HEAD_DIM:(g + 1) * VROWS, cols] = ones_row.astype(ref.dtype)

    for c, zc in enumerate(chunks(proj(0, Q_DIM))):
        put_heads_t(c, zc, None, qa_ref)
    z = proj(Q_DIM, 2 * KV_DIM)
    put_padded(ka_ref, 0, rope(z[:, :KV_DIM]), 0.0)
    put_values_t(vat_ref, z[:, KV_DIM:])
    c0 = Q_DIM + 2 * KV_DIM
    for c, zc in enumerate(chunks(proj(c0, Q_DIM))):
        put_heads_t(c, zc, qn_ref, qnr_ref)
    c0 += Q_DIM
    z = proj(c0, 2 * KV_DIM)
    kc_ref[...] = z[:, :KV_DIM]
    vc_ref[...] = z[:, KV_DIM:]
    c0 += 2 * KV_DIM
    z = proj(c0, 2 * KV_DIM)
    put_padded(ksl_ref, 0, rope(z[:, :KV_DIM]), ke_ref[...])
    put_values_t(vslt_ref, z[:, KV_DIM:])
    c0 += 2 * KV_DIM
    z = proj(c0, 2 * KV_DIM)
    put_padded(kw_ref, 0, rope(z[:, :KV_DIM]), 0.0)
    put_values_t(vwt_ref, z[:, KV_DIM:])
    put_transposed(gnt_ref, jax.nn.sigmoid(_dot(xb, wg_ref[...])))


def _in_proj(x2, w_main, w_gn, cos, sin, ke, batch, seq, tm):
    n = x2.shape[0]
    d = x2.shape[1]
    spt = seq // tm
    tok = lambda width: pl.BlockSpec((tm, width), lambda i: (i, 0))
    full = lambda a: pl.BlockSpec(a.shape, lambda i: (0, 0))
    tab = pl.BlockSpec((tm, LANES), lambda i: (i % spt, 0))
    tab_t = pl.BlockSpec((tm // LANES, LANES, LANES), lambda i: (i % spt, 0, 0))
    cos_t, sin_t = (jnp.transpose(t.reshape(seq // LANES, LANES, LANES), (0, 2, 1)) for t in (cos, sin))
    flat = lambda w, dt: (tok(w), jax.ShapeDtypeStruct((n, w), dt))
    trans = lambda r, keys, dt: (pl.BlockSpec((None, tm // keys, r, keys), lambda i: (i // spt, i % spt, 0, 0)),
                                 jax.ShapeDtypeStruct((batch, seq // keys, r, keys), dt))
    outs = [trans(Q_DIM, LANES, BF16), flat(KV_HEADS * LANES, BF16),
            trans(KV_HEADS * VROWS, A_WINDOW, BF16),
            trans(Q_DIM, LANES, BF16), trans(Q_DIM, LANES, BF16),
            flat(KV_DIM, F32), flat(KV_DIM, F32),
            flat(KV_HEADS * LANES, BF16), trans(KV_HEADS * VROWS, KSTEP, BF16),
            flat(KV_HEADS * LANES, BF16), trans(KV_HEADS * VROWS, KSTEP, BF16),
            trans(LANES, LANES, F32)]
    return pl.pallas_call(
        _inproj_kernel,
        grid=(n // tm,),
        in_specs=[tok(d), full(w_main), full(w_gn), tab, tab, tab_t, tab_t, tab],
        out_specs=[o[0] for o in outs],
        out_shape=[o[1] for o in outs],
        compiler_params=pltpu.CompilerParams(dimension_semantics=("parallel",),
                                             vmem_limit_bytes=VMEM_PROJ),
        name="in_proj",
    )(x2, w_main, w_gn, cos, sin, cos_t, sin_t, ke)


CMP_PAIR = 2


def _compress_one(src_ref, pe_ref, w1_ref, b1_ref, w2_ref):
    ratio = CMP_BLOCK // CMP_STRIDE
    nchunk = src_ref.shape[0] // CMP_STRIDE
    both = lambda row: jnp.concatenate([row] * KV_HEADS, axis=1)
    parts = [jnp.zeros((nchunk, w1_ref.shape[3]), F32) for _ in range(ratio)]
    for p in range(CMP_STRIDE // CMP_PAIR):
        rows = [src_ref[pl.ds(p * CMP_PAIR + i, nchunk, stride=CMP_STRIDE), :] for i in range(CMP_PAIR)]
        for j in range(ratio):
            pos = j * CMP_STRIDE + p * CMP_PAIR
            a = jnp.concatenate([(rows[i] + both(pe_ref[pos + i:pos + i + 1, :])).astype(BF16)
                                 for i in range(CMP_PAIR)], axis=1)
            parts[j] = parts[j] + _dot(a, w1_ref[j, p])
    h = parts[0]
    for j in range(1, ratio):
        h = h + pltpu.roll(parts[j], nchunk - j, 0)
    h = jax.nn.gelu(h + both(b1_ref[...]))
    return _dot(h.astype(BF16), w2_ref[...])


def _group_diagonal(w1_ref, w2_ref, w1d_ref, w2d_ref):
    hid = w1_ref.shape[1]
    w1d_ref[...] = jnp.zeros(w1d_ref.shape, w1d_ref.dtype)
    w2d_ref[...] = jnp.zeros(w2d_ref.shape, w2d_ref.dtype)
    for g in range(KV_HEADS):
        w2d_ref[g * hid:(g + 1) * hid, g * LANES:(g + 1) * LANES] = w2_ref[...]
        for j in range(CMP_BLOCK // CMP_STRIDE):
            for p in range(CMP_STRIDE // CMP_PAIR):
                for i in range(CMP_PAIR):
                    pos = j * CMP_STRIDE + p * CMP_PAIR + i
                    row = i * KV_DIM + g * HEAD_DIM
                    w1d_ref[j, p, row:row + HEAD_DIM, g * hid:(g + 1) * hid] = w1_ref[pos * HEAD_DIM:(pos + 1) * HEAD_DIM, :]


def _compress_kernel(kc_ref, vc_ref, pek_ref, w1k_ref, b1k_ref, w2k_ref,
                     pev_ref, w1v_ref, b1v_ref, w2v_ref, kcmp_ref, vcmpt_ref,
                     w1dk_ref, w2dk_ref, w1dv_ref, w2dv_ref):
    @pl.when(pl.program_id(0) == 0)
    def _():
        _group_diagonal(w1k_ref, w2k_ref, w1dk_ref, w2dk_ref)
        _group_diagonal(w1v_ref, w2v_ref, w1dv_ref, w2dv_ref)

    kc = _compress_one(kc_ref, pek_ref, w1dk_ref, b1k_ref, w2dk_ref)
    vc = _compress_one(vc_ref, pev_ref, w1dv_ref, b1v_ref, w2dv_ref)
    for g in range(KV_HEADS):
        kcmp_ref[g] = kc[:, g * LANES:(g + 1) * LANES].astype(BF16)
        vcmpt_ref[g] = vc[:, g * LANES:(g + 1) * LANES].T[:HEAD_DIM, :].astype(BF16)


def _compress(kc, vc, wk, wv, batch, seq):
    nchunk = seq // CMP_STRIDE
    hid = wk[1].shape[1]
    src = pl.BlockSpec((seq, KV_DIM), lambda b: (b, 0))
    full = lambda a: pl.BlockSpec(a.shape, lambda b: (0,) * a.ndim)
    return pl.pallas_call(
        _compress_kernel,
        grid=(batch,),
        in_specs=[src, src] + [full(a) for a in wk] + [full(a) for a in wv],
        out_specs=[pl.BlockSpec((None, KV_HEADS, nchunk, LANES), lambda b: (b, 0, 0, 0)),
                   pl.BlockSpec((None, KV_HEADS, HEAD_DIM, nchunk), lambda b: (b, 0, 0, 0))],
        out_shape=[jax.ShapeDtypeStruct((batch, KV_HEADS, nchunk, LANES), BF16),
                   jax.ShapeDtypeStruct((batch, KV_HEADS, HEAD_DIM, nchunk), BF16)],
        scratch_shapes=[pltpu.VMEM((CMP_BLOCK // CMP_STRIDE, CMP_STRIDE // CMP_PAIR, CMP_PAIR * KV_DIM, KV_HEADS * hid), BF16),
                        pltpu.VMEM((KV_HEADS * hid, KV_HEADS * LANES), BF16)] * 2,
        compiler_params=pltpu.CompilerParams(dimension_semantics=("arbitrary",)),
        name="compress",
    )(kc, vc, *wk, *wv)


MASK_COL = HEAD_DIM


def _select_blocks_t(imp, t0):
    nb, cols = imp.shape
    j = lax.broadcasted_iota(jnp.int32, (nb, cols), 0)
    cur = (t0 + lax.broadcasted_iota(jnp.int32, (nb, cols), 1)) >> int(np.log2(SLC_BLOCK))
    forced = (j == 0) | (j == cur) | (j == cur - 1)
    score = jnp.where(forced, BIG, jnp.where(j <= cur, imp, -BIG))
    sub = 8
    tiles = [score[v * sub:(v + 1) * sub] for v in range(nb // sub)]
    ranks = [jnp.zeros((sub, cols), F32) for _ in tiles]
    jl = lax.broadcasted_iota(jnp.int32, (sub, cols), 0)
    for jp in range(nb):
        row = score[jp:jp + 1, :]
        for v, tile in enumerate(tiles):
            if v * sub > jp:
                beats = row >= tile
            elif (v + 1) * sub - 1 <= jp:
                beats = row > tile
            else:
                beats = (row > tile) | ((row == tile) & (jl > jp - v * sub))
            ranks[v] = jnp.where(beats, ranks[v] + 1.0, ranks[v])
    rank = jnp.concatenate(ranks, axis=0)
    return jnp.where((rank < float(min(SLC_TOP_N, nb))) & (j <= cur), 1.0, 0.0)


def _put_heads(o_ref, g, o, nq=QBLK):
    for i in range(REP // 2):
        pair = jnp.concatenate([o[:, (2 * i) * nq:(2 * i + 1) * nq],
                                o[:, (2 * i + 1) * nq:(2 * i + 2) * nq]], axis=0)
        col = (g * REP + 2 * i) * HEAD_DIM
        o_ref[:, col:col + LANES] = pair.T.astype(o_ref.dtype)


class _Pipeline:
    def __init__(self, k_ref, vt_ref, q_ref, s_ref, p_ref, acc_ref, chunk_of):
        self.k_ref, self.vt_ref, self.q_ref = k_ref, vt_ref, q_ref
        self.s_ref, self.p_ref, self.acc_ref = s_ref, p_ref, acc_ref
        self.chunk_of = chunk_of
        self.groups = range(KV_HEADS)

    def qk(self, g, j):
        start = pl.multiple_of(self.chunk_of(j) * KSTEP, KSTEP)
        depth = self.q_ref.shape[1]
        return _dot(self.k_ref[pl.ds(start, KSTEP), g * LANES:g * LANES + depth], self.q_ref[g])

    def keep(self, g, st, bias):
        if bias is not None:
            st = st + bias
        self.s_ref[g] = st
        return jnp.max(st, axis=0, keepdims=True)

    def softmax(self, g, m, cmax):
        m_new = jnp.maximum(m, cmax)
        a = jnp.exp2(m - m_new)
        self.p_ref[g] = jnp.exp2(self.s_ref[g] - m_new).astype(BF16)
        return m_new, a

    def values(self, g, j, a):
        vt = self.vt_ref[self.chunk_of(j), g * VROWS:(g + 1) * VROWS, :]
        self.acc_ref[g] = a * self.acc_ref[g] + _dot(vt, self.p_ref[g])

    def start(self, bias0, bias1):
        m0 = jnp.full((1, self.s_ref.shape[2]), NEG, F32)
        cmax = [self.keep(g, self.qk(g, 0), bias0) for g in self.groups]
        st1 = [self.qk(g, 1) for g in self.groups]
        sm = [self.softmax(g, m0, cmax[g]) for g in self.groups]
        cmax = tuple(self.keep(g, st1[g], bias1) for g in self.groups)
        for g in self.groups:
            self.acc_ref[g] = jnp.zeros(self.acc_ref.shape[1:], F32)
        return (tuple(s[0] for s in sm), tuple(s[1] for s in sm), cmax)

    def step(self, j, state, last=False, bias=None):
        m, a, cmax = state
        st = None if last else [self.qk(g, j + 2) for g in self.groups]
        for g in self.groups:
            self.values(g, j, a[g])
        sm = [self.softmax(g, m[g], cmax[g]) for g in self.groups]
        if not last:
            cmax = tuple(self.keep(g, st[g], bias) for g in self.groups)
        return (tuple(s[0] for s in sm), tuple(s[1] for s in sm), cmax)

    def finish(self, count, state):
        _, a, _ = state
        out = []
        for g in self.groups:
            self.values(g, count - 1, a[g])
            acc = self.acc_ref[g]
            out.append(acc[:HEAD_DIM] * (1.0 / acc[HEAD_DIM:HEAD_DIM + 1]))
        return out


ATTN_BLOCKS = 2


def _attn_kernel(sink_ref, abias_ref, pbias_ref, cbias_ref, ovt_ref,
                 qa_ref, ka_ref, vat_ref, qn_ref, qnr_ref, gnt_ref, kcmp_ref, vcmpt_ref,
                 ksl_ref, vslt_ref, kw_ref, vwt_ref, oa_ref, ob_ref, *scratch):
    tiles = QBLK // LANES
    for i in range(ATTN_BLOCKS):
        tok = lambda ref: ref.at[pl.ds(i * tiles, tiles)]
        out = lambda ref: ref.at[pl.ds(i * QBLK, QBLK), :]
        _attn_block(pl.program_id(1) * ATTN_BLOCKS + i, sink_ref, abias_ref, pbias_ref, cbias_ref, ovt_ref,
                    tok(qa_ref), ka_ref, vat_ref, tok(qn_ref), tok(qnr_ref), tok(gnt_ref), kcmp_ref, vcmpt_ref,
                    ksl_ref, vslt_ref, kw_ref, vwt_ref, out(oa_ref), out(ob_ref), *scratch)


def _attn_block(n, sink_ref, abias_ref, pbias_ref, cbias_ref, ovt_ref,
                qa_ref, ka_ref, vat_ref, qn_ref, qnr_ref, gnt_ref, kcmp_ref, vcmpt_ref,
                ksl_ref, vslt_ref, kw_ref, vwt_ref,
                oa_ref, ob_ref,
                qw_ref, sw_ref, pw_ref, accw_ref, qs_ref, ss_ref, ps_ref, accs_ref, sa_ref, pa_ref, sc_ref):
    groups = range(KV_HEADS)
    width = REP * QBLK
    nb = ovt_ref.shape[0]
    nc = kcmp_ref.shape[1]
    t0 = pl.multiple_of(n * QBLK, QBLK)
    heads = lambda g: [g * REP + r for r in range(REP)]
    parts = range(QBLK // LANES)
    head_t = lambda ref, h, i: ref[i, h * HEAD_DIM:(h + 1) * HEAD_DIM, :]
    cols = lambda ref, g: jnp.concatenate([head_t(ref, h, i) for h in heads(g) for i in parts], axis=1)
    lane_chunks = lambda a: [a[:, r * QBLK:(r + 1) * QBLK] for r in range(REP)]
    gone = lambda cond: jnp.where(cond, 0.0, NEG)

    assert B_WINDOW == 2 * KSTEP
    win = _Pipeline(kw_ref, vwt_ref, qw_ref, sw_ref, pw_ref, accw_ref, lambda j: jnp.maximum(n - j, 0))
    slc = _Pipeline(ksl_ref, vslt_ref, qs_ref, ss_ref, ps_ref, accs_ref,
                    lambda j: jnp.where(j == 0, n, jnp.maximum(j - 1, 0)))
    slc_count = jnp.maximum(n, 1) + 1

    for g in groups:
        qw_ref[g] = cols(qnr_ref, g)
    a_chunk = [n * len(parts) + i for i in parts]
    a_first = [jnp.maximum(c - 1, 0) for c in a_chunk]
    part_cols = lambda i, g: jnp.concatenate([head_t(qa_ref, h, i) for h in heads(g)], axis=1)
    amax = {}
    for i in parts:
        abias = abias_ref[jnp.minimum(a_chunk[i], 1)]
        astart = pl.multiple_of(a_first[i] * A_WINDOW, A_WINDOW)
        for g in groups:
            st = _dot(ka_ref[pl.ds(astart, 2 * A_WINDOW), g * LANES:g * LANES + HEAD_DIM], part_cols(i, g)) + abias
            sa_ref[i, g] = st
            amax[i, g] = jnp.max(st, axis=0, keepdims=True)
    cstart = pl.multiple_of(cbias_ref.shape[0] - nc - n * (QBLK // CMP_STRIDE), 8)
    cbias = cbias_ref[pl.ds(cstart, nc), :]
    cmax = []
    for g in groups:
        s = _dot(kcmp_ref[g, :, :HEAD_DIM], cols(qn_ref, g)) + cbias
        sc_ref[g] = s
        cmax.append(jnp.max(s, axis=0, keepdims=True))
    win_state = win.start(pbias_ref[0], gone(n >= 1))

    x_swa = {}
    for i in parts:
        for g in groups:
            sk = jnp.concatenate([jnp.full((1, A_WINDOW), sink_ref[h] * LOG2E, F32) for h in heads(g)], axis=1)
            m = jnp.maximum(amax[i, g], sk)
            pa_ref[i, g] = jnp.exp2(sa_ref[i, g] - m).astype(BF16)
            x_swa[i, g] = jnp.exp2(sk - m)
    tq = t0 + (lax.broadcasted_iota(jnp.int32, (1, width), 1) & (QBLK - 1))
    sees_any = tq >= CMP_BLOCK - 1
    p_cmp = []
    for g in groups:
        e = jnp.exp2(sc_ref[g] - cmax[g])
        inv = jnp.where(sees_any, 1.0 / jnp.maximum(jnp.sum(e, axis=0, keepdims=True), 1e-30), 0.0)
        p_cmp.append(e * inv)

    win_state = win.step(0, win_state, bias=pbias_ref[1] + gone(n >= 2))

    for i in parts:
        for g in groups:
            vt = jnp.concatenate([vat_ref[a_first[i] + k, g * VROWS:(g + 1) * VROWS, :] for k in range(2)], axis=1)
            o = _dot(vt, pa_ref[i, g])
            o = o[:HEAD_DIM] * (1.0 / (o[HEAD_DIM:HEAD_DIM + 1] + x_swa[i, g]))
            _put_heads(oa_ref.at[i * A_WINDOW:(i + 1) * A_WINDOW, :], g, o, A_WINDOW)
    o_cmp = [_dot(vcmpt_ref[g], p_cmp[g].astype(BF16)) for g in groups]
    imp = []
    for g in groups:
        pc = lane_chunks(p_cmp[g])
        psum = (pc[0] + pc[1]) + (pc[2] + pc[3])
        hi = psum.astype(BF16)
        lo = (psum - hi.astype(F32)).astype(BF16)
        imp.append(_dot(ovt_ref[...], hi) + _dot(ovt_ref[...], lo))
    o_win = win.finish(3, win.step(1, win_state, last=True))

    sel = [_select_blocks_t(imp[g], t0) for g in groups]

    for g in groups:
        qs_ref[g, :HEAD_DIM, :] = cols(qnr_ref, g)
        mrows = jnp.concatenate([sel[g] - 1.0, jnp.zeros((LANES - MASK_COL - nb, QBLK), F32)], axis=0).astype(BF16)
        for r in range(REP):
            qs_ref[g, MASK_COL:, r * QBLK:(r + 1) * QBLK] = mrows
    slc_state = slc.start(pbias_ref[0], gone(n >= 1))

    no_bias = gone(n >= 0)
    one_step = lambda j, s: slc.step(j, s, bias=no_bias)
    trips = slc_count - 2
    slc_state = lax.fori_loop(0, trips // 2, lambda i, s: one_step(2 * i + 1, one_step(2 * i, s)), slc_state)
    slc_state = lax.cond(trips % 2 == 1, lambda s: one_step(trips - 1, s), lambda s: s, slc_state)

    o_slc = slc.finish(slc_count, slc.step(slc_count - 2, slc_state, last=True))
    for g in groups:
        gate = lambda br: jnp.concatenate(
            [gnt_ref[i, br * Q_HEADS + h:br * Q_HEADS + h + 1, :]
             for h in heads(g) for i in range(gnt_ref.shape[0])], axis=1)
        _put_heads(ob_ref, g, gate(0) * o_cmp[g] + gate(1) * o_slc[g] + gate(2) * o_win[g])


def _attn(sinks, qa, ka, vat, qn, qnr, gnt, kcmp, vcmpt, ksl, vslt, kw, vwt, batch, seq):
    nq = seq // QBLK
    nc = kcmp.shape[2]
    nb = seq // SLC_BLOCK
    width = REP * QBLK
    assert nb <= LANES - MASK_COL and QBLK % SLC_BLOCK == 0 and QBLK % A_WINDOW == 0 and A_WINDOW % LANES == 0
    c_start = np.arange(nc)[None, :] * CMP_STRIDE
    s_start = np.arange(nb)[:, None] * SLC_BLOCK
    ovt = jnp.asarray((c_start < s_start + SLC_BLOCK) & (s_start < c_start + CMP_BLOCK), BF16)
    ql = np.arange(width)[None, :] % QBLK
    masked = lambda visible: np.where(visible, 0.0, NEG)
    kq = np.arange(KSTEP)[:, None] - ql
    pbias = jnp.asarray(np.stack([masked(kq <= 0),
                                  masked(kq > 0)]), F32)
    band = np.arange(2 * A_WINDOW)[:, None]
    qpart = np.arange(REP * A_WINDOW)[None, :] % A_WINDOW
    in_window = lambda dist: (dist >= 0) & (dist < A_WINDOW)
    abias = jnp.asarray(np.stack([masked(in_window(qpart - band)),
                                  masked(in_window(qpart - (band - A_WINDOW)))]), F32)
    per_blk = QBLK // CMP_STRIDE
    d = np.arange(nc + (nq - 1) * per_blk)[:, None] - (nq - 1) * per_blk
    cbias = jnp.asarray(masked(d * CMP_STRIDE + CMP_BLOCK - 1 <= ql), F32)
    assert nq % ATTN_BLOCKS == 0
    steps = nq // ATTN_BLOCKS
    tiles = ATTN_BLOCKS * QBLK // LANES
    q_spec = pl.BlockSpec((None, tiles, Q_DIM, LANES), lambda b, n: (b, n, 0, 0))
    g_spec = pl.BlockSpec((None, tiles, LANES, LANES), lambda b, n: (b, n, 0, 0))
    kc_spec = pl.BlockSpec((None, KV_HEADS, nc, LANES), lambda b, n: (b, 0, 0, 0))
    vc_spec = pl.BlockSpec((None, KV_HEADS, HEAD_DIM, nc), lambda b, n: (b, 0, 0, 0))
    const = lambda a: pl.BlockSpec(a.shape, lambda b, n: (0,) * a.ndim, pipeline_mode=pl.Buffered(1))
    k_spec = pl.BlockSpec((seq, KV_HEADS * LANES), lambda b, n: (b, 0))
    vt_spec = lambda keys: pl.BlockSpec((None, seq // keys, KV_HEADS * VROWS, keys), lambda b, n: (b, 0, 0, 0))
    o_spec = pl.BlockSpec((ATTN_BLOCKS * QBLK, Q_DIM), lambda b, n: (b * steps + n, 0))
    o_shape = jax.ShapeDtypeStruct((batch * seq, Q_DIM), BF16)
    pipe_scratch = lambda depth: [
        pltpu.VMEM((KV_HEADS, depth, width), BF16),
        pltpu.VMEM((KV_HEADS, KSTEP, width), F32),
        pltpu.VMEM((KV_HEADS, KSTEP, width), BF16),
        pltpu.VMEM((KV_HEADS, VROWS, width), F32)]
    return pl.pallas_call(
        _attn_kernel,
        grid=(batch, steps),
        in_specs=[pl.BlockSpec(memory_space=pltpu.SMEM), const(abias), const(pbias), const(cbias), const(ovt),
                  q_spec, k_spec, vt_spec(A_WINDOW), q_spec, q_spec, g_spec, kc_spec, vc_spec,
                  k_spec, vt_spec(KSTEP), k_spec, vt_spec(KSTEP)],
        out_specs=[o_spec, o_spec],
        out_shape=[o_shape, o_shape],
        scratch_shapes=pipe_scratch(HEAD_DIM) + pipe_scratch(LANES) + [
            pltpu.VMEM((QBLK // A_WINDOW, KV_HEADS, 2 * A_WINDOW, REP * A_WINDOW), F32),
            pltpu.VMEM((QBLK // A_WINDOW, KV_HEADS, 2 * A_WINDOW, REP * A_WINDOW), BF16),
            pltpu.VMEM((KV_HEADS, nc, width), F32)],
        compiler_params=pltpu.CompilerParams(dimension_semantics=("parallel", "arbitrary"),
                                             vmem_limit_bytes=VMEM_WIDE),
        name="attn",
    )(sinks, abias, pbias, cbias, ovt, qa, ka, vat, qn, qnr, gnt, kcmp, vcmpt, ksl, vslt, kw, vwt)


def _layer_norm(r, g, b):
    mu = jnp.mean(r, axis=-1, keepdims=True)
    d = r - mu
    var = jnp.mean(d * d, axis=-1, keepdims=True)
    return d * lax.rsqrt(var + LN_EPS) * g + b


POST_PARTS = 4


def _post_kernel(alpha, x_ref, oa_ref, ob_ref, wgm_ref, wpa_ref, wpb_ref, wout_ref, g_ref, b_ref, h_ref):
    d = x_ref.shape[1]
    rows = [slice(i * (x_ref.shape[0] // POST_PARTS), (i + 1) * (x_ref.shape[0] // POST_PARTS))
            for i in range(POST_PARTS)]
    ys = []
    for r in rows:
        xb = x_ref[r, :].astype(BF16)
        pa = _dot(oa_ref[r, :], wpa_ref[...])
        pb = _dot(ob_ref[r, :], wpb_ref[...])
        ys.append(jax.nn.sigmoid(_dot(xb, wgm_ref[:, :d])) * pa + jax.nn.sigmoid(_dot(xb, wgm_ref[:, d:])) * pb)
    for r, y in zip(rows, ys):
        m = _dot(y.astype(BF16), wout_ref[...])
        h_ref[r, :] = _layer_norm(alpha * x_ref[r, :] + m, g_ref[...], b_ref[...])


FFN_CHUNK = 256


FFN_PARTS = 2


def _ffn_kernel(alpha, h_ref, wg_ref, wu_ref, wd_ref, g_ref, b_ref, o_ref):
    hidden = wg_ref.shape[1]
    rows = h_ref.shape[0] // FFN_PARTS
    for p in range(FFN_PARTS):
        r = slice(p * rows, (p + 1) * rows)
        h = h_ref[r, :]
        hb = h.astype(BF16)
        acc = jnp.zeros(h.shape, F32)
        for c in range(hidden // FFN_CHUNK):
            sl = slice(c * FFN_CHUNK, (c + 1) * FFN_CHUNK)
            a = jax.nn.silu(_dot(hb, wg_ref[:, sl])) * _dot(hb, wu_ref[:, sl])
            acc = acc + _dot(a.astype(BF16), wd_ref[sl, :])
        o_ref[r, :] = _layer_norm(alpha * h + acc, g_ref[...], b_ref[...])


def _tail_kernel(alpha, x_ref, oa_ref, ob_ref, wgm_ref, wpa_ref, wpb_ref, wout_ref, g1_ref, b1_ref,
                 wg_ref, wu_ref, wd_ref, g2_ref, b2_ref, o_ref, h_ref):
    _post_kernel(alpha, x_ref, oa_ref, ob_ref, wgm_ref, wpa_ref, wpb_ref, wout_ref, g1_ref, b1_ref, h_ref)
    _ffn_kernel(alpha, h_ref, wg_ref, wu_ref, wd_ref, g2_ref, b2_ref, o_ref)


def _tail(x2, oa, ob, post_w, ffn_w, alpha, tm):
    n, d = x2.shape
    tok = lambda width: pl.BlockSpec((tm, width), lambda i: (i, 0))
    once = lambda a: pl.BlockSpec(a.shape, lambda i: (0, 0), pipeline_mode=pl.Buffered(1))
    return pl.pallas_call(
        functools.partial(_tail_kernel, alpha),
        grid=(n // tm,),
        in_specs=[tok(d), tok(Q_DIM), tok(Q_DIM)] + [once(a) for a in post_w + ffn_w],
        out_specs=tok(d),
        out_shape=jax.ShapeDtypeStruct((n, d), F32),
        scratch_shapes=[pltpu.VMEM((tm, d), F32)],
        compiler_params=pltpu.CompilerParams(dimension_semantics=("parallel",),
                                             vmem_limit_bytes=VMEM_WIDE),
        name="tail",
    )(x2, oa, ob, *post_w, *ffn_w)


def _position_tables(seq):
    half = HEAD_DIM // 2
    inv = ROPE_THETA ** (-jnp.arange(half, dtype=F32) / half)
    ang = jnp.arange(seq).astype(F32)[:, None] * inv[None, :]
    cos, sin = jnp.cos(ang), jnp.sin(ang)
    reps = LANES // HEAD_DIM
    lane = np.arange(LANES)[None, :]
    blk = (np.arange(seq) // SLC_BLOCK)[:, None]
    ke = jnp.asarray(np.where(lane - MASK_COL == blk, -NEG, 0.0), F32)
    return jnp.tile(cos, (1, 2 * reps)), jnp.tile(jnp.concatenate([-sin, sin], axis=1), (1, reps)), ke


def kernel(x, w_in, sinks, cmp_pe_k, cmp_w1_k, cmp_b1_k, cmp_w2_k, cmp_pe_v, cmp_w1_v, cmp_b1_v, cmp_w2_v,
           w_proj_a, w_proj_b, w_out, ln1_g, ln1_b, w_gate, w_up, w_down, ln2_g, ln2_b):
    batch, seq, d = x.shape
    depth = w_in.shape[0]
    alpha = (2 * depth) ** 0.25
    n_main = 2 * Q_DIM + 8 * KV_DIM
    n_gate = 3 * Q_HEADS
    tm = ROW_TILE
    cos, sin, ke = _position_tables(seq)
    pad_cols = lambda w: jnp.pad(w, ((0, 0), (0, LANES - w.shape[1]))).astype(BF16)
    xt = x.reshape(batch * seq, d)
    for l in range(depth):
        w_main = w_in[l, :, :n_main].astype(BF16)
        w_gn = pad_cols(w_in[l, :, n_main:n_main + n_gate])
        w_gm = w_in[l, :, n_main + n_gate:].astype(BF16)
        (qa, ka, vat, qn, qnr, kc, vc, ksl, vslt, kw, vwt, gnt) = _in_proj(
            xt, w_main, w_gn, cos, sin, ke, batch, seq, tm)
        wk = (cmp_pe_k[l], cmp_w1_k[l].astype(BF16), cmp_b1_k[l][None, :], pad_cols(cmp_w2_k[l]))
        wv = (cmp_pe_v[l], cmp_w1_v[l].astype(BF16), cmp_b1_v[l][None, :], pad_cols(cmp_w2_v[l]))
        kcmp, vcmpt = _compress(kc, vc, wk, wv, batch, seq)
        oa, ob = _attn(sinks[l], qa, ka, vat, qn, qnr, gnt, kcmp, vcmpt, ksl, vslt, kw, vwt, batch, seq)
        post_w = (w_gm, w_proj_a[l].astype(BF16), w_proj_b[l].astype(BF16), w_out[l].astype(BF16),
                  ln1_g[l][None, :], ln1_b[l][None, :])
        ffn_w = (w_gate[l].astype(BF16), w_up[l].astype(BF16), w_down[l].astype(BF16),
                 ln2_g[l][None, :], ln2_b[l][None, :])
        xt = _tail(xt, oa, ob, post_w, ffn_w, alpha, tm)
    return xt.reshape(batch, seq, d)
```

```python
import functools

import jax
import jax.numpy as jnp
import numpy as np
from jax import lax
from jax.experimental import pallas as pl
from jax.experimental.pallas import tpu as pltpu

HEAD_DIM = 64
ROPE_THETA = 10000.0
Q_HEADS = 8
KV_HEADS = 2
REP = Q_HEADS // KV_HEADS
A_WINDOW = 128
B_WINDOW = 512
CMP_BLOCK = 32
CMP_STRIDE = 16
SLC_BLOCK = 64
SLC_TOP_N = 16
LN_EPS = 1e-5
NEG = -1e30
BIG = 1e9
Q_DIM = Q_HEADS * HEAD_DIM
KV_DIM = KV_HEADS * HEAD_DIM
LANES = 128
QBLK = 256
KSTEP = QBLK
LOG2E = 1.4426950408889634
SCALE = HEAD_DIM ** -0.5 * LOG2E
VROWS = HEAD_DIM + 16
ROW_TILE = 1024
MIB = 1024 * 1024
VMEM_PROJ = 48 * MIB
VMEM_WIDE = 56 * MIB

F32 = jnp.float32
BF16 = jnp.bfloat16


def _dot(a, b):
    return jnp.dot(a, b, preferred_element_type=F32)


INPROJ_PARTS = 4


def _inproj_kernel(x_ref, w_ref, wg_ref, *refs):
    rows = x_ref.shape[0] // INPROJ_PARTS
    for p in range(INPROJ_PARTS):
        def part(ref):
            n = rows // (ref.shape[2] if len(ref.shape) == 3 else 1)
            return ref.at[pl.ds(p * n, n)]
        _inproj_part(part(x_ref), w_ref, wg_ref, *[part(r) for r in refs])


def _inproj_part(x_ref, w_ref, wg_ref, cos_ref, sin_ref, cost_ref, sint_ref, ke_ref,
                 qa_ref, ka_ref, vat_ref, qn_ref, qnr_ref, kc_ref, vc_ref,
                 ksl_ref, vslt_ref, kw_ref, vwt_ref, gnt_ref):
    tm = x_ref.shape[0]
    xb = x_ref[...].astype(BF16)
    cos = cos_ref[...]
    sin = sin_ref[...]
    lane = lax.broadcasted_iota(jnp.int32, (tm, LANES), 1)
    first_half = (lane & (HEAD_DIM - 1)) < (HEAD_DIM // 2)
    low = lane < HEAD_DIM

    def rope(z):
        sw = jnp.where(first_half, pltpu.roll(z, LANES - HEAD_DIM // 2, 1), pltpu.roll(z, HEAD_DIM // 2, 1))
        return z * cos + sw * sin

    def proj(c0, n):
        return _dot(xb, w_ref[:, c0:c0 + n])

    def chunks(z):
        return [z[:, c * LANES:(c + 1) * LANES] for c in range(z.shape[1] // LANES)]

    def put_padded(ref, c, z, fill):
        ref[:, (2 * c) * LANES:(2 * c + 1) * LANES] = jnp.where(low, z, fill).astype(ref.dtype)
        ref[:, (2 * c + 1) * LANES:(2 * c + 2) * LANES] = jnp.where(low, pltpu.roll(z, HEAD_DIM, 1), fill).astype(ref.dtype)

    def put_transposed(ref, z):
        for j in range(tm // LANES):
            ref[j] = z[j * LANES:(j + 1) * LANES, :].T.astype(ref.dtype)

    def put_heads_t(c, z, plain_ref, rope_ref):
        half = HEAD_DIM // 2
        for j in range(tm // LANES):
            zt = z[j * LANES:(j + 1) * LANES, :].T * SCALE
            rows = slice(c * LANES, (c + 1) * LANES)
            if plain_ref is not None:
                plain_ref[j, rows, :] = zt.astype(plain_ref.dtype)
            sw = jnp.concatenate([zt[(s ^ 1) * half:((s ^ 1) + 1) * half] for s in range(LANES // half)], axis=0)
            rope_ref[j, rows, :] = (zt * cost_ref[j] + sw * sint_ref[j]).astype(rope_ref.dtype)

    ones_row = jnp.where(lax.broadcasted_iota(jnp.int32, (VROWS - HEAD_DIM, LANES), 0) == 0, 1.0, 0.0)

    def put_values_t(ref, z):
        keys = ref.shape[2]
        for j in range(tm // keys):
            for i in range(keys // LANES):
                zt = z[j * keys + i * LANES:j * keys + (i + 1) * LANES, :].T
                cols = slice(i * LANES, (i + 1) * LANES)
                for g in range(KV_HEADS):
                    ref[j, g * VROWS:g * VROWS + HEAD_DIM, cols] = zt[g * HEAD_DIM:(g + 1) * HEAD_DIM, :].astype(ref.dtype)
                    ref[j, g * VROWS + HEAD_DIM:(g + 1) * VROWS, cols] = ones_row.astype(ref.dtype)

    for c, zc in enumerate(chunks(proj(0, Q_DIM))):
        put_heads_t(c, zc, None, qa_ref)
    z = proj(Q_DIM, 2 * KV_DIM)
    put_padded(ka_ref, 0, rope(z[:, :KV_DIM]), 0.0)
    put_values_t(vat_ref, z[:, KV_DIM:])
    c0 = Q_DIM + 2 * KV_DIM
    for c, zc in enumerate(chunks(proj(c0, Q_DIM))):
        put_heads_t(c, zc, qn_ref, qnr_ref)
    c0 += Q_DIM
    z = proj(c0, 2 * KV_DIM)
    kc_ref[...] = z[:, :KV_DIM]
    vc_ref[...] = z[:, KV_DIM:]
    c0 += 2 * KV_DIM
    z = proj(c0, 2 * KV_DIM)
    put_padded(ksl_ref, 0, rope(z[:, :KV_DIM]), ke_ref[...])
    put_values_t(vslt_ref, z[:, KV_DIM:])
    c0 += 2 * KV_DIM
    z = proj(c0, 2 * KV_DIM)
    put_padded(kw_ref, 0, rope(z[:, :KV_DIM]), 0.0)
    put_values_t(vwt_ref, z[:, KV_DIM:])
    put_transposed(gnt_ref, jax.nn.sigmoid(_dot(xb, wg_ref[...])))


def _in_proj(x2, w_main, w_gn, cos, sin, ke, batch, seq, tm):
    n = x2.shape[0]
    d = x2.shape[1]
    spt = seq // tm
    tok = lambda width: pl.BlockSpec((tm, width), lambda i: (i, 0))
    full = lambda a: pl.BlockSpec(a.shape, lambda i: (0, 0))
    tab = pl.BlockSpec((tm, LANES), lambda i: (i % spt, 0))
    tab_t = pl.BlockSpec((tm // LANES, LANES, LANES), lambda i: (i % spt, 0, 0))
    cos_t, sin_t = (jnp.transpose(t.reshape(seq // LANES, LANES, LANES), (0, 2, 1)) for t in (cos, sin))
    flat = lambda w, dt: (tok(w), jax.ShapeDtypeStruct((n, w), dt))
    trans = lambda r, keys, dt: (pl.BlockSpec((None, tm // keys, r, keys), lambda i: (i // spt, i % spt, 0, 0)),
                                 jax.ShapeDtypeStruct((batch, seq // keys, r, keys), dt))
    outs = [trans(Q_DIM, LANES, BF16), flat(KV_HEADS * LANES, BF16),
            trans(KV_HEADS * VROWS, A_WINDOW, BF16),
            trans(Q_DIM, LANES, BF16), trans(Q_DIM, LANES, BF16),
            flat(KV_DIM, F32), flat(KV_DIM, F32),
            flat(KV_HEADS * LANES, BF16), trans(KV_HEADS * VROWS, KSTEP, BF16),
            flat(KV_HEADS * LANES, BF16), trans(KV_HEADS * VROWS, KSTEP, BF16),
            trans(LANES, LANES, F32)]
    return pl.pallas_call(
        _inproj_kernel,
        grid=(n // tm,),
        in_specs=[tok(d), full(w_main), full(w_gn), tab, tab, tab_t, tab_t, tab],
        out_specs=[o[0] for o in outs],
        out_shape=[o[1] for o in outs],
        compiler_params=pltpu.CompilerParams(dimension_semantics=("parallel",),
                                             vmem_limit_bytes=VMEM_PROJ),
        name="in_proj",
    )(x2, w_main, w_gn, cos, sin, cos_t, sin_t, ke)


CMP_PAIR = 2


def _compress_one(src_ref, pe_ref, w1_ref, b1_ref, w2_ref):
    ratio = CMP_BLOCK // CMP_STRIDE
    nchunk = src_ref.shape[0] // CMP_STRIDE
    both = lambda row: jnp.concatenate([row] * KV_HEADS, axis=1)
    parts = [jnp.zeros((nchunk, w1_ref.shape[3]), F32) for _ in range(ratio)]
    for p in range(CMP_STRIDE // CMP_PAIR):
        rows = [src_ref[pl.ds(p * CMP_PAIR + i, nchunk, stride=CMP_STRIDE), :] for i in range(CMP_PAIR)]
        for j in range(ratio):
            pos = j * CMP_STRIDE + p * CMP_PAIR
            a = jnp.concatenate([(rows[i] + both(pe_ref[pos + i:pos + i + 1, :])).astype(BF16)
                                 for i in range(CMP_PAIR)], axis=1)
            parts[j] = parts[j] + _dot(a, w1_ref[j, p])
    h = parts[0]
    for j in range(1, ratio):
        h = h + pltpu.roll(parts[j], nchunk - j, 0)
    h = jax.nn.gelu(h + both(b1_ref[...]))
    return _dot(h.astype(BF16), w2_ref[...])


def _group_diagonal(w1_ref, w2_ref, w1d_ref, w2d_ref):
    hid = w1_ref.shape[1]
    w1d_ref[...] = jnp.zeros(w1d_ref.shape, w1d_ref.dtype)
    w2d_ref[...] = jnp.zeros(w2d_ref.shape, w2d_ref.dtype)
    for g in range(KV_HEADS):
        w2d_ref[g * hid:(g + 1) * hid, g * LANES:(g + 1) * LANES] = w2_ref[...]
        for j in range(CMP_BLOCK // CMP_STRIDE):
            for p in range(CMP_STRIDE // CMP_PAIR):
                for i in range(CMP_PAIR):
                    pos = j * CMP_STRIDE + p * CMP_PAIR + i
                    row = i * KV_DIM + g * HEAD_DIM
                    w1d_ref[j, p, row:row + HEAD_DIM, g * hid:(g + 1) * hid] = w1_ref[pos * HEAD_DIM:(pos + 1) * HEAD_DIM, :]


def _compress_kernel(kc_ref, vc_ref, pek_ref, w1k_ref, b1k_ref, w2k_ref,
                     pev_ref, w1v_ref, b1v_ref, w2v_ref, kcmp_ref, vcmpt_ref,
                     w1dk_ref, w2dk_ref, w1dv_ref, w2dv_ref):
    @pl.when(pl.program_id(0) == 0)
    def _():
        _group_diagonal(w1k_ref, w2k_ref, w1dk_ref, w2dk_ref)
        _group_diagonal(w1v_ref, w2v_ref, w1dv_ref, w2dv_ref)

    kc = _compress_one(kc_ref, pek_ref, w1dk_ref, b1k_ref, w2dk_ref)
    vc = _compress_one(vc_ref, pev_ref, w1dv_ref, b1v_ref, w2dv_ref)
    for g in range(KV_HEADS):
        kcmp_ref[g] = kc[:, g * LANES:(g + 1) * LANES].astype(BF16)
        vcmpt_ref[g] = vc[:, g * LANES:(g + 1) * LANES].T[:HEAD_DIM, :].astype(BF16)


def _compress(kc, vc, wk, wv, batch, seq):
    nchunk = seq // CMP_STRIDE
    hid = wk[1].shape[1]
    src = pl.BlockSpec((seq, KV_DIM), lambda b: (b, 0))
    full = lambda a: pl.BlockSpec(a.shape, lambda b: (0,) * a.ndim)
    return pl.pallas_call(
        _compress_kernel,
        grid=(batch,),
        in_specs=[src, src] + [full(a) for a in wk] + [full(a) for a in wv],
        out_specs=[pl.BlockSpec((None, KV_HEADS, nchunk, LANES), lambda b: (b, 0, 0, 0)),
                   pl.BlockSpec((None, KV_HEADS, HEAD_DIM, nchunk), lambda b: (b, 0, 0, 0))],
        out_shape=[jax.ShapeDtypeStruct((batch, KV_HEADS, nchunk, LANES), BF16),
                   jax.ShapeDtypeStruct((batch, KV_HEADS, HEAD_DIM, nchunk), BF16)],
        scratch_shapes=[pltpu.VMEM((CMP_BLOCK // CMP_STRIDE, CMP_STRIDE // CMP_PAIR, CMP_PAIR * KV_DIM, KV_HEADS * hid), BF16),
                        pltpu.VMEM((KV_HEADS * hid, KV_HEADS * LANES), BF16)] * 2,
        compiler_params=pltpu.CompilerParams(dimension_semantics=("arbitrary",)),
        name="compress",
    )(kc, vc, *wk, *wv)


MASK_COL = HEAD_DIM


def _select_blocks_t(imp, t0):
    nb, cols = imp.shape
    j = lax.broadcasted_iota(jnp.int32, (nb, cols), 0)
    cur = (t0 + lax.broadcasted_iota(jnp.int32, (nb, cols), 1)) >> int(np.log2(SLC_BLOCK))
    forced = (j == 0) | (j == cur) | (j == cur - 1)
    score = jnp.where(forced, BIG, jnp.where(j <= cur, imp, -BIG))
    sub = 8
    tiles = [score[v * sub:(v + 1) * sub] for v in range(nb // sub)]
    ranks = [jnp.zeros((sub, cols), F32) for _ in tiles]
    jl = lax.broadcasted_iota(jnp.int32, (sub, cols), 0)
    for jp in range(nb):
        row = score[jp:jp + 1, :]
        for v, tile in enumerate(tiles):
            if v * sub > jp:
                beats = row >= tile
            elif (v + 1) * sub - 1 <= jp:
                beats = row > tile
            else:
                beats = (row > tile) | ((row == tile) & (jl > jp - v * sub))
            ranks[v] = jnp.where(beats, ranks[v] + 1.0, ranks[v])
    rank = jnp.concatenate(ranks, axis=0)
    return jnp.where((rank < float(min(SLC_TOP_N, nb))) & (j <= cur), 1.0, 0.0)


def _put_heads(o_ref, g, o, nq=QBLK):
    for i in range(REP // 2):
        pair = jnp.concatenate([o[:, (2 * i) * nq:(2 * i + 1) * nq],
                                o[:, (2 * i + 1) * nq:(2 * i + 2) * nq]], axis=0)
        col = (g * REP + 2 * i) * HEAD_DIM
        o_ref[:, col:col + LANES] = pair.T.astype(o_ref.dtype)


class _Pipeline:
    def __init__(self, k_ref, vt_ref, q_ref, s_ref, p_ref, acc_ref, chunk_of):
        self.k_ref, self.vt_ref, self.q_ref = k_ref, vt_ref, q_ref
        self.s_ref, self.p_ref, self.acc_ref = s_ref, p_ref, acc_ref
        self.chunk_of = chunk_of
        self.groups = range(KV_HEADS)

    def qk(self, g, j):
        start = pl.multiple_of(self.chunk_of(j) * KSTEP, KSTEP)
        depth = self.q_ref.shape[1]
        return _dot(self.k_ref[pl.ds(start, KSTEP), g * LANES:g * LANES + depth], self.q_ref[g])

    def keep(self, g, st, bias):
        if bias is not None:
            st = st + bias
        self.s_ref[g] = st
        return jnp.max(st, axis=0, keepdims=True)

    def softmax(self, g, m, cmax):
        m_new = jnp.maximum(m, cmax)
        a = jnp.exp2(m - m_new)
        self.p_ref[g] = jnp.exp2(self.s_ref[g] - m_new).astype(BF16)
        return m_new, a

    def values(self, g, j, a):
        vt = self.vt_ref[self.chunk_of(j), g * VROWS:(g + 1) * VROWS, :]
        self.acc_ref[g] = a * self.acc_ref[g] + _dot(vt, self.p_ref[g])

    def start(self, bias0, bias1):
        m0 = jnp.full((1, self.s_ref.shape[2]), NEG, F32)
        cmax = [self.keep(g, self.qk(g, 0), bias0) for g in self.groups]
        st1 = [self.qk(g, 1) for g in self.groups]
        sm = [self.softmax(g, m0, cmax[g]) for g in self.groups]
        cmax = tuple(self.keep(g, st1[g], bias1) for g in self.groups)
        for g in self.groups:
            self.acc_ref[g] = jnp.zeros(self.acc_ref.shape[1:], F32)
        return (tuple(s[0] for s in sm), tuple(s[1] for s in sm), cmax)

    def step(self, j, state, last=False, bias=None):
        m, a, cmax = state
        st = None if last else [self.qk(g, j + 2) for g in self.groups]
        for g in self.groups:
            self.values(g, j, a[g])
        sm = [self.softmax(g, m[g], cmax[g]) for g in self.groups]
        if not last:
            cmax = tuple(self.keep(g, st[g], bias) for g in self.groups)
        return (tuple(s[0] for s in sm), tuple(s[1] for s in sm), cmax)

    def finish(self, count, state):
        _, a, _ = state
        out = []
        for g in self.groups:
            self.values(g, count - 1, a[g])
            acc = self.acc_ref[g]
            out.append(acc[:HEAD_DIM] * (1.0 / acc[HEAD_DIM:HEAD_DIM + 1]))
        return out


ATTN_BLOCKS = 2


def _attn_kernel(sink_ref, abias_ref, pbias_ref, cbias_ref, ovt_ref,
                 qa_ref, ka_ref, vat_ref, qn_ref, qnr_ref, gnt_ref, kcmp_ref, vcmpt_ref,
                 ksl_ref, vslt_ref, kw_ref, vwt_ref, oa_ref, ob_ref, *scratch):
    tiles = QBLK // LANES
    for i in range(ATTN_BLOCKS):
        tok = lambda ref: ref.at[pl.ds(i * tiles, tiles)]
        out = lambda ref: ref.at[pl.ds(i * QBLK, QBLK), :]
        _attn_block(pl.program_id(1) * ATTN_BLOCKS + i, sink_ref, abias_ref, pbias_ref, cbias_ref, ovt_ref,
                    tok(qa_ref), ka_ref, vat_ref, tok(qn_ref), tok(qnr_ref), tok(gnt_ref), kcmp_ref, vcmpt_ref,
                    ksl_ref, vslt_ref, kw_ref, vwt_ref, out(oa_ref), out(ob_ref), *scratch)


def _attn_block(n, sink_ref, abias_ref, pbias_ref, cbias_ref, ovt_ref,
                qa_ref, ka_ref, vat_ref, qn_ref, qnr_ref, gnt_ref, kcmp_ref, vcmpt_ref,
                ksl_ref, vslt_ref, kw_ref, vwt_ref,
                oa_ref, ob_ref,
                qw_ref, sw_ref, pw_ref, accw_ref, qs_ref, ss_ref, ps_ref, accs_ref, sa_ref, pa_ref, sc_ref):
    groups = range(KV_HEADS)
    width = REP * QBLK
    nb = ovt_ref.shape[0]
    nc = kcmp_ref.shape[1]
    t0 = pl.multiple_of(n * QBLK, QBLK)
    heads = lambda g: [g * REP + r for r in range(REP)]
    parts = range(QBLK // LANES)
    head_t = lambda ref, h, i: ref[i, h * HEAD_DIM:(h + 1) * HEAD_DIM, :]
    cols = lambda ref, g: jnp.concatenate([head_t(ref, h, i) for h in heads(g) for i in parts], axis=1)
    lane_chunks = lambda a: [a[:, r * QBLK:(r + 1) * QBLK] for r in range(REP)]
    gone = lambda cond: jnp.where(cond, 0.0, NEG)

    assert B_WINDOW == 2 * KSTEP
    win = _Pipeline(kw_ref, vwt_ref, qw_ref, sw_ref, pw_ref, accw_ref, lambda j: jnp.maximum(n - j, 0))
    slc = _Pipeline(ksl_ref, vslt_ref, qs_ref, ss_ref, ps_ref, accs_ref,
                    lambda j: jnp.where(j == 0, n, jnp.maximum(j - 1, 0)))
    slc_count = jnp.maximum(n, 1) + 1

    for g in groups:
        qw_ref[g] = cols(qnr_ref, g)
    a_chunk = [n * len(parts) + i for i in parts]
    a_first = [jnp.maximum(c - 1, 0) for c in a_chunk]
    part_cols = lambda i, g: jnp.concatenate([head_t(qa_ref, h, i) for h in heads(g)], axis=1)
    amax = {}
    for i in parts:
        abias = abias_ref[jnp.minimum(a_chunk[i], 1)]
        astart = pl.multiple_of(a_first[i] * A_WINDOW, A_WINDOW)
        for g in groups:
            st = _dot(ka_ref[pl.ds(astart, 2 * A_WINDOW), g * LANES:g * LANES + HEAD_DIM], part_cols(i, g)) + abias
            sa_ref[i, g] = st
            amax[i, g] = jnp.max(st, axis=0, keepdims=True)
    cstart = pl.multiple_of(cbias_ref.shape[0] - nc - n * (QBLK // CMP_STRIDE), 8)
    cbias = cbias_ref[pl.ds(cstart, nc), :]
    cmax = []
    for g in groups:
        s = _dot(kcmp_ref[g, :, :HEAD_DIM], cols(qn_ref, g)) + cbias
        sc_ref[g] = s
        cmax.append(jnp.max(s, axis=0, keepdims=True))
    win_state = win.start(pbias_ref[0], gone(n >= 1))

    x_swa = {}
    for i in parts:
        for g in groups:
            sk = jnp.concatenate([jnp.full((1, A_WINDOW), sink_ref[h] * LOG2E, F32) for h in heads(g)], axis=1)
            m = jnp.maximum(amax[i, g], sk)
            pa_ref[i, g] = jnp.exp2(sa_ref[i, g] - m).astype(BF16)
            x_swa[i, g] = jnp.exp2(sk - m)
    tq = t0 + (lax.broadcasted_iota(jnp.int32, (1, width), 1) & (QBLK - 1))
    sees_any = tq >= CMP_BLOCK - 1
    p_cmp = []
    for g in groups:
        e = jnp.exp2(sc_ref[g] - cmax[g])
        inv = jnp.where(sees_any, 1.0 / jnp.maximum(jnp.sum(e, axis=0, keepdims=True), 1e-30), 0.0)
        p_cmp.append(e * inv)

    win_state = win.step(0, win_state, bias=pbias_ref[1] + gone(n >= 2))

    for i in parts:
        for g in groups:
            vt = jnp.concatenate([vat_ref[a_first[i] + k, g * VROWS:(g + 1) * VROWS, :] for k in range(2)], axis=1)
            o = _dot(vt, pa_ref[i, g])
            o = o[:HEAD_DIM] * (1.0 / (o[HEAD_DIM:HEAD_DIM + 1] + x_swa[i, g]))
            _put_heads(oa_ref.at[i * A_WINDOW:(i + 1) * A_WINDOW, :], g, o, A_WINDOW)
    o_cmp = [_dot(vcmpt_ref[g], p_cmp[g].astype(BF16)) for g in groups]
    imp = []
    for g in groups:
        pc = lane_chunks(p_cmp[g])
        psum = (pc[0] + pc[1]) + (pc[2] + pc[3])
        hi = psum.astype(BF16)
        lo = (psum - hi.astype(F32)).astype(BF16)
        imp.append(_dot(ovt_ref[...], hi) + _dot(ovt_ref[...], lo))
    o_win = win.finish(3, win.step(1, win_state, last=True))

    jb = lax.broadcasted_iota(jnp.int32, (nb, QBLK), 0)
    curb = (t0 + lax.broadcasted_iota(jnp.int32, (nb, QBLK), 1)) >> int(np.log2(SLC_BLOCK))
    sel = lax.cond((t0 + QBLK - 1) // SLC_BLOCK < SLC_TOP_N,
                   lambda: tuple(jnp.where(jb <= curb, 1.0, 0.0) for _ in groups),
                   lambda: tuple(_select_blocks_t(imp[g], t0) for g in groups))

    for g in groups:
        qs_ref[g, :HEAD_DIM, :] = cols(qnr_ref, g)
        mrows = jnp.concatenate([sel[g] - 1.0, jnp.zeros((LANES - MASK_COL - nb, QBLK), F32)], axis=0).astype(BF16)
        for r in range(REP):
            qs_ref[g, MASK_COL:, r * QBLK:(r + 1) * QBLK] = mrows
    slc_state = slc.start(pbias_ref[0], gone(n >= 1))

    no_bias = gone(n >= 0)
    one_step = lambda j, s: slc.step(j, s, bias=no_bias)
    trips = slc_count - 2
    slc_state = lax.fori_loop(0, trips // 2, lambda i, s: one_step(2 * i + 1, one_step(2 * i, s)), slc_state)
    slc_state = lax.cond(trips % 2 == 1, lambda s: one_step(trips - 1, s), lambda s: s, slc_state)

    o_slc = slc.finish(slc_count, slc.step(slc_count - 2, slc_state, last=True))
    for g in groups:
        gate = lambda br: jnp.concatenate(
            [gnt_ref[i, br * Q_HEADS + h:br * Q_HEADS + h + 1, :]
             for h in heads(g) for i in range(gnt_ref.shape[0])], axis=1)
        _put_heads(ob_ref, g, gate(0) * o_cmp[g] + gate(1) * o_slc[g] + gate(2) * o_win[g])


def _attn(sinks, qa, ka, vat, qn, qnr, gnt, kcmp, vcmpt, ksl, vslt, kw, vwt, batch, seq):
    nq = seq // QBLK
    nc = kcmp.shape[2]
    nb = seq // SLC_BLOCK
    width = REP * QBLK
    assert nb <= LANES - MASK_COL and QBLK % SLC_BLOCK == 0 and QBLK % A_WINDOW == 0 and A_WINDOW % LANES == 0
    c_start = np.arange(nc)[None, :] * CMP_STRIDE
    s_start = np.arange(nb)[:, None] * SLC_BLOCK
    ovt = jnp.asarray((c_start < s_start + SLC_BLOCK) & (s_start < c_start + CMP_BLOCK), BF16)
    ql = np.arange(width)[None, :] % QBLK
    masked = lambda visible: np.where(visible, 0.0, NEG)
    kq = np.arange(KSTEP)[:, None] - ql
    pbias = jnp.asarray(np.stack([masked(kq <= 0),
                                  masked(kq > 0)]), F32)
    band = np.arange(2 * A_WINDOW)[:, None]
    qpart = np.arange(REP * A_WINDOW)[None, :] % A_WINDOW
    in_window = lambda dist: (dist >= 0) & (dist < A_WINDOW)
    abias = jnp.asarray(np.stack([masked(in_window(qpart - band)),
                                  masked(in_window(qpart - (band - A_WINDOW)))]), F32)
    per_blk = QBLK // CMP_STRIDE
    d = np.arange(nc + (nq - 1) * per_blk)[:, None] - (nq - 1) * per_blk
    cbias = jnp.asarray(masked(d * CMP_STRIDE + CMP_BLOCK - 1 <= ql), F32)
    assert nq % ATTN_BLOCKS == 0
    steps = nq // ATTN_BLOCKS
    tiles = ATTN_BLOCKS * QBLK // LANES
    q_spec = pl.BlockSpec((None, tiles, Q_DIM, LANES), lambda b, n: (b, n, 0, 0))
    g_spec = pl.BlockSpec((None, tiles, LANES, LANES), lambda b, n: (b, n, 0, 0))
    kc_spec = pl.BlockSpec((None, KV_HEADS, nc, LANES), lambda b, n: (b, 0, 0, 0))
    vc_spec = pl.BlockSpec((None, KV_HEADS, HEAD_DIM, nc), lambda b, n: (b, 0, 0, 0))
    const = lambda a: pl.BlockSpec(a.shape, lambda b, n: (0,) * a.ndim, pipeline_mode=pl.Buffered(1))
    k_spec = pl.BlockSpec((seq, KV_HEADS * LANES), lambda b, n: (b, 0))
    vt_spec = lambda keys: pl.BlockSpec((None, seq // keys, KV_HEADS * VROWS, keys), lambda b, n: (b, 0, 0, 0))
    o_spec = pl.BlockSpec((ATTN_BLOCKS * QBLK, Q_DIM), lambda b, n: (b * steps + n, 0))
    o_shape = jax.ShapeDtypeStruct((batch * seq, Q_DIM), BF16)
    pipe_scratch = lambda depth: [
        pltpu.VMEM((KV_HEADS, depth, width), BF16),
        pltpu.VMEM((KV_HEADS, KSTEP, width), F32),
        pltpu.VMEM((KV_HEADS, KSTEP, width), BF16),
        pltpu.VMEM((KV_HEADS, VROWS, width), F32)]
    return pl.pallas_call(
        _attn_kernel,
        grid=(batch, steps),
        in_specs=[pl.BlockSpec(memory_space=pltpu.SMEM), const(abias), const(pbias), const(cbias), const(ovt),
                  q_spec, k_spec, vt_spec(A_WINDOW), q_spec, q_spec, g_spec, kc_spec, vc_spec,
                  k_spec, vt_spec(KSTEP), k_spec, vt_spec(KSTEP)],
        out_specs=[o_spec, o_spec],
        out_shape=[o_shape, o_shape],
        scratch_shapes=pipe_scratch(HEAD_DIM) + pipe_scratch(LANES) + [
            pltpu.VMEM((QBLK // A_WINDOW, KV_HEADS, 2 * A_WINDOW, REP * A_WINDOW), F32),
            pltpu.VMEM((QBLK // A_WINDOW, KV_HEADS, 2 * A_WINDOW, REP * A_WINDOW), BF16),
            pltpu.VMEM((KV_HEADS, nc, width), F32)],
        compiler_params=pltpu.CompilerParams(dimension_semantics=("parallel", "arbitrary"),
                                             vmem_limit_bytes=VMEM_WIDE),
        name="attn",
    )(sinks, abias, pbias, cbias, ovt, qa, ka, vat, qn, qnr, gnt, kcmp, vcmpt, ksl, vslt, kw, vwt)


def _layer_norm(r, g, b):
    mu = jnp.mean(r, axis=-1, keepdims=True)
    d = r - mu
    var = jnp.mean(d * d, axis=-1, keepdims=True)
    return d * lax.rsqrt(var + LN_EPS) * g + b


POST_PARTS = 4


def _post_kernel(alpha, x_ref, oa_ref, ob_ref, wgm_ref, wpa_ref, wpb_ref, wout_ref, g_ref, b_ref, h_ref):
    d = x_ref.shape[1]
    rows = [slice(i * (x_ref.shape[0] // POST_PARTS), (i + 1) * (x_ref.shape[0] // POST_PARTS))
            for i in range(POST_PARTS)]
    ys = []
    for r in rows:
        xb = x_ref[r, :].astype(BF16)
        pa = _dot(oa_ref[r, :], wpa_ref[...])
        pb = _dot(ob_ref[r, :], wpb_ref[...])
        ys.append(jax.nn.sigmoid(_dot(xb, wgm_ref[:, :d])) * pa + jax.nn.sigmoid(_dot(xb, wgm_ref[:, d:])) * pb)
    for r, y in zip(rows, ys):
        m = _dot(y.astype(BF16), wout_ref[...])
        h_ref[r, :] = _layer_norm(alpha * x_ref[r, :] + m, g_ref[...], b_ref[...])


FFN_CHUNK = 256


FFN_PARTS = 2


def _ffn_kernel(alpha, h_ref, wg_ref, wu_ref, wd_ref, g_ref, b_ref, o_ref):
    hidden = wg_ref.shape[1]
    rows = h_ref.shape[0] // FFN_PARTS
    for p in range(FFN_PARTS):
        r = slice(p * rows, (p + 1) * rows)
        h = h_ref[r, :]
        hb = h.astype(BF16)
        acc = jnp.zeros(h.shape, F32)
        for c in range(hidden // FFN_CHUNK):
            sl = slice(c * FFN_CHUNK, (c + 1) * FFN_CHUNK)
            a = jax.nn.silu(_dot(hb, wg_ref[:, sl])) * _dot(hb, wu_ref[:, sl])
            acc = acc + _dot(a.astype(BF16), wd_ref[sl, :])
        o_ref[r, :] = _layer_norm(alpha * h + acc, g_ref[...], b_ref[...])


def _tail_kernel(alpha, x_ref, oa_ref, ob_ref, wgm_ref, wpa_ref, wpb_ref, wout_ref, g1_ref, b1_ref,
                 wg_ref, wu_ref, wd_ref, g2_ref, b2_ref, o_ref, h_ref):
    _post_kernel(alpha, x_ref, oa_ref, ob_ref, wgm_ref, wpa_ref, wpb_ref, wout_ref, g1_ref, b1_ref, h_ref)
    _ffn_kernel(alpha, h_ref, wg_ref, wu_ref, wd_ref, g2_ref, b2_ref, o_ref)


def _tail(x2, oa, ob, post_w, ffn_w, alpha, tm):
    n, d = x2.shape
    tok = lambda width: pl.BlockSpec((tm, width), lambda i: (i, 0))
    once = lambda a: pl.BlockSpec(a.shape, lambda i: (0, 0), pipeline_mode=pl.Buffered(1))
    return pl.pallas_call(
        functools.partial(_tail_kernel, alpha),
        grid=(n // tm,),
        in_specs=[tok(d), tok(Q_DIM), tok(Q_DIM)] + [once(a) for a in post_w + ffn_w],
        out_specs=tok(d),
        out_shape=jax.ShapeDtypeStruct((n, d), F32),
        scratch_shapes=[pltpu.VMEM((tm, d), F32)],
        compiler_params=pltpu.CompilerParams(dimension_semantics=("parallel",),
                                             vmem_limit_bytes=VMEM_WIDE),
        name="tail",
    )(x2, oa, ob, *post_w, *ffn_w)


def _position_tables(seq):
    half = HEAD_DIM // 2
    inv = ROPE_THETA ** (-jnp.arange(half, dtype=F32) / half)
    ang = jnp.arange(seq).astype(F32)[:, None] * inv[None, :]
    cos, sin = jnp.cos(ang), jnp.sin(ang)
    reps = LANES // HEAD_DIM
    lane = np.arange(LANES)[None, :]
    blk = (np.arange(seq) // SLC_BLOCK)[:, None]
    ke = jnp.asarray(np.where(lane - MASK_COL == blk, -NEG, 0.0), F32)
    return jnp.tile(cos, (1, 2 * reps)), jnp.tile(jnp.concatenate([-sin, sin], axis=1), (1, reps)), ke


def kernel(x, w_in, sinks, cmp_pe_k, cmp_w1_k, cmp_b1_k, cmp_w2_k, cmp_pe_v, cmp_w1_v, cmp_b1_v, cmp_w2_v,
           w_proj_a, w_proj_b, w_out, ln1_g, ln1_b, w_gate, w_up, w_down, ln2_g, ln2_b):
    batch, seq, d = x.shape
    depth = w_in.shape[0]
    alpha = (2 * depth) ** 0.25
    n_main = 2 * Q_DIM + 8 * KV_DIM
    n_gate = 3 * Q_HEADS
    tm = ROW_TILE
    cos, sin, ke = _position_tables(seq)
    pad_cols = lambda w: jnp.pad(w, ((0, 0), (0, LANES - w.shape[1]))).astype(BF16)
    xt = x.reshape(batch * seq, d)
    for l in range(depth):
        w_main = w_in[l, :, :n_main].astype(BF16)
        w_gn = pad_cols(w_in[l, :, n_main:n_main + n_gate])
        w_gm = w_in[l, :, n_main + n_gate:].astype(BF16)
        (qa, ka, vat, qn, qnr, kc, vc, ksl, vslt, kw, vwt, gnt) = _in_proj(
            xt, w_main, w_gn, cos, sin, ke, batch, seq, tm)
        wk = (cmp_pe_k[l], cmp_w1_k[l].astype(BF16), cmp_b1_k[l][None, :], pad_cols(cmp_w2_k[l]))
        wv = (cmp_pe_v[l], cmp_w1_v[l].astype(BF16), cmp_b1_v[l][None, :], pad_cols(cmp_w2_v[l]))
        kcmp, vcmpt = _compress(kc, vc, wk, wv, batch, seq)
        oa, ob = _attn(sinks[l], qa, ka, vat, qn, qnr, gnt, kcmp, vcmpt, ksl, vslt, kw, vwt, batch, seq)
        post_w = (w_gm, w_proj_a[l].astype(BF16), w_proj_b[l].astype(BF16), w_out[l].astype(BF16),
                  ln1_g[l][None, :], ln1_b[l][None, :])
        ffn_w = (w_gate[l].astype(BF16), w_up[l].astype(BF16), w_down[l].astype(BF16),
                 ln2_g[l][None, :], ln2_b[l][None, :])
        xt = _tail(xt, oa, ob, post_w, ffn_w, alpha, tm)
    return xt.reshape(batch, seq, d)
```

```python
import functools

import jax
import jax.numpy as jnp
import numpy as np
from jax import lax
from jax.experimental import pallas as pl
from jax.experimental.pallas import tpu as pltpu

HEAD_DIM = 64
ROPE_THETA = 10000.0
Q_HEADS = 8
KV_HEADS = 2
REP = Q_HEADS // KV_HEADS
A_WINDOW = 128
B_WINDOW = 512
CMP_BLOCK = 32
CMP_STRIDE = 16
SLC_BLOCK = 64
SLC_TOP_N = 16
LN_EPS = 1e-5
NEG = -1e30
BIG = 1e9
Q_DIM = Q_HEADS * HEAD_DIM
KV_DIM = KV_HEADS * HEAD_DIM
LANES = 128
QBLK = 256
KSTEP = QBLK
LOG2E = 1.4426950408889634
SCALE = HEAD_DIM ** -0.5 * LOG2E
VROWS = HEAD_DIM + 16
ROW_TILE = 1024
MIB = 1024 * 1024
VMEM_PROJ = 48 * MIB
VMEM_WIDE = 60 * MIB

F32 = jnp.float32
BF16 = jnp.bfloat16


def _dot(a, b):
    return jnp.dot(a, b, preferred_element_type=F32)


INPROJ_PARTS = 4


def _inproj_kernel(x_ref, w_ref, wg_ref, *refs):
    rows = x_ref.shape[0] // INPROJ_PARTS
    for p in range(INPROJ_PARTS):
        def part(ref):
            n = rows // (ref.shape[2] if len(ref.shape) == 3 else 1)
            return ref.at[pl.ds(p * n, n)]
        _inproj_part(part(x_ref), w_ref, wg_ref, *[part(r) for r in refs])


def _inproj_part(x_ref, w_ref, wg_ref, cos_ref, sin_ref, cost_ref, sint_ref, ke_ref,
                 qa_ref, ka_ref, vat_ref, qn_ref, qnr_ref, kc_ref, vc_ref,
                 ksl_ref, vslt_ref, kw_ref, vwt_ref, gnt_ref):
    tm = x_ref.shape[0]
    xb = x_ref[...].astype(BF16)
    cos = cos_ref[...]
    sin = sin_ref[...]
    lane = lax.broadcasted_iota(jnp.int32, (tm, LANES), 1)
    first_half = (lane & (HEAD_DIM - 1)) < (HEAD_DIM // 2)
    low = lane < HEAD_DIM

    def rope(z):
        sw = jnp.where(first_half, pltpu.roll(z, LANES - HEAD_DIM // 2, 1), pltpu.roll(z, HEAD_DIM // 2, 1))
        return z * cos + sw * sin

    def proj(c0, n):
        return _dot(xb, w_ref[:, c0:c0 + n])

    def chunks(z):
        return [z[:, c * LANES:(c + 1) * LANES] for c in range(z.shape[1] // LANES)]

    def put_padded(ref, c, z, fill):
        ref[:, (2 * c) * LANES:(2 * c + 1) * LANES] = jnp.where(low, z, fill).astype(ref.dtype)
        ref[:, (2 * c + 1) * LANES:(2 * c + 2) * LANES] = jnp.where(low, pltpu.roll(z, HEAD_DIM, 1), fill).astype(ref.dtype)

    def put_transposed(ref, z):
        for j in range(tm // LANES):
            ref[j] = z[j * LANES:(j + 1) * LANES, :].T.astype(ref.dtype)

    def put_heads_t(c, z, plain_ref, rope_ref):
        half = HEAD_DIM // 2
        for j in range(tm // LANES):
            zt = z[j * LANES:(j + 1) * LANES, :].T * SCALE
            rows = slice(c * LANES, (c + 1) * LANES)
            if plain_ref is not None:
                plain_ref[j, rows, :] = zt.astype(plain_ref.dtype)
            sw = jnp.concatenate([zt[(s ^ 1) * half:((s ^ 1) + 1) * half] for s in range(LANES // half)], axis=0)
            rope_ref[j, rows, :] = (zt * cost_ref[j] + sw * sint_ref[j]).astype(rope_ref.dtype)

    ones_row = jnp.where(lax.broadcasted_iota(jnp.int32, (VROWS - HEAD_DIM, LANES), 0) == 0, 1.0, 0.0)

    def put_values_t(ref, z):
        keys = ref.shape[2]
        for j in range(tm // keys):
            for i in range(keys // LANES):
                zt = z[j * keys + i * LANES:j * keys + (i + 1) * LANES, :].T
                cols = slice(i * LANES, (i + 1) * LANES)
                for g in range(KV_HEADS):
                    ref[j, g * VROWS:g * VROWS + HEAD_DIM, cols] = zt[g * HEAD_DIM:(g + 1) * HEAD_DIM, :].astype(ref.dtype)
                    ref[j, g * VROWS + HEAD_DIM:(g + 1) * VROWS, cols] = ones_row.astype(ref.dtype)

    for c, zc in enumerate(chunks(proj(0, Q_DIM))):
        put_heads_t(c, zc, None, qa_ref)
    z = proj(Q_DIM, 2 * KV_DIM)
    put_padded(ka_ref, 0, rope(z[:, :KV_DIM]), 0.0)
    put_values_t(vat_ref, z[:, KV_DIM:])
    c0 = Q_DIM + 2 * KV_DIM
    for c, zc in enumerate(chunks(proj(c0, Q_DIM))):
        put_heads_t(c, zc, qn_ref, qnr_ref)
    c0 += Q_DIM
    z = proj(c0, 2 * KV_DIM)
    kc_ref[...] = z[:, :KV_DIM]
    vc_ref[...] = z[:, KV_DIM:]
    c0 += 2 * KV_DIM
    z = proj(c0, 2 * KV_DIM)
    put_padded(ksl_ref, 0, rope(z[:, :KV_DIM]), ke_ref[...])
    put_values_t(vslt_ref, z[:, KV_DIM:])
    c0 += 2 * KV_DIM
    z = proj(c0, 2 * KV_DIM)
    put_padded(kw_ref, 0, rope(z[:, :KV_DIM]), 0.0)
    put_values_t(vwt_ref, z[:, KV_DIM:])
    put_transposed(gnt_ref, jax.nn.sigmoid(_dot(xb, wg_ref[...])))


def _in_proj(x2, w_main, w_gn, cos, sin, ke, batch, seq, tm):
    n = x2.shape[0]
    d = x2.shape[1]
    spt = seq // tm
    tok = lambda width: pl.BlockSpec((tm, width), lambda i: (i, 0))
    full = lambda a: pl.BlockSpec(a.shape, lambda i: (0, 0))
    tab = pl.BlockSpec((tm, LANES), lambda i: (i % spt, 0))
    tab_t = pl.BlockSpec((tm // LANES, LANES, LANES), lambda i: (i % spt, 0, 0))
    cos_t, sin_t = (jnp.transpose(t.reshape(seq // LANES, LANES, LANES), (0, 2, 1)) for t in (cos, sin))
    flat = lambda w, dt: (tok(w), jax.ShapeDtypeStruct((n, w), dt))
    trans = lambda r, keys, dt: (pl.BlockSpec((None, tm // keys, r, keys), lambda i: (i // spt, i % spt, 0, 0)),
                                 jax.ShapeDtypeStruct((batch, seq // keys, r, keys), dt))
    outs = [trans(Q_DIM, LANES, BF16), flat(KV_HEADS * LANES, BF16),
            trans(KV_HEADS * VROWS, A_WINDOW, BF16),
            trans(Q_DIM, LANES, BF16), trans(Q_DIM, LANES, BF16),
            flat(KV_DIM, F32), flat(KV_DIM, F32),
            flat(KV_HEADS * LANES, BF16), trans(KV_HEADS * VROWS, KSTEP, BF16),
            flat(KV_HEADS * LANES, BF16), trans(KV_HEADS * VROWS, KSTEP, BF16),
            trans(LANES, LANES, F32)]
    return pl.pallas_call(
        _inproj_kernel,
        grid=(n // tm,),
        in_specs=[tok(d), full(w_main), full(w_gn), tab, tab, tab_t, tab_t, tab],
        out_specs=[o[0] for o in outs],
        out_shape=[o[1] for o in outs],
        compiler_params=pltpu.CompilerParams(dimension_semantics=("parallel",),
                                             vmem_limit_bytes=VMEM_PROJ),
        name="in_proj",
    )(x2, w_main, w_gn, cos, sin, cos_t, sin_t, ke)


CMP_PAIR = 2


def _compress_one(src_ref, pe_ref, w1_ref, b1_ref, w2_ref):
    ratio = CMP_BLOCK // CMP_STRIDE
    nchunk = src_ref.shape[0] // CMP_STRIDE
    both = lambda row: jnp.concatenate([row] * KV_HEADS, axis=1)
    parts = [jnp.zeros((nchunk, w1_ref.shape[3]), F32) for _ in range(ratio)]
    for p in range(CMP_STRIDE // CMP_PAIR):
        rows = [src_ref[pl.ds(p * CMP_PAIR + i, nchunk, stride=CMP_STRIDE), :] for i in range(CMP_PAIR)]
        for j in range(ratio):
            pos = j * CMP_STRIDE + p * CMP_PAIR
            a = jnp.concatenate([(rows[i] + both(pe_ref[pos + i:pos + i + 1, :])).astype(BF16)
                                 for i in range(CMP_PAIR)], axis=1)
            parts[j] = parts[j] + _dot(a, w1_ref[j, p])
    h = parts[0]
    for j in range(1, ratio):
        h = h + pltpu.roll(parts[j], nchunk - j, 0)
    h = jax.nn.gelu(h + both(b1_ref[...]))
    return _dot(h.astype(BF16), w2_ref[...])


def _group_diagonal(w1_ref, w2_ref, w1d_ref, w2d_ref):
    hid = w1_ref.shape[1]
    w1d_ref[...] = jnp.zeros(w1d_ref.shape, w1d_ref.dtype)
    w2d_ref[...] = jnp.zeros(w2d_ref.shape, w2d_ref.dtype)
    for g in range(KV_HEADS):
        w2d_ref[g * hid:(g + 1) * hid, g * LANES:(g + 1) * LANES] = w2_ref[...]
        for j in range(CMP_BLOCK // CMP_STRIDE):
            for p in range(CMP_STRIDE // CMP_PAIR):
                for i in range(CMP_PAIR):
                    pos = j * CMP_STRIDE + p * CMP_PAIR + i
                    row = i * KV_DIM + g * HEAD_DIM
                    w1d_ref[j, p, row:row + HEAD_DIM, g * hid:(g + 1) * hid] = w1_ref[pos * HEAD_DIM:(pos + 1) * HEAD_DIM, :]


def _compress_kernel(kc_ref, vc_ref, pek_ref, w1k_ref, b1k_ref, w2k_ref,
                     pev_ref, w1v_ref, b1v_ref, w2v_ref, kcmp_ref, vcmpt_ref,
                     w1dk_ref, w2dk_ref, w1dv_ref, w2dv_ref):
    @pl.when(pl.program_id(0) == 0)
    def _():
        _group_diagonal(w1k_ref, w2k_ref, w1dk_ref, w2dk_ref)
        _group_diagonal(w1v_ref, w2v_ref, w1dv_ref, w2dv_ref)

    kc = _compress_one(kc_ref, pek_ref, w1dk_ref, b1k_ref, w2dk_ref)
    vc = _compress_one(vc_ref, pev_ref, w1dv_ref, b1v_ref, w2dv_ref)
    for g in range(KV_HEADS):
        kcmp_ref[g] = kc[:, g * LANES:(g + 1) * LANES].astype(BF16)
        vcmpt_ref[g] = vc[:, g * LANES:(g + 1) * LANES].T[:HEAD_DIM, :].astype(BF16)


def _compress(kc, vc, wk, wv, batch, seq):
    nchunk = seq // CMP_STRIDE
    hid = wk[1].shape[1]
    src = pl.BlockSpec((seq, KV_DIM), lambda b: (b, 0))
    full = lambda a: pl.BlockSpec(a.shape, lambda b: (0,) * a.ndim)
    return pl.pallas_call(
        _compress_kernel,
        grid=(batch,),
        in_specs=[src, src] + [full(a) for a in wk] + [full(a) for a in wv],
        out_specs=[pl.BlockSpec((None, KV_HEADS, nchunk, LANES), lambda b: (b, 0, 0, 0)),
                   pl.BlockSpec((None, KV_HEADS, HEAD_DIM, nchunk), lambda b: (b, 0, 0, 0))],
        out_shape=[jax.ShapeDtypeStruct((batch, KV_HEADS, nchunk, LANES), BF16),
                   jax.ShapeDtypeStruct((batch, KV_HEADS, HEAD_DIM, nchunk), BF16)],
        scratch_shapes=[pltpu.VMEM((CMP_BLOCK // CMP_STRIDE, CMP_STRIDE // CMP_PAIR, CMP_PAIR * KV_DIM, KV_HEADS * hid), BF16),
                        pltpu.VMEM((KV_HEADS * hid, KV_HEADS * LANES), BF16)] * 2,
        compiler_params=pltpu.CompilerParams(dimension_semantics=("arbitrary",)),
        name="compress",
    )(kc, vc, *wk, *wv)


MASK_COL = HEAD_DIM


def _select_blocks_t(imp, t0):
    nb, cols = imp.shape
    j = lax.broadcasted_iota(jnp.int32, (nb, cols), 0)
    cur = (t0 + lax.broadcasted_iota(jnp.int32, (nb, cols), 1)) >> int(np.log2(SLC_BLOCK))
    forced = (j == 0) | (j == cur) | (j == cur - 1)
    score = jnp.where(forced, BIG, jnp.where(j <= cur, imp, -BIG))
    sub = 8
    tiles = [score[v * sub:(v + 1) * sub] for v in range(nb // sub)]
    ranks = [jnp.zeros((sub, cols), F32) for _ in tiles]
    jl = lax.broadcasted_iota(jnp.int32, (sub, cols), 0)
    for jp in range(nb):
        row = score[jp:jp + 1, :]
        for v, tile in enumerate(tiles):
            if v * sub > jp:
                beats = row >= tile
            elif (v + 1) * sub - 1 <= jp:
                beats = row > tile
            else:
                beats = (row > tile) | ((row == tile) & (jl > jp - v * sub))
            ranks[v] = jnp.where(beats, ranks[v] + 1.0, ranks[v])
    rank = jnp.concatenate(ranks, axis=0)
    return jnp.where((rank < float(min(SLC_TOP_N, nb))) & (j <= cur), 1.0, 0.0)


def _put_heads(o_ref, g, o, nq=QBLK):
    for i in range(REP // 2):
        pair = jnp.concatenate([o[:, (2 * i) * nq:(2 * i + 1) * nq],
                                o[:, (2 * i + 1) * nq:(2 * i + 2) * nq]], axis=0)
        col = (g * REP + 2 * i) * HEAD_DIM
        o_ref[:, col:col + LANES] = pair.T.astype(o_ref.dtype)


class _Pipeline:
    def __init__(self, k_ref, vt_ref, q_ref, s_ref, p_ref, acc_ref, chunk_of):
        self.k_ref, self.vt_ref, self.q_ref = k_ref, vt_ref, q_ref
        self.s_ref, self.p_ref, self.acc_ref = s_ref, p_ref, acc_ref
        self.chunk_of = chunk_of
        self.groups = range(KV_HEADS)

    def qk(self, g, j):
        start = pl.multiple_of(self.chunk_of(j) * KSTEP, KSTEP)
        depth = self.q_ref.shape[1]
        return _dot(self.k_ref[pl.ds(start, KSTEP), g * LANES:g * LANES + depth], self.q_ref[g])

    def keep(self, g, st, bias):
        if bias is not None:
            st = st + bias
        self.s_ref[g] = st
        return jnp.max(st, axis=0, keepdims=True)

    def softmax(self, g, m, cmax):
        m_new = jnp.maximum(m, cmax)
        a = jnp.exp2(m - m_new)
        self.p_ref[g] = jnp.exp2(self.s_ref[g] - m_new).astype(BF16)
        return m_new, a

    def values(self, g, j, a):
        vt = self.vt_ref[self.chunk_of(j), g * VROWS:(g + 1) * VROWS, :]
        self.acc_ref[g] = a * self.acc_ref[g] + _dot(vt, self.p_ref[g])

    def start(self, bias0, bias1):
        m0 = jnp.full((1, self.s_ref.shape[2]), NEG, F32)
        cmax = [self.keep(g, self.qk(g, 0), bias0) for g in self.groups]
        st1 = [self.qk(g, 1) for g in self.groups]
        sm = [self.softmax(g, m0, cmax[g]) for g in self.groups]
        cmax = tuple(self.keep(g, st1[g], bias1) for g in self.groups)
        for g in self.groups:
            self.acc_ref[g] = jnp.zeros(self.acc_ref.shape[1:], F32)
        return (tuple(s[0] for s in sm), tuple(s[1] for s in sm), cmax)

    def step(self, j, state, last=False, bias=None):
        m, a, cmax = state
        st = None if last else [self.qk(g, j + 2) for g in self.groups]
        for g in self.groups:
            self.values(g, j, a[g])
        sm = [self.softmax(g, m[g], cmax[g]) for g in self.groups]
        if not last:
            cmax = tuple(self.keep(g, st[g], bias) for g in self.groups)
        return (tuple(s[0] for s in sm), tuple(s[1] for s in sm), cmax)

    def finish(self, count, state):
        _, a, _ = state
        out = []
        for g in self.groups:
            self.values(g, count - 1, a[g])
            acc = self.acc_ref[g]
            out.append(acc[:HEAD_DIM] * (1.0 / acc[HEAD_DIM:HEAD_DIM + 1]))
        return out


ATTN_BLOCKS = 2


def _attn_kernel(sink_ref, abias_ref, pbias_ref, cbias_ref, ovt_ref,
                 qa_ref, ka_ref, vat_ref, qn_ref, qnr_ref, gnt_ref, kcmp_ref, vcmpt_ref,
                 ksl_ref, vslt_ref, kw_ref, vwt_ref, oa_ref, ob_ref, *scratch):
    tiles = QBLK // LANES
    for i in range(ATTN_BLOCKS):
        tok = lambda ref: ref.at[pl.ds(i * tiles, tiles)]
        out = lambda ref: ref.at[pl.ds(i * QBLK, QBLK), :]
        _attn_block(pl.program_id(1) * ATTN_BLOCKS + i, sink_ref, abias_ref, pbias_ref, cbias_ref, ovt_ref,
                    tok(qa_ref), ka_ref, vat_ref, tok(qn_ref), tok(qnr_ref), tok(gnt_ref), kcmp_ref, vcmpt_ref,
                    ksl_ref, vslt_ref, kw_ref, vwt_ref, out(oa_ref), out(ob_ref), *scratch)


def _attn_block(n, sink_ref, abias_ref, pbias_ref, cbias_ref, ovt_ref,
                qa_ref, ka_ref, vat_ref, qn_ref, qnr_ref, gnt_ref, kcmp_ref, vcmpt_ref,
                ksl_ref, vslt_ref, kw_ref, vwt_ref,
                oa_ref, ob_ref,
                qw_ref, sw_ref, pw_ref, accw_ref, qs_ref, ss_ref, ps_ref, accs_ref, sa_ref, pa_ref, sc_ref):
    groups = range(KV_HEADS)
    width = REP * QBLK
    nb = ovt_ref.shape[0]
    nc = kcmp_ref.shape[1]
    t0 = pl.multiple_of(n * QBLK, QBLK)
    heads = lambda g: [g * REP + r for r in range(REP)]
    parts = range(QBLK // LANES)
    head_t = lambda ref, h, i: ref[i, h * HEAD_DIM:(h + 1) * HEAD_DIM, :]
    cols = lambda ref, g: jnp.concatenate([head_t(ref, h, i) for h in heads(g) for i in parts], axis=1)
    lane_chunks = lambda a: [a[:, r * QBLK:(r + 1) * QBLK] for r in range(REP)]
    gone = lambda cond: jnp.where(cond, 0.0, NEG)

    assert B_WINDOW == 2 * KSTEP
    win = _Pipeline(kw_ref, vwt_ref, qw_ref, sw_ref, pw_ref, accw_ref, lambda j: jnp.maximum(n - j, 0))
    slc = _Pipeline(ksl_ref, vslt_ref, qs_ref, ss_ref, ps_ref, accs_ref,
                    lambda j: jnp.where(j == 0, n, jnp.maximum(j - 1, 0)))
    slc_count = jnp.maximum(n, 1) + 1

    for g in groups:
        qw_ref[g] = cols(qnr_ref, g)
    a_chunk = [n * len(parts) + i for i in parts]
    a_first = [jnp.maximum(c - 1, 0) for c in a_chunk]
    part_cols = lambda i, g: jnp.concatenate([head_t(qa_ref, h, i) for h in heads(g)], axis=1)
    amax = {}
    for i in parts:
        abias = abias_ref[jnp.minimum(a_chunk[i], 1)]
        astart = pl.multiple_of(a_first[i] * A_WINDOW, A_WINDOW)
        for g in groups:
            st = _dot(ka_ref[pl.ds(astart, 2 * A_WINDOW), g * LANES:g * LANES + HEAD_DIM], part_cols(i, g)) + abias
            sa_ref[i, g] = st
            amax[i, g] = jnp.max(st, axis=0, keepdims=True)
    cstart = pl.multiple_of(cbias_ref.shape[0] - nc - n * (QBLK // CMP_STRIDE), 8)
    cbias = cbias_ref[pl.ds(cstart, nc), :]
    cmax = []
    for g in groups:
        s = _dot(kcmp_ref[g, :, :HEAD_DIM], cols(qn_ref, g)) + cbias
        sc_ref[g] = s
        cmax.append(jnp.max(s, axis=0, keepdims=True))
    win_state = win.start(pbias_ref[0], gone(n >= 1))

    x_swa = {}
    for i in parts:
        for g in groups:
            sk = jnp.concatenate([jnp.full((1, A_WINDOW), sink_ref[h] * LOG2E, F32) for h in heads(g)], axis=1)
            m = jnp.maximum(amax[i, g], sk)
            pa_ref[i, g] = jnp.exp2(sa_ref[i, g] - m).astype(BF16)
            x_swa[i, g] = jnp.exp2(sk - m)
    tq = t0 + (lax.broadcasted_iota(jnp.int32, (1, width), 1) & (QBLK - 1))
    sees_any = tq >= CMP_BLOCK - 1
    p_cmp = []
    for g in groups:
        e = jnp.exp2(sc_ref[g] - cmax[g])
        inv = jnp.where(sees_any, 1.0 / jnp.maximum(jnp.sum(e, axis=0, keepdims=True), 1e-30), 0.0)
        p_cmp.append(e * inv)

    win_state = win.step(0, win_state, bias=pbias_ref[1] + gone(n >= 2))

    for i in parts:
        for g in groups:
            vt = jnp.concatenate([vat_ref[a_first[i] + k, g * VROWS:(g + 1) * VROWS, :] for k in range(2)], axis=1)
            o = _dot(vt, pa_ref[i, g])
            o = o[:HEAD_DIM] * (1.0 / (o[HEAD_DIM:HEAD_DIM + 1] + x_swa[i, g]))
            _put_heads(oa_ref.at[i * A_WINDOW:(i + 1) * A_WINDOW, :], g, o, A_WINDOW)
    o_cmp = [_dot(vcmpt_ref[g], p_cmp[g].astype(BF16)) for g in groups]
    imp = []
    for g in groups:
        pc = lane_chunks(p_cmp[g])
        psum = (pc[0] + pc[1]) + (pc[2] + pc[3])
        hi = psum.astype(BF16)
        lo = (psum - hi.astype(F32)).astype(BF16)
        imp.append(_dot(ovt_ref[...], hi) + _dot(ovt_ref[...], lo))
    o_win = win.finish(3, win.step(1, win_state, last=True))

    jb = lax.broadcasted_iota(jnp.int32, (nb, QBLK), 0)
    curb = (t0 + lax.broadcasted_iota(jnp.int32, (nb, QBLK), 1)) >> int(np.log2(SLC_BLOCK))
    sel = lax.cond((t0 + QBLK - 1) // SLC_BLOCK < SLC_TOP_N,
                   lambda: tuple(jnp.where(jb <= curb, 1.0, 0.0) for _ in groups),
                   lambda: tuple(_select_blocks_t(imp[g], t0) for g in groups))

    for g in groups:
        qs_ref[g, :HEAD_DIM, :] = cols(qnr_ref, g)
        mrows = jnp.concatenate([sel[g] - 1.0, jnp.zeros((LANES - MASK_COL - nb, QBLK), F32)], axis=0).astype(BF16)
        for r in range(REP):
            qs_ref[g, MASK_COL:, r * QBLK:(r + 1) * QBLK] = mrows
    slc_state = slc.start(pbias_ref[0], gone(n >= 1))

    no_bias = gone(n >= 0)
    one_step = lambda j, s: slc.step(j, s, bias=no_bias)
    trips = slc_count - 2
    slc_state = lax.fori_loop(0, trips // 2, lambda i, s: one_step(2 * i + 1, one_step(2 * i, s)), slc_state)
    slc_state = lax.cond(trips % 2 == 1, lambda s: one_step(trips - 1, s), lambda s: s, slc_state)

    o_slc = slc.finish(slc_count, slc.step(slc_count - 2, slc_state, last=True))
    for g in groups:
        gate = lambda br: jnp.concatenate(
            [gnt_ref[i, br * Q_HEADS + h:br * Q_HEADS + h + 1, :]
             for h in heads(g) for i in range(gnt_ref.shape[0])], axis=1)
        _put_heads(ob_ref, g, gate(0) * o_cmp[g] + gate(1) * o_slc[g] + gate(2) * o_win[g])


def _attn(sinks, qa, ka, vat, qn, qnr, gnt, kcmp, vcmpt, ksl, vslt, kw, vwt, batch, seq):
    nq = seq // QBLK
    nc = kcmp.shape[2]
    nb = seq // SLC_BLOCK
    width = REP * QBLK
    assert nb <= LANES - MASK_COL and QBLK % SLC_BLOCK == 0 and QBLK % A_WINDOW == 0 and A_WINDOW % LANES == 0
    c_start = np.arange(nc)[None, :] * CMP_STRIDE
    s_start = np.arange(nb)[:, None] * SLC_BLOCK
    ovt = jnp.asarray((c_start < s_start + SLC_BLOCK) & (s_start < c_start + CMP_BLOCK), BF16)
    ql = np.arange(width)[None, :] % QBLK
    masked = lambda visible: np.where(visible, 0.0, NEG)
    kq = np.arange(KSTEP)[:, None] - ql
    pbias = jnp.asarray(np.stack([masked(kq <= 0),
                                  masked(kq > 0)]), F32)
    band = np.arange(2 * A_WINDOW)[:, None]
    qpart = np.arange(REP * A_WINDOW)[None, :] % A_WINDOW
    in_window = lambda dist: (dist >= 0) & (dist < A_WINDOW)
    abias = jnp.asarray(np.stack([masked(in_window(qpart - band)),
                                  masked(in_window(qpart - (band - A_WINDOW)))]), F32)
    per_blk = QBLK // CMP_STRIDE
    d = np.arange(nc + (nq - 1) * per_blk)[:, None] - (nq - 1) * per_blk
    cbias = jnp.asarray(masked(d * CMP_STRIDE + CMP_BLOCK - 1 <= ql), F32)
    assert nq % ATTN_BLOCKS == 0
    steps = nq // ATTN_BLOCKS
    tiles = ATTN_BLOCKS * QBLK // LANES
    q_spec = pl.BlockSpec((None, tiles, Q_DIM, LANES), lambda b, n: (b, n, 0, 0))
    g_spec = pl.BlockSpec((None, tiles, LANES, LANES), lambda b, n: (b, n, 0, 0))
    kc_spec = pl.BlockSpec((None, KV_HEADS, nc, LANES), lambda b, n: (b, 0, 0, 0))
    vc_spec = pl.BlockSpec((None, KV_HEADS, HEAD_DIM, nc), lambda b, n: (b, 0, 0, 0))
    const = lambda a: pl.BlockSpec(a.shape, lambda b, n: (0,) * a.ndim, pipeline_mode=pl.Buffered(1))
    k_spec = pl.BlockSpec((seq, KV_HEADS * LANES), lambda b, n: (b, 0))
    vt_spec = lambda keys: pl.BlockSpec((None, seq // keys, KV_HEADS * VROWS, keys), lambda b, n: (b, 0, 0, 0))
    o_spec = pl.BlockSpec((ATTN_BLOCKS * QBLK, Q_DIM), lambda b, n: (b * steps + n, 0))
    o_shape = jax.ShapeDtypeStruct((batch * seq, Q_DIM), BF16)
    pipe_scratch = lambda depth: [
        pltpu.VMEM((KV_HEADS, depth, width), BF16),
        pltpu.VMEM((KV_HEADS, KSTEP, width), F32),
        pltpu.VMEM((KV_HEADS, KSTEP, width), BF16),
        pltpu.VMEM((KV_HEADS, VROWS, width), F32)]
    return pl.pallas_call(
        _attn_kernel,
        grid=(batch, steps),
        in_specs=[pl.BlockSpec(memory_space=pltpu.SMEM), const(abias), const(pbias), const(cbias), const(ovt),
                  q_spec, k_spec, vt_spec(A_WINDOW), q_spec, q_spec, g_spec, kc_spec, vc_spec,
                  k_spec, vt_spec(KSTEP), k_spec, vt_spec(KSTEP)],
        out_specs=[o_spec, o_spec],
        out_shape=[o_shape, o_shape],
        scratch_shapes=pipe_scratch(HEAD_DIM) + pipe_scratch(LANES) + [
            pltpu.VMEM((QBLK // A_WINDOW, KV_HEADS, 2 * A_WINDOW, REP * A_WINDOW), F32),
            pltpu.VMEM((QBLK // A_WINDOW, KV_HEADS, 2 * A_WINDOW, REP * A_WINDOW), BF16),
            pltpu.VMEM((KV_HEADS, nc, width), F32)],
        compiler_params=pltpu.CompilerParams(dimension_semantics=("parallel", "arbitrary"),
                                             vmem_limit_bytes=VMEM_WIDE),
        name="attn",
    )(sinks, abias, pbias, cbias, ovt, qa, ka, vat, qn, qnr, gnt, kcmp, vcmpt, ksl, vslt, kw, vwt)


def _layer_norm(r, g, b):
    mu = jnp.mean(r, axis=-1, keepdims=True)
    d = r - mu
    var = jnp.mean(d * d, axis=-1, keepdims=True)
    return d * lax.rsqrt(var + LN_EPS) * g + b


POST_PARTS = 4


def _post_kernel(alpha, x_ref, oa_ref, ob_ref, wgm_ref, wpa_ref, wpb_ref, wout_ref, g_ref, b_ref, h_ref):
    d = x_ref.shape[1]
    rows = [slice(i * (x_ref.shape[0] // POST_PARTS), (i + 1) * (x_ref.shape[0] // POST_PARTS))
            for i in range(POST_PARTS)]
    ys = []
    for r in rows:
        xb = x_ref[r, :].astype(BF16)
        pa = _dot(oa_ref[r, :], wpa_ref[...])
        pb = _dot(ob_ref[r, :], wpb_ref[...])
        ys.append(jax.nn.sigmoid(_dot(xb, wgm_ref[:, :d])) * pa + jax.nn.sigmoid(_dot(xb, wgm_ref[:, d:])) * pb)
    for r, y in zip(rows, ys):
        m = _dot(y.astype(BF16), wout_ref[...])
        h_ref[r, :] = _layer_norm(alpha * x_ref[r, :] + m, g_ref[...], b_ref[...])


FFN_CHUNK = 256


FFN_PARTS = 2


def _ffn_kernel(alpha, h_ref, wg_ref, wu_ref, wd_ref, g_ref, b_ref, o_ref):
    hidden = wg_ref.shape[1]
    rows = h_ref.shape[0] // FFN_PARTS
    for p in range(FFN_PARTS):
        r = slice(p * rows, (p + 1) * rows)
        h = h_ref[r, :]
        hb = h.astype(BF16)
        acc = jnp.zeros(h.shape, F32)
        for c in range(hidden // FFN_CHUNK):
            sl = slice(c * FFN_CHUNK, (c + 1) * FFN_CHUNK)
            a = jax.nn.silu(_dot(hb, wg_ref[:, sl])) * _dot(hb, wu_ref[:, sl])
            acc = acc + _dot(a.astype(BF16), wd_ref[sl, :])
        o_ref[r, :] = _layer_norm(alpha * h + acc, g_ref[...], b_ref[...])


def _tail_kernel(alpha, x_ref, oa_ref, ob_ref, wgm_ref, wpa_ref, wpb_ref, wout_ref, g1_ref, b1_ref,
                 wg_hbm, wu_hbm, wd_hbm, g2_ref, b2_ref, o_ref, h_ref, wg_ref, wu_ref, wd_ref, sem):
    copies = [pltpu.make_async_copy(src, dst, sem.at[i])
              for i, (src, dst) in enumerate(((wg_hbm, wg_ref), (wu_hbm, wu_ref), (wd_hbm, wd_ref)))]
    first = pl.program_id(0) == 0

    @pl.when(first)
    def _():
        for c in copies:
            c.start()

    _post_kernel(alpha, x_ref, oa_ref, ob_ref, wgm_ref, wpa_ref, wpb_ref, wout_ref, g1_ref, b1_ref, h_ref)

    @pl.when(first)
    def _():
        for c in copies:
            c.wait()

    _ffn_kernel(alpha, h_ref, wg_ref, wu_ref, wd_ref, g2_ref, b2_ref, o_ref)


def _tail(x2, oa, ob, post_w, ffn_w, alpha, tm):
    n, d = x2.shape
    tok = lambda width: pl.BlockSpec((tm, width), lambda i: (i, 0))
    once = lambda a: pl.BlockSpec(a.shape, lambda i: (0, 0), pipeline_mode=pl.Buffered(1))
    anywhere = pl.BlockSpec(memory_space=pl.ANY)
    ffn_mats, ffn_norm = ffn_w[:3], ffn_w[3:]
    return pl.pallas_call(
        functools.partial(_tail_kernel, alpha),
        grid=(n // tm,),
        in_specs=([tok(d), tok(Q_DIM), tok(Q_DIM)] + [once(a) for a in post_w]
                  + [anywhere] * len(ffn_mats) + [once(a) for a in ffn_norm]),
        out_specs=tok(d),
        out_shape=jax.ShapeDtypeStruct((n, d), F32),
        scratch_shapes=([pltpu.VMEM((tm, d), F32)] + [pltpu.VMEM(a.shape, a.dtype) for a in ffn_mats]
                        + [pltpu.SemaphoreType.DMA((len(ffn_mats),))]),
        compiler_params=pltpu.CompilerParams(dimension_semantics=("arbitrary",),
                                             vmem_limit_bytes=VMEM_WIDE),
        name="tail",
    )(x2, oa, ob, *post_w, *ffn_w)


def _position_tables(seq):
    half = HEAD_DIM // 2
    inv = ROPE_THETA ** (-jnp.arange(half, dtype=F32) / half)
    ang = jnp.arange(seq).astype(F32)[:, None] * inv[None, :]
    cos, sin = jnp.cos(ang), jnp.sin(ang)
    reps = LANES // HEAD_DIM
    lane = np.arange(LANES)[None, :]
    blk = (np.arange(seq) // SLC_BLOCK)[:, None]
    ke = jnp.asarray(np.where(lane - MASK_COL == blk, -NEG, 0.0), F32)
    return jnp.tile(cos, (1, 2 * reps)), jnp.tile(jnp.concatenate([-sin, sin], axis=1), (1, reps)), ke


def kernel(x, w_in, sinks, cmp_pe_k, cmp_w1_k, cmp_b1_k, cmp_w2_k, cmp_pe_v, cmp_w1_v, cmp_b1_v, cmp_w2_v,
           w_proj_a, w_proj_b, w_out, ln1_g, ln1_b, w_gate, w_up, w_down, ln2_g, ln2_b):
    batch, seq, d = x.shape
    depth = w_in.shape[0]
    alpha = (2 * depth) ** 0.25
    n_main = 2 * Q_DIM + 8 * KV_DIM
    n_gate = 3 * Q_HEADS
    tm = ROW_TILE
    cos, sin, ke = _position_tables(seq)
    pad_cols = lambda w: jnp.pad(w, ((0, 0), (0, LANES - w.shape[1]))).astype(BF16)
    xt = x.reshape(batch * seq, d)
    for l in range(depth):
        w_main = w_in[l, :, :n_main].astype(BF16)
        w_gn = pad_cols(w_in[l, :, n_main:n_main + n_gate])
        w_gm = w_in[l, :, n_main + n_gate:].astype(BF16)
        (qa, ka, vat, qn, qnr, kc, vc, ksl, vslt, kw, vwt, gnt) = _in_proj(
            xt, w_main, w_gn, cos, sin, ke, batch, seq, tm)
        wk = (cmp_pe_k[l], cmp_w1_k[l].astype(BF16), cmp_b1_k[l][None, :], pad_cols(cmp_w2_k[l]))
        wv = (cmp_pe_v[l], cmp_w1_v[l].astype(BF16), cmp_b1_v[l][None, :], pad_cols(cmp_w2_v[l]))
        kcmp, vcmpt = _compress(kc, vc, wk, wv, batch, seq)
        oa, ob = _attn(sinks[l], qa, ka, vat, qn, qnr, gnt, kcmp, vcmpt, ksl, vslt, kw, vwt, batch, seq)
        post_w = (w_gm, w_proj_a[l].astype(BF16), w_proj_b[l].astype(BF16), w_out[l].astype(BF16),
                  ln1_g[l][None, :], ln1_b[l][None, :])
        ffn_w = (w_gate[l].astype(BF16), w_up[l].astype(BF16), w_down[l].astype(BF16),
                 ln2_g[l][None, :], ln2_b[l][None, :])
        xt = _tail(xt, oa, ob, post_w, ffn_w, alpha, tm)
    return xt.reshape(batch, seq, d)
```
